```python
import jax
import jax.numpy as jnp
from jax import lax
import numpy as np

D_MODEL = 1024
BATCH = 4
SEQ = 4096
DEPTH = 2

GRID_W = 64
CTX_LEN = 256
HEAD_DIM = 64
QBLK = 128
ROPE_THETA = 10000.0
LN_EPS = 1e-5
RMS_EPS = 1e-6
NEG_INF = -1e30

CONV_CH = 512
CONV_WIDTH = 31
WIN_HEADS = 8
WIN_KV_HEADS = 2
WIN_GROUP = WIN_HEADS // WIN_KV_HEADS
WINDOW = 128
GQA_HEADS = 8
GQA_KV_HEADS = 2
GQA_GROUP = GQA_HEADS // GQA_KV_HEADS
MLA_HEADS = 8
MLA_Q_RANK = 256
MLA_KV_RANK = 128
MLA_NOPE = 64
MLA_ROPE = 32
MLA_V = 64
N_GROUPS = 4
EXP_PER_GROUP = 8
N_EXPERTS = N_GROUPS * EXP_PER_GROUP
EXPERT_FF = 512
TOP_K = 2

N_EVEN = (DEPTH + 1) // 2
N_ODD = DEPTH // 2
DN_ALPHA = float((2 * DEPTH) ** 0.25)
DN_BETA = float((8 * DEPTH) ** -0.25)

EVEN_CUTS = (CONV_CH, 2 * CONV_CH, 2 * CONV_CH + WIN_HEADS * HEAD_DIM,
             2 * CONV_CH + (WIN_HEADS + WIN_KV_HEADS) * HEAD_DIM)
EVEN_KV_START = EVEN_CUTS[2]
EVEN_IN = 2 * CONV_CH + (WIN_HEADS + 2 * WIN_KV_HEADS) * HEAD_DIM
EVEN_MIX = CONV_CH + WIN_HEADS * HEAD_DIM
ODD_CUTS = (GQA_HEADS * HEAD_DIM,
            GQA_HEADS * HEAD_DIM + MLA_Q_RANK,
            GQA_HEADS * HEAD_DIM + MLA_Q_RANK + GQA_KV_HEADS * HEAD_DIM,
            GQA_HEADS * HEAD_DIM + MLA_Q_RANK + 2 * GQA_KV_HEADS * HEAD_DIM,
            GQA_HEADS * HEAD_DIM + MLA_Q_RANK + 2 * GQA_KV_HEADS * HEAD_DIM + MLA_KV_RANK)
ODD_KV_START = ODD_CUTS[1]
ODD_KV_CUTS = tuple(cut - ODD_KV_START for cut in ODD_CUTS[2:])
ODD_IN = ODD_CUTS[4] + MLA_ROPE
ODD_MIX = GQA_HEADS * HEAD_DIM + MLA_HEADS * MLA_V

kernel_name = 'hybrid_conv_window_mla_hmoe_dit'


def _layer_norm(x, g, b):
    xf = x.astype(jnp.float32)
    mu = jnp.mean(xf, axis=-1, keepdims=True)
    var = jnp.mean(jnp.square(xf - mu), axis=-1, keepdims=True)
    y = (xf - mu) * lax.rsqrt(var + LN_EPS) * g.astype(jnp.float32) + b.astype(jnp.float32)
    return y.astype(x.dtype)


def _rms_norm(x, g):
    xf = x.astype(jnp.float32)
    y = xf * lax.rsqrt(jnp.mean(jnp.square(xf), axis=-1, keepdims=True) + RMS_EPS) * g.astype(jnp.float32)
    return y.astype(x.dtype)


def _axial_tables(row, col, rot_dim):
    axis_dim = rot_dim // 2
    inv_freq = ROPE_THETA ** (-jnp.arange(0, axis_dim, 2, dtype=jnp.float32) / axis_dim)
    ang_r = row.astype(jnp.float32)[:, None] * inv_freq[None, :]
    ang_c = col.astype(jnp.float32)[:, None] * inv_freq[None, :]
    return (jnp.cos(ang_r), jnp.sin(ang_r), jnp.cos(ang_c), jnp.sin(ang_c))


def _rotate_half(x, cos, sin):
    x1, x2 = jnp.split(x, 2, axis=-1)
    return jnp.concatenate([x1 * cos - x2 * sin, x1 * sin + x2 * cos], axis=-1)


def _axial_rope(x, tables):
    n_mid = x.ndim - 3
    cos_r, sin_r, cos_c, sin_c = (t.reshape((t.shape[0],) + (1,) * n_mid + (t.shape[1],)).astype(x.dtype)
                                  for t in tables)
    x_r, x_c = jnp.split(x, 2, axis=-1)
    return jnp.concatenate([_rotate_half(x_r, cos_r, sin_r), _rotate_half(x_c, cos_c, sin_c)], axis=-1)


def _softmax_with_sink(s, sink):
    sink = jnp.broadcast_to(sink.astype(jnp.float32), s.shape[:-1] + (1,))
    return jax.nn.softmax(jnp.concatenate([sink, s], axis=-1), axis=-1)[..., 1:]


def _sweep_query_blocks(fn, *qs):
    bsz, n_q = qs[0].shape[:2]
    n_blk = n_q // QBLK
    blocks = tuple(jnp.moveaxis(q.reshape((bsz, n_blk, QBLK) + q.shape[2:]), 1, 0) for q in qs)
    out = lax.map(lambda blk: fn(*blk), blocks)
    return jnp.moveaxis(out, 0, 1).reshape((bsz, n_q) + out.shape[3:])


def _conv_module(a_val, a_gate, conv_w, conv_b, ln_g, ln_b):
    h = a_val * jax.nn.sigmoid(a_gate)
    h = lax.conv_general_dilated(h, conv_w, window_strides=(1,),
                                 padding=[(CONV_WIDTH // 2, CONV_WIDTH // 2)],
                                 dimension_numbers=('NWC', 'WIO', 'NWC'),
                                 feature_group_count=CONV_CH) + conv_b
    return jax.nn.silu(_layer_norm(h, ln_g, ln_b))


def _window_attention(q, k, v, k_ctx, v_ctx, sink):
    bsz, n, n_kv, n_g, hd = q.shape
    n_blk = n // WINDOW
    n_ctx = k_ctx.shape[1]
    scale = hd ** -0.5
    qb = q.reshape(bsz, n_blk, WINDOW, n_kv, n_g, hd)
    pad = ((0, 0), (WINDOW, WINDOW), (0, 0), (0, 0))

    def band(t):
        tb = jnp.pad(t, pad).reshape(bsz, n_blk + 2, WINDOW, n_kv, hd)
        return jnp.concatenate([tb[:, :-2], tb[:, 1:-1], tb[:, 2:]], axis=2)

    kb, vb = band(k), band(v)
    q_pos = (jnp.arange(n_blk) * WINDOW)[:, None, None] + jnp.arange(WINDOW)[None, :, None]
    k_pos = (jnp.arange(n_blk) * WINDOW - WINDOW)[:, None, None] + jnp.arange(3 * WINDOW)[None, None, :]
    valid = (jnp.abs(q_pos - k_pos) <= WINDOW) & (k_pos >= 0) & (k_pos < n)
    s_loc = jnp.einsum('bnqhgd,bnkhd->bnhgqk', qb, kb).astype(jnp.float32) * scale
    s_loc = jnp.where(valid[None, :, None, None], s_loc, NEG_INF)
    s_ctx = jnp.einsum('bnqhgd,bchd->bnhgqc', qb, k_ctx).astype(jnp.float32) * scale
    p = _softmax_with_sink(jnp.concatenate([s_ctx, s_loc], axis=-1), sink[None, None, :, :, None, None])
    p = p.astype(v.dtype)
    o = (jnp.einsum('bnhgqc,bchd->bnqhgd', p[..., :n_ctx], v_ctx)
         + jnp.einsum('bnhgqk,bnkhd->bnqhgd', p[..., n_ctx:], vb))
    return o.reshape(bsz, n, n_kv * n_g * hd)


def _context_sink_attention(q, k, v, sink):
    bsz, n, n_kv, n_g, hd = q.shape
    s = jnp.einsum('bqhgd,bkhd->bhgqk', q, k).astype(jnp.float32) * hd ** -0.5
    p = _softmax_with_sink(s, sink[None, :, :, None, None]).astype(v.dtype)
    return jnp.einsum('bhgqk,bkhd->bqhgd', p, v).reshape(bsz, n, n_kv * n_g * hd)


def _dense_gqa(q, k, v):
    scale = q.shape[-1] ** -0.5

    def block(qb):
        s = jnp.einsum('bqhgd,bkhd->bhgqk', qb, k).astype(jnp.float32) * scale
        p = jax.nn.softmax(s, axis=-1).astype(v.dtype)
        return jnp.einsum('bhgqk,bkhd->bqhgd', p, v)

    return _sweep_query_blocks(block, q)


def _dense_mla(q_nope, q_rope, k_nope, k_rope, v):
    scale = (MLA_NOPE + MLA_ROPE) ** -0.5

    def block(qn, qr):
        s = (jnp.einsum('bqhd,bkhd->bhqk', qn, k_nope)
             + jnp.einsum('bqhr,bkr->bhqk', qr, k_rope)).astype(jnp.float32) * scale
        p = jax.nn.softmax(s, axis=-1).astype(v.dtype)
        return jnp.einsum('bhqk,bkhd->bqhd', p, v)

    return _sweep_query_blocks(block, q_nope, q_rope)


def _even_mixer(u, uc, w_in, b_in, conv_w, conv_b, conv_ln_g, conv_ln_b, sink, w_out, b_out,
                tabs_hd, need_ctx):
    bsz, n, _ = u.shape
    n_ctx = uc.shape[1]
    sink_hg = sink.reshape(WIN_KV_HEADS, WIN_GROUP)
    a_val, a_gate, q, k, v = jnp.split(u @ w_in + b_in, EVEN_CUTS, axis=-1)
    conv_out = _conv_module(a_val, a_gate, conv_w, conv_b, conv_ln_g, conv_ln_b)
    q = _axial_rope(q.reshape(bsz, n, WIN_KV_HEADS, WIN_GROUP, HEAD_DIM), tabs_hd)
    k = _axial_rope(k.reshape(bsz, n, WIN_KV_HEADS, HEAD_DIM), tabs_hd)
    v = v.reshape(bsz, n, WIN_KV_HEADS, HEAD_DIM)
    if need_ctx:
        ac_val, ac_gate, qc, kc, vc = jnp.split(uc @ w_in + b_in, EVEN_CUTS, axis=-1)
    else:
        kc, vc = jnp.split(uc @ w_in[:, EVEN_KV_START:] + b_in[EVEN_KV_START:], 2, axis=-1)
    kc = kc.reshape(bsz, n_ctx, WIN_KV_HEADS, HEAD_DIM)
    vc = vc.reshape(bsz, n_ctx, WIN_KV_HEADS, HEAD_DIM)
    attn = _window_attention(q, k, v, kc, vc, sink_hg)
    y = jnp.concatenate([conv_out, attn], axis=-1) @ w_out + b_out
    if not need_ctx:
        return y, None
    conv_c = _conv_module(ac_val, ac_gate, conv_w, conv_b, conv_ln_g, conv_ln_b)
    attn_c = _context_sink_attention(qc.reshape(bsz, n_ctx, WIN_KV_HEADS, WIN_GROUP, HEAD_DIM), kc, vc, sink_hg)
    yc = jnp.concatenate([conv_c, attn_c], axis=-1) @ w_out + b_out
    return y, yc


def _odd_queries(q_g, q_c, q_norm, mla_q_norm, w_uq, tabs_hd, tabs_mla):
    bsz, n, _ = q_g.shape
    q = _rms_norm(q_g.reshape(bsz, n, GQA_KV_HEADS, GQA_GROUP, HEAD_DIM), q_norm)
    qm = (_rms_norm(q_c, mla_q_norm) @ w_uq).reshape(bsz, n, MLA_HEADS, MLA_NOPE + MLA_ROPE)
    q_nope, q_rope = qm[..., :MLA_NOPE], qm[..., MLA_NOPE:]
    if tabs_hd is not None:
        q = _axial_rope(q, tabs_hd)
        q_rope = _axial_rope(q_rope, tabs_mla)
    return q, q_nope, q_rope


def _odd_keys(k_g, v_g, kv_c, k_r, k_norm, mla_kv_norm, w_ukv, tabs_hd, tabs_mla):
    bsz, n, _ = k_g.shape
    k = _rms_norm(k_g.reshape(bsz, n, GQA_KV_HEADS, HEAD_DIM), k_norm)
    v = v_g.reshape(bsz, n, GQA_KV_HEADS, HEAD_DIM)
    kv = (_rms_norm(kv_c, mla_kv_norm) @ w_ukv).reshape(bsz, n, MLA_HEADS, MLA_NOPE + MLA_V)
    k_nope, v_m = kv[..., :MLA_NOPE], kv[..., MLA_NOPE:]
    if tabs_hd is not None:
        k = _axial_rope(k, tabs_hd)
        k_r = _axial_rope(k_r, tabs_mla)
    return k, v, k_nope, k_r, v_m


def _odd_mixer(u, uc, w_in, b_in, q_norm, k_norm, mla_q_norm, mla_kv_norm, w_uq, w_ukv, w_out, b_out,
               tabs_hd, tabs_mla, need_ctx):
    bsz, n, _ = u.shape
    q_g, q_c, k_g, v_g, kv_c, k_r = jnp.split(u @ w_in + b_in, ODD_CUTS, axis=-1)
    q, q_nope, q_rope = _odd_queries(q_g, q_c, q_norm, mla_q_norm, w_uq, tabs_hd, tabs_mla)
    k, v, k_nope, k_rope, v_m = _odd_keys(k_g, v_g, kv_c, k_r, k_norm, mla_kv_norm, w_ukv, tabs_hd, tabs_mla)
    if need_ctx:
        qc_g, qc_c, kc_g, vc_g, kvc_c, kc_r = jnp.split(uc @ w_in + b_in, ODD_CUTS, axis=-1)
    else:
        kc_g, vc_g, kvc_c, kc_r = jnp.split(uc @ w_in[:, ODD_KV_START:] + b_in[ODD_KV_START:], ODD_KV_CUTS, axis=-1)
    kc, vc, kc_nope, kc_rope, vc_m = _odd_keys(kc_g, vc_g, kvc_c, kc_r, k_norm, mla_kv_norm, w_ukv, None, None)
    o_g = _dense_gqa(q, jnp.concatenate([k, kc], axis=1), jnp.concatenate([v, vc], axis=1))
    o_m = _dense_mla(q_nope, q_rope, jnp.concatenate([k_nope, kc_nope], axis=1),
                     jnp.concatenate([k_rope, kc_rope], axis=1), jnp.concatenate([v_m, vc_m], axis=1))
    y = jnp.concatenate([o_g.reshape(bsz, n, -1), o_m.reshape(bsz, n, -1)], axis=-1) @ w_out + b_out
    if not need_ctx:
        return y, None
    n_ctx = uc.shape[1]
    qc, qc_nope, qc_rope = _odd_queries(qc_g, qc_c, q_norm, mla_q_norm, w_uq, None, None)
    oc_g = _dense_gqa(qc, kc, vc)
    oc_m = _dense_mla(qc_nope, qc_rope, kc_nope, kc_rope, vc_m)
    yc = jnp.concatenate([oc_g.reshape(bsz, n_ctx, -1), oc_m.reshape(bsz, n_ctx, -1)], axis=-1) @ w_out + b_out
    return y, yc


def _hier_moe(h, w_rg, b_rg, w_re, b_re, w1, w3, w2):
    p_grp = jax.nn.softmax((h @ w_rg + b_rg).astype(jnp.float32), axis=-1)
    g_w, g_idx = lax.top_k(p_grp, 1)
    e_logit = (jnp.einsum('td,gde->tge', h, w_re) + b_re).astype(jnp.float32)
    e_logit = jnp.take_along_axis(e_logit, g_idx[:, :, None], axis=1)[:, 0]
    e_val, e_idx = lax.top_k(e_logit, TOP_K)
    e_w = jax.nn.softmax(e_val, axis=-1) * g_w
    in_group = jnp.sum(jax.nn.one_hot(e_idx, EXP_PER_GROUP, dtype=jnp.float32) * e_w[..., None], axis=1)
    out = jnp.zeros_like(h)
    for g in range(N_GROUPS):
        sl = slice(g * EXP_PER_GROUP, (g + 1) * EXP_PER_GROUP)
        w_tok = jnp.where(g_idx == g, in_group, 0.0).astype(h.dtype)
        hid = jax.nn.silu(jnp.einsum('td,edf->tef', h, w1[sl])) * jnp.einsum('td,edf->tef', h, w3[sl])
        out = out + jnp.einsum('tef,efd->td', hid * w_tok[:, :, None], w2[sl])
    return out


def setup_inputs(seed: int = 0) -> dict:
    key = jax.random.key(seed)
    keys = iter(jax.random.split(key, 64))

    def nrm(shape, scale):
        return scale * jax.random.normal(next(keys), shape, dtype=jnp.float32)

    def gain(shape):
        return 1.0 + nrm(shape, 0.02)

    d = D_MODEL
    return {
        'x': nrm((BATCH, SEQ, d), 1.0),
        'c': nrm((BATCH, d), 1.0),
        'ctx': nrm((BATCH, CTX_LEN, d), 1.0),
        'c_ctx': nrm((d,), 1.0),
        'even_w_in': nrm((N_EVEN, d, EVEN_IN), d ** -0.5),
        'even_b_in': nrm((N_EVEN, EVEN_IN), 0.02),
        'even_conv_w': nrm((N_EVEN, CONV_WIDTH, 1, CONV_CH), CONV_WIDTH ** -0.5),
        'even_conv_b': nrm((N_EVEN, CONV_CH), 0.02),
        'even_conv_ln_g': gain((N_EVEN, CONV_CH)),
        'even_conv_ln_b': nrm((N_EVEN, CONV_CH), 0.02),
        'even_sink': nrm((N_EVEN, WIN_HEADS), 0.5),
        'even_w_out': nrm((N_EVEN, EVEN_MIX, d), DN_BETA * EVEN_MIX ** -0.5),
        'even_b_out': nrm((N_EVEN, d), 0.02),
        'odd_w_in': nrm((N_ODD, d, ODD_IN), d ** -0.5),
        'odd_b_in': nrm((N_ODD, ODD_IN), 0.02),
        'odd_q_norm': gain((N_ODD, HEAD_DIM)),
        'odd_k_norm': gain((N_ODD, HEAD_DIM)),
        'odd_mla_q_norm': gain((N_ODD, MLA_Q_RANK)),
        'odd_mla_kv_norm': gain((N_ODD, MLA_KV_RANK)),
        'odd_mla_w_uq': nrm((N_ODD, MLA_Q_RANK, MLA_HEADS * (MLA_NOPE + MLA_ROPE)), MLA_Q_RANK ** -0.5),
        'odd_mla_w_ukv': nrm((N_ODD, MLA_KV_RANK, MLA_HEADS * (MLA_NOPE + MLA_V)), MLA_KV_RANK ** -0.5),
        'odd_w_out': nrm((N_ODD, ODD_MIX, d), DN_BETA * ODD_MIX ** -0.5),
        'odd_b_out': nrm((N_ODD, d), 0.02),
        'ada_w': nrm((DEPTH, d, 6 * d), 0.5 * d ** -0.5),
        'ada_b': nrm((DEPTH, 6 * d), 0.02),
        'ln1_g': gain((DEPTH, d)),
        'ln1_b': nrm((DEPTH, d), 0.02),
        'ln2_g': gain((DEPTH, d)),
        'ln2_b': nrm((DEPTH, d), 0.02),
        'moe_w_rg': nrm((DEPTH, d, N_GROUPS), d ** -0.5),
        'moe_b_rg': nrm((DEPTH, N_GROUPS), 0.01),
        'moe_w_re': nrm((DEPTH, N_GROUPS, d, EXP_PER_GROUP), d ** -0.5),
        'moe_b_re': nrm((DEPTH, N_GROUPS, EXP_PER_GROUP), 0.01),
        'moe_w1': nrm((DEPTH, N_EXPERTS, d, EXPERT_FF), d ** -0.5),
        'moe_w3': nrm((DEPTH, N_EXPERTS, d, EXPERT_FF), d ** -0.5),
        'moe_w2': nrm((DEPTH, N_EXPERTS, EXPERT_FF, d), DN_BETA * EXPERT_FF ** -0.5),
    }


def reference(x, c, ctx, c_ctx,
              even_w_in, even_b_in, even_conv_w, even_conv_b, even_conv_ln_g, even_conv_ln_b, even_sink,
              even_w_out, even_b_out,
              odd_w_in, odd_b_in, odd_q_norm, odd_k_norm, odd_mla_q_norm, odd_mla_kv_norm, odd_mla_w_uq,
              odd_mla_w_ukv, odd_w_out, odd_b_out,
              ada_w, ada_b, ln1_g, ln1_b, ln2_g, ln2_b,
              moe_w_rg, moe_b_rg, moe_w_re, moe_b_re, moe_w1, moe_w3, moe_w2):
    bsz, n_lat, d = x.shape
    n_rows = n_lat // GRID_W
    row = jnp.repeat(jnp.arange(n_rows), GRID_W)
    col = jnp.tile(jnp.arange(GRID_W), n_rows)
    tabs_hd = _axial_tables(row, col, HEAD_DIM)
    tabs_mla = _axial_tables(row, col, MLA_ROPE)
    silu_c = jax.nn.silu(c)
    silu_cc = jax.nn.silu(c_ctx)
    for i in range(DEPTH):
        last = i == DEPTH - 1
        sh1, sc1, g1, sh2, sc2, g2 = (m[:, None, :] for m in jnp.split(silu_c @ ada_w[i] + ada_b[i], 6, axis=-1))
        shc1, scc1, gc1, shc2, scc2, gc2 = jnp.split(silu_cc @ ada_w[i] + ada_b[i], 6, axis=-1)
        u = x * (1 + sc1) + sh1
        uc = ctx * (1 + scc1) + shc1
        j = i // 2
        if i % 2 == 0:
            y, yc = _even_mixer(u, uc, even_w_in[j], even_b_in[j], even_conv_w[j], even_conv_b[j],
                                even_conv_ln_g[j], even_conv_ln_b[j], even_sink[j], even_w_out[j], even_b_out[j],
                                tabs_hd, not last)
        else:
            y, yc = _odd_mixer(u, uc, odd_w_in[j], odd_b_in[j], odd_q_norm[j], odd_k_norm[j], odd_mla_q_norm[j],
                               odd_mla_kv_norm[j], odd_mla_w_uq[j], odd_mla_w_ukv[j], odd_w_out[j], odd_b_out[j],
                               tabs_hd, tabs_mla, not last)
        x = _layer_norm(DN_ALPHA * x + (1 + g1) * y, ln1_g[i], ln1_b[i])
        u2 = x * (1 + sc2) + sh2
        if last:
            f = _hier_moe(u2.reshape(-1, d), moe_w_rg[i], moe_b_rg[i], moe_w_re[i], moe_b_re[i],
                          moe_w1[i], moe_w3[i], moe_w2[i])
            x = _layer_norm(DN_ALPHA * x + (1 + g2) * f.reshape(x.shape), ln2_g[i], ln2_b[i])
        else:
            ctx = _layer_norm(DN_ALPHA * ctx + (1 + gc1) * yc, ln1_g[i], ln1_b[i])
            u2c = ctx * (1 + scc2) + shc2
            f = _hier_moe(jnp.concatenate([u2.reshape(-1, d), u2c.reshape(-1, d)], axis=0),
                          moe_w_rg[i], moe_b_rg[i], moe_w_re[i], moe_b_re[i], moe_w1[i], moe_w3[i], moe_w2[i])
            f_lat = f[:bsz * n_lat].reshape(x.shape)
            f_ctx = f[bsz * n_lat:].reshape(ctx.shape)
            x = _layer_norm(DN_ALPHA * x + (1 + g2) * f_lat, ln2_g[i], ln2_b[i])
            ctx = _layer_norm(DN_ALPHA * ctx + (1 + gc2) * f_ctx, ln2_g[i], ln2_b[i])
    return x
```

```python
import functools

import jax
import jax.numpy as jnp
from jax import lax
from jax.experimental import pallas as pl
from jax.experimental.pallas import tpu as pltpu

F32 = jnp.float32
BF16 = jnp.bfloat16

D_MODEL = 1024
BATCH = 4
SEQ = 4096
DEPTH = 2
GRID_W = 64
CTX_LEN = 256
HEAD_DIM = 64
ROPE_THETA = 10000.0
LN_EPS = 1e-5
RMS_EPS = 1e-6
NEG_INF = -1e30

CONV_CH = 512
CONV_WIDTH = 31
WIN_HEADS = 8
WIN_KV_HEADS = 2
WINDOW = 128
GQA_HEADS = 8
GQA_KV_HEADS = 2
MLA_HEADS = 8
MLA_Q_RANK = 256
MLA_KV_RANK = 128
MLA_NOPE = 64
MLA_ROPE = 32
MLA_V = 64
N_GROUPS = 4
EXP_PER_GROUP = 8
N_EXPERTS = N_GROUPS * EXP_PER_GROUP
EXPERT_FF = 512
DN_ALPHA = float((2 * DEPTH) ** 0.25)

EVEN_IN = 2 * CONV_CH + (WIN_HEADS + 2 * WIN_KV_HEADS) * HEAD_DIM
ODD_IN = 1184
ODD_IN_PAD = 1280
MLA_PAD = 128

R_LAT = BATCH * SEQ
R_CTX = BATCH * CTX_LEN
R_ALL = R_LAT + R_CTX
TM = 256
NT_LAT = R_LAT // TM
NT_ALL = R_ALL // TM
TILES_PER_SEQ = SEQ // TM
HALO = 16
CONV_CHUNK = 32
TMM = 256
ROUTE_W = 128
VMEM_LIMIT = 56 * 1024 * 1024

SH1, SC1, G1, SH2, SC2, G2 = range(6)


def _sigmoid(x):
    return 1.0 / (1.0 + jnp.exp(-x))


def _layer_norm(z, g, b):
    mu = jnp.mean(z, axis=-1, keepdims=True)
    zc = z - mu
    var = jnp.mean(zc * zc, axis=-1, keepdims=True)
    return zc * lax.rsqrt(var + LN_EPS) * g + b


def _rope(x, cos, sin, half):
    n = x.shape[-1]
    lane = lax.broadcasted_iota(jnp.int32, x.shape, 1)
    first = (lane % (2 * half)) < half
    partner = jnp.where(first, pltpu.roll(x, n - half, 1), pltpu.roll(x, half, 1))
    return x * cos + partner * sin


def _mod_row(i):
    return jnp.where(i < NT_LAT, i // TILES_PER_SEQ, BATCH)


def _mod_spec(chunk):
    return pl.BlockSpec((None, None, 1, D_MODEL), lambda i: (_mod_row(i), chunk, 0, 0))


def _rope_row_block(i):
    return jnp.where(i < NT_LAT, i % TILES_PER_SEQ, TILES_PER_SEQ)


def _full(shape):
    nd = len(shape)
    return pl.BlockSpec(shape, lambda *_: (0,) * nd)


def _params():
    return pltpu.CompilerParams(vmem_limit_bytes=VMEM_LIMIT)


def _ada_kernel(cv_ref, w_ref, b_ref, o_ref):
    cv = cv_ref[...]
    s = cv * _sigmoid(cv)
    o_ref[...] = jnp.dot(s, w_ref[...], precision=lax.Precision.HIGHEST,
                         preferred_element_type=F32) + b_ref[...]


def _ada_table(cv, ada_w, ada_b):
    bn = 1536
    nb = (6 * D_MODEL) // bn
    return pl.pallas_call(
        _ada_kernel,
        grid=(DEPTH, nb),
        in_specs=[pl.BlockSpec((8, D_MODEL), lambda l, j: (0, 0)),
                  pl.BlockSpec((None, D_MODEL, bn), lambda l, j: (l, 0, j)),
                  pl.BlockSpec((None, 1, bn), lambda l, j: (l, 0, j))],
        out_specs=pl.BlockSpec((None, 8, bn), lambda l, j: (l, 0, j)),
        out_shape=jax.ShapeDtypeStruct((DEPTH, 8, 6 * D_MODEL), F32),
        compiler_params=_params(),
        name="ada_table",
    )(cv, ada_w, ada_b.reshape(DEPTH, 1, 6 * D_MODEL))


def _rope_tables(rot_dim):
    axis_dim = rot_dim // 2
    inv_freq = ROPE_THETA ** (-jnp.arange(0, axis_dim, 2, dtype=F32) / axis_dim)
    t = jnp.arange(SEQ)
    ang_r = (t // GRID_W).astype(F32)[:, None] * inv_freq[None, :]
    ang_c = (t % GRID_W).astype(F32)[:, None] * inv_freq[None, :]
    cos = jnp.concatenate([jnp.cos(ang_r), jnp.cos(ang_r), jnp.cos(ang_c), jnp.cos(ang_c)], axis=-1)
    sin = jnp.concatenate([-jnp.sin(ang_r), jnp.sin(ang_r), -jnp.sin(ang_c), jnp.sin(ang_c)], axis=-1)
    return cos, sin


def _pad_table(cos, sin, lead, period, width):
    rot = cos.shape[1]
    one = jnp.ones((SEQ, period), F32).at[:, lead:lead + rot].set(cos)
    zero = jnp.zeros((SEQ, period), F32).at[:, lead:lead + rot].set(sin)
    cos_w = jnp.tile(one, (1, width // period))
    sin_w = jnp.tile(zero, (1, width // period))
    cos_w = jnp.concatenate([cos_w, jnp.ones((TM, width), F32)], axis=0)
    sin_w = jnp.concatenate([sin_w, jnp.zeros((TM, width), F32)], axis=0)
    return cos_w, sin_w


def _proj0_kernel(x_ref, sh_ref, sc_ref, w_ref, b_ref, cos_ref, sin_ref,
                  h_ref, q_ref, k_ref, v_ref):
    u = x_ref[...] * (1.0 + sc_ref[...]) + sh_ref[...]
    y = jnp.dot(u.astype(BF16), w_ref[...], preferred_element_type=F32) + b_ref[...]
    h_ref[...] = y[:, :CONV_CH] * _sigmoid(y[:, CONV_CH:2 * CONV_CH])
    cos = cos_ref[...]
    sin = sin_ref[...]
    q0 = 2 * CONV_CH
    k0 = q0 + WIN_HEADS * HEAD_DIM
    v0 = k0 + WIN_KV_HEADS * HEAD_DIM
    cos4 = jnp.concatenate([cos] * 4, axis=1)
    sin4 = jnp.concatenate([sin] * 4, axis=1)
    q = _rope(y[:, q0:k0], cos4, sin4, HEAD_DIM // 4) * (HEAD_DIM ** -0.5)
    q_ref[...] = q.astype(BF16)
    k_ref[...] = _rope(y[:, k0:v0], cos, sin, HEAD_DIM // 4).astype(BF16)
    v_ref[...] = y[:, v0:].astype(BF16)


def _proj0(x_all, mods, w_in, b_in, cos_hd, sin_hd):
    kvw = WIN_KV_HEADS * HEAD_DIM
    row = lambda w: pl.BlockSpec((TM, w), lambda i: (i, 0))
    tab = pl.BlockSpec((TM, 128), lambda i: (_rope_row_block(i), 0))
    return pl.pallas_call(
        _proj0_kernel,
        grid=(NT_ALL,),
        in_specs=[row(D_MODEL), _mod_spec(SH1), _mod_spec(SC1),
                  _full((D_MODEL, EVEN_IN)), _full((1, EVEN_IN)), tab, tab],
        out_specs=[row(CONV_CH), row(WIN_HEADS * HEAD_DIM), row(kvw), row(kvw)],
        out_shape=[jax.ShapeDtypeStruct((R_ALL, CONV_CH), F32),
                   jax.ShapeDtypeStruct((R_ALL, WIN_HEADS * HEAD_DIM), BF16),
                   jax.ShapeDtypeStruct((R_ALL, kvw), BF16),
                   jax.ShapeDtypeStruct((R_ALL, kvw), BF16)],
        compiler_params=_params(),
        name="proj0",
    )(x_all, mods, mods, w_in, b_in, cos_hd, sin_hd)


def _conv_kernel(prev_ref, cur_ref, next_ref, w_ref, cb_ref, g_ref, b_ref, o_ref, buf):
    i = pl.program_id(0)
    is_ctx = i >= NT_LAT
    first = jnp.logical_or(is_ctx, i % TILES_PER_SEQ == 0)
    last = jnp.logical_or(is_ctx, i % TILES_PER_SEQ == TILES_PER_SEQ - 1)
    buf[0:HALO, :] = jnp.where(first, 0.0, prev_ref[...])
    buf[HALO:HALO + TM, :] = cur_ref[...]
    buf[HALO + TM:, :] = jnp.where(last, 0.0, next_ref[...])
    off = HALO - CONV_WIDTH // 2
    for c in range(TM // CONV_CHUNK):
        r0 = c * CONV_CHUNK
        acc = jnp.zeros((CONV_CHUNK, CONV_CH), F32)
        for k in range(CONV_WIDTH):
            acc = acc + buf[r0 + off + k:r0 + off + k + CONV_CHUNK, :] * w_ref[k:k + 1, :]
        z = _layer_norm(acc + cb_ref[...], g_ref[...], b_ref[...])
        o_ref[r0:r0 + CONV_CHUNK, :] = (z * _sigmoid(z)).astype(BF16)


def _conv(h, conv_w, conv_b, ln_g, ln_b):
    nh = R_ALL // HALO
    per = TM // HALO
    vec = _full((1, CONV_CH))
    return pl.pallas_call(
        _conv_kernel,
        grid=(NT_ALL,),
        in_specs=[pl.BlockSpec((HALO, CONV_CH), lambda i: (jnp.maximum(i * per - 1, 0), 0)),
                  pl.BlockSpec((TM, CONV_CH), lambda i: (i, 0)),
                  pl.BlockSpec((HALO, CONV_CH), lambda i: (jnp.minimum((i + 1) * per, nh - 1), 0)),
                  _full((CONV_WIDTH, CONV_CH)), vec, vec, vec],
        out_specs=pl.BlockSpec((TM, CONV_CH), lambda i: (i, 0)),
        out_shape=jax.ShapeDtypeStruct((R_ALL, CONV_CH), BF16),
        scratch_shapes=[pltpu.VMEM((TM + 2 * HALO, CONV_CH), F32)],
        compiler_params=_params(),
        name="conv_module",
    )(h, h, h, conv_w, conv_b, ln_g, ln_b)


def _nt_dot(a, b):
    return lax.dot_general(a, b, (((1,), (1,)), ((), ())), preferred_element_type=F32)


def _head_attention(q, ks, vs, valid, sink):
    ss = []
    for k, msk in zip(ks, valid):
        s = _nt_dot(q, k)
        if msk is not None:
            s = jnp.where(msk, s, NEG_INF)
        ss.append(s)
    m = functools.reduce(jnp.maximum, [jnp.max(s, axis=-1, keepdims=True) for s in ss])
    if sink is not None:
        m = jnp.maximum(m, sink)
    ps = [jnp.exp(s - m) for s in ss]
    l = functools.reduce(jnp.add, [jnp.sum(p, axis=-1, keepdims=True) for p in ps])
    if sink is not None:
        l = l + jnp.exp(sink - m)
    o = functools.reduce(jnp.add, [jnp.dot(p.astype(BF16), v, preferred_element_type=F32)
                                   for p, v in zip(ps, vs)])
    return o / l


def _win_kernel(sink_ref, q_ref, kp_ref, kc_ref, kn_ref, kx_ref, vp_ref, vc_ref, vn_ref, vx_ref, o_ref):
    n = pl.program_id(1)
    k_loc = jnp.concatenate([kp_ref[...], kc_ref[...], kn_ref[...]], axis=0)
    v_loc = jnp.concatenate([vp_ref[...], vc_ref[...], vn_ref[...]], axis=0)
    k_ctx = kx_ref[...]
    v_ctx = vx_ref[...]
    qi = lax.broadcasted_iota(jnp.int32, (WINDOW, 3 * WINDOW), 0)
    kj = lax.broadcasted_iota(jnp.int32, (WINDOW, 3 * WINDOW), 1)
    k_pos = jnp.where(n < SEQ // WINDOW, kj + (n - 1) * WINDOW, SEQ)
    valid = jnp.where(kj >= qi, jnp.where(kj <= qi + 2 * WINDOW, 1, 0), 0)
    valid = jnp.where(k_pos >= 0, jnp.where(k_pos < SEQ, valid, 0), 0) > 0
    group = WIN_HEADS // WIN_KV_HEADS
    for h in range(WIN_HEADS):
        kv = h // group
        sl = slice(kv * HEAD_DIM, (kv + 1) * HEAD_DIM)
        o = _head_attention(q_ref[:, h * HEAD_DIM:(h + 1) * HEAD_DIM],
                            [k_ctx[:, sl], k_loc[:, sl]], [v_ctx[:, sl], v_loc[:, sl]],
                            [None, valid], sink_ref[h])
        o_ref[:, h * HEAD_DIM:(h + 1) * HEAD_DIM] = o.astype(BF16)


def _win_attention(sink, q, k, v):
    nblk = SEQ // WINDOW
    cblk = CTX_LEN // WINDOW
    kvw = WIN_KV_HEADS * HEAD_DIM
    ctx0 = R_LAT // CTX_LEN
    lat = lambda n: jnp.minimum(n, nblk - 1)
    prev = lambda b, n: (b * nblk + jnp.maximum(lat(n) - 1, 0), 0)
    cur = lambda b, n: (b * nblk + lat(n), 0)
    nxt = lambda b, n: (b * nblk + jnp.minimum(lat(n) + 1, nblk - 1), 0)
    qrow = lambda b, n: (jnp.where(n < nblk, b * nblk + n, R_LAT // WINDOW + b * cblk + n - nblk), 0)
    ctx = lambda b, n: (ctx0 + b, 0)
    kvb = lambda f: pl.BlockSpec((WINDOW, kvw), f)
    cxb = pl.BlockSpec((CTX_LEN, kvw), ctx)
    return pl.pallas_call(
        _win_kernel,
        grid=(BATCH, nblk + cblk),
        in_specs=[pl.BlockSpec(memory_space=pltpu.SMEM),
                  pl.BlockSpec((WINDOW, WIN_HEADS * HEAD_DIM), qrow),
                  kvb(prev), kvb(cur), kvb(nxt), cxb, kvb(prev), kvb(cur), kvb(nxt), cxb],
        out_specs=pl.BlockSpec((WINDOW, WIN_HEADS * HEAD_DIM), qrow),
        out_shape=jax.ShapeDtypeStruct((R_ALL, WIN_HEADS * HEAD_DIM), BF16),
        compiler_params=_params(),
        name="window_attention",
    )(sink, q, k, k, k, k, v, v, v, v)


def _outproj_kernel(a_ref, b_ref, wa_ref, wb_ref, bo_ref, x_ref, g1_ref, sh2_ref, sc2_ref,
                    lng_ref, lnb_ref, wr_ref, br_ref, xo_ref, u2_ref, route_ref):
    y = (jnp.dot(a_ref[...], wa_ref[...], preferred_element_type=F32)
         + jnp.dot(b_ref[...], wb_ref[...], preferred_element_type=F32) + bo_ref[...])
    xn = _layer_norm(DN_ALPHA * x_ref[...] + (1.0 + g1_ref[...]) * y, lng_ref[...], lnb_ref[...])
    xo_ref[...] = xn
    u2 = xn * (1.0 + sc2_ref[...]) + sh2_ref[...]
    u2_ref[...] = u2
    logits = jnp.dot(u2, wr_ref[...], precision=lax.Precision.HIGHEST,
                     preferred_element_type=F32) + br_ref[...]
    lane = lax.broadcasted_iota(jnp.int32, logits.shape, 1).astype(F32)
    ninf = -jnp.inf
    big = float(ROUTE_W)
    gl = jnp.where(lane < N_GROUPS, logits, ninf)
    gmax = jnp.max(gl, axis=-1, keepdims=True)
    gidx = jnp.min(jnp.where(gl == gmax, lane, big), axis=-1, keepdims=True)
    g_w = 1.0 / jnp.sum(jnp.exp(gl - gmax), axis=-1, keepdims=True)
    lo = N_GROUPS + EXP_PER_GROUP * gidx
    el = jnp.where(lane >= lo, jnp.where(lane < lo + EXP_PER_GROUP, logits, ninf), ninf)
    v1 = jnp.max(el, axis=-1, keepdims=True)
    i1 = jnp.min(jnp.where(el == v1, lane, big), axis=-1, keepdims=True)
    el2 = jnp.where(lane == i1, ninf, el)
    v2 = jnp.max(el2, axis=-1, keepdims=True)
    i2 = jnp.min(jnp.where(el2 == v2, lane, big), axis=-1, keepdims=True)
    e2 = jnp.exp(v2 - v1)
    w1 = g_w / (1.0 + e2)
    w2 = g_w * e2 / (1.0 + e2)
    rec = jnp.where(lane == 0.0, i1 - N_GROUPS,
                    jnp.where(lane == 1.0, i2 - N_GROUPS,
                              jnp.where(lane == 2.0, w1, jnp.where(lane == 3.0, w2, 0.0))))
    route_ref[...] = rec


def _outproj(n_tiles, mix_a, mix_b, w_a, w_b, b_out, x_all, mods, ln_g, ln_b, w_r, b_r):
    rows = n_tiles * TM
    half = mix_a.shape[1]
    row = lambda w: pl.BlockSpec((TM, w), lambda i: (i, 0))
    vec = _full((1, D_MODEL))
    return pl.pallas_call(
        _outproj_kernel,
        grid=(n_tiles,),
        in_specs=[row(half), row(half), _full((half, D_MODEL)), _full((half, D_MODEL)), vec,
                  row(D_MODEL), _mod_spec(G1), _mod_spec(SH2), _mod_spec(SC2), vec, vec,
                  _full((D_MODEL, ROUTE_W)), _full((1, ROUTE_W))],
        out_specs=[row(D_MODEL), row(D_MODEL), row(ROUTE_W)],
        out_shape=[jax.ShapeDtypeStruct((rows, D_MODEL), F32),
                   jax.ShapeDtypeStruct((rows, D_MODEL), F32),
                   jax.ShapeDtypeStruct((rows, ROUTE_W), F32)],
        compiler_params=_params(),
        name="outproj_ln_router",
    )(mix_a, mix_b, w_a, w_b, b_out, x_all, mods, mods, mods, ln_g, ln_b, w_r, b_r)


def _moe_kernel(te_ref, tv_ref, st_ref, u_hbm, w1_ref, w3_ref, w2_ref, ys_ref,
                xbuf, wb1, wb3, wb2, sem):
    i = pl.program_id(0)
    nt = pl.num_programs(0)
    slot = i % 2

    def row_copy(tok, slot_, r):
        return pltpu.make_async_copy(u_hbm.at[pl.ds(tok, 1)], xbuf.at[slot_, pl.ds(r, 1)], sem.at[slot_])

    def issue(tile, slot_):
        def body(r, carry):
            row_copy(st_ref[tile * TMM + r], slot_, r).start()
            return carry
        lax.fori_loop(0, TMM, body, 0, unroll=8)

    @pl.when(jnp.logical_and(i == 0, tv_ref[0] == 1))
    def _():
        issue(0, 0)

    nxt = jnp.minimum(i + 1, nt - 1)

    @pl.when(jnp.logical_and(i + 1 < nt, tv_ref[nxt] == 1))
    def _():
        issue(i + 1, 1 - slot)

    @pl.when(jnp.logical_or(i == 0, te_ref[i] != te_ref[jnp.maximum(i - 1, 0)]))
    def _():
        wb1[...] = w1_ref[...].astype(BF16)
        wb3[...] = w3_ref[...].astype(BF16)
        wb2[...] = w2_ref[...].astype(BF16)

    @pl.when(tv_ref[i] == 1)
    def _():
        def wbody(r, carry):
            row_copy(0, slot, r).wait()
            return carry
        lax.fori_loop(0, TMM, wbody, 0, unroll=8)
        x = xbuf[slot].astype(BF16)
        h1 = jnp.dot(x, wb1[...], preferred_element_type=F32)
        h3 = jnp.dot(x, wb3[...], preferred_element_type=F32)
        hid = h1 * _sigmoid(h1) * h3
        ys_ref[...] = jnp.dot(hid.astype(BF16), wb2[...], preferred_element_type=F32)

    @pl.when(tv_ref[i] == 0)
    def _():
        ys_ref[...] = jnp.zeros_like(ys_ref)


def _moe_experts(layer, tile_expert, tile_valid, src_tok, u2, w1, w3, w2):
    nt = tile_expert.shape[0]
    wmap = lambda i, te, tv, st: (layer, te[i], 0, 0)
    grid_spec = pltpu.PrefetchScalarGridSpec(
        num_scalar_prefetch=3,
        grid=(nt,),
        in_specs=[pl.BlockSpec(memory_space=pl.ANY),
                  pl.BlockSpec((None, None, D_MODEL, EXPERT_FF), wmap),
                  pl.BlockSpec((None, None, D_MODEL, EXPERT_FF), wmap),
                  pl.BlockSpec((None, None, EXPERT_FF, D_MODEL), wmap)],
        out_specs=pl.BlockSpec((TMM, D_MODEL), lambda i, te, tv, st: (i, 0)),
        scratch_shapes=[pltpu.VMEM((2, TMM, D_MODEL), F32),
                        pltpu.VMEM((D_MODEL, EXPERT_FF), BF16),
                        pltpu.VMEM((D_MODEL, EXPERT_FF), BF16),
                        pltpu.VMEM((EXPERT_FF, D_MODEL), BF16),
                        pltpu.SemaphoreType.DMA((2,))])
    return pl.pallas_call(
        _moe_kernel,
        grid_spec=grid_spec,
        out_shape=jax.ShapeDtypeStruct((nt * TMM, D_MODEL), F32),
        compiler_params=_params(),
        name="moe_experts",
    )(tile_expert, tile_valid, src_tok, u2, w1, w3, w2)


def _combine_kernel(pos_ref, ys_hbm, x_ref, route_ref, g2_ref, lng_ref, lnb_ref, o_ref, fbuf, sem,
                    *, n_tok):
    i = pl.program_id(0)
    nt = pl.num_programs(0)
    slot = i % 2

    def row_copy(p, slot_, s, r):
        return pltpu.make_async_copy(ys_hbm.at[pl.ds(p, 1)], fbuf.at[slot_, s, pl.ds(r, 1)], sem.at[slot_])

    def issue(tile, slot_):
        for s in range(2):
            def body(r, carry):
                row_copy(pos_ref[s * n_tok + tile * TM + r], slot_, s, r).start()
                return carry
            lax.fori_loop(0, TM, body, 0, unroll=8)

    @pl.when(i == 0)
    def _():
        issue(0, 0)

    @pl.when(i + 1 < nt)
    def _():
        issue(i + 1, 1 - slot)

    for s in range(2):
        def wbody(r, carry):
            row_copy(0, slot, s, r).wait()
            return carry
        lax.fori_loop(0, TM, wbody, 0, unroll=8)

    route = route_ref[...]
    f = route[:, 2:3] * fbuf[slot, 0] + route[:, 3:4] * fbuf[slot, 1]
    z = DN_ALPHA * x_ref[...] + (1.0 + g2_ref[...]) * f
    o_ref[...] = _layer_norm(z, lng_ref[...], lnb_ref[...])


def _moe_combine(n_tiles, pos, ys, x_all, route, mods, ln_g, ln_b):
    rows = n_tiles * TM
    row = lambda w: pl.BlockSpec((TM, w), lambda i, p: (i, 0))
    vec = pl.BlockSpec((1, D_MODEL), lambda i, p: (0, 0))
    grid_spec = pltpu.PrefetchScalarGridSpec(
        num_scalar_prefetch=1,
        grid=(n_tiles,),
        in_specs=[pl.BlockSpec(memory_space=pl.ANY), row(D_MODEL), row(ROUTE_W),
                  pl.BlockSpec((None, None, 1, D_MODEL), lambda i, p: (_mod_row(i), G2, 0, 0)),
                  vec, vec],
        out_specs=row(D_MODEL),
        scratch_shapes=[pltpu.VMEM((2, 2, TM, D_MODEL), F32), pltpu.SemaphoreType.DMA((2,))])
    return pl.pallas_call(
        functools.partial(_combine_kernel, n_tok=rows),
        grid_spec=grid_spec,
        out_shape=jax.ShapeDtypeStruct((rows, D_MODEL), F32),
        compiler_params=_params(),
        name="moe_combine_ln",
    )(pos, ys, x_all, route, mods, ln_g, ln_b)


def _moe_plan(route, n_tok):
    n_asg = 2 * n_tok
    nt_max = n_asg // TMM + N_EXPERTS
    e_flat = route[:n_tok, :2].astype(jnp.int32).reshape(-1)
    onehot = (e_flat[:, None] == jnp.arange(N_EXPERTS, dtype=jnp.int32)[None, :]).astype(jnp.int32)
    csum = jnp.cumsum(onehot, axis=0)
    rank = jnp.sum(onehot * csum, axis=1) - 1
    counts = csum[-1]
    tiles_e = (counts + TMM - 1) // TMM
    tile_end = jnp.cumsum(tiles_e)
    tile_start = tile_end - tiles_e
    dest = jnp.sum(onehot * tile_start[None, :], axis=1) * TMM + rank
    tile_id = jnp.arange(nt_max, dtype=jnp.int32)
    tile_expert = jnp.minimum(jnp.sum((tile_id[:, None] >= tile_end[None, :]).astype(jnp.int32), axis=1),
                              N_EXPERTS - 1)
    tile_valid = (tile_id < tile_end[-1]).astype(jnp.int32)
    src_tok = jnp.zeros((nt_max * TMM,), jnp.int32).at[dest].set(jnp.arange(n_asg, dtype=jnp.int32) // 2)
    pos = dest.reshape(n_tok, 2).T.reshape(-1)
    return tile_expert, tile_valid, src_tok, pos


def _router_weights(w_rg, b_rg, w_re, b_re):
    w = jnp.concatenate([w_rg, jnp.transpose(w_re, (1, 0, 2)).reshape(D_MODEL, N_EXPERTS)], axis=1)
    b = jnp.concatenate([b_rg, b_re.reshape(-1)])
    pad = ROUTE_W - w.shape[1]
    return jnp.pad(w, ((0, 0), (0, pad))), jnp.pad(b, (0, pad)).reshape(1, ROUTE_W)


def _proj1_kernel(x_ref, sh_ref, sc_ref, w_ref, b_ref, cos_ref, sin_ref, cosm_ref, sinm_ref, cosr_ref,
                  sinr_ref, gq_ref, gk_ref, gqc_ref, gkv_ref, avg_ref, wuq_ref, wuk_ref, wuv_ref,
                  q_ref, qm_ref, k_ref, v_ref, km_ref, vm_ref):
    u = x_ref[...] * (1.0 + sc_ref[...]) + sh_ref[...]
    y = jnp.dot(u.astype(BF16), w_ref[...], preferred_element_type=F32) + b_ref[...]
    c_q = GQA_HEADS * HEAD_DIM
    c_qc = c_q + MLA_Q_RANK
    c_k = c_qc + GQA_KV_HEADS * HEAD_DIM
    c_v = c_k + GQA_KV_HEADS * HEAD_DIM
    c_kv = c_v + MLA_KV_RANK
    avg = avg_ref[...]

    def head_rms(t, gain):
        sq = t * t
        hi = sq.astype(BF16)
        lo = (sq - hi.astype(F32)).astype(BF16)
        a = avg[:t.shape[1], :t.shape[1]]
        ms = jnp.dot(hi, a, preferred_element_type=F32) + jnp.dot(lo, a, preferred_element_type=F32)
        return t * lax.rsqrt(ms + RMS_EPS) * gain

    def row_rms(t, gain):
        ms = jnp.mean(t * t, axis=-1, keepdims=True)
        return t * lax.rsqrt(ms + RMS_EPS) * gain

    cos = cos_ref[...]
    sin = sin_ref[...]
    cos4 = jnp.concatenate([cos] * 4, axis=1)
    sin4 = jnp.concatenate([sin] * 4, axis=1)
    q = _rope(head_rms(y[:, :c_q], gq_ref[...]), cos4, sin4, HEAD_DIM // 4) * (HEAD_DIM ** -0.5)
    q_ref[...] = q.astype(BF16)
    k = _rope(head_rms(y[:, c_qc:c_k], gk_ref[...]), cos, sin, HEAD_DIM // 4)
    k_ref[...] = k.astype(BF16)
    v_ref[...] = y[:, c_k:c_v].astype(BF16)

    qc = row_rms(y[:, c_q:c_qc], gqc_ref[...]).astype(BF16)
    qm = jnp.dot(qc, wuq_ref[...], preferred_element_type=F32)
    cosm = jnp.concatenate([cosm_ref[...]] * MLA_HEADS, axis=1)
    sinm = jnp.concatenate([sinm_ref[...]] * MLA_HEADS, axis=1)
    qm = _rope(qm, cosm, sinm, MLA_ROPE // 4) * ((MLA_NOPE + MLA_ROPE) ** -0.5)
    qm_ref[...] = qm.astype(BF16)

    kvn = row_rms(y[:, c_v:c_kv], gkv_ref[...]).astype(BF16)
    kr = _rope(y[:, c_kv:], cosr_ref[...], sinr_ref[...], MLA_ROPE // 4).astype(BF16)
    km = jnp.dot(jnp.concatenate([kvn, kr], axis=1), wuk_ref[...], preferred_element_type=F32)
    km_ref[...] = km.astype(BF16)
    vm_ref[...] = jnp.dot(kvn, wuv_ref[...], preferred_element_type=F32).astype(BF16)


def _proj1(x_all, mods, w_in, b_in, tabs, gq, gk, gqc, gkv, avg, wuq, wuk, wuv):
    cos_hd, sin_hd, cos_m, sin_m, cos_r, sin_r = tabs
    kvw = GQA_KV_HEADS * HEAD_DIM
    qw = GQA_HEADS * HEAD_DIM
    mw = MLA_HEADS * MLA_PAD
    vw = MLA_HEADS * MLA_V
    row = lambda w: pl.BlockSpec((TM, w), lambda i: (i, 0))
    tab = pl.BlockSpec((TM, 128), lambda i: (_rope_row_block(i), 0))
    return pl.pallas_call(
        _proj1_kernel,
        grid=(NT_ALL,),
        in_specs=[row(D_MODEL), _mod_spec(SH1), _mod_spec(SC1),
                  _full((D_MODEL, ODD_IN_PAD)), _full((1, ODD_IN_PAD)), tab, tab, tab, tab, tab, tab,
                  _full((1, qw)), _full((1, kvw)), _full((1, MLA_Q_RANK)), _full((1, MLA_KV_RANK)),
                  _full((qw, qw)), _full((MLA_Q_RANK, mw)), _full((MLA_KV_RANK + 128, mw)),
                  _full((MLA_KV_RANK, vw))],
        out_specs=[row(qw), row(mw), row(kvw), row(kvw), row(mw), row(vw)],
        out_shape=[jax.ShapeDtypeStruct((R_ALL, qw), BF16),
                   jax.ShapeDtypeStruct((R_ALL, mw), BF16),
                   jax.ShapeDtypeStruct((R_ALL, kvw), BF16),
                   jax.ShapeDtypeStruct((R_ALL, kvw), BF16),
                   jax.ShapeDtypeStruct((R_ALL, mw), BF16),
                   jax.ShapeDtypeStruct((R_ALL, vw), BF16)],
        compiler_params=_params(),
        name="proj1",
    )(x_all, mods, mods, w_in, b_in, cos_hd, sin_hd, cos_m, sin_m, cos_r, sin_r,
      gq, gk, gqc, gkv, avg, wuq, wuk, wuv)


def _dense_kernel(q_ref, kl_ref, kc_ref, vl_ref, vc_ref, o_ref, *, n_heads, group, dk, dv):
    for h in range(n_heads):
        kv = h // group
        ks = slice(kv * dk, (kv + 1) * dk)
        vs = slice(kv * dv, (kv + 1) * dv)
        o = _head_attention(q_ref[:, h * dk:(h + 1) * dk], [kl_ref[:, ks], kc_ref[:, ks]],
                            [vl_ref[:, vs], vc_ref[:, vs]], [None, None], None)
        o_ref[:, h * dv:(h + 1) * dv] = o.astype(BF16)


def _dense_attention(q, k, v, *, n_heads, group, dk, dv, tq, name):
    n_kv = n_heads // group
    nq = SEQ // tq
    ctx0 = R_LAT // CTX_LEN
    lat = lambda w: pl.BlockSpec((SEQ, w), lambda b, j: (b, 0))
    ctx = lambda w: pl.BlockSpec((CTX_LEN, w), lambda b, j: (ctx0 + b, 0))
    return pl.pallas_call(
        functools.partial(_dense_kernel, n_heads=n_heads, group=group, dk=dk, dv=dv),
        grid=(BATCH, nq),
        in_specs=[pl.BlockSpec((tq, n_heads * dk), lambda b, j: (b * nq + j, 0)),
                  lat(n_kv * dk), ctx(n_kv * dk), lat(n_kv * dv), ctx(n_kv * dv)],
        out_specs=pl.BlockSpec((tq, n_heads * dv), lambda b, j: (b * nq + j, 0)),
        out_shape=jax.ShapeDtypeStruct((R_LAT, n_heads * dv), BF16),
        compiler_params=_params(),
        name=name,
    )(q, k, k, v, v)


def _mla_weights(w_uq, w_ukv):
    wq = w_uq.reshape(MLA_Q_RANK, MLA_HEADS, MLA_NOPE + MLA_ROPE)
    wq = jnp.pad(wq, ((0, 0), (0, 0), (0, MLA_PAD - MLA_NOPE - MLA_ROPE))).reshape(MLA_Q_RANK, -1)
    wkv = w_ukv.reshape(MLA_KV_RANK, MLA_HEADS, MLA_NOPE + MLA_V)
    wk = jnp.pad(wkv[:, :, :MLA_NOPE], ((0, 0), (0, 0), (0, MLA_PAD - MLA_NOPE))).reshape(MLA_KV_RANK, -1)
    wv = wkv[:, :, MLA_NOPE:].reshape(MLA_KV_RANK, -1)
    r = jnp.arange(128)[:, None]
    c = jnp.arange(MLA_HEADS * MLA_PAD)[None, :]
    place = jnp.logical_and(r < MLA_ROPE, (c % MLA_PAD) == MLA_NOPE + r).astype(F32)
    wk = jnp.concatenate([wk, place], axis=0)
    return wq.astype(BF16), wk.astype(BF16), wv.astype(BF16)


def kernel(x, c, ctx, c_ctx, even_w_in, even_b_in, even_conv_w, even_conv_b, even_conv_ln_g, even_conv_ln_b, even_sink, even_w_out, even_b_out, odd_w_in, odd_b_in, odd_q_norm, odd_k_norm, odd_mla_q_norm, odd_mla_kv_norm, odd_mla_w_uq, odd_mla_w_ukv, odd_w_out, odd_b_out, ada_w, ada_b, ln1_g, ln1_b, ln2_g, ln2_b, moe_w_rg, moe_b_rg, moe_w_re, moe_b_re, moe_w1, moe_w3, moe_w2):
    vec = lambda a: a.reshape(1, -1)
    x_all = jnp.concatenate([x.reshape(R_LAT, D_MODEL), ctx.reshape(R_CTX, D_MODEL)], axis=0)

    cv = jnp.concatenate([c, c_ctx[None, :], jnp.zeros((8 - BATCH - 1, D_MODEL), F32)], axis=0)
    mods = _ada_table(cv, ada_w, ada_b).reshape(DEPTH, 8, 6, 1, D_MODEL)

    cos64, sin64 = _rope_tables(HEAD_DIM)
    cos_hd, sin_hd = _pad_table(cos64, sin64, 0, HEAD_DIM, 128)
    cos32, sin32 = _rope_tables(MLA_ROPE)
    cos_m, sin_m = _pad_table(cos32, sin32, MLA_NOPE, MLA_PAD, 128)
    cos_r, sin_r = _pad_table(cos32, sin32, 0, 128, 128)

    m0 = mods[0]
    h, q0, k0, v0 = _proj0(x_all, m0, even_w_in[0].astype(BF16), vec(even_b_in[0]), cos_hd, sin_hd)
    conv_out = _conv(h, even_conv_w[0].reshape(CONV_WIDTH, CONV_CH), vec(even_conv_b[0]),
                     vec(even_conv_ln_g[0]), vec(even_conv_ln_b[0]))
    attn = _win_attention(even_sink[0], q0, k0, v0)
    w_out = even_w_out[0].astype(BF16)
    w_r, b_r = _router_weights(moe_w_rg[0], moe_b_rg[0], moe_w_re[0], moe_b_re[0])
    x_all, u2, route = _outproj(NT_ALL, conv_out, attn, w_out[:CONV_CH], w_out[CONV_CH:], vec(even_b_out[0]),
                                x_all, m0, vec(ln1_g[0]), vec(ln1_b[0]), w_r, b_r)
    tile_expert, tile_valid, src_tok, pos = _moe_plan(route, R_ALL)
    ys = _moe_experts(0, tile_expert, tile_valid, src_tok, u2, moe_w1, moe_w3, moe_w2)
    x_all = _moe_combine(NT_ALL, pos, ys, x_all, route, m0, vec(ln2_g[0]), vec(ln2_b[0]))

    m1 = mods[1]
    w_in1 = jnp.pad(odd_w_in[0], ((0, 0), (0, ODD_IN_PAD - ODD_IN))).astype(BF16)
    b_in1 = jnp.pad(odd_b_in[0], (0, ODD_IN_PAD - ODD_IN)).reshape(1, -1)
    wuq, wuk, wuv = _mla_weights(odd_mla_w_uq[0], odd_mla_w_ukv[0])
    qw = GQA_HEADS * HEAD_DIM
    hid = jnp.arange(qw) // HEAD_DIM
    avg = ((hid[:, None] == hid[None, :]).astype(F32) / HEAD_DIM).astype(BF16)
    q1, qm, k1, v1, km, vm = _proj1(
        x_all, m1, w_in1, b_in1, (cos_hd, sin_hd, cos_m, sin_m, cos_r, sin_r),
        vec(jnp.tile(odd_q_norm[0], GQA_HEADS)), vec(jnp.tile(odd_k_norm[0], GQA_KV_HEADS)),
        vec(odd_mla_q_norm[0]), vec(odd_mla_kv_norm[0]), avg, wuq, wuk, wuv)
    o_g = _dense_attention(q1, k1, v1, n_heads=GQA_HEADS, group=GQA_HEADS // GQA_KV_HEADS,
                           dk=HEAD_DIM, dv=HEAD_DIM, tq=256, name="gqa_attention")
    o_m = _dense_attention(qm, km, vm, n_heads=MLA_HEADS, group=1, dk=MLA_PAD, dv=MLA_V, tq=128,
                           name="mla_attention")
    w_out = odd_w_out[0].astype(BF16)
    w_r, b_r = _router_weights(moe_w_rg[1], moe_b_rg[1], moe_w_re[1], moe_b_re[1])
    x_lat, u2, route = _outproj(NT_LAT, o_g, o_m, w_out[:qw], w_out[qw:], vec(odd_b_out[0]),
                                x_all, m1, vec(ln1_g[1]), vec(ln1_b[1]), w_r, b_r)
    tile_expert, tile_valid, src_tok, pos = _moe_plan(route, R_LAT)
    ys = _moe_experts(1, tile_expert, tile_valid, src_tok, u2, moe_w1, moe_w3, moe_w2)
    x_lat = _moe_combine(NT_LAT, pos, ys, x_lat, route, m1, vec(ln2_g[1]), vec(ln2_b[1]))
    return x_lat.reshape(BATCH, SEQ, D_MODEL)
```

```python
import functools

import jax
import jax.numpy as jnp
from jax import lax
from jax.experimental import pallas as pl
from jax.experimental.pallas import tpu as pltpu

F32 = jnp.float32
BF16 = jnp.bfloat16

D_MODEL = 1024
BATCH = 4
SEQ = 4096
DEPTH = 2
GRID_W = 64
CTX_LEN = 256
HEAD_DIM = 64
ROPE_THETA = 10000.0
LN_EPS = 1e-5
RMS_EPS = 1e-6
NEG_INF = -1e30

CONV_CH = 512
CONV_WIDTH = 31
WIN_HEADS = 8
WIN_KV_HEADS = 2
WINDOW = 128
GQA_HEADS = 8
GQA_KV_HEADS = 2
MLA_HEADS = 8
MLA_Q_RANK = 256
MLA_KV_RANK = 128
MLA_NOPE = 64
MLA_ROPE = 32
MLA_V = 64
N_GROUPS = 4
EXP_PER_GROUP = 8
N_EXPERTS = N_GROUPS * EXP_PER_GROUP
EXPERT_FF = 512
DN_ALPHA = float((2 * DEPTH) ** 0.25)

EVEN_IN = 2 * CONV_CH + (WIN_HEADS + 2 * WIN_KV_HEADS) * HEAD_DIM
ODD_IN = 1184
ODD_IN_PAD = 1280
MLA_PAD = 128
VAL_PAD = 128
LOG2E = 1.4426950408889634

R_LAT = BATCH * SEQ
R_CTX = BATCH * CTX_LEN
R_ALL = R_LAT + R_CTX
TM = 256
NT_LAT = R_LAT // TM
NT_ALL = R_ALL // TM
TILES_PER_SEQ = SEQ // TM
HALO = 16
CONV_CHUNK = 32
TMM = 256
ROUTE_W = 128
VMEM_LIMIT = 56 * 1024 * 1024

SH1, SC1, G1, SH2, SC2, G2 = range(6)


def _sigmoid(x):
    return 1.0 / (1.0 + jnp.exp(-x))


def _layer_norm(z, g, b):
    mu = jnp.mean(z, axis=-1, keepdims=True)
    zc = z - mu
    var = jnp.mean(zc * zc, axis=-1, keepdims=True)
    return zc * lax.rsqrt(var + LN_EPS) * g + b


def _rope(x, cos, sin, half):
    n = x.shape[-1]
    lane = lax.broadcasted_iota(jnp.int32, x.shape, 1)
    first = (lane % (2 * half)) < half
    partner = jnp.where(first, pltpu.roll(x, n - half, 1), pltpu.roll(x, half, 1))
    return x * cos + partner * sin


def _mod_row(i):
    return jnp.where(i < NT_LAT, i // TILES_PER_SEQ, BATCH)


def _mod_spec(chunk):
    return pl.BlockSpec((None, None, 1, D_MODEL), lambda i: (_mod_row(i), chunk, 0, 0))


def _rope_row_block(i):
    return jnp.where(i < NT_LAT, i % TILES_PER_SEQ, TILES_PER_SEQ)


def _full(shape):
    nd = len(shape)
    return pl.BlockSpec(shape, lambda *_: (0,) * nd)


def _params():
    return pltpu.CompilerParams(vmem_limit_bytes=VMEM_LIMIT)


def _ada_kernel(cv_ref, w_ref, b_ref, o_ref):
    cv = cv_ref[...]
    s = cv * _sigmoid(cv)
    o_ref[...] = jnp.dot(s, w_ref[...], precision=lax.Precision.HIGHEST,
                         preferred_element_type=F32) + b_ref[...]


def _ada_table(cv, ada_w, ada_b):
    bn = 1536
    nb = (6 * D_MODEL) // bn
    return pl.pallas_call(
        _ada_kernel,
        grid=(DEPTH, nb),
        in_specs=[pl.BlockSpec((8, D_MODEL), lambda l, j: (0, 0)),
                  pl.BlockSpec((None, D_MODEL, bn), lambda l, j: (l, 0, j)),
                  pl.BlockSpec((None, 1, bn), lambda l, j: (l, 0, j))],
        out_specs=pl.BlockSpec((None, 8, bn), lambda l, j: (l, 0, j)),
        out_shape=jax.ShapeDtypeStruct((DEPTH, 8, 6 * D_MODEL), F32),
        compiler_params=_params(),
        name="ada_table",
    )(cv, ada_w, ada_b.reshape(DEPTH, 1, 6 * D_MODEL))


def _rope_tables(rot_dim):
    axis_dim = rot_dim // 2
    inv_freq = ROPE_THETA ** (-jnp.arange(0, axis_dim, 2, dtype=F32) / axis_dim)
    t = jnp.arange(SEQ)
    ang_r = (t // GRID_W).astype(F32)[:, None] * inv_freq[None, :]
    ang_c = (t % GRID_W).astype(F32)[:, None] * inv_freq[None, :]
    cos = jnp.concatenate([jnp.cos(ang_r), jnp.cos(ang_r), jnp.cos(ang_c), jnp.cos(ang_c)], axis=-1)
    sin = jnp.concatenate([-jnp.sin(ang_r), jnp.sin(ang_r), -jnp.sin(ang_c), jnp.sin(ang_c)], axis=-1)
    return cos, sin


def _pad_table(cos, sin, lead, period, width):
    rot = cos.shape[1]
    one = jnp.ones((SEQ, period), F32).at[:, lead:lead + rot].set(cos)
    zero = jnp.zeros((SEQ, period), F32).at[:, lead:lead + rot].set(sin)
    cos_w = jnp.tile(one, (1, width // period))
    sin_w = jnp.tile(zero, (1, width // period))
    cos_w = jnp.concatenate([cos_w, jnp.ones((TM, width), F32)], axis=0)
    sin_w = jnp.concatenate([sin_w, jnp.zeros((TM, width), F32)], axis=0)
    return cos_w, sin_w


def _proj0_kernel(x_ref, sh_ref, sc_ref, w_ref, b_ref, cos_ref, sin_ref,
                  h_ref, q_ref, k_ref, v_ref):
    u = x_ref[...] * (1.0 + sc_ref[...]) + sh_ref[...]
    y = jnp.dot(u.astype(BF16), w_ref[...], preferred_element_type=F32) + b_ref[...]
    h_ref[...] = y[:, :CONV_CH] * _sigmoid(y[:, CONV_CH:2 * CONV_CH])
    cos = cos_ref[...]
    sin = sin_ref[...]
    q0 = 2 * CONV_CH
    k0 = q0 + WIN_HEADS * HEAD_DIM
    v0 = k0 + WIN_KV_HEADS * HEAD_DIM
    cos4 = jnp.concatenate([cos] * 4, axis=1)
    sin4 = jnp.concatenate([sin] * 4, axis=1)
    q = _rope(y[:, q0:k0], cos4, sin4, HEAD_DIM // 4) * (HEAD_DIM ** -0.5 * LOG2E)
    q_ref[...] = q.astype(BF16)
    k_ref[...] = _rope(y[:, k0:v0], cos, sin, HEAD_DIM // 4).astype(BF16)
    v_ref[...] = _values_with_ones(y[:, v0:].astype(BF16), WIN_KV_HEADS, HEAD_DIM)


def _proj0(x_all, mods, w_in, b_in, cos_hd, sin_hd):
    kvw = WIN_KV_HEADS * HEAD_DIM
    row = lambda w: pl.BlockSpec((TM, w), lambda i: (i, 0))
    tab = pl.BlockSpec((TM, 128), lambda i: (_rope_row_block(i), 0))
    return pl.pallas_call(
        _proj0_kernel,
        grid=(NT_ALL,),
        in_specs=[row(D_MODEL), _mod_spec(SH1), _mod_spec(SC1),
                  _full((D_MODEL, EVEN_IN)), _full((1, EVEN_IN)), tab, tab],
        out_specs=[row(CONV_CH), row(WIN_HEADS * HEAD_DIM), row(kvw), row(WIN_KV_HEADS * VAL_PAD)],
        out_shape=[jax.ShapeDtypeStruct((R_ALL, CONV_CH), F32),
                   jax.ShapeDtypeStruct((R_ALL, WIN_HEADS * HEAD_DIM), BF16),
                   jax.ShapeDtypeStruct((R_ALL, kvw), BF16),
                   jax.ShapeDtypeStruct((R_ALL, WIN_KV_HEADS * VAL_PAD), BF16)],
        compiler_params=_params(),
        name="proj0",
    )(x_all, mods, mods, w_in, b_in, cos_hd, sin_hd)


def _conv_kernel(prev_ref, cur_ref, next_ref, w_ref, cb_ref, g_ref, b_ref, o_ref, buf):
    i = pl.program_id(0)
    is_ctx = i >= NT_LAT
    first = jnp.logical_or(is_ctx, i % TILES_PER_SEQ == 0)
    last = jnp.logical_or(is_ctx, i % TILES_PER_SEQ == TILES_PER_SEQ - 1)
    buf[0:HALO, :] = jnp.where(first, 0.0, prev_ref[...])
    buf[HALO:HALO + TM, :] = cur_ref[...]
    buf[HALO + TM:, :] = jnp.where(last, 0.0, next_ref[...])
    off = HALO - CONV_WIDTH // 2
    for c in range(TM // CONV_CHUNK):
        r0 = c * CONV_CHUNK
        acc = jnp.zeros((CONV_CHUNK, CONV_CH), F32)
        for k in range(CONV_WIDTH):
            acc = acc + buf[r0 + off + k:r0 + off + k + CONV_CHUNK, :] * w_ref[k:k + 1, :]
        z = _layer_norm(acc + cb_ref[...], g_ref[...], b_ref[...])
        o_ref[r0:r0 + CONV_CHUNK, :] = (z * _sigmoid(z)).astype(BF16)


def _conv(h, conv_w, conv_b, ln_g, ln_b):
    nh = R_ALL // HALO
    per = TM // HALO
    vec = _full((1, CONV_CH))
    return pl.pallas_call(
        _conv_kernel,
        grid=(NT_ALL,),
        in_specs=[pl.BlockSpec((HALO, CONV_CH), lambda i: (jnp.maximum(i * per - 1, 0), 0)),
                  pl.BlockSpec((TM, CONV_CH), lambda i: (i, 0)),
                  pl.BlockSpec((HALO, CONV_CH), lambda i: (jnp.minimum((i + 1) * per, nh - 1), 0)),
                  _full((CONV_WIDTH, CONV_CH)), vec, vec, vec],
        out_specs=pl.BlockSpec((TM, CONV_CH), lambda i: (i, 0)),
        out_shape=jax.ShapeDtypeStruct((R_ALL, CONV_CH), BF16),
        scratch_shapes=[pltpu.VMEM((TM + 2 * HALO, CONV_CH), F32)],
        compiler_params=_params(),
        name="conv_module",
    )(h, h, h, conv_w, conv_b, ln_g, ln_b)


def _nt_dot(a, b):
    return lax.dot_general(a, b, (((1,), (1,)), ((), ())), preferred_element_type=F32)


def _values_with_ones(v, n_kv, dv):
    lane = lax.broadcasted_iota(jnp.int32, (v.shape[0], VAL_PAD - dv), 1)
    tail = jnp.where(lane == 0, 1.0, 0.0).astype(v.dtype)
    pieces = []
    for h in range(n_kv):
        pieces += [v[:, h * dv:(h + 1) * dv], tail]
    return jnp.concatenate(pieces, axis=1)


def _attend(units, dv):
    def scores(unit):
        q, ks, _, masks, _ = unit
        out = []
        for k, msk in zip(ks, masks):
            s = _nt_dot(q, k)
            if msk is not None:
                s = jnp.where(msk, s, NEG_INF)
            out.append(s)
        return out

    results = []
    ss = scores(units[0])
    for idx, unit in enumerate(units):
        nxt = scores(units[idx + 1]) if idx + 1 < len(units) else None
        _, _, vs, _, sink = unit
        m = functools.reduce(jnp.maximum, [jnp.max(s, axis=-1, keepdims=True) for s in ss])
        if sink is not None:
            m = jnp.maximum(m, sink)
        acc = functools.reduce(jnp.add, [jnp.dot(jnp.exp2(s - m).astype(BF16), v, preferred_element_type=F32)
                                         for s, v in zip(ss, vs)])
        l = acc[:, dv:dv + 1]
        if sink is not None:
            l = l + jnp.exp2(sink - m)
        results.append(acc[:, :dv] / l)
        ss = nxt
    return results


def _win_kernel(sink_ref, q_ref, kp_ref, kc_ref, kn_ref, kx_ref, vp_ref, vc_ref, vn_ref, vx_ref, o_ref):
    n = pl.program_id(1)
    k_loc = jnp.concatenate([kp_ref[...], kc_ref[...], kn_ref[...]], axis=0)
    v_loc = jnp.concatenate([vp_ref[...], vc_ref[...], vn_ref[...]], axis=0)
    k_ctx = kx_ref[...]
    v_ctx = vx_ref[...]
    qi = lax.broadcasted_iota(jnp.int32, (WINDOW, 3 * WINDOW), 0)
    kj = lax.broadcasted_iota(jnp.int32, (WINDOW, 3 * WINDOW), 1)
    k_pos = jnp.where(n < SEQ // WINDOW, kj + (n - 1) * WINDOW, SEQ)
    valid = jnp.where(kj >= qi, jnp.where(kj <= qi + 2 * WINDOW, 1, 0), 0)
    valid = jnp.where(k_pos >= 0, jnp.where(k_pos < SEQ, valid, 0), 0) > 0
    group = WIN_HEADS // WIN_KV_HEADS
    units = []
    for h in range(WIN_HEADS):
        kv = h // group
        ksl = slice(kv * HEAD_DIM, (kv + 1) * HEAD_DIM)
        vsl = slice(kv * VAL_PAD, (kv + 1) * VAL_PAD)
        units.append((q_ref[:, h * HEAD_DIM:(h + 1) * HEAD_DIM], [k_ctx[:, ksl], k_loc[:, ksl]],
                      [v_ctx[:, vsl], v_loc[:, vsl]], [None, valid], sink_ref[h] * LOG2E))
    for h, o in enumerate(_attend(units, HEAD_DIM)):
        o_ref[:, h * HEAD_DIM:(h + 1) * HEAD_DIM] = o.astype(BF16)


def _win_attention(sink, q, k, v):
    nblk = SEQ // WINDOW
    cblk = CTX_LEN // WINDOW
    kvw = WIN_KV_HEADS * HEAD_DIM
    ctx0 = R_LAT // CTX_LEN
    lat = lambda n: jnp.minimum(n, nblk - 1)
    prev = lambda b, n: (b * nblk + jnp.maximum(lat(n) - 1, 0), 0)
    cur = lambda b, n: (b * nblk + lat(n), 0)
    nxt = lambda b, n: (b * nblk + jnp.minimum(lat(n) + 1, nblk - 1), 0)
    qrow = lambda b, n: (jnp.where(n < nblk, b * nblk + n, R_LAT // WINDOW + b * cblk + n - nblk), 0)
    ctx = lambda b, n: (ctx0 + b, 0)
    vw = WIN_KV_HEADS * VAL_PAD
    kvb = lambda f, w: pl.BlockSpec((WINDOW, w), f)
    cxb = lambda w: pl.BlockSpec((CTX_LEN, w), ctx)
    return pl.pallas_call(
        _win_kernel,
        grid=(BATCH, nblk + cblk),
        in_specs=[pl.BlockSpec(memory_space=pltpu.SMEM),
                  pl.BlockSpec((WINDOW, WIN_HEADS * HEAD_DIM), qrow),
                  kvb(prev, kvw), kvb(cur, kvw), kvb(nxt, kvw), cxb(kvw),
                  kvb(prev, vw), kvb(cur, vw), kvb(nxt, vw), cxb(vw)],
        out_specs=pl.BlockSpec((WINDOW, WIN_HEADS * HEAD_DIM), qrow),
        out_shape=jax.ShapeDtypeStruct((R_ALL, WIN_HEADS * HEAD_DIM), BF16),
        compiler_params=_params(),
        name="window_attention",
    )(sink, q, k, k, k, k, v, v, v, v)


def _outproj_kernel(a_ref, b_ref, wa_ref, wb_ref, bo_ref, x_ref, g1_ref, sh2_ref, sc2_ref,
                    lng_ref, lnb_ref, wrh_ref, wrl_ref, br_ref, xo_ref, u2_ref, route_ref):
    y = (jnp.dot(a_ref[...], wa_ref[...], preferred_element_type=F32)
         + jnp.dot(b_ref[...], wb_ref[...], preferred_element_type=F32) + bo_ref[...])
    xn = _layer_norm(DN_ALPHA * x_ref[...] + (1.0 + g1_ref[...]) * y, lng_ref[...], lnb_ref[...])
    xo_ref[...] = xn
    u2 = xn * (1.0 + sc2_ref[...]) + sh2_ref[...]
    u2_ref[...] = u2
    u_hi = u2.astype(BF16)
    u_lo = (u2 - u_hi.astype(F32)).astype(BF16)
    logits = (jnp.dot(u_hi, wrh_ref[...], preferred_element_type=F32)
              + jnp.dot(u_lo, wrh_ref[...], preferred_element_type=F32)
              + jnp.dot(u_hi, wrl_ref[...], preferred_element_type=F32) + br_ref[...])
    lane = lax.broadcasted_iota(jnp.int32, logits.shape, 1).astype(F32)
    ninf = -jnp.inf
    big = float(ROUTE_W)
    gl = jnp.where(lane < N_GROUPS, logits, ninf)
    gmax = jnp.max(gl, axis=-1, keepdims=True)
    gidx = jnp.min(jnp.where(gl == gmax, lane, big), axis=-1, keepdims=True)
    g_w = 1.0 / jnp.sum(jnp.exp(gl - gmax), axis=-1, keepdims=True)
    lo = N_GROUPS + EXP_PER_GROUP * gidx
    el = jnp.where(lane >= lo, jnp.where(lane < lo + EXP_PER_GROUP, logits, ninf), ninf)
    v1 = jnp.max(el, axis=-1, keepdims=True)
    i1 = jnp.min(jnp.where(el == v1, lane, big), axis=-1, keepdims=True)
    el2 = jnp.where(lane == i1, ninf, el)
    v2 = jnp.max(el2, axis=-1, keepdims=True)
    i2 = jnp.min(jnp.where(el2 == v2, lane, big), axis=-1, keepdims=True)
    e2 = jnp.exp(v2 - v1)
    w1 = g_w / (1.0 + e2)
    w2 = g_w * e2 / (1.0 + e2)
    rec = jnp.where(lane == 0.0, i1 - N_GROUPS,
                    jnp.where(lane == 1.0, i2 - N_GROUPS,
                              jnp.where(lane == 2.0, w1, jnp.where(lane == 3.0, w2, 0.0))))
    route_ref[...] = rec


def _outproj(n_tiles, mix_a, mix_b, w_a, w_b, b_out, x_all, mods, ln_g, ln_b, w_r, b_r):
    rows = n_tiles * TM
    half = mix_a.shape[1]
    w_rh = w_r.astype(BF16)
    w_rl = (w_r - w_rh.astype(F32)).astype(BF16)
    row = lambda w: pl.BlockSpec((TM, w), lambda i: (i, 0))
    vec = _full((1, D_MODEL))
    return pl.pallas_call(
        _outproj_kernel,
        grid=(n_tiles,),
        in_specs=[row(half), row(half), _full((half, D_MODEL)), _full((half, D_MODEL)), vec,
                  row(D_MODEL), _mod_spec(G1), _mod_spec(SH2), _mod_spec(SC2), vec, vec,
                  _full((D_MODEL, ROUTE_W)), _full((D_MODEL, ROUTE_W)), _full((1, ROUTE_W))],
        out_specs=[row(D_MODEL), row(D_MODEL), row(ROUTE_W)],
        out_shape=[jax.ShapeDtypeStruct((rows, D_MODEL), F32),
                   jax.ShapeDtypeStruct((rows, D_MODEL), F32),
                   jax.ShapeDtypeStruct((rows, ROUTE_W), F32)],
        compiler_params=_params(),
        name="outproj_ln_router",
    )(mix_a, mix_b, w_a, w_b, b_out, x_all, mods, mods, mods, ln_g, ln_b, w_rh, w_rl, b_r)


def _moe_kernel(te_ref, tv_ref, st_ref, u_hbm, w1_ref, w3_ref, w2_ref, ys_ref,
                xbuf, wb1, wb3, wb2, sem):
    i = pl.program_id(0)
    nt = pl.num_programs(0)
    slot = i % 2

    def row_copy(tok, slot_, r):
        return pltpu.make_async_copy(u_hbm.at[pl.ds(tok, 1)], xbuf.at[slot_, pl.ds(r, 1)], sem.at[slot_])

    def issue(tile, slot_):
        def body(r, carry):
            row_copy(st_ref[tile * TMM + r], slot_, r).start()
            return carry
        lax.fori_loop(0, TMM, body, 0, unroll=8)

    @pl.when(jnp.logical_and(i == 0, tv_ref[0] == 1))
    def _():
        issue(0, 0)

    nxt = jnp.minimum(i + 1, nt - 1)

    @pl.when(jnp.logical_and(i + 1 < nt, tv_ref[nxt] == 1))
    def _():
        issue(i + 1, 1 - slot)

    @pl.when(jnp.logical_or(i == 0, te_ref[i] != te_ref[jnp.maximum(i - 1, 0)]))
    def _():
        wb1[...] = w1_ref[...].astype(BF16)
        wb3[...] = w3_ref[...].astype(BF16)
        wb2[...] = w2_ref[...].astype(BF16)

    @pl.when(tv_ref[i] == 1)
    def _():
        pltpu.make_async_copy(u_hbm.at[pl.ds(0, TMM)], xbuf.at[slot], sem.at[slot]).wait()
        x = xbuf[slot].astype(BF16)
        h1 = jnp.dot(x, wb1[...], preferred_element_type=F32)
        h3 = jnp.dot(x, wb3[...], preferred_element_type=F32)
        hid = h1 * _sigmoid(h1) * h3
        ys_ref[...] = jnp.dot(hid.astype(BF16), wb2[...], preferred_element_type=F32)

    @pl.when(tv_ref[i] == 0)
    def _():
        ys_ref[...] = jnp.zeros_like(ys_ref)


def _moe_experts(layer, tile_expert, tile_valid, src_tok, u2, w1, w3, w2):
    nt = tile_expert.shape[0]
    wmap = lambda i, te, tv, st: (layer, te[i], 0, 0)
    grid_spec = pltpu.PrefetchScalarGridSpec(
        num_scalar_prefetch=3,
        grid=(nt,),
        in_specs=[pl.BlockSpec(memory_space=pl.ANY),
                  pl.BlockSpec((None, None, D_MODEL, EXPERT_FF), wmap),
                  pl.BlockSpec((None, None, D_MODEL, EXPERT_FF), wmap),
                  pl.BlockSpec((None, None, EXPERT_FF, D_MODEL), wmap)],
        out_specs=pl.BlockSpec((TMM, D_MODEL), lambda i, te, tv, st: (i, 0)),
        scratch_shapes=[pltpu.VMEM((2, TMM, D_MODEL), F32),
                        pltpu.VMEM((D_MODEL, EXPERT_FF), BF16),
                        pltpu.VMEM((D_MODEL, EXPERT_FF), BF16),
                        pltpu.VMEM((EXPERT_FF, D_MODEL), BF16),
                        pltpu.SemaphoreType.DMA((2,))])
    return pl.pallas_call(
        _moe_kernel,
        grid_spec=grid_spec,
        out_shape=jax.ShapeDtypeStruct((nt * TMM, D_MODEL), F32),
        compiler_params=_params(),
        name="moe_experts",
    )(tile_expert, tile_valid, src_tok, u2, w1, w3, w2)


def _combine_kernel(pos_ref, ys_hbm, x_ref, route_ref, g2_ref, lng_ref, lnb_ref, o_ref, fbuf, sem,
                    *, n_tok):
    i = pl.program_id(0)
    nt = pl.num_programs(0)
    slot = i % 2

    def row_copy(p, slot_, s, r):
        return pltpu.make_async_copy(ys_hbm.at[pl.ds(p, 1)], fbuf.at[slot_, s, pl.ds(r, 1)], sem.at[slot_])

    def issue(tile, slot_):
        for s in range(2):
            def body(r, carry):
                row_copy(pos_ref[s * n_tok + tile * TM + r], slot_, s, r).start()
                return carry
            lax.fori_loop(0, TM, body, 0, unroll=8)

    @pl.when(i == 0)
    def _():
        issue(0, 0)

    @pl.when(i + 1 < nt)
    def _():
        issue(i + 1, 1 - slot)

    for s in range(2):
        pltpu.make_async_copy(ys_hbm.at[pl.ds(0, TM)], fbuf.at[slot, s], sem.at[slot]).wait()

    route = route_ref[...]
    f = route[:, 2:3] * fbuf[slot, 0] + route[:, 3:4] * fbuf[slot, 1]
    z = DN_ALPHA * x_ref[...] + (1.0 + g2_ref[...]) * f
    o_ref[...] = _layer_norm(z, lng_ref[...], lnb_ref[...])


def _moe_combine(n_tiles, pos, ys, x_all, route, mods, ln_g, ln_b):
    rows = n_tiles * TM
    row = lambda w: pl.BlockSpec((TM, w), lambda i, p: (i, 0))
    vec = pl.BlockSpec((1, D_MODEL), lambda i, p: (0, 0))
    grid_spec = pltpu.PrefetchScalarGridSpec(
        num_scalar_prefetch=1,
        grid=(n_tiles,),
        in_specs=[pl.BlockSpec(memory_space=pl.ANY), row(D_MODEL), row(ROUTE_W),
                  pl.BlockSpec((None, None, 1, D_MODEL), lambda i, p: (_mod_row(i), G2, 0, 0)),
                  vec, vec],
        out_specs=row(D_MODEL),
        scratch_shapes=[pltpu.VMEM((2, 2, TM, D_MODEL), F32), pltpu.SemaphoreType.DMA((2,))])
    return pl.pallas_call(
        functools.partial(_combine_kernel, n_tok=rows),
        grid_spec=grid_spec,
        out_shape=jax.ShapeDtypeStruct((rows, D_MODEL), F32),
        compiler_params=_params(),
        name="moe_combine_ln",
    )(pos, ys, x_all, route, mods, ln_g, ln_b)


def _moe_plan(route, n_tok):
    n_asg = 2 * n_tok
    nt_max = n_asg // TMM + N_EXPERTS
    e_flat = route[:n_tok, :2].astype(jnp.int32).reshape(-1)
    onehot = (e_flat[:, None] == jnp.arange(N_EXPERTS, dtype=jnp.int32)[None, :]).astype(jnp.int32)
    csum = jnp.cumsum(onehot, axis=0)
    rank = jnp.sum(onehot * csum, axis=1) - 1
    counts = csum[-1]
    tiles_e = (counts + TMM - 1) // TMM
    tile_end = jnp.cumsum(tiles_e)
    tile_start = tile_end - tiles_e
    dest = jnp.sum(onehot * tile_start[None, :], axis=1) * TMM + rank
    tile_id = jnp.arange(nt_max, dtype=jnp.int32)
    tile_expert = jnp.minimum(jnp.sum((tile_id[:, None] >= tile_end[None, :]).astype(jnp.int32), axis=1),
                              N_EXPERTS - 1)
    tile_valid = (tile_id < tile_end[-1]).astype(jnp.int32)
    src_tok = jnp.zeros((nt_max * TMM,), jnp.int32).at[dest].set(jnp.arange(n_asg, dtype=jnp.int32) // 2)
    pos = dest.reshape(n_tok, 2).T.reshape(-1)
    return tile_expert, tile_valid, src_tok, pos


def _router_weights(w_rg, b_rg, w_re, b_re):
    w = jnp.concatenate([w_rg, jnp.transpose(w_re, (1, 0, 2)).reshape(D_MODEL, N_EXPERTS)], axis=1)
    b = jnp.concatenate([b_rg, b_re.reshape(-1)])
    pad = ROUTE_W - w.shape[1]
    return jnp.pad(w, ((0, 0), (0, pad))), jnp.pad(b, (0, pad)).reshape(1, ROUTE_W)


def _proj1_kernel(x_ref, sh_ref, sc_ref, w_ref, b_ref, cos_ref, sin_ref, cosm_ref, sinm_ref, cosr_ref,
                  sinr_ref, gq_ref, gk_ref, gqc_ref, gkv_ref, avg_ref, wuq_ref, wuk_ref, wuv_ref,
                  q_ref, qm_ref, k_ref, v_ref, km_ref, vm_ref):
    u = x_ref[...] * (1.0 + sc_ref[...]) + sh_ref[...]
    y = jnp.dot(u.astype(BF16), w_ref[...], preferred_element_type=F32) + b_ref[...]
    c_q = GQA_HEADS * HEAD_DIM
    c_qc = c_q + MLA_Q_RANK
    c_k = c_qc + GQA_KV_HEADS * HEAD_DIM
    c_v = c_k + GQA_KV_HEADS * HEAD_DIM
    c_kv = c_v + MLA_KV_RANK
    avg = avg_ref[...]

    def head_rms(t, gain):
        sq = t * t
        hi = sq.astype(BF16)
        lo = (sq - hi.astype(F32)).astype(BF16)
        a = avg[:t.shape[1], :t.shape[1]]
        ms = jnp.dot(hi, a, preferred_element_type=F32) + jnp.dot(lo, a, preferred_element_type=F32)
        return t * lax.rsqrt(ms + RMS_EPS) * gain

    def row_rms(t, gain):
        ms = jnp.mean(t * t, axis=-1, keepdims=True)
        return t * lax.rsqrt(ms + RMS_EPS) * gain

    cos = cos_ref[...]
    sin = sin_ref[...]
    cos4 = jnp.concatenate([cos] * 4, axis=1)
    sin4 = jnp.concatenate([sin] * 4, axis=1)
    q = _rope(head_rms(y[:, :c_q], gq_ref[...]), cos4, sin4, HEAD_DIM // 4) * (HEAD_DIM ** -0.5 * LOG2E)
    q_ref[...] = q.astype(BF16)
    k = _rope(head_rms(y[:, c_qc:c_k], gk_ref[...]), cos, sin, HEAD_DIM // 4)
    k_ref[...] = k.astype(BF16)
    v_ref[...] = _values_with_ones(y[:, c_k:c_v].astype(BF16), GQA_KV_HEADS, HEAD_DIM)

    qc = row_rms(y[:, c_q:c_qc], gqc_ref[...]).astype(BF16)
    qm = jnp.dot(qc, wuq_ref[...], preferred_element_type=F32)
    cosm = jnp.concatenate([cosm_ref[...]] * MLA_HEADS, axis=1)
    sinm = jnp.concatenate([sinm_ref[...]] * MLA_HEADS, axis=1)
    qm = _rope(qm, cosm, sinm, MLA_ROPE // 4) * ((MLA_NOPE + MLA_ROPE) ** -0.5 * LOG2E)
    qm_ref[...] = qm.astype(BF16)

    kvn = row_rms(y[:, c_v:c_kv], gkv_ref[...]).astype(BF16)
    kr = _rope(y[:, c_kv:], cosr_ref[...], sinr_ref[...], MLA_ROPE // 4).astype(BF16)
    km = jnp.dot(jnp.concatenate([kvn, kr], axis=1), wuk_ref[...], preferred_element_type=F32)
    km_ref[...] = km.astype(BF16)
    vm = jnp.dot(kvn, wuv_ref[...], preferred_element_type=F32)
    lane = lax.broadcasted_iota(jnp.int32, vm.shape, 1)
    vm_ref[...] = jnp.where(lane % VAL_PAD == MLA_V, 1.0, vm).astype(BF16)


def _proj1(x_all, mods, w_in, b_in, tabs, gq, gk, gqc, gkv, avg, wuq, wuk, wuv):
    cos_hd, sin_hd, cos_m, sin_m, cos_r, sin_r = tabs
    kvw = GQA_KV_HEADS * HEAD_DIM
    qw = GQA_HEADS * HEAD_DIM
    mw = MLA_HEADS * MLA_PAD
    vw = MLA_HEADS * VAL_PAD
    gvw = GQA_KV_HEADS * VAL_PAD
    row = lambda w: pl.BlockSpec((TM, w), lambda i: (i, 0))
    tab = pl.BlockSpec((TM, 128), lambda i: (_rope_row_block(i), 0))
    return pl.pallas_call(
        _proj1_kernel,
        grid=(NT_ALL,),
        in_specs=[row(D_MODEL), _mod_spec(SH1), _mod_spec(SC1),
                  _full((D_MODEL, ODD_IN_PAD)), _full((1, ODD_IN_PAD)), tab, tab, tab, tab, tab, tab,
                  _full((1, qw)), _full((1, kvw)), _full((1, MLA_Q_RANK)), _full((1, MLA_KV_RANK)),
                  _full((qw, qw)), _full((MLA_Q_RANK, mw)), _full((MLA_KV_RANK + 128, mw)),
                  _full((MLA_KV_RANK, vw))],
        out_specs=[row(qw), row(mw), row(kvw), row(gvw), row(mw), row(vw)],
        out_shape=[jax.ShapeDtypeStruct((R_ALL, qw), BF16),
                   jax.ShapeDtypeStruct((R_ALL, mw), BF16),
                   jax.ShapeDtypeStruct((R_ALL, kvw), BF16),
                   jax.ShapeDtypeStruct((R_ALL, gvw), BF16),
                   jax.ShapeDtypeStruct((R_ALL, mw), BF16),
                   jax.ShapeDtypeStruct((R_ALL, vw), BF16)],
        compiler_params=_params(),
        name="proj1",
    )(x_all, mods, mods, w_in, b_in, cos_hd, sin_hd, cos_m, sin_m, cos_r, sin_r,
      gq, gk, gqc, gkv, avg, wuq, wuk, wuv)


def _dense_kernel(q_ref, kl_ref, kc_ref, vl_ref, vc_ref, o_ref, *, n_heads, group, dk, dv):
    tq = q_ref.shape[0]
    units = []
    for kv in range(n_heads // group):
        qs = [q_ref[:, h * dk:(h + 1) * dk] for h in range(kv * group, (kv + 1) * group)]
        q = qs[0] if group == 1 else jnp.concatenate(qs, axis=0)
        ks = slice(kv * dk, (kv + 1) * dk)
        vs = slice(kv * VAL_PAD, (kv + 1) * VAL_PAD)
        units.append((q, [kl_ref[:, ks], kc_ref[:, ks]], [vl_ref[:, vs], vc_ref[:, vs]], [None, None], None))
    for kv, o in enumerate(_attend(units, dv)):
        for g in range(group):
            h = kv * group + g
            o_ref[:, h * dv:(h + 1) * dv] = o[g * tq:(g + 1) * tq].astype(BF16)


def _dense_attention(q, k, v, *, n_heads, group, dk, dv, tq, name):
    n_kv = n_heads // group
    nq = SEQ // tq
    ctx0 = R_LAT // CTX_LEN
    lat = lambda w: pl.BlockSpec((SEQ, w), lambda b, j: (b, 0), pipeline_mode=pl.Buffered(1))
    ctx = lambda w: pl.BlockSpec((CTX_LEN, w), lambda b, j: (ctx0 + b, 0))
    return pl.pallas_call(
        functools.partial(_dense_kernel, n_heads=n_heads, group=group, dk=dk, dv=dv),
        grid=(BATCH, nq),
        in_specs=[pl.BlockSpec((tq, n_heads * dk), lambda b, j: (b * nq + j, 0)),
                  lat(n_kv * dk), ctx(n_kv * dk), lat(n_kv * VAL_PAD), ctx(n_kv * VAL_PAD)],
        out_specs=pl.BlockSpec((tq, n_heads * dv), lambda b, j: (b * nq + j, 0)),
        out_shape=jax.ShapeDtypeStruct((R_LAT, n_heads * dv), BF16),
        compiler_params=_params(),
        name=name,
    )(q, k, k, v, v)


def _mla_weights(w_uq, w_ukv):
    wq = w_uq.reshape(MLA_Q_RANK, MLA_HEADS, MLA_NOPE + MLA_ROPE)
    wq = jnp.pad(wq, ((0, 0), (0, 0), (0, MLA_PAD - MLA_NOPE - MLA_ROPE))).reshape(MLA_Q_RANK, -1)
    wkv = w_ukv.reshape(MLA_KV_RANK, MLA_HEADS, MLA_NOPE + MLA_V)
    wk = jnp.pad(wkv[:, :, :MLA_NOPE], ((0, 0), (0, 0), (0, MLA_PAD - MLA_NOPE))).reshape(MLA_KV_RANK, -1)
    wv = jnp.pad(wkv[:, :, MLA_NOPE:], ((0, 0), (0, 0), (0, VAL_PAD - MLA_V))).reshape(MLA_KV_RANK, -1)
    r = jnp.arange(128)[:, None]
    c = jnp.arange(MLA_HEADS * MLA_PAD)[None, :]
    place = jnp.logical_and(r < MLA_ROPE, (c % MLA_PAD) == MLA_NOPE + r).astype(F32)
    wk = jnp.concatenate([wk, place], axis=0)
    return wq.astype(BF16), wk.astype(BF16), wv.astype(BF16)


def kernel(x, c, ctx, c_ctx, even_w_in, even_b_in, even_conv_w, even_conv_b, even_conv_ln_g, even_conv_ln_b, even_sink, even_w_out, even_b_out, odd_w_in, odd_b_in, odd_q_norm, odd_k_norm, odd_mla_q_norm, odd_mla_kv_norm, odd_mla_w_uq, odd_mla_w_ukv, odd_w_out, odd_b_out, ada_w, ada_b, ln1_g, ln1_b, ln2_g, ln2_b, moe_w_rg, moe_b_rg, moe_w_re, moe_b_re, moe_w1, moe_w3, moe_w2):
    vec = lambda a: a.reshape(1, -1)
    x_all = jnp.concatenate([x.reshape(R_LAT, D_MODEL), ctx.reshape(R_CTX, D_MODEL)], axis=0)

    cv = jnp.concatenate([c, c_ctx[None, :], jnp.zeros((8 - BATCH - 1, D_MODEL), F32)], axis=0)
    mods = _ada_table(cv, ada_w, ada_b).reshape(DEPTH, 8, 6, 1, D_MODEL)

    cos64, sin64 = _rope_tables(HEAD_DIM)
    cos_hd, sin_hd = _pad_table(cos64, sin64, 0, HEAD_DIM, 128)
    cos32, sin32 = _rope_tables(MLA_ROPE)
    cos_m, sin_m = _pad_table(cos32, sin32, MLA_NOPE, MLA_PAD, 128)
    cos_r, sin_r = _pad_table(cos32, sin32, 0, 128, 128)

    m0 = mods[0]
    h, q0, k0, v0 = _proj0(x_all, m0, even_w_in[0].astype(BF16), vec(even_b_in[0]), cos_hd, sin_hd)
    conv_out = _conv(h, even_conv_w[0].reshape(CONV_WIDTH, CONV_CH), vec(even_conv_b[0]),
                     vec(even_conv_ln_g[0]), vec(even_conv_ln_b[0]))
    attn = _win_attention(even_sink[0], q0, k0, v0)
    w_out = even_w_out[0].astype(BF16)
    w_r, b_r = _router_weights(moe_w_rg[0], moe_b_rg[0], moe_w_re[0], moe_b_re[0])
    x_all, u2, route = _outproj(NT_ALL, conv_out, attn, w_out[:CONV_CH], w_out[CONV_CH:], vec(even_b_out[0]),
                                x_all, m0, vec(ln1_g[0]), vec(ln1_b[0]), w_r, b_r)
    tile_expert, tile_valid, src_tok, pos = _moe_plan(route, R_ALL)
    ys = _moe_experts(0, tile_expert, tile_valid, src_tok, u2, moe_w1, moe_w3, moe_w2)
    x_all = _moe_combine(NT_ALL, pos, ys, x_all, route, m0, vec(ln2_g[0]), vec(ln2_b[0]))

    m1 = mods[1]
    w_in1 = jnp.pad(odd_w_in[0], ((0, 0), (0, ODD_IN_PAD - ODD_IN))).astype(BF16)
    b_in1 = jnp.pad(odd_b_in[0], (0, ODD_IN_PAD - ODD_IN)).reshape(1, -1)
    wuq, wuk, wuv = _mla_weights(odd_mla_w_uq[0], odd_mla_w_ukv[0])
    qw = GQA_HEADS * HEAD_DIM
    hid = jnp.arange(qw) // HEAD_DIM
    avg = ((hid[:, None] == hid[None, :]).astype(F32) / HEAD_DIM).astype(BF16)
    q1, qm, k1, v1, km, vm = _proj1(
        x_all, m1, w_in1, b_in1, (cos_hd, sin_hd, cos_m, sin_m, cos_r, sin_r),
        vec(jnp.tile(odd_q_norm[0], GQA_HEADS)), vec(jnp.tile(odd_k_norm[0], GQA_KV_HEADS)),
        vec(odd_mla_q_norm[0]), vec(odd_mla_kv_norm[0]), avg, wuq, wuk, wuv)
    o_g = _dense_attention(q1, k1, v1, n_heads=GQA_HEADS, group=GQA_HEADS // GQA_KV_HEADS,
                           dk=HEAD_DIM, dv=HEAD_DIM, tq=128, name="gqa_attention")
    o_m = _dense_attention(qm, km, vm, n_heads=MLA_HEADS, group=1, dk=MLA_PAD, dv=MLA_V, tq=256,
                           name="mla_attention")
    w_out = odd_w_out[0].astype(BF16)
    w_r, b_r = _router_weights(moe_w_rg[1], moe_b_rg[1], moe_w_re[1], moe_b_re[1])
    x_lat, u2, route = _outproj(NT_LAT, o_g, o_m, w_out[:qw], w_out[qw:], vec(odd_b_out[0]),
                                x_all, m1, vec(ln1_g[1]), vec(ln1_b[1]), w_r, b_r)
    tile_expert, tile_valid, src_tok, pos = _moe_plan(route, R_LAT)
    ys = _moe_experts(1, tile_expert, tile_valid, src_tok, u2, moe_w1, moe_w3, moe_w2)
    x_lat = _moe_combine(NT_LAT, pos, ys, x_lat, route, m1, vec(ln2_g[1]), vec(ln2_b[1]))
    return x_lat.reshape(BATCH, SEQ, D_MODEL)
```

```python
import functools

import jax
import jax.numpy as jnp
from jax import lax
from jax.experimental import pallas as pl
from jax.experimental.pallas import tpu as pltpu

F32 = jnp.float32
BF16 = jnp.bfloat16

D_MODEL = 1024
BATCH = 4
SEQ = 4096
DEPTH = 2
GRID_W = 64
CTX_LEN = 256
HEAD_DIM = 64
ROPE_THETA = 10000.0
LN_EPS = 1e-5
RMS_EPS = 1e-6
NEG_INF = -1e30

CONV_CH = 512
CONV_WIDTH = 31
WIN_HEADS = 8
WIN_KV_HEADS = 2
WINDOW = 128
GQA_HEADS = 8
GQA_KV_HEADS = 2
MLA_HEADS = 8
MLA_Q_RANK = 256
MLA_KV_RANK = 128
MLA_NOPE = 64
MLA_ROPE = 32
MLA_V = 64
N_GROUPS = 4
EXP_PER_GROUP = 8
N_EXPERTS = N_GROUPS * EXP_PER_GROUP
EXPERT_FF = 512
DN_ALPHA = float((2 * DEPTH) ** 0.25)

EVEN_IN = 2 * CONV_CH + (WIN_HEADS + 2 * WIN_KV_HEADS) * HEAD_DIM
ODD_IN = 1184
ODD_IN_PAD = 1280
MLA_PAD = 128
VAL_PAD = 128
LOG2E = 1.4426950408889634

R_LAT = BATCH * SEQ
R_CTX = BATCH * CTX_LEN
R_ALL = R_LAT + R_CTX
TM = 256
NT_LAT = R_LAT // TM
NT_ALL = R_ALL // TM
TILES_PER_SEQ = SEQ // TM
HALO = 16
CONV_CHUNK = 32
TMM = 256
ROUTE_W = 128
SUBLANES = 8
LANES = 128
ROW_TILE = (SUBLANES, D_MODEL // SUBLANES)
VMEM_LIMIT = 56 * 1024 * 1024

SH1, SC1, G1, SH2, SC2, G2 = range(6)


def _sigmoid(x):
    return 1.0 / (1.0 + jnp.exp(-x))


def _layer_norm(z, g, b):
    mu = jnp.mean(z, axis=-1, keepdims=True)
    zc = z - mu
    var = jnp.mean(zc * zc, axis=-1, keepdims=True)
    return zc * lax.rsqrt(var + LN_EPS) * g + b


def _rope(x, cos, sin, half):
    n = x.shape[-1]
    lane = lax.broadcasted_iota(jnp.int32, x.shape, 1)
    first = (lane % (2 * half)) < half
    partner = jnp.where(first, pltpu.roll(x, n - half, 1), pltpu.roll(x, half, 1))
    return x * cos + partner * sin


def _mod_row(i):
    return jnp.where(i < NT_LAT, i // TILES_PER_SEQ, BATCH)


def _mod_spec(chunk):
    return pl.BlockSpec((None, None, 1, D_MODEL), lambda i: (_mod_row(i), chunk, 0, 0))


def _rope_row_block(i):
    return jnp.where(i < NT_LAT, i % TILES_PER_SEQ, TILES_PER_SEQ)


def _full(shape):
    nd = len(shape)
    return pl.BlockSpec(shape, lambda *_: (0,) * nd)


def _params():
    return pltpu.CompilerParams(vmem_limit_bytes=VMEM_LIMIT)


def _ada_kernel(cv_ref, w_ref, b_ref, o_ref):
    cv = cv_ref[...]
    s = cv * _sigmoid(cv)
    o_ref[...] = jnp.dot(s, w_ref[...], precision=lax.Precision.HIGHEST,
                         preferred_element_type=F32) + b_ref[...]


def _ada_table(cv, ada_w, ada_b):
    bn = 1536
    nb = (6 * D_MODEL) // bn
    return pl.pallas_call(
        _ada_kernel,
        grid=(DEPTH, nb),
        in_specs=[pl.BlockSpec((8, D_MODEL), lambda l, j: (0, 0)),
                  pl.BlockSpec((None, D_MODEL, bn), lambda l, j: (l, 0, j)),
                  pl.BlockSpec((None, 1, bn), lambda l, j: (l, 0, j))],
        out_specs=pl.BlockSpec((None, 8, bn), lambda l, j: (l, 0, j)),
        out_shape=jax.ShapeDtypeStruct((DEPTH, 8, 6 * D_MODEL), F32),
        compiler_params=_params(),
        name="ada_table",
    )(cv, ada_w, ada_b.reshape(DEPTH, 1, 6 * D_MODEL))


def _rope_tables(rot_dim):
    axis_dim = rot_dim // 2
    inv_freq = ROPE_THETA ** (-jnp.arange(0, axis_dim, 2, dtype=F32) / axis_dim)
    t = jnp.arange(SEQ)
    ang_r = (t // GRID_W).astype(F32)[:, None] * inv_freq[None, :]
    ang_c = (t % GRID_W).astype(F32)[:, None] * inv_freq[None, :]
    cos = jnp.concatenate([jnp.cos(ang_r), jnp.cos(ang_r), jnp.cos(ang_c), jnp.cos(ang_c)], axis=-1)
    sin = jnp.concatenate([-jnp.sin(ang_r), jnp.sin(ang_r), -jnp.sin(ang_c), jnp.sin(ang_c)], axis=-1)
    return cos, sin


def _pad_table(cos, sin, lead, period, width):
    rot = cos.shape[1]
    one = jnp.ones((SEQ, period), F32).at[:, lead:lead + rot].set(cos)
    zero = jnp.zeros((SEQ, period), F32).at[:, lead:lead + rot].set(sin)
    cos_w = jnp.tile(one, (1, width // period))
    sin_w = jnp.tile(zero, (1, width // period))
    cos_w = jnp.concatenate([cos_w, jnp.ones((TM, width), F32)], axis=0)
    sin_w = jnp.concatenate([sin_w, jnp.zeros((TM, width), F32)], axis=0)
    return cos_w, sin_w


def _proj0_kernel(x_ref, sh_ref, sc_ref, w_ref, b_ref, cos_ref, sin_ref,
                  h_ref, q_ref, k_ref, v_ref):
    u = x_ref[...] * (1.0 + sc_ref[...]) + sh_ref[...]
    y = jnp.dot(u.astype(BF16), w_ref[...], preferred_element_type=F32) + b_ref[...]
    h_ref[...] = y[:, :CONV_CH] * _sigmoid(y[:, CONV_CH:2 * CONV_CH])
    cos = cos_ref[...]
    sin = sin_ref[...]
    q0 = 2 * CONV_CH
    k0 = q0 + WIN_HEADS * HEAD_DIM
    v0 = k0 + WIN_KV_HEADS * HEAD_DIM
    cos4 = jnp.concatenate([cos] * 4, axis=1)
    sin4 = jnp.concatenate([sin] * 4, axis=1)
    q = _rope(y[:, q0:k0], cos4, sin4, HEAD_DIM // 4) * (HEAD_DIM ** -0.5 * LOG2E)
    q_ref[...] = q.astype(BF16)
    k_ref[...] = _rope(y[:, k0:v0], cos, sin, HEAD_DIM // 4).astype(BF16)
    v_ref[...] = _values_with_ones(y[:, v0:].astype(BF16), WIN_KV_HEADS, HEAD_DIM)


def _proj0(x_all, mods, w_in, b_in, cos_hd, sin_hd):
    kvw = WIN_KV_HEADS * HEAD_DIM
    row = lambda w: pl.BlockSpec((TM, w), lambda i: (i, 0))
    tab = pl.BlockSpec((TM, 128), lambda i: (_rope_row_block(i), 0))
    return pl.pallas_call(
        _proj0_kernel,
        grid=(NT_ALL,),
        in_specs=[row(D_MODEL), _mod_spec(SH1), _mod_spec(SC1),
                  _full((D_MODEL, EVEN_IN)), _full((1, EVEN_IN)), tab, tab],
        out_specs=[row(CONV_CH), row(WIN_HEADS * HEAD_DIM), row(kvw), row(WIN_KV_HEADS * VAL_PAD)],
        out_shape=[jax.ShapeDtypeStruct((R_ALL, CONV_CH), F32),
                   jax.ShapeDtypeStruct((R_ALL, WIN_HEADS * HEAD_DIM), BF16),
                   jax.ShapeDtypeStruct((R_ALL, kvw), BF16),
                   jax.ShapeDtypeStruct((R_ALL, WIN_KV_HEADS * VAL_PAD), BF16)],
        compiler_params=_params(),
        name="proj0",
    )(x_all, mods, mods, w_in, b_in, cos_hd, sin_hd)


def _conv_kernel(prev_ref, cur_ref, next_ref, w_ref, cb_ref, g_ref, b_ref, o_ref, buf):
    i = pl.program_id(0)
    is_ctx = i >= NT_LAT
    first = jnp.logical_or(is_ctx, i % TILES_PER_SEQ == 0)
    last = jnp.logical_or(is_ctx, i % TILES_PER_SEQ == TILES_PER_SEQ - 1)
    buf[0:HALO, :] = jnp.where(first, 0.0, prev_ref[...])
    buf[HALO:HALO + TM, :] = cur_ref[...]
    buf[HALO + TM:, :] = jnp.where(last, 0.0, next_ref[...])
    off = HALO - CONV_WIDTH // 2
    for c in range(TM // CONV_CHUNK):
        r0 = c * CONV_CHUNK
        acc = jnp.zeros((CONV_CHUNK, CONV_CH), F32)
        for k in range(CONV_WIDTH):
            acc = acc + buf[r0 + off + k:r0 + off + k + CONV_CHUNK, :] * w_ref[k:k + 1, :]
        z = _layer_norm(acc + cb_ref[...], g_ref[...], b_ref[...])
        o_ref[r0:r0 + CONV_CHUNK, :] = (z * _sigmoid(z)).astype(BF16)


def _conv(h, conv_w, conv_b, ln_g, ln_b):
    nh = R_ALL // HALO
    per = TM // HALO
    vec = _full((1, CONV_CH))
    return pl.pallas_call(
        _conv_kernel,
        grid=(NT_ALL,),
        in_specs=[pl.BlockSpec((HALO, CONV_CH), lambda i: (jnp.maximum(i * per - 1, 0), 0)),
                  pl.BlockSpec((TM, CONV_CH), lambda i: (i, 0)),
                  pl.BlockSpec((HALO, CONV_CH), lambda i: (jnp.minimum((i + 1) * per, nh - 1), 0)),
                  _full((CONV_WIDTH, CONV_CH)), vec, vec, vec],
        out_specs=pl.BlockSpec((TM, CONV_CH), lambda i: (i, 0)),
        out_shape=jax.ShapeDtypeStruct((R_ALL, CONV_CH), BF16),
        scratch_shapes=[pltpu.VMEM((TM + 2 * HALO, CONV_CH), F32)],
        compiler_params=_params(),
        name="conv_module",
    )(h, h, h, conv_w, conv_b, ln_g, ln_b)


def _nt_dot(a, b):
    return lax.dot_general(a, b, (((1,), (1,)), ((), ())), preferred_element_type=F32)


def _values_with_ones(v, n_kv, dv):
    lane = lax.broadcasted_iota(jnp.int32, (v.shape[0], VAL_PAD - dv), 1)
    tail = jnp.where(lane == 0, 1.0, 0.0).astype(v.dtype)
    pieces = []
    for h in range(n_kv):
        pieces += [v[:, h * dv:(h + 1) * dv], tail]
    return jnp.concatenate(pieces, axis=1)


def _attend(units, dv):
    def scores(unit):
        q, ks, _, masks, _ = unit
        out = []
        for k, msk in zip(ks, masks):
            s = _nt_dot(q, k)
            if msk is not None:
                s = jnp.where(msk, s, NEG_INF)
            out.append(s)
        return out

    results = []
    ss = scores(units[0])
    for idx, unit in enumerate(units):
        nxt = scores(units[idx + 1]) if idx + 1 < len(units) else None
        _, _, vs, _, sink = unit
        m = functools.reduce(jnp.maximum, [jnp.max(s, axis=-1, keepdims=True) for s in ss])
        if sink is not None:
            m = jnp.maximum(m, sink)
        acc = functools.reduce(jnp.add, [jnp.dot(jnp.exp2(s - m).astype(BF16), v, preferred_element_type=F32)
                                         for s, v in zip(ss, vs)])
        l = acc[:, dv:dv + 1]
        if sink is not None:
            l = l + jnp.exp2(sink - m)
        results.append(acc[:, :dv] / l)
        ss = nxt
    return results


def _win_kernel(sink_ref, q_ref, kp_ref, kc_ref, kn_ref, kx_ref, vp_ref, vc_ref, vn_ref, vx_ref, o_ref):
    n = pl.program_id(1)
    k_loc = jnp.concatenate([kp_ref[...], kc_ref[...], kn_ref[...]], axis=0)
    v_loc = jnp.concatenate([vp_ref[...], vc_ref[...], vn_ref[...]], axis=0)
    k_ctx = kx_ref[...]
    v_ctx = vx_ref[...]
    qi = lax.broadcasted_iota(jnp.int32, (WINDOW, 3 * WINDOW), 0)
    kj = lax.broadcasted_iota(jnp.int32, (WINDOW, 3 * WINDOW), 1)
    k_pos = jnp.where(n < SEQ // WINDOW, kj + (n - 1) * WINDOW, SEQ)
    valid = jnp.where(kj >= qi, jnp.where(kj <= qi + 2 * WINDOW, 1, 0), 0)
    valid = jnp.where(k_pos >= 0, jnp.where(k_pos < SEQ, valid, 0), 0) > 0
    group = WIN_HEADS // WIN_KV_HEADS
    units = []
    for h in range(WIN_HEADS):
        kv = h // group
        ksl = slice(kv * HEAD_DIM, (kv + 1) * HEAD_DIM)
        vsl = slice(kv * VAL_PAD, (kv + 1) * VAL_PAD)
        units.append((q_ref[:, h * HEAD_DIM:(h + 1) * HEAD_DIM], [k_ctx[:, ksl], k_loc[:, ksl]],
                      [v_ctx[:, vsl], v_loc[:, vsl]], [None, valid], sink_ref[h] * LOG2E))
    for h, o in enumerate(_attend(units, HEAD_DIM)):
        o_ref[:, h * HEAD_DIM:(h + 1) * HEAD_DIM] = o.astype(BF16)


def _win_attention(sink, q, k, v):
    nblk = SEQ // WINDOW
    cblk = CTX_LEN // WINDOW
    kvw = WIN_KV_HEADS * HEAD_DIM
    ctx0 = R_LAT // CTX_LEN
    lat = lambda n: jnp.minimum(n, nblk - 1)
    prev = lambda b, n: (b * nblk + jnp.maximum(lat(n) - 1, 0), 0)
    cur = lambda b, n: (b * nblk + lat(n), 0)
    nxt = lambda b, n: (b * nblk + jnp.minimum(lat(n) + 1, nblk - 1), 0)
    qrow = lambda b, n: (jnp.where(n < nblk, b * nblk + n, R_LAT // WINDOW + b * cblk + n - nblk), 0)
    ctx = lambda b, n: (ctx0 + b, 0)
    vw = WIN_KV_HEADS * VAL_PAD
    kvb = lambda f, w: pl.BlockSpec((WINDOW, w), f)
    cxb = lambda w: pl.BlockSpec((CTX_LEN, w), ctx)
    return pl.pallas_call(
        _win_kernel,
        grid=(BATCH, nblk + cblk),
        in_specs=[pl.BlockSpec(memory_space=pltpu.SMEM),
                  pl.BlockSpec((WINDOW, WIN_HEADS * HEAD_DIM), qrow),
                  kvb(prev, kvw), kvb(cur, kvw), kvb(nxt, kvw), cxb(kvw),
                  kvb(prev, vw), kvb(cur, vw), kvb(nxt, vw), cxb(vw)],
        out_specs=pl.BlockSpec((WINDOW, WIN_HEADS * HEAD_DIM), qrow),
        out_shape=jax.ShapeDtypeStruct((R_ALL, WIN_HEADS * HEAD_DIM), BF16),
        compiler_params=_params(),
        name="window_attention",
    )(sink, q, k, k, k, k, v, v, v, v)


def _outproj_kernel(a_ref, b_ref, wa_ref, wb_ref, bo_ref, x_ref, g1_ref, sh2_ref, sc2_ref,
                    lng_ref, lnb_ref, wrh_ref, wrl_ref, br_ref, upper_ref, lower_ref,
                    xo_ref, route_ref, cnt_ref, xs_ref):
    y = (jnp.dot(a_ref[...], wa_ref[...], preferred_element_type=F32)
         + jnp.dot(b_ref[...], wb_ref[...], preferred_element_type=F32) + bo_ref[...])
    xn = _layer_norm(DN_ALPHA * x_ref[...] + (1.0 + g1_ref[...]) * y, lng_ref[...], lnb_ref[...])
    xo_ref[...] = xn
    u2 = xn * (1.0 + sc2_ref[...]) + sh2_ref[...]
    u_hi = u2.astype(BF16)
    u_lo = (u2 - u_hi.astype(F32)).astype(BF16)
    logits = (jnp.dot(u_hi, wrh_ref[...], preferred_element_type=F32)
              + jnp.dot(u_lo, wrh_ref[...], preferred_element_type=F32)
              + jnp.dot(u_hi, wrl_ref[...], preferred_element_type=F32) + br_ref[...])
    lane = lax.broadcasted_iota(jnp.int32, logits.shape, 1).astype(F32)
    ninf = -jnp.inf
    big = float(ROUTE_W)
    gl = jnp.where(lane < N_GROUPS, logits, ninf)
    gmax = jnp.max(gl, axis=-1, keepdims=True)
    gidx = jnp.min(jnp.where(gl == gmax, lane, big), axis=-1, keepdims=True)
    g_w = 1.0 / jnp.sum(jnp.exp(gl - gmax), axis=-1, keepdims=True)
    lo = N_GROUPS + EXP_PER_GROUP * gidx
    el = jnp.where(lane >= lo, jnp.where(lane < lo + EXP_PER_GROUP, logits, ninf), ninf)
    v1 = jnp.max(el, axis=-1, keepdims=True)
    i1 = jnp.min(jnp.where(el == v1, lane, big), axis=-1, keepdims=True)
    el2 = jnp.where(lane == i1, ninf, el)
    v2 = jnp.max(el2, axis=-1, keepdims=True)
    i2 = jnp.min(jnp.where(el2 == v2, lane, big), axis=-1, keepdims=True)
    e2 = jnp.exp(v2 - v1)
    w1 = g_w / (1.0 + e2)
    w2 = g_w * e2 / (1.0 + e2)
    onehot = [jnp.where(lane == i1 - N_GROUPS, 1.0, 0.0), jnp.where(lane == i2 - N_GROUPS, 1.0, 0.0)]
    cnt = [jnp.sum(o, axis=0, keepdims=True) for o in onehot]
    cnt8 = [jnp.broadcast_to(c, (8, ROUTE_W)).astype(BF16) for c in cnt]
    upper = upper_ref[...]
    below = (jnp.dot(cnt8[0], upper, preferred_element_type=F32)
             + jnp.dot(cnt8[1], upper, preferred_element_type=F32))[0:1]
    lower = lower_ref[...]
    base = [below, below + cnt[0]]
    lp = []
    for s in range(2):
        earlier = jnp.dot(lower, onehot[s].astype(BF16), preferred_element_type=F32)
        lp.append(jnp.sum(onehot[s] * (base[s] + earlier), axis=-1, keepdims=True))
    rec = jnp.where(lane == 0.0, i1 - N_GROUPS,
                    jnp.where(lane == 1.0, i2 - N_GROUPS,
                              jnp.where(lane == 2.0, w1,
                                        jnp.where(lane == 3.0, w2,
                                                  jnp.where(lane == 4.0, lp[0],
                                                            jnp.where(lane == 5.0, lp[1], 0.0))))))
    route_ref[...] = rec
    cnt_ref[...] = jnp.broadcast_to(cnt[0] + cnt[1], (8, ROUTE_W))
    pos = lax.broadcasted_iota(jnp.int32, (TM, 2 * TM), 1).astype(F32)
    xs = functools.reduce(jnp.add, [
        lax.dot_general(jnp.where(pos == lp[s], 1.0, 0.0).astype(BF16), u_hi,
                        (((0,), (0,)), ((), ())), preferred_element_type=F32) for s in range(2)])
    _store_row_tiles(xs_ref, xs)


def _outproj(n_tiles, mix_a, mix_b, w_a, w_b, b_out, x_all, mods, ln_g, ln_b, w_r, b_r):
    rows = n_tiles * TM
    half = mix_a.shape[1]
    w_rh = w_r.astype(BF16)
    w_rl = (w_r - w_rh.astype(F32)).astype(BF16)
    upper = (jnp.arange(ROUTE_W)[:, None] < jnp.arange(ROUTE_W)[None, :]).astype(BF16)
    lower = (jnp.arange(TM)[:, None] > jnp.arange(TM)[None, :]).astype(BF16)
    row = lambda w: pl.BlockSpec((TM, w), lambda i: (i, 0))
    vec = _full((1, D_MODEL))
    return pl.pallas_call(
        _outproj_kernel,
        grid=(n_tiles,),
        in_specs=[row(half), row(half), _full((half, D_MODEL)), _full((half, D_MODEL)), vec,
                  row(D_MODEL), _mod_spec(G1), _mod_spec(SH2), _mod_spec(SC2), vec, vec,
                  _full((D_MODEL, ROUTE_W)), _full((D_MODEL, ROUTE_W)), _full((1, ROUTE_W)),
                  _full((ROUTE_W, ROUTE_W)), _full((TM, TM))],
        out_specs=[row(D_MODEL), row(ROUTE_W), pl.BlockSpec((8, ROUTE_W), lambda i: (i, 0)),
                   pl.BlockSpec((2 * TM,) + ROW_TILE, lambda i: (i, 0, 0))],
        out_shape=[jax.ShapeDtypeStruct((rows, D_MODEL), F32),
                   jax.ShapeDtypeStruct((rows, ROUTE_W), F32),
                   jax.ShapeDtypeStruct((n_tiles * 8, ROUTE_W), F32),
                   jax.ShapeDtypeStruct((2 * rows,) + ROW_TILE, F32)],
        compiler_params=_params(),
        name="outproj_ln_router",
    )(mix_a, mix_b, w_a, w_b, b_out, x_all, mods, mods, mods, ln_g, ln_b, w_rh, w_rl, b_r, upper, lower)


def _store_row_tiles(ref, y):
    for j in range(SUBLANES):
        ref[:, j, :] = y[:, j * LANES:(j + 1) * LANES]


def _load_row_tiles(ref):
    return jnp.concatenate([ref[:, j, :] for j in range(SUBLANES)], axis=1)


def _shuffle_kernel(src_ref, dst_ref, len_ref, zdst_ref, zlen_ref, x_hbm, o_hbm, zbuf, sem, zsem,
                    *, n_seg, n_zero):
    def seg_copy(s):
        n = len_ref[s]
        return pltpu.make_async_copy(x_hbm.at[pl.ds(src_ref[s], n)], o_hbm.at[pl.ds(dst_ref[s], n)], sem)

    def zero_copy(s):
        n = zlen_ref[s]
        return pltpu.make_async_copy(zbuf.at[pl.ds(0, n)], o_hbm.at[pl.ds(zdst_ref[s], n)], zsem)

    def run(copy, lens, count, do):
        def body(s, carry):
            @pl.when(lens[s] > 0)
            def _():
                do(copy(s))
            return carry
        lax.fori_loop(0, count, body, 0)

    zbuf[...] = jnp.zeros_like(zbuf)
    run(seg_copy, len_ref, n_seg, lambda c: c.start())
    run(zero_copy, zlen_ref, n_zero, lambda c: c.start())
    run(seg_copy, len_ref, n_seg, lambda c: c.wait())
    run(zero_copy, zlen_ref, n_zero, lambda c: c.wait())


def _shuffle(src, dst, length, zdst, zlen, x, out_rows, name):
    grid_spec = pltpu.PrefetchScalarGridSpec(
        num_scalar_prefetch=5,
        grid=(1,),
        in_specs=[pl.BlockSpec(memory_space=pl.ANY)],
        out_specs=pl.BlockSpec(memory_space=pl.ANY),
        scratch_shapes=[pltpu.VMEM((TMM,) + x.shape[1:], x.dtype),
                        pltpu.SemaphoreType.DMA(()), pltpu.SemaphoreType.DMA(())])
    return pl.pallas_call(
        functools.partial(_shuffle_kernel, n_seg=src.shape[0], n_zero=zdst.shape[0]),
        grid_spec=grid_spec,
        out_shape=jax.ShapeDtypeStruct((out_rows,) + x.shape[1:], x.dtype),
        compiler_params=_params(),
        name=name,
    )(src, dst, length, zdst, zlen, x)


def _moe_kernel(te_ref, tv_ref, xs_ref, w1_ref, w3_ref, w2_ref, ys_ref, wb1, wb3, wb2):
    i = pl.program_id(0)

    @pl.when(jnp.logical_or(i == 0, te_ref[i] != te_ref[jnp.maximum(i - 1, 0)]))
    def _():
        wb1[...] = w1_ref[...].astype(BF16)
        wb3[...] = w3_ref[...].astype(BF16)
        wb2[...] = w2_ref[...].astype(BF16)

    @pl.when(tv_ref[i] == 1)
    def _():
        x = _load_row_tiles(xs_ref).astype(BF16)
        h1 = jnp.dot(x, wb1[...], preferred_element_type=F32)
        h3 = jnp.dot(x, wb3[...], preferred_element_type=F32)
        hid = h1 * _sigmoid(h1) * h3
        _store_row_tiles(ys_ref, jnp.dot(hid.astype(BF16), wb2[...], preferred_element_type=F32))

    @pl.when(tv_ref[i] == 0)
    def _():
        ys_ref[...] = jnp.zeros_like(ys_ref)


def _moe_experts(layer, tile_expert, tile_valid, xs, w1, w3, w2):
    nt = tile_expert.shape[0]
    wmap = lambda i, te, tv: (layer, te[i], 0, 0)
    xmap = lambda i, te, tv: (jnp.where(tv[i] == 1, i, 0), 0, 0)
    grid_spec = pltpu.PrefetchScalarGridSpec(
        num_scalar_prefetch=2,
        grid=(nt,),
        in_specs=[pl.BlockSpec((TMM,) + ROW_TILE, xmap),
                  pl.BlockSpec((None, None, D_MODEL, EXPERT_FF), wmap),
                  pl.BlockSpec((None, None, D_MODEL, EXPERT_FF), wmap),
                  pl.BlockSpec((None, None, EXPERT_FF, D_MODEL), wmap)],
        out_specs=pl.BlockSpec((TMM,) + ROW_TILE, lambda i, te, tv: (i, 0, 0)),
        scratch_shapes=[pltpu.VMEM((D_MODEL, EXPERT_FF), BF16),
                        pltpu.VMEM((D_MODEL, EXPERT_FF), BF16),
                        pltpu.VMEM((EXPERT_FF, D_MODEL), BF16)])
    return pl.pallas_call(
        _moe_kernel,
        grid_spec=grid_spec,
        out_shape=jax.ShapeDtypeStruct((nt * TMM,) + ROW_TILE, F32),
        compiler_params=_params(),
        name="moe_experts",
    )(tile_expert, tile_valid, xs, w1, w3, w2)


def _combine_kernel(ys_ref, x_ref, route_ref, g2_ref, lng_ref, lnb_ref, o_ref):
    route = route_ref[...]
    pos = lax.broadcasted_iota(jnp.int32, (TM, 2 * TM), 1).astype(F32)
    sel = (jnp.where(pos == route[:, 4:5], route[:, 2:3], 0.0)
           + jnp.where(pos == route[:, 5:6], route[:, 3:4], 0.0))
    sel_hi = sel.astype(BF16)
    sel_lo = (sel - sel_hi.astype(F32)).astype(BF16)
    y = _load_row_tiles(ys_ref)
    y_hi = y.astype(BF16)
    y_lo = (y - y_hi.astype(F32)).astype(BF16)
    f = (jnp.dot(sel_hi, y_hi, preferred_element_type=F32) + jnp.dot(sel_lo, y_hi, preferred_element_type=F32)
         + jnp.dot(sel_hi, y_lo, preferred_element_type=F32))
    z = DN_ALPHA * x_ref[...] + (1.0 + g2_ref[...]) * f
    o_ref[...] = _layer_norm(z, lng_ref[...], lnb_ref[...])


def _moe_combine(n_tiles, ys_local, x_all, route, mods, ln_g, ln_b):
    rows = n_tiles * TM
    row = lambda w: pl.BlockSpec((TM, w), lambda i: (i, 0))
    vec = _full((1, D_MODEL))
    return pl.pallas_call(
        _combine_kernel,
        grid=(n_tiles,),
        in_specs=[pl.BlockSpec((2 * TM,) + ROW_TILE, lambda i: (i, 0, 0)), row(D_MODEL), row(ROUTE_W),
                  _mod_spec(G2), vec, vec],
        out_specs=row(D_MODEL),
        out_shape=jax.ShapeDtypeStruct((rows, D_MODEL), F32),
        compiler_params=_params(),
        name="moe_combine_ln",
    )(ys_local, x_all, route, mods, ln_g, ln_b)


def _moe_plan(cnt_rec, n_tiles):
    cnt = cnt_rec.reshape(n_tiles, 8, ROUTE_W)[:, 0, :N_EXPERTS].astype(jnp.int32)
    nt_max = (2 * n_tiles * TM) // TMM + N_EXPERTS
    total = jnp.sum(cnt, axis=0)
    tiles_e = (total + TMM - 1) // TMM
    tile_end = jnp.cumsum(tiles_e)
    group_start = (tile_end - tiles_e) * TMM
    before_tile = jnp.cumsum(cnt, axis=0) - cnt
    before_expert = jnp.cumsum(cnt, axis=1) - cnt
    seg_local = (jnp.arange(n_tiles, dtype=jnp.int32)[:, None] * (2 * TM) + before_expert).reshape(-1)
    seg_group = (group_start[None, :] + before_tile).reshape(-1)
    seg_len = cnt.reshape(-1)
    spare = tile_end[-1] + jnp.arange(N_EXPERTS, dtype=jnp.int32)
    pad_start = jnp.concatenate([group_start + total, spare * TMM])
    pad_len = jnp.concatenate([tiles_e * TMM - total, jnp.where(spare < nt_max, TMM, 0)])
    tile_id = jnp.arange(nt_max, dtype=jnp.int32)
    tile_expert = jnp.minimum(jnp.sum((tile_id[:, None] >= tile_end[None, :]).astype(jnp.int32), axis=1),
                              N_EXPERTS - 1)
    tile_valid = (tile_id < tile_end[-1]).astype(jnp.int32)
    return tile_expert, tile_valid, seg_local, seg_group, seg_len, pad_start, pad_len


def _moe(layer, n_tiles, xs_local, cnt_rec, w1, w3, w2):
    tile_expert, tile_valid, seg_local, seg_group, seg_len, pad_start, pad_len = _moe_plan(cnt_rec, n_tiles)
    nt_max = tile_expert.shape[0]
    none = jnp.zeros((1,), jnp.int32)
    xs = _shuffle(seg_local, seg_group, seg_len, pad_start, pad_len, xs_local, nt_max * TMM, "moe_dispatch")
    ys = _moe_experts(layer, tile_expert, tile_valid, xs, w1, w3, w2)
    return _shuffle(seg_group, seg_local, seg_len, none, none, ys, 2 * n_tiles * TM, "moe_return")


def _router_weights(w_rg, b_rg, w_re, b_re):
    w = jnp.concatenate([w_rg, jnp.transpose(w_re, (1, 0, 2)).reshape(D_MODEL, N_EXPERTS)], axis=1)
    b = jnp.concatenate([b_rg, b_re.reshape(-1)])
    pad = ROUTE_W - w.shape[1]
    return jnp.pad(w, ((0, 0), (0, pad))), jnp.pad(b, (0, pad)).reshape(1, ROUTE_W)


def _proj1_kernel(x_ref, sh_ref, sc_ref, w_ref, b_ref, cos_ref, sin_ref, cosm_ref, sinm_ref, cosr_ref,
                  sinr_ref, gq_ref, gk_ref, gqc_ref, gkv_ref, avg_ref, wuq_ref, wuk_ref, wuv_ref,
                  q_ref, qm_ref, k_ref, v_ref, km_ref, vm_ref):
    u = x_ref[...] * (1.0 + sc_ref[...]) + sh_ref[...]
    y = jnp.dot(u.astype(BF16), w_ref[...], preferred_element_type=F32) + b_ref[...]
    c_q = GQA_HEADS * HEAD_DIM
    c_qc = c_q + MLA_Q_RANK
    c_k = c_qc + GQA_KV_HEADS * HEAD_DIM
    c_v = c_k + GQA_KV_HEADS * HEAD_DIM
    c_kv = c_v + MLA_KV_RANK
    avg = avg_ref[...]

    def head_rms(t, gain):
        sq = t * t
        hi = sq.astype(BF16)
        lo = (sq - hi.astype(F32)).astype(BF16)
        a = avg[:t.shape[1], :t.shape[1]]
        ms = jnp.dot(hi, a, preferred_element_type=F32) + jnp.dot(lo, a, preferred_element_type=F32)
        return t * lax.rsqrt(ms + RMS_EPS) * gain

    def row_rms(t, gain):
        ms = jnp.mean(t * t, axis=-1, keepdims=True)
        return t * lax.rsqrt(ms + RMS_EPS) * gain

    cos = cos_ref[...]
    sin = sin_ref[...]
    cos4 = jnp.concatenate([cos] * 4, axis=1)
    sin4 = jnp.concatenate([sin] * 4, axis=1)
    q = _rope(head_rms(y[:, :c_q], gq_ref[...]), cos4, sin4, HEAD_DIM // 4) * (HEAD_DIM ** -0.5 * LOG2E)
    q_ref[...] = q.astype(BF16)
    k = _rope(head_rms(y[:, c_qc:c_k], gk_ref[...]), cos, sin, HEAD_DIM // 4)
    k_ref[...] = k.astype(BF16)
    v_ref[...] = _values_with_ones(y[:, c_k:c_v].astype(BF16), GQA_KV_HEADS, HEAD_DIM)

    qc = row_rms(y[:, c_q:c_qc], gqc_ref[...]).astype(BF16)
    qm = jnp.dot(qc, wuq_ref[...], preferred_element_type=F32)
    cosm = jnp.concatenate([cosm_ref[...]] * MLA_HEADS, axis=1)
    sinm = jnp.concatenate([sinm_ref[...]] * MLA_HEADS, axis=1)
    qm = _rope(qm, cosm, sinm, MLA_ROPE // 4) * ((MLA_NOPE + MLA_ROPE) ** -0.5 * LOG2E)
    qm_ref[...] = qm.astype(BF16)

    kvn = row_rms(y[:, c_v:c_kv], gkv_ref[...]).astype(BF16)
    kr = _rope(y[:, c_kv:], cosr_ref[...], sinr_ref[...], MLA_ROPE // 4).astype(BF16)
    km = jnp.dot(jnp.concatenate([kvn, kr], axis=1), wuk_ref[...], preferred_element_type=F32)
    km_ref[...] = km.astype(BF16)
    vm = jnp.dot(kvn, wuv_ref[...], preferred_element_type=F32)
    lane = lax.broadcasted_iota(jnp.int32, vm.shape, 1)
    vm_ref[...] = jnp.where(lane % VAL_PAD == MLA_V, 1.0, vm).astype(BF16)


def _proj1(x_all, mods, w_in, b_in, tabs, gq, gk, gqc, gkv, avg, wuq, wuk, wuv):
    cos_hd, sin_hd, cos_m, sin_m, cos_r, sin_r = tabs
    kvw = GQA_KV_HEADS * HEAD_DIM
    qw = GQA_HEADS * HEAD_DIM
    mw = MLA_HEADS * MLA_PAD
    vw = MLA_HEADS * VAL_PAD
    gvw = GQA_KV_HEADS * VAL_PAD
    row = lambda w: pl.BlockSpec((TM, w), lambda i: (i, 0))
    tab = pl.BlockSpec((TM, 128), lambda i: (_rope_row_block(i), 0))
    return pl.pallas_call(
        _proj1_kernel,
        grid=(NT_ALL,),
        in_specs=[row(D_MODEL), _mod_spec(SH1), _mod_spec(SC1),
                  _full((D_MODEL, ODD_IN_PAD)), _full((1, ODD_IN_PAD)), tab, tab, tab, tab, tab, tab,
                  _full((1, qw)), _full((1, kvw)), _full((1, MLA_Q_RANK)), _full((1, MLA_KV_RANK)),
                  _full((qw, qw)), _full((MLA_Q_RANK, mw)), _full((MLA_KV_RANK + 128, mw)),
                  _full((MLA_KV_RANK, vw))],
        out_specs=[row(qw), row(mw), row(kvw), row(gvw), row(mw), row(vw)],
        out_shape=[jax.ShapeDtypeStruct((R_ALL, qw), BF16),
                   jax.ShapeDtypeStruct((R_ALL, mw), BF16),
                   jax.ShapeDtypeStruct((R_ALL, kvw), BF16),
                   jax.ShapeDtypeStruct((R_ALL, gvw), BF16),
                   jax.ShapeDtypeStruct((R_ALL, mw), BF16),
                   jax.ShapeDtypeStruct((R_ALL, vw), BF16)],
        compiler_params=_params(),
        name="proj1",
    )(x_all, mods, mods, w_in, b_in, cos_hd, sin_hd, cos_m, sin_m, cos_r, sin_r,
      gq, gk, gqc, gkv, avg, wuq, wuk, wuv)


def _dense_kernel(q_ref, kl_ref, kc_ref, vl_ref, vc_ref, o_ref, *, n_heads, group, dk, dv):
    tq = q_ref.shape[0]
    units = []
    for kv in range(n_heads // group):
        qs = [q_ref[:, h * dk:(h + 1) * dk] for h in range(kv * group, (kv + 1) * group)]
        q = qs[0] if group == 1 else jnp.concatenate(qs, axis=0)
        ks = slice(kv * dk, (kv + 1) * dk)
        vs = slice(kv * VAL_PAD, (kv + 1) * VAL_PAD)
        units.append((q, [kl_ref[:, ks], kc_ref[:, ks]], [vl_ref[:, vs], vc_ref[:, vs]], [None, None], None))
    for kv, o in enumerate(_attend(units, dv)):
        for g in range(group):
            h = kv * group + g
            o_ref[:, h * dv:(h + 1) * dv] = o[g * tq:(g + 1) * tq].astype(BF16)


def _dense_attention(q, k, v, *, n_heads, group, dk, dv, tq, name):
    n_kv = n_heads // group
    nq = SEQ // tq
    ctx0 = R_LAT // CTX_LEN
    lat = lambda w: pl.BlockSpec((SEQ, w), lambda b, j: (b, 0), pipeline_mode=pl.Buffered(1))
    ctx = lambda w: pl.BlockSpec((CTX_LEN, w), lambda b, j: (ctx0 + b, 0))
    return pl.pallas_call(
        functools.partial(_dense_kernel, n_heads=n_heads, group=group, dk=dk, dv=dv),
        grid=(BATCH, nq),
        in_specs=[pl.BlockSpec((tq, n_heads * dk), lambda b, j: (b * nq + j, 0)),
                  lat(n_kv * dk), ctx(n_kv * dk), lat(n_kv * VAL_PAD), ctx(n_kv * VAL_PAD)],
        out_specs=pl.BlockSpec((tq, n_heads * dv), lambda b, j: (b * nq + j, 0)),
        out_shape=jax.ShapeDtypeStruct((R_LAT, n_heads * dv), BF16),
        compiler_params=_params(),
        name=name,
    )(q, k, k, v, v)


def _mla_weights(w_uq, w_ukv):
    wq = w_uq.reshape(MLA_Q_RANK, MLA_HEADS, MLA_NOPE + MLA_ROPE)
    wq = jnp.pad(wq, ((0, 0), (0, 0), (0, MLA_PAD - MLA_NOPE - MLA_ROPE))).reshape(MLA_Q_RANK, -1)
    wkv = w_ukv.reshape(MLA_KV_RANK, MLA_HEADS, MLA_NOPE + MLA_V)
    wk = jnp.pad(wkv[:, :, :MLA_NOPE], ((0, 0), (0, 0), (0, MLA_PAD - MLA_NOPE))).reshape(MLA_KV_RANK, -1)
    wv = jnp.pad(wkv[:, :, MLA_NOPE:], ((0, 0), (0, 0), (0, VAL_PAD - MLA_V))).reshape(MLA_KV_RANK, -1)
    r = jnp.arange(128)[:, None]
    c = jnp.arange(MLA_HEADS * MLA_PAD)[None, :]
    place = jnp.logical_and(r < MLA_ROPE, (c % MLA_PAD) == MLA_NOPE + r).astype(F32)
    wk = jnp.concatenate([wk, place], axis=0)
    return wq.astype(BF16), wk.astype(BF16), wv.astype(BF16)


def kernel(x, c, ctx, c_ctx, even_w_in, even_b_in, even_conv_w, even_conv_b, even_conv_ln_g, even_conv_ln_b, even_sink, even_w_out, even_b_out, odd_w_in, odd_b_in, odd_q_norm, odd_k_norm, odd_mla_q_norm, odd_mla_kv_norm, odd_mla_w_uq, odd_mla_w_ukv, odd_w_out, odd_b_out, ada_w, ada_b, ln1_g, ln1_b, ln2_g, ln2_b, moe_w_rg, moe_b_rg, moe_w_re, moe_b_re, moe_w1, moe_w3, moe_w2):
    vec = lambda a: a.reshape(1, -1)
    x_all = jnp.concatenate([x.reshape(R_LAT, D_MODEL), ctx.reshape(R_CTX, D_MODEL)], axis=0)

    cv = jnp.concatenate([c, c_ctx[None, :], jnp.zeros((8 - BATCH - 1, D_MODEL), F32)], axis=0)
    mods = _ada_table(cv, ada_w, ada_b).reshape(DEPTH, 8, 6, 1, D_MODEL)

    cos64, sin64 = _rope_tables(HEAD_DIM)
    cos_hd, sin_hd = _pad_table(cos64, sin64, 0, HEAD_DIM, 128)
    cos32, sin32 = _rope_tables(MLA_ROPE)
    cos_m, sin_m = _pad_table(cos32, sin32, MLA_NOPE, MLA_PAD, 128)
    cos_r, sin_r = _pad_table(cos32, sin32, 0, 128, 128)

    m0 = mods[0]
    h, q0, k0, v0 = _proj0(x_all, m0, even_w_in[0].astype(BF16), vec(even_b_in[0]), cos_hd, sin_hd)
    conv_out = _conv(h, even_conv_w[0].reshape(CONV_WIDTH, CONV_CH), vec(even_conv_b[0]),
                     vec(even_conv_ln_g[0]), vec(even_conv_ln_b[0]))
    attn = _win_attention(even_sink[0], q0, k0, v0)
    w_out = even_w_out[0].astype(BF16)
    w_r, b_r = _router_weights(moe_w_rg[0], moe_b_rg[0], moe_w_re[0], moe_b_re[0])
    x_all, route, cnt_rec, xs_local = _outproj(
        NT_ALL, conv_out, attn, w_out[:CONV_CH], w_out[CONV_CH:], vec(even_b_out[0]),
        x_all, m0, vec(ln1_g[0]), vec(ln1_b[0]), w_r, b_r)
    ys_local = _moe(0, NT_ALL, xs_local, cnt_rec, moe_w1, moe_w3, moe_w2)
    x_all = _moe_combine(NT_ALL, ys_local, x_all, route, m0, vec(ln2_g[0]), vec(ln2_b[0]))

    m1 = mods[1]
    w_in1 = jnp.pad(odd_w_in[0], ((0, 0), (0, ODD_IN_PAD - ODD_IN))).astype(BF16)
    b_in1 = jnp.pad(odd_b_in[0], (0, ODD_IN_PAD - ODD_IN)).reshape(1, -1)
    wuq, wuk, wuv = _mla_weights(odd_mla_w_uq[0], odd_mla_w_ukv[0])
    qw = GQA_HEADS * HEAD_DIM
    hid = jnp.arange(qw) // HEAD_DIM
    avg = ((hid[:, None] == hid[None, :]).astype(F32) / HEAD_DIM).astype(BF16)
    q1, qm, k1, v1, km, vm = _proj1(
        x_all, m1, w_in1, b_in1, (cos_hd, sin_hd, cos_m, sin_m, cos_r, sin_r),
        vec(jnp.tile(odd_q_norm[0], GQA_HEADS)), vec(jnp.tile(odd_k_norm[0], GQA_KV_HEADS)),
        vec(odd_mla_q_norm[0]), vec(odd_mla_kv_norm[0]), avg, wuq, wuk, wuv)
    o_g = _dense_attention(q1, k1, v1, n_heads=GQA_HEADS, group=GQA_HEADS // GQA_KV_HEADS,
                           dk=HEAD_DIM, dv=HEAD_DIM, tq=128, name="gqa_attention")
    o_m = _dense_attention(qm, km, vm, n_heads=MLA_HEADS, group=1, dk=MLA_PAD, dv=MLA_V, tq=256,
                           name="mla_attention")
    w_out = odd_w_out[0].astype(BF16)
    w_r, b_r = _router_weights(moe_w_rg[1], moe_b_rg[1], moe_w_re[1], moe_b_re[1])
    x_lat, route, cnt_rec, xs_local = _outproj(
        NT_LAT, o_g, o_m, w_out[:qw], w_out[qw:], vec(odd_b_out[0]),
        x_all, m1, vec(ln1_g[1]), vec(ln1_b[1]), w_r, b_r)
    ys_local = _moe(1, NT_LAT, xs_local, cnt_rec, moe_w1, moe_w3, moe_w2)
    x_lat = _moe_combine(NT_LAT, ys_local, x_lat, route, m1, vec(ln2_g[1]), vec(ln2_b[1]))
    return x_lat.reshape(BATCH, SEQ, D_MODEL)
```

```python
import functools

import jax
import jax.numpy as jnp
from jax import lax
from jax.experimental import pallas as pl
from jax.experimental.pallas import tpu as pltpu

F32 = jnp.float32
BF16 = jnp.bfloat16

D_MODEL = 1024
BATCH = 4
SEQ = 4096
DEPTH = 2
GRID_W = 64
CTX_LEN = 256
HEAD_DIM = 64
ROPE_THETA = 10000.0
LN_EPS = 1e-5
RMS_EPS = 1e-6
NEG_INF = -1e30

CONV_CH = 512
CONV_WIDTH = 31
WIN_HEADS = 8
WIN_KV_HEADS = 2
WINDOW = 128
GQA_HEADS = 8
GQA_KV_HEADS = 2
MLA_HEADS = 8
MLA_Q_RANK = 256
MLA_KV_RANK = 128
MLA_NOPE = 64
MLA_ROPE = 32
MLA_V = 64
N_GROUPS = 4
EXP_PER_GROUP = 8
N_EXPERTS = N_GROUPS * EXP_PER_GROUP
EXPERT_FF = 512
DN_ALPHA = float((2 * DEPTH) ** 0.25)

EVEN_IN = 2 * CONV_CH + (WIN_HEADS + 2 * WIN_KV_HEADS) * HEAD_DIM
ODD_IN = 1184
ODD_IN_PAD = 1280
MLA_PAD = 128
VAL_PAD = 128
LOG2E = 1.4426950408889634

R_LAT = BATCH * SEQ
R_CTX = BATCH * CTX_LEN
R_ALL = R_LAT + R_CTX
TM = 256
NT_LAT = R_LAT // TM
NT_ALL = R_ALL // TM
TILES_PER_SEQ = SEQ // TM
HALO = 16
CONV_CHUNK = 32
TMM = 256
ROUTE_W = 128
SUBLANES = 8
LANES = 128
ROW_TILE = (SUBLANES, D_MODEL // SUBLANES)
VMEM_LIMIT = 56 * 1024 * 1024

SH1, SC1, G1, SH2, SC2, G2 = range(6)


def _sigmoid(x):
    return 1.0 / (1.0 + jnp.exp(-x))


def _layer_norm(z, g, b):
    mu = jnp.mean(z, axis=-1, keepdims=True)
    zc = z - mu
    var = jnp.mean(zc * zc, axis=-1, keepdims=True)
    return zc * lax.rsqrt(var + LN_EPS) * g + b


def _rope(x, cos, sin, half):
    n = x.shape[-1]
    lane = lax.broadcasted_iota(jnp.int32, x.shape, 1)
    first = (lane % (2 * half)) < half
    partner = jnp.where(first, pltpu.roll(x, n - half, 1), pltpu.roll(x, half, 1))
    return x * cos + partner * sin


def _mod_row(i):
    return jnp.where(i < NT_LAT, i // TILES_PER_SEQ, BATCH)


def _mod_spec(chunk):
    return pl.BlockSpec((None, None, 1, D_MODEL), lambda i: (_mod_row(i), chunk, 0, 0))


def _rope_row_block(i):
    return jnp.where(i < NT_LAT, i % TILES_PER_SEQ, TILES_PER_SEQ)


def _full(shape):
    nd = len(shape)
    return pl.BlockSpec(shape, lambda *_: (0,) * nd)


def _params():
    return pltpu.CompilerParams(vmem_limit_bytes=VMEM_LIMIT)


def _ada_kernel(cv_ref, w_ref, b_ref, o_ref):
    cv = cv_ref[...]
    s = cv * _sigmoid(cv)
    o_ref[...] = jnp.dot(s, w_ref[...], precision=lax.Precision.HIGHEST,
                         preferred_element_type=F32) + b_ref[...]


def _ada_table(cv, ada_w, ada_b):
    bn = 1536
    nb = (6 * D_MODEL) // bn
    return pl.pallas_call(
        _ada_kernel,
        grid=(DEPTH, nb),
        in_specs=[pl.BlockSpec((8, D_MODEL), lambda l, j: (0, 0)),
                  pl.BlockSpec((None, D_MODEL, bn), lambda l, j: (l, 0, j)),
                  pl.BlockSpec((None, 1, bn), lambda l, j: (l, 0, j))],
        out_specs=pl.BlockSpec((None, 8, bn), lambda l, j: (l, 0, j)),
        out_shape=jax.ShapeDtypeStruct((DEPTH, 8, 6 * D_MODEL), F32),
        compiler_params=_params(),
        name="ada_table",
    )(cv, ada_w, ada_b.reshape(DEPTH, 1, 6 * D_MODEL))


def _rope_tables(rot_dim):
    axis_dim = rot_dim // 2
    inv_freq = ROPE_THETA ** (-jnp.arange(0, axis_dim, 2, dtype=F32) / axis_dim)
    t = jnp.arange(SEQ)
    ang_r = (t // GRID_W).astype(F32)[:, None] * inv_freq[None, :]
    ang_c = (t % GRID_W).astype(F32)[:, None] * inv_freq[None, :]
    cos = jnp.concatenate([jnp.cos(ang_r), jnp.cos(ang_r), jnp.cos(ang_c), jnp.cos(ang_c)], axis=-1)
    sin = jnp.concatenate([-jnp.sin(ang_r), jnp.sin(ang_r), -jnp.sin(ang_c), jnp.sin(ang_c)], axis=-1)
    return cos, sin


def _pad_table(cos, sin, lead, period, width):
    rot = cos.shape[1]
    one = jnp.ones((SEQ, period), F32).at[:, lead:lead + rot].set(cos)
    zero = jnp.zeros((SEQ, period), F32).at[:, lead:lead + rot].set(sin)
    cos_w = jnp.tile(one, (1, width // period))
    sin_w = jnp.tile(zero, (1, width // period))
    cos_w = jnp.concatenate([cos_w, jnp.ones((TM, width), F32)], axis=0)
    sin_w = jnp.concatenate([sin_w, jnp.zeros((TM, width), F32)], axis=0)
    return cos_w, sin_w


def _proj0_kernel(x_ref, sh_ref, sc_ref, w_ref, b_ref, cos_ref, sin_ref,
                  h_ref, q_ref, k_ref, v_ref):
    u = x_ref[...] * (1.0 + sc_ref[...]) + sh_ref[...]
    y = jnp.dot(u.astype(BF16), w_ref[...], preferred_element_type=F32) + b_ref[...]
    h_ref[...] = y[:, :CONV_CH] * _sigmoid(y[:, CONV_CH:2 * CONV_CH])
    cos = cos_ref[...]
    sin = sin_ref[...]
    q0 = 2 * CONV_CH
    k0 = q0 + WIN_HEADS * HEAD_DIM
    v0 = k0 + WIN_KV_HEADS * HEAD_DIM
    cos4 = jnp.concatenate([cos] * 4, axis=1)
    sin4 = jnp.concatenate([sin] * 4, axis=1)
    q = _rope(y[:, q0:k0], cos4, sin4, HEAD_DIM // 4) * (HEAD_DIM ** -0.5 * LOG2E)
    q_ref[...] = q.astype(BF16)
    k_ref[...] = _rope(y[:, k0:v0], cos, sin, HEAD_DIM // 4).astype(BF16)
    v_ref[...] = _values_with_ones(y[:, v0:].astype(BF16), WIN_KV_HEADS, HEAD_DIM)


def _proj0(x_all, mods, w_in, b_in, cos_hd, sin_hd):
    kvw = WIN_KV_HEADS * HEAD_DIM
    row = lambda w: pl.BlockSpec((TM, w), lambda i: (i, 0))
    tab = pl.BlockSpec((TM, 128), lambda i: (_rope_row_block(i), 0))
    return pl.pallas_call(
        _proj0_kernel,
        grid=(NT_ALL,),
        in_specs=[row(D_MODEL), _mod_spec(SH1), _mod_spec(SC1),
                  _full((D_MODEL, EVEN_IN)), _full((1, EVEN_IN)), tab, tab],
        out_specs=[row(CONV_CH), row(WIN_HEADS * HEAD_DIM), row(kvw), row(WIN_KV_HEADS * VAL_PAD)],
        out_shape=[jax.ShapeDtypeStruct((R_ALL, CONV_CH), F32),
                   jax.ShapeDtypeStruct((R_ALL, WIN_HEADS * HEAD_DIM), BF16),
                   jax.ShapeDtypeStruct((R_ALL, kvw), BF16),
                   jax.ShapeDtypeStruct((R_ALL, WIN_KV_HEADS * VAL_PAD), BF16)],
        compiler_params=_params(),
        name="proj0",
    )(x_all, mods, mods, w_in, b_in, cos_hd, sin_hd)


def _conv_kernel(prev_ref, cur_ref, next_ref, w_ref, cb_ref, g_ref, b_ref, o_ref, buf):
    i = pl.program_id(0)
    is_ctx = i >= NT_LAT
    first = jnp.logical_or(is_ctx, i % TILES_PER_SEQ == 0)
    last = jnp.logical_or(is_ctx, i % TILES_PER_SEQ == TILES_PER_SEQ - 1)
    buf[0:HALO, :] = jnp.where(first, 0.0, prev_ref[...])
    buf[HALO:HALO + TM, :] = cur_ref[...]
    buf[HALO + TM:, :] = jnp.where(last, 0.0, next_ref[...])
    off = HALO - CONV_WIDTH // 2
    for c in range(TM // CONV_CHUNK):
        r0 = c * CONV_CHUNK
        acc = jnp.zeros((CONV_CHUNK, CONV_CH), F32)
        for k in range(CONV_WIDTH):
            acc = acc + buf[r0 + off + k:r0 + off + k + CONV_CHUNK, :] * w_ref[k:k + 1, :]
        z = _layer_norm(acc + cb_ref[...], g_ref[...], b_ref[...])
        o_ref[r0:r0 + CONV_CHUNK, :] = (z * _sigmoid(z)).astype(BF16)


def _conv(h, conv_w, conv_b, ln_g, ln_b):
    nh = R_ALL // HALO
    per = TM // HALO
    vec = _full((1, CONV_CH))
    return pl.pallas_call(
        _conv_kernel,
        grid=(NT_ALL,),
        in_specs=[pl.BlockSpec((HALO, CONV_CH), lambda i: (jnp.maximum(i * per - 1, 0), 0)),
                  pl.BlockSpec((TM, CONV_CH), lambda i: (i, 0)),
                  pl.BlockSpec((HALO, CONV_CH), lambda i: (jnp.minimum((i + 1) * per, nh - 1), 0)),
                  _full((CONV_WIDTH, CONV_CH)), vec, vec, vec],
        out_specs=pl.BlockSpec((TM, CONV_CH), lambda i: (i, 0)),
        out_shape=jax.ShapeDtypeStruct((R_ALL, CONV_CH), BF16),
        scratch_shapes=[pltpu.VMEM((TM + 2 * HALO, CONV_CH), F32)],
        compiler_params=_params(),
        name="conv_module",
    )(h, h, h, conv_w, conv_b, ln_g, ln_b)


def _nt_dot(a, b):
    return lax.dot_general(a, b, (((1,), (1,)), ((), ())), preferred_element_type=F32)


def _values_with_ones(v, n_kv, dv):
    lane = lax.broadcasted_iota(jnp.int32, (v.shape[0], VAL_PAD - dv), 1)
    tail = jnp.where(lane == 0, 1.0, 0.0).astype(v.dtype)
    pieces = []
    for h in range(n_kv):
        pieces += [v[:, h * dv:(h + 1) * dv], tail]
    return jnp.concatenate(pieces, axis=1)


def _attend(units, dv):
    def scores(unit):
        q, ks, _, masks, _ = unit
        out = []
        for k, msk in zip(ks, masks):
            s = _nt_dot(q, k)
            if msk is not None:
                s = jnp.where(msk, s, NEG_INF)
            out.append(s)
        return out

    results = []
    ss = scores(units[0])
    for idx, unit in enumerate(units):
        nxt = scores(units[idx + 1]) if idx + 1 < len(units) else None
        _, _, vs, _, sink = unit
        m = functools.reduce(jnp.maximum, [jnp.max(s, axis=-1, keepdims=True) for s in ss])
        if sink is not None:
            m = jnp.maximum(m, sink)
        acc = functools.reduce(jnp.add, [jnp.dot(jnp.exp2(s - m).astype(BF16), v, preferred_element_type=F32)
                                         for s, v in zip(ss, vs)])
        l = acc[:, dv:dv + 1]
        if sink is not None:
            l = l + jnp.exp2(sink - m)
        results.append(acc[:, :dv] / l)
        ss = nxt
    return results


def _win_kernel(sink_ref, q_ref, kp_ref, kc_ref, kn_ref, kx_ref, vp_ref, vc_ref, vn_ref, vx_ref, o_ref):
    n = pl.program_id(1)
    k_loc = jnp.concatenate([kp_ref[...], kc_ref[...], kn_ref[...]], axis=0)
    v_loc = jnp.concatenate([vp_ref[...], vc_ref[...], vn_ref[...]], axis=0)
    k_ctx = kx_ref[...]
    v_ctx = vx_ref[...]
    qi = lax.broadcasted_iota(jnp.int32, (WINDOW, 3 * WINDOW), 0)
    kj = lax.broadcasted_iota(jnp.int32, (WINDOW, 3 * WINDOW), 1)
    k_pos = jnp.where(n < SEQ // WINDOW, kj + (n - 1) * WINDOW, SEQ)
    valid = jnp.where(kj >= qi, jnp.where(kj <= qi + 2 * WINDOW, 1, 0), 0)
    valid = jnp.where(k_pos >= 0, jnp.where(k_pos < SEQ, valid, 0), 0) > 0
    group = WIN_HEADS // WIN_KV_HEADS
    units = []
    for h in range(WIN_HEADS):
        kv = h // group
        ksl = slice(kv * HEAD_DIM, (kv + 1) * HEAD_DIM)
        vsl = slice(kv * VAL_PAD, (kv + 1) * VAL_PAD)
        units.append((q_ref[:, h * HEAD_DIM:(h + 1) * HEAD_DIM], [k_ctx[:, ksl], k_loc[:, ksl]],
                      [v_ctx[:, vsl], v_loc[:, vsl]], [None, valid], sink_ref[h] * LOG2E))
    for h, o in enumerate(_attend(units, HEAD_DIM)):
        o_ref[:, h * HEAD_DIM:(h + 1) * HEAD_DIM] = o.astype(BF16)


def _win_attention(sink, q, k, v):
    nblk = SEQ // WINDOW
    cblk = CTX_LEN // WINDOW
    kvw = WIN_KV_HEADS * HEAD_DIM
    ctx0 = R_LAT // CTX_LEN
    lat = lambda n: jnp.minimum(n, nblk - 1)
    prev = lambda b, n: (b * nblk + jnp.maximum(lat(n) - 1, 0), 0)
    cur = lambda b, n: (b * nblk + lat(n), 0)
    nxt = lambda b, n: (b * nblk + jnp.minimum(lat(n) + 1, nblk - 1), 0)
    qrow = lambda b, n: (jnp.where(n < nblk, b * nblk + n, R_LAT // WINDOW + b * cblk + n - nblk), 0)
    ctx = lambda b, n: (ctx0 + b, 0)
    vw = WIN_KV_HEADS * VAL_PAD
    kvb = lambda f, w: pl.BlockSpec((WINDOW, w), f)
    cxb = lambda w: pl.BlockSpec((CTX_LEN, w), ctx)
    return pl.pallas_call(
        _win_kernel,
        grid=(BATCH, nblk + cblk),
        in_specs=[pl.BlockSpec(memory_space=pltpu.SMEM),
                  pl.BlockSpec((WINDOW, WIN_HEADS * HEAD_DIM), qrow),
                  kvb(prev, kvw), kvb(cur, kvw), kvb(nxt, kvw), cxb(kvw),
                  kvb(prev, vw), kvb(cur, vw), kvb(nxt, vw), cxb(vw)],
        out_specs=pl.BlockSpec((WINDOW, WIN_HEADS * HEAD_DIM), qrow),
        out_shape=jax.ShapeDtypeStruct((R_ALL, WIN_HEADS * HEAD_DIM), BF16),
        compiler_params=_params(),
        name="window_attention",
    )(sink, q, k, k, k, k, v, v, v, v)


def _outproj_kernel(a_ref, b_ref, wa_ref, wb_ref, bo_ref, x_ref, g1_ref, sh2_ref, sc2_ref,
                    lng_ref, lnb_ref, wrh_ref, wrl_ref, br_ref, upper_ref, lower_ref,
                    xo_ref, route_ref, cnt_ref, xs_ref):
    y = (jnp.dot(a_ref[...], wa_ref[...], preferred_element_type=F32)
         + jnp.dot(b_ref[...], wb_ref[...], preferred_element_type=F32) + bo_ref[...])
    xn = _layer_norm(DN_ALPHA * x_ref[...] + (1.0 + g1_ref[...]) * y, lng_ref[...], lnb_ref[...])
    xo_ref[...] = xn
    u2 = xn * (1.0 + sc2_ref[...]) + sh2_ref[...]
    u_hi = u2.astype(BF16)
    u_lo = (u2 - u_hi.astype(F32)).astype(BF16)
    logits = (jnp.dot(u_hi, wrh_ref[...], preferred_element_type=F32)
              + jnp.dot(u_lo, wrh_ref[...], preferred_element_type=F32)
              + jnp.dot(u_hi, wrl_ref[...], preferred_element_type=F32) + br_ref[...])
    lane = lax.broadcasted_iota(jnp.int32, logits.shape, 1).astype(F32)
    ninf = -jnp.inf
    big = float(ROUTE_W)
    gl = jnp.where(lane < N_GROUPS, logits, ninf)
    gmax = jnp.max(gl, axis=-1, keepdims=True)
    gidx = jnp.min(jnp.where(gl == gmax, lane, big), axis=-1, keepdims=True)
    g_w = 1.0 / jnp.sum(jnp.exp(gl - gmax), axis=-1, keepdims=True)
    lo = N_GROUPS + EXP_PER_GROUP * gidx
    el = jnp.where(lane >= lo, jnp.where(lane < lo + EXP_PER_GROUP, logits, ninf), ninf)
    v1 = jnp.max(el, axis=-1, keepdims=True)
    i1 = jnp.min(jnp.where(el == v1, lane, big), axis=-1, keepdims=True)
    el2 = jnp.where(lane == i1, ninf, el)
    v2 = jnp.max(el2, axis=-1, keepdims=True)
    i2 = jnp.min(jnp.where(el2 == v2, lane, big), axis=-1, keepdims=True)
    e2 = jnp.exp(v2 - v1)
    w1 = g_w / (1.0 + e2)
    w2 = g_w * e2 / (1.0 + e2)
    onehot = [jnp.where(lane == i1 - N_GROUPS, 1.0, 0.0), jnp.where(lane == i2 - N_GROUPS, 1.0, 0.0)]
    cnt = [jnp.sum(o, axis=0, keepdims=True) for o in onehot]
    cnt8 = [jnp.broadcast_to(c, (8, ROUTE_W)).astype(BF16) for c in cnt]
    upper = upper_ref[...]
    below = (jnp.dot(cnt8[0], upper, preferred_element_type=F32)
             + jnp.dot(cnt8[1], upper, preferred_element_type=F32))[0:1]
    lower = lower_ref[...]
    base = [below, below + cnt[0]]
    lp = []
    for s in range(2):
        earlier = jnp.dot(lower, onehot[s].astype(BF16), preferred_element_type=F32)
        lp.append(jnp.sum(onehot[s] * (base[s] + earlier), axis=-1, keepdims=True))
    rec = jnp.where(lane == 0.0, i1 - N_GROUPS,
                    jnp.where(lane == 1.0, i2 - N_GROUPS,
                              jnp.where(lane == 2.0, w1,
                                        jnp.where(lane == 3.0, w2,
                                                  jnp.where(lane == 4.0, lp[0],
                                                            jnp.where(lane == 5.0, lp[1], 0.0))))))
    route_ref[...] = rec
    cnt_ref[...] = jnp.broadcast_to(cnt[0] + cnt[1], (8, ROUTE_W))
    pos = lax.broadcasted_iota(jnp.int32, (TM, 2 * TM), 1).astype(F32)
    xs = functools.reduce(jnp.add, [
        lax.dot_general(jnp.where(pos == lp[s], 1.0, 0.0).astype(BF16), u_hi,
                        (((0,), (0,)), ((), ())), preferred_element_type=F32) for s in range(2)])
    _store_row_tiles(xs_ref, xs)


def _outproj(n_tiles, mix_a, mix_b, w_a, w_b, b_out, x_all, mods, ln_g, ln_b, w_r, b_r):
    rows = n_tiles * TM
    half = mix_a.shape[1]
    w_rh = w_r.astype(BF16)
    w_rl = (w_r - w_rh.astype(F32)).astype(BF16)
    upper = (jnp.arange(ROUTE_W)[:, None] < jnp.arange(ROUTE_W)[None, :]).astype(BF16)
    lower = (jnp.arange(TM)[:, None] > jnp.arange(TM)[None, :]).astype(BF16)
    row = lambda w: pl.BlockSpec((TM, w), lambda i: (i, 0))
    vec = _full((1, D_MODEL))
    return pl.pallas_call(
        _outproj_kernel,
        grid=(n_tiles,),
        in_specs=[row(half), row(half), _full((half, D_MODEL)), _full((half, D_MODEL)), vec,
                  row(D_MODEL), _mod_spec(G1), _mod_spec(SH2), _mod_spec(SC2), vec, vec,
                  _full((D_MODEL, ROUTE_W)), _full((D_MODEL, ROUTE_W)), _full((1, ROUTE_W)),
                  _full((ROUTE_W, ROUTE_W)), _full((TM, TM))],
        out_specs=[row(D_MODEL), row(ROUTE_W), pl.BlockSpec((8, ROUTE_W), lambda i: (i, 0)),
                   pl.BlockSpec((2 * TM,) + ROW_TILE, lambda i: (i, 0, 0))],
        out_shape=[jax.ShapeDtypeStruct((rows, D_MODEL), F32),
                   jax.ShapeDtypeStruct((rows, ROUTE_W), F32),
                   jax.ShapeDtypeStruct((n_tiles * 8, ROUTE_W), F32),
                   jax.ShapeDtypeStruct((2 * rows,) + ROW_TILE, F32)],
        compiler_params=_params(),
        name="outproj_ln_router",
    )(mix_a, mix_b, w_a, w_b, b_out, x_all, mods, mods, mods, ln_g, ln_b, w_rh, w_rl, b_r, upper, lower)


def _store_row_tiles(ref, y):
    for j in range(SUBLANES):
        ref[:, j, :] = y[:, j * LANES:(j + 1) * LANES]


def _load_row_tiles(ref):
    return jnp.concatenate([ref[:, j, :] for j in range(SUBLANES)], axis=1)


def _moe_kernel(te_ref, tk_ref, rows_ref, lo_ref, hi_ref, cnt_ref, bt_ref, be_ref, xs_hbm, w1_ref, w3_ref,
                w2_ref, ys_ref, xbuf, wb1, wb3, wb2, sem):
    j = pl.program_id(0)
    nt = pl.num_programs(0)
    slot = j % 2

    def issue(tile, slot_):
        e = te_ref[tile]
        first = tk_ref[tile] * TMM

        def body(i, carry):
            idx = i * N_EXPERTS + e
            start = bt_ref[idx]
            lo = jnp.maximum(start, first)
            n = jnp.minimum(start + cnt_ref[idx], first + TMM) - lo

            @pl.when(n > 0)
            def _():
                src = i * (2 * TM) + be_ref[idx] + lo - start
                pltpu.make_async_copy(xs_hbm.at[pl.ds(src, n)], xbuf.at[slot_, pl.ds(lo - first, n)],
                                      sem.at[slot_]).start()
            return carry
        lax.fori_loop(lo_ref[tile], hi_ref[tile], body, 0)

    @pl.when(j == 0)
    def _():
        xbuf[...] = jnp.zeros_like(xbuf)
        issue(0, 0)

    @pl.when(j + 1 < nt)
    def _():
        issue(j + 1, 1 - slot)

    @pl.when(jnp.logical_or(j == 0, te_ref[j] != te_ref[jnp.maximum(j - 1, 0)]))
    def _():
        wb1[...] = w1_ref[...].astype(BF16)
        wb3[...] = w3_ref[...].astype(BF16)
        wb2[...] = w2_ref[...].astype(BF16)

    n_real = rows_ref[j]

    @pl.when(n_real > 0)
    def _():
        pltpu.make_async_copy(xs_hbm.at[pl.ds(0, n_real)], xbuf.at[slot, pl.ds(0, n_real)], sem.at[slot]).wait()
        x = _load_row_tiles(xbuf.at[slot]).astype(BF16)
        h1 = jnp.dot(x, wb1[...], preferred_element_type=F32)
        h3 = jnp.dot(x, wb3[...], preferred_element_type=F32)
        hid = h1 * _sigmoid(h1) * h3
        _store_row_tiles(ys_ref, jnp.dot(hid.astype(BF16), wb2[...], preferred_element_type=F32))

    @pl.when(n_real == 0)
    def _():
        ys_ref[...] = jnp.zeros_like(ys_ref)


def _moe_experts(layer, plan, xs_local, w1, w3, w2):
    tile_expert, tile_k, tile_rows, src_lo, src_hi, cnt, before_tile, before_expert, _ = plan
    nt = tile_expert.shape[0]
    wmap = lambda j, te, *_: (layer, te[j], 0, 0)
    grid_spec = pltpu.PrefetchScalarGridSpec(
        num_scalar_prefetch=8,
        grid=(nt,),
        in_specs=[pl.BlockSpec(memory_space=pl.ANY),
                  pl.BlockSpec((None, None, D_MODEL, EXPERT_FF), wmap),
                  pl.BlockSpec((None, None, D_MODEL, EXPERT_FF), wmap),
                  pl.BlockSpec((None, None, EXPERT_FF, D_MODEL), wmap)],
        out_specs=pl.BlockSpec((TMM,) + ROW_TILE, lambda j, *_: (j, 0, 0)),
        scratch_shapes=[pltpu.VMEM((2, TMM) + ROW_TILE, F32),
                        pltpu.VMEM((D_MODEL, EXPERT_FF), BF16),
                        pltpu.VMEM((D_MODEL, EXPERT_FF), BF16),
                        pltpu.VMEM((EXPERT_FF, D_MODEL), BF16),
                        pltpu.SemaphoreType.DMA((2,))])
    return pl.pallas_call(
        _moe_kernel,
        grid_spec=grid_spec,
        out_shape=jax.ShapeDtypeStruct((nt * TMM,) + ROW_TILE, F32),
        compiler_params=_params(),
        name="moe_experts",
    )(tile_expert, tile_k, tile_rows, src_lo, src_hi, cnt, before_tile, before_expert, xs_local, w1, w3, w2)


def _combine_kernel(cnt_ref, bt_ref, be_ref, gs_ref, ys_hbm, x_ref, route_ref, g2_ref, lng_ref, lnb_ref, o_ref,
                    ybuf, sem):
    i = pl.program_id(0)
    nt = pl.num_programs(0)
    slot = i % 2

    def issue(tile, slot_):
        def body(e, carry):
            idx = tile * N_EXPERTS + e
            n = cnt_ref[idx]

            @pl.when(n > 0)
            def _():
                pltpu.make_async_copy(ys_hbm.at[pl.ds(gs_ref[e] + bt_ref[idx], n)],
                                      ybuf.at[slot_, pl.ds(be_ref[idx], n)], sem.at[slot_]).start()
            return carry
        lax.fori_loop(0, N_EXPERTS, body, 0)

    @pl.when(i == 0)
    def _():
        issue(0, 0)

    @pl.when(i + 1 < nt)
    def _():
        issue(i + 1, 1 - slot)

    pltpu.make_async_copy(ys_hbm.at[pl.ds(0, 2 * TM)], ybuf.at[slot], sem.at[slot]).wait()
    ys_ref = ybuf.at[slot]
    route = route_ref[...]
    pos = lax.broadcasted_iota(jnp.int32, (TM, 2 * TM), 1).astype(F32)
    sel = (jnp.where(pos == route[:, 4:5], route[:, 2:3], 0.0)
           + jnp.where(pos == route[:, 5:6], route[:, 3:4], 0.0))
    sel_hi = sel.astype(BF16)
    sel_lo = (sel - sel_hi.astype(F32)).astype(BF16)
    y = _load_row_tiles(ys_ref)
    y_hi = y.astype(BF16)
    y_lo = (y - y_hi.astype(F32)).astype(BF16)
    f = (jnp.dot(sel_hi, y_hi, preferred_element_type=F32) + jnp.dot(sel_lo, y_hi, preferred_element_type=F32)
         + jnp.dot(sel_hi, y_lo, preferred_element_type=F32))
    z = DN_ALPHA * x_ref[...] + (1.0 + g2_ref[...]) * f
    o_ref[...] = _layer_norm(z, lng_ref[...], lnb_ref[...])


def _moe_combine(n_tiles, plan, ys, x_all, route, mods, ln_g, ln_b):
    _, _, _, _, _, cnt, before_tile, before_expert, group_start = plan
    rows = n_tiles * TM
    row = lambda w: pl.BlockSpec((TM, w), lambda i, *_: (i, 0))
    vec = pl.BlockSpec((1, D_MODEL), lambda i, *_: (0, 0))
    grid_spec = pltpu.PrefetchScalarGridSpec(
        num_scalar_prefetch=4,
        grid=(n_tiles,),
        in_specs=[pl.BlockSpec(memory_space=pl.ANY), row(D_MODEL), row(ROUTE_W),
                  pl.BlockSpec((None, None, 1, D_MODEL), lambda i, *_: (_mod_row(i), G2, 0, 0)), vec, vec],
        out_specs=row(D_MODEL),
        scratch_shapes=[pltpu.VMEM((2, 2 * TM) + ROW_TILE, F32), pltpu.SemaphoreType.DMA((2,))])
    return pl.pallas_call(
        _combine_kernel,
        grid_spec=grid_spec,
        out_shape=jax.ShapeDtypeStruct((rows, D_MODEL), F32),
        compiler_params=_params(),
        name="moe_combine_ln",
    )(cnt, before_tile, before_expert, group_start, ys, x_all, route, mods, ln_g, ln_b)


def _moe_plan(cnt_rec, n_tiles):
    cnt = cnt_rec.reshape(n_tiles, 8, ROUTE_W)[:, 0, :N_EXPERTS].astype(jnp.int32)
    nt_max = (2 * n_tiles * TM) // TMM + N_EXPERTS
    total = jnp.sum(cnt, axis=0)
    tiles_e = (total + TMM - 1) // TMM
    tile_end = jnp.cumsum(tiles_e)
    first_tile = tile_end - tiles_e
    before_tile = jnp.cumsum(cnt, axis=0) - cnt
    before_expert = jnp.cumsum(cnt, axis=1) - cnt
    tile_id = jnp.arange(nt_max, dtype=jnp.int32)
    tile_expert = jnp.minimum(jnp.sum((tile_id[:, None] >= tile_end[None, :]).astype(jnp.int32), axis=1),
                              N_EXPERTS - 1)
    tile_k = tile_id - first_tile[tile_expert]
    tile_rows = jnp.clip(total[tile_expert] - tile_k * TMM, 0, TMM)
    first = (tile_k * TMM)[:, None]
    run_start = before_tile.T[tile_expert]
    run_end = run_start + cnt.T[tile_expert]
    src_lo = jnp.sum((run_end <= first).astype(jnp.int32), axis=1)
    src_hi = jnp.sum((run_start < first + TMM).astype(jnp.int32), axis=1)
    return (tile_expert, tile_k, tile_rows, src_lo, src_hi, cnt.reshape(-1), before_tile.reshape(-1),
            before_expert.reshape(-1), first_tile * TMM)


def _router_weights(w_rg, b_rg, w_re, b_re):
    w = jnp.concatenate([w_rg, jnp.transpose(w_re, (1, 0, 2)).reshape(D_MODEL, N_EXPERTS)], axis=1)
    b = jnp.concatenate([b_rg, b_re.reshape(-1)])
    pad = ROUTE_W - w.shape[1]
    return jnp.pad(w, ((0, 0), (0, pad))), jnp.pad(b, (0, pad)).reshape(1, ROUTE_W)


def _proj1_kernel(x_ref, sh_ref, sc_ref, w_ref, b_ref, cos_ref, sin_ref, cosm_ref, sinm_ref, cosr_ref,
                  sinr_ref, gq_ref, gk_ref, gqc_ref, gkv_ref, avg_ref, wuq_ref, wuk_ref, wuv_ref,
                  q_ref, qm_ref, k_ref, v_ref, km_ref, vm_ref):
    u = x_ref[...] * (1.0 + sc_ref[...]) + sh_ref[...]
    y = jnp.dot(u.astype(BF16), w_ref[...], preferred_element_type=F32) + b_ref[...]
    c_q = GQA_HEADS * HEAD_DIM
    c_qc = c_q + MLA_Q_RANK
    c_k = c_qc + GQA_KV_HEADS * HEAD_DIM
    c_v = c_k + GQA_KV_HEADS * HEAD_DIM
    c_kv = c_v + MLA_KV_RANK
    avg = avg_ref[...]

    def head_rms(t, gain):
        sq = t * t
        hi = sq.astype(BF16)
        lo = (sq - hi.astype(F32)).astype(BF16)
        a = avg[:t.shape[1], :t.shape[1]]
        ms = jnp.dot(hi, a, preferred_element_type=F32) + jnp.dot(lo, a, preferred_element_type=F32)
        return t * lax.rsqrt(ms + RMS_EPS) * gain

    def row_rms(t, gain):
        ms = jnp.mean(t * t, axis=-1, keepdims=True)
        return t * lax.rsqrt(ms + RMS_EPS) * gain

    cos = cos_ref[...]
    sin = sin_ref[...]
    cos4 = jnp.concatenate([cos] * 4, axis=1)
    sin4 = jnp.concatenate([sin] * 4, axis=1)
    q = _rope(head_rms(y[:, :c_q], gq_ref[...]), cos4, sin4, HEAD_DIM // 4) * (HEAD_DIM ** -0.5 * LOG2E)
    q_ref[...] = q.astype(BF16)
    k = _rope(head_rms(y[:, c_qc:c_k], gk_ref[...]), cos, sin, HEAD_DIM // 4)
    k_ref[...] = k.astype(BF16)
    v_ref[...] = _values_with_ones(y[:, c_k:c_v].astype(BF16), GQA_KV_HEADS, HEAD_DIM)

    qc = row_rms(y[:, c_q:c_qc], gqc_ref[...]).astype(BF16)
    qm = jnp.dot(qc, wuq_ref[...], preferred_element_type=F32)
    cosm = jnp.concatenate([cosm_ref[...]] * MLA_HEADS, axis=1)
    sinm = jnp.concatenate([sinm_ref[...]] * MLA_HEADS, axis=1)
    qm = _rope(qm, cosm, sinm, MLA_ROPE // 4) * ((MLA_NOPE + MLA_ROPE) ** -0.5 * LOG2E)
    qm_ref[...] = qm.astype(BF16)

    kvn = row_rms(y[:, c_v:c_kv], gkv_ref[...]).astype(BF16)
    kr = _rope(y[:, c_kv:], cosr_ref[...], sinr_ref[...], MLA_ROPE // 4).astype(BF16)
    km = jnp.dot(jnp.concatenate([kvn, kr], axis=1), wuk_ref[...], preferred_element_type=F32)
    km_ref[...] = km.astype(BF16)
    vm = jnp.dot(kvn, wuv_ref[...], preferred_element_type=F32)
    lane = lax.broadcasted_iota(jnp.int32, vm.shape, 1)
    vm_ref[...] = jnp.where(lane % VAL_PAD == MLA_V, 1.0, vm).astype(BF16)


def _proj1(x_all, mods, w_in, b_in, tabs, gq, gk, gqc, gkv, avg, wuq, wuk, wuv):
    cos_hd, sin_hd, cos_m, sin_m, cos_r, sin_r = tabs
    kvw = GQA_KV_HEADS * HEAD_DIM
    qw = GQA_HEADS * HEAD_DIM
    mw = MLA_HEADS * MLA_PAD
    vw = MLA_HEADS * VAL_PAD
    gvw = GQA_KV_HEADS * VAL_PAD
    row = lambda w: pl.BlockSpec((TM, w), lambda i: (i, 0))
    tab = pl.BlockSpec((TM, 128), lambda i: (_rope_row_block(i), 0))
    return pl.pallas_call(
        _proj1_kernel,
        grid=(NT_ALL,),
        in_specs=[row(D_MODEL), _mod_spec(SH1), _mod_spec(SC1),
                  _full((D_MODEL, ODD_IN_PAD)), _full((1, ODD_IN_PAD)), tab, tab, tab, tab, tab, tab,
                  _full((1, qw)), _full((1, kvw)), _full((1, MLA_Q_RANK)), _full((1, MLA_KV_RANK)),
                  _full((qw, qw)), _full((MLA_Q_RANK, mw)), _full((MLA_KV_RANK + 128, mw)),
                  _full((MLA_KV_RANK, vw))],
        out_specs=[row(qw), row(mw), row(kvw), row(gvw), row(mw), row(vw)],
        out_shape=[jax.ShapeDtypeStruct((R_ALL, qw), BF16),
                   jax.ShapeDtypeStruct((R_ALL, mw), BF16),
                   jax.ShapeDtypeStruct((R_ALL, kvw), BF16),
                   jax.ShapeDtypeStruct((R_ALL, gvw), BF16),
                   jax.ShapeDtypeStruct((R_ALL, mw), BF16),
                   jax.ShapeDtypeStruct((R_ALL, vw), BF16)],
        compiler_params=_params(),
        name="proj1",
    )(x_all, mods, mods, w_in, b_in, cos_hd, sin_hd, cos_m, sin_m, cos_r, sin_r,
      gq, gk, gqc, gkv, avg, wuq, wuk, wuv)


def _dense_kernel(q_ref, kl_ref, kc_ref, vl_ref, vc_ref, o_ref, *, n_heads, group, dk, dv):
    tq = q_ref.shape[0]
    units = []
    for kv in range(n_heads // group):
        qs = [q_ref[:, h * dk:(h + 1) * dk] for h in range(kv * group, (kv + 1) * group)]
        q = qs[0] if group == 1 else jnp.concatenate(qs, axis=0)
        ks = slice(kv * dk, (kv + 1) * dk)
        vs = slice(kv * VAL_PAD, (kv + 1) * VAL_PAD)
        units.append((q, [kl_ref[:, ks], kc_ref[:, ks]], [vl_ref[:, vs], vc_ref[:, vs]], [None, None], None))
    for kv, o in enumerate(_attend(units, dv)):
        for g in range(group):
            h = kv * group + g
            o_ref[:, h * dv:(h + 1) * dv] = o[g * tq:(g + 1) * tq].astype(BF16)


def _dense_attention(q, k, v, *, n_heads, group, dk, dv, tq, name):
    n_kv = n_heads // group
    nq = SEQ // tq
    ctx0 = R_LAT // CTX_LEN
    lat = lambda w: pl.BlockSpec((SEQ, w), lambda b, j: (b, 0), pipeline_mode=pl.Buffered(1))
    ctx = lambda w: pl.BlockSpec((CTX_LEN, w), lambda b, j: (ctx0 + b, 0))
    return pl.pallas_call(
        functools.partial(_dense_kernel, n_heads=n_heads, group=group, dk=dk, dv=dv),
        grid=(BATCH, nq),
        in_specs=[pl.BlockSpec((tq, n_heads * dk), lambda b, j: (b * nq + j, 0)),
                  lat(n_kv * dk), ctx(n_kv * dk), lat(n_kv * VAL_PAD), ctx(n_kv * VAL_PAD)],
        out_specs=pl.BlockSpec((tq, n_heads * dv), lambda b, j: (b * nq + j, 0)),
        out_shape=jax.ShapeDtypeStruct((R_LAT, n_heads * dv), BF16),
        compiler_params=_params(),
        name=name,
    )(q, k, k, v, v)


def _mla_weights(w_uq, w_ukv):
    wq = w_uq.reshape(MLA_Q_RANK, MLA_HEADS, MLA_NOPE + MLA_ROPE)
    wq = jnp.pad(wq, ((0, 0), (0, 0), (0, MLA_PAD - MLA_NOPE - MLA_ROPE))).reshape(MLA_Q_RANK, -1)
    wkv = w_ukv.reshape(MLA_KV_RANK, MLA_HEADS, MLA_NOPE + MLA_V)
    wk = jnp.pad(wkv[:, :, :MLA_NOPE], ((0, 0), (0, 0), (0, MLA_PAD - MLA_NOPE))).reshape(MLA_KV_RANK, -1)
    wv = jnp.pad(wkv[:, :, MLA_NOPE:], ((0, 0), (0, 0), (0, VAL_PAD - MLA_V))).reshape(MLA_KV_RANK, -1)
    r = jnp.arange(128)[:, None]
    c = jnp.arange(MLA_HEADS * MLA_PAD)[None, :]
    place = jnp.logical_and(r < MLA_ROPE, (c % MLA_PAD) == MLA_NOPE + r).astype(F32)
    wk = jnp.concatenate([wk, place], axis=0)
    return wq.astype(BF16), wk.astype(BF16), wv.astype(BF16)


def kernel(x, c, ctx, c_ctx, even_w_in, even_b_in, even_conv_w, even_conv_b, even_conv_ln_g, even_conv_ln_b, even_sink, even_w_out, even_b_out, odd_w_in, odd_b_in, odd_q_norm, odd_k_norm, odd_mla_q_norm, odd_mla_kv_norm, odd_mla_w_uq, odd_mla_w_ukv, odd_w_out, odd_b_out, ada_w, ada_b, ln1_g, ln1_b, ln2_g, ln2_b, moe_w_rg, moe_b_rg, moe_w_re, moe_b_re, moe_w1, moe_w3, moe_w2):
    vec = lambda a: a.reshape(1, -1)
    x_all = jnp.concatenate([x.reshape(R_LAT, D_MODEL), ctx.reshape(R_CTX, D_MODEL)], axis=0)

    cv = jnp.concatenate([c, c_ctx[None, :], jnp.zeros((8 - BATCH - 1, D_MODEL), F32)], axis=0)
    mods = _ada_table(cv, ada_w, ada_b).reshape(DEPTH, 8, 6, 1, D_MODEL)

    cos64, sin64 = _rope_tables(HEAD_DIM)
    cos_hd, sin_hd = _pad_table(cos64, sin64, 0, HEAD_DIM, 128)
    cos32, sin32 = _rope_tables(MLA_ROPE)
    cos_m, sin_m = _pad_table(cos32, sin32, MLA_NOPE, MLA_PAD, 128)
    cos_r, sin_r = _pad_table(cos32, sin32, 0, 128, 128)

    m0 = mods[0]
    h, q0, k0, v0 = _proj0(x_all, m0, even_w_in[0].astype(BF16), vec(even_b_in[0]), cos_hd, sin_hd)
    conv_out = _conv(h, even_conv_w[0].reshape(CONV_WIDTH, CONV_CH), vec(even_conv_b[0]),
                     vec(even_conv_ln_g[0]), vec(even_conv_ln_b[0]))
    attn = _win_attention(even_sink[0], q0, k0, v0)
    w_out = even_w_out[0].astype(BF16)
    w_r, b_r = _router_weights(moe_w_rg[0], moe_b_rg[0], moe_w_re[0], moe_b_re[0])
    x_all, route, cnt_rec, xs_local = _outproj(
        NT_ALL, conv_out, attn, w_out[:CONV_CH], w_out[CONV_CH:], vec(even_b_out[0]),
        x_all, m0, vec(ln1_g[0]), vec(ln1_b[0]), w_r, b_r)
    plan = _moe_plan(cnt_rec, NT_ALL)
    ys = _moe_experts(0, plan, xs_local, moe_w1, moe_w3, moe_w2)
    x_all = _moe_combine(NT_ALL, plan, ys, x_all, route, m0, vec(ln2_g[0]), vec(ln2_b[0]))

    m1 = mods[1]
    w_in1 = jnp.pad(odd_w_in[0], ((0, 0), (0, ODD_IN_PAD - ODD_IN))).astype(BF16)
    b_in1 = jnp.pad(odd_b_in[0], (0, ODD_IN_PAD - ODD_IN)).reshape(1, -1)
    wuq, wuk, wuv = _mla_weights(odd_mla_w_uq[0], odd_mla_w_ukv[0])
    qw = GQA_HEADS * HEAD_DIM
    hid = jnp.arange(qw) // HEAD_DIM
    avg = ((hid[:, None] == hid[None, :]).astype(F32) / HEAD_DIM).astype(BF16)
    q1, qm, k1, v1, km, vm = _proj1(
        x_all, m1, w_in1, b_in1, (cos_hd, sin_hd, cos_m, sin_m, cos_r, sin_r),
        vec(jnp.tile(odd_q_norm[0], GQA_HEADS)), vec(jnp.tile(odd_k_norm[0], GQA_KV_HEADS)),
        vec(odd_mla_q_norm[0]), vec(odd_mla_kv_norm[0]), avg, wuq, wuk, wuv)
    o_g = _dense_attention(q1, k1, v1, n_heads=GQA_HEADS, group=GQA_HEADS // GQA_KV_HEADS,
                           dk=HEAD_DIM, dv=HEAD_DIM, tq=128, name="gqa_attention")
    o_m = _dense_attention(qm, km, vm, n_heads=MLA_HEADS, group=1, dk=MLA_PAD, dv=MLA_V, tq=256,
                           name="mla_attention")
    w_out = odd_w_out[0].astype(BF16)
    w_r, b_r = _router_weights(moe_w_rg[1], moe_b_rg[1], moe_w_re[1], moe_b_re[1])
    x_lat, route, cnt_rec, xs_local = _outproj(
        NT_LAT, o_g, o_m, w_out[:qw], w_out[qw:], vec(odd_b_out[0]),
        x_all, m1, vec(ln1_g[1]), vec(ln1_b[1]), w_r, b_r)
    plan = _moe_plan(cnt_rec, NT_LAT)
    ys = _moe_experts(1, plan, xs_local, moe_w1, moe_w3, moe_w2)
    x_lat = _moe_combine(NT_LAT, plan, ys, x_lat, route, m1, vec(ln2_g[1]), vec(ln2_b[1]))
    return x_lat.reshape(BATCH, SEQ, D_MODEL)
```

```python
import functools

import jax
import jax.numpy as jnp
from jax import lax
from jax.experimental import pallas as pl
from jax.experimental.pallas import tpu as pltpu

F32 = jnp.float32
BF16 = jnp.bfloat16

D_MODEL = 1024
BATCH = 4
SEQ = 4096
DEPTH = 2
GRID_W = 64
CTX_LEN = 256
HEAD_DIM = 64
ROPE_THETA = 10000.0
LN_EPS = 1e-5
RMS_EPS = 1e-6
NEG_INF = -1e30

CONV_CH = 512
CONV_WIDTH = 31
WIN_HEADS = 8
WIN_KV_HEADS = 2
WINDOW = 128
GQA_HEADS = 8
GQA_KV_HEADS = 2
MLA_HEADS = 8
MLA_Q_RANK = 256
MLA_KV_RANK = 128
MLA_NOPE = 64
MLA_ROPE = 32
MLA_V = 64
N_GROUPS = 4
EXP_PER_GROUP = 8
N_EXPERTS = N_GROUPS * EXP_PER_GROUP
EXPERT_FF = 512
DN_ALPHA = float((2 * DEPTH) ** 0.25)

EVEN_IN = 2 * CONV_CH + (WIN_HEADS + 2 * WIN_KV_HEADS) * HEAD_DIM
ODD_IN = 1184
ODD_IN_PAD = 1280
MLA_PAD = 128
VAL_PAD = 128
LOG2E = 1.4426950408889634

R_LAT = BATCH * SEQ
R_CTX = BATCH * CTX_LEN
R_ALL = R_LAT + R_CTX
TM = 256
NT_LAT = R_LAT // TM
NT_ALL = R_ALL // TM
TILES_PER_SEQ = SEQ // TM
HALO = 16
CONV_CHUNK = 32
TMM = 256
ROUTE_W = 128
RUN_ALIGN = 8
LOCAL_ROWS = 768
VMEM_LIMIT = 56 * 1024 * 1024

SH1, SC1, G1, SH2, SC2, G2 = range(6)


def _sigmoid(x):
    return 1.0 / (1.0 + jnp.exp(-x))


def _layer_norm(z, g, b):
    mu = jnp.mean(z, axis=-1, keepdims=True)
    zc = z - mu
    var = jnp.mean(zc * zc, axis=-1, keepdims=True)
    return zc * lax.rsqrt(var + LN_EPS) * g + b


def _rope(x, cos, sin, half):
    n = x.shape[-1]
    lane = lax.broadcasted_iota(jnp.int32, x.shape, 1)
    first = (lane % (2 * half)) < half
    partner = jnp.where(first, pltpu.roll(x, n - half, 1), pltpu.roll(x, half, 1))
    return x * cos + partner * sin


def _mod_row(i):
    return jnp.where(i < NT_LAT, i // TILES_PER_SEQ, BATCH)


def _mod_spec(chunk):
    return pl.BlockSpec((None, None, 1, D_MODEL), lambda i: (_mod_row(i), chunk, 0, 0))


def _rope_row_block(i):
    return jnp.where(i < NT_LAT, i % TILES_PER_SEQ, TILES_PER_SEQ)


def _full(shape):
    nd = len(shape)
    return pl.BlockSpec(shape, lambda *_: (0,) * nd)


def _params():
    return pltpu.CompilerParams(vmem_limit_bytes=VMEM_LIMIT)


def _ada_kernel(cv_ref, w_ref, b_ref, o_ref):
    cv = cv_ref[...]
    s = cv * _sigmoid(cv)
    o_ref[...] = jnp.dot(s, w_ref[...], precision=lax.Precision.HIGHEST,
                         preferred_element_type=F32) + b_ref[...]


def _ada_table(cv, ada_w, ada_b):
    bn = 1536
    nb = (6 * D_MODEL) // bn
    return pl.pallas_call(
        _ada_kernel,
        grid=(DEPTH, nb),
        in_specs=[pl.BlockSpec((8, D_MODEL), lambda l, j: (0, 0)),
                  pl.BlockSpec((None, D_MODEL, bn), lambda l, j: (l, 0, j)),
                  pl.BlockSpec((None, 1, bn), lambda l, j: (l, 0, j))],
        out_specs=pl.BlockSpec((None, 8, bn), lambda l, j: (l, 0, j)),
        out_shape=jax.ShapeDtypeStruct((DEPTH, 8, 6 * D_MODEL), F32),
        compiler_params=_params(),
        name="ada_table",
    )(cv, ada_w, ada_b.reshape(DEPTH, 1, 6 * D_MODEL))


def _rope_tables(rot_dim):
    axis_dim = rot_dim // 2
    inv_freq = ROPE_THETA ** (-jnp.arange(0, axis_dim, 2, dtype=F32) / axis_dim)
    t = jnp.arange(SEQ)
    ang_r = (t // GRID_W).astype(F32)[:, None] * inv_freq[None, :]
    ang_c = (t % GRID_W).astype(F32)[:, None] * inv_freq[None, :]
    cos = jnp.concatenate([jnp.cos(ang_r), jnp.cos(ang_r), jnp.cos(ang_c), jnp.cos(ang_c)], axis=-1)
    sin = jnp.concatenate([-jnp.sin(ang_r), jnp.sin(ang_r), -jnp.sin(ang_c), jnp.sin(ang_c)], axis=-1)
    return cos, sin


def _pad_table(cos, sin, lead, period, width):
    rot = cos.shape[1]
    one = jnp.ones((SEQ, period), F32).at[:, lead:lead + rot].set(cos)
    zero = jnp.zeros((SEQ, period), F32).at[:, lead:lead + rot].set(sin)
    cos_w = jnp.tile(one, (1, width // period))
    sin_w = jnp.tile(zero, (1, width // period))
    cos_w = jnp.concatenate([cos_w, jnp.ones((TM, width), F32)], axis=0)
    sin_w = jnp.concatenate([sin_w, jnp.zeros((TM, width), F32)], axis=0)
    return cos_w, sin_w


def _proj0_kernel(x_ref, sh_ref, sc_ref, w_ref, b_ref, cos_ref, sin_ref,
                  h_ref, q_ref, k_ref, v_ref):
    u = x_ref[...] * (1.0 + sc_ref[...]) + sh_ref[...]
    y = jnp.dot(u.astype(BF16), w_ref[...], preferred_element_type=F32) + b_ref[...]
    h_ref[...] = y[:, :CONV_CH] * _sigmoid(y[:, CONV_CH:2 * CONV_CH])
    cos = cos_ref[...]
    sin = sin_ref[...]
    q0 = 2 * CONV_CH
    k0 = q0 + WIN_HEADS * HEAD_DIM
    v0 = k0 + WIN_KV_HEADS * HEAD_DIM
    cos4 = jnp.concatenate([cos] * 4, axis=1)
    sin4 = jnp.concatenate([sin] * 4, axis=1)
    q = _rope(y[:, q0:k0], cos4, sin4, HEAD_DIM // 4) * (HEAD_DIM ** -0.5 * LOG2E)
    q_ref[...] = q.astype(BF16)
    k_ref[...] = _rope(y[:, k0:v0], cos, sin, HEAD_DIM // 4).astype(BF16)
    v_ref[...] = _values_with_ones(y[:, v0:].astype(BF16), WIN_KV_HEADS, HEAD_DIM)


def _proj0(x_all, mods, w_in, b_in, cos_hd, sin_hd):
    kvw = WIN_KV_HEADS * HEAD_DIM
    row = lambda w: pl.BlockSpec((TM, w), lambda i: (i, 0))
    tab = pl.BlockSpec((TM, 128), lambda i: (_rope_row_block(i), 0))
    return pl.pallas_call(
        _proj0_kernel,
        grid=(NT_ALL,),
        in_specs=[row(D_MODEL), _mod_spec(SH1), _mod_spec(SC1),
                  _full((D_MODEL, EVEN_IN)), _full((1, EVEN_IN)), tab, tab],
        out_specs=[row(CONV_CH), row(WIN_HEADS * HEAD_DIM), row(kvw), row(WIN_KV_HEADS * VAL_PAD)],
        out_shape=[jax.ShapeDtypeStruct((R_ALL, CONV_CH), F32),
                   jax.ShapeDtypeStruct((R_ALL, WIN_HEADS * HEAD_DIM), BF16),
                   jax.ShapeDtypeStruct((R_ALL, kvw), BF16),
                   jax.ShapeDtypeStruct((R_ALL, WIN_KV_HEADS * VAL_PAD), BF16)],
        compiler_params=_params(),
        name="proj0",
    )(x_all, mods, mods, w_in, b_in, cos_hd, sin_hd)


def _conv_kernel(prev_ref, cur_ref, next_ref, w_ref, cb_ref, g_ref, b_ref, o_ref, buf):
    i = pl.program_id(0)
    is_ctx = i >= NT_LAT
    first = jnp.logical_or(is_ctx, i % TILES_PER_SEQ == 0)
    last = jnp.logical_or(is_ctx, i % TILES_PER_SEQ == TILES_PER_SEQ - 1)
    buf[0:HALO, :] = jnp.where(first, 0.0, prev_ref[...])
    buf[HALO:HALO + TM, :] = cur_ref[...]
    buf[HALO + TM:, :] = jnp.where(last, 0.0, next_ref[...])
    off = HALO - CONV_WIDTH // 2
    for c in range(TM // CONV_CHUNK):
        r0 = c * CONV_CHUNK
        acc = jnp.zeros((CONV_CHUNK, CONV_CH), F32)
        for k in range(CONV_WIDTH):
            acc = acc + buf[r0 + off + k:r0 + off + k + CONV_CHUNK, :] * w_ref[k:k + 1, :]
        z = _layer_norm(acc + cb_ref[...], g_ref[...], b_ref[...])
        o_ref[r0:r0 + CONV_CHUNK, :] = (z * _sigmoid(z)).astype(BF16)


def _conv(h, conv_w, conv_b, ln_g, ln_b):
    nh = R_ALL // HALO
    per = TM // HALO
    vec = _full((1, CONV_CH))
    return pl.pallas_call(
        _conv_kernel,
        grid=(NT_ALL,),
        in_specs=[pl.BlockSpec((HALO, CONV_CH), lambda i: (jnp.maximum(i * per - 1, 0), 0)),
                  pl.BlockSpec((TM, CONV_CH), lambda i: (i, 0)),
                  pl.BlockSpec((HALO, CONV_CH), lambda i: (jnp.minimum((i + 1) * per, nh - 1), 0)),
                  _full((CONV_WIDTH, CONV_CH)), vec, vec, vec],
        out_specs=pl.BlockSpec((TM, CONV_CH), lambda i: (i, 0)),
        out_shape=jax.ShapeDtypeStruct((R_ALL, CONV_CH), BF16),
        scratch_shapes=[pltpu.VMEM((TM + 2 * HALO, CONV_CH), F32)],
        compiler_params=_params(),
        name="conv_module",
    )(h, h, h, conv_w, conv_b, ln_g, ln_b)


def _nt_dot(a, b):
    return lax.dot_general(a, b, (((1,), (1,)), ((), ())), preferred_element_type=F32)


def _values_with_ones(v, n_kv, dv):
    lane = lax.broadcasted_iota(jnp.int32, (v.shape[0], VAL_PAD - dv), 1)
    tail = jnp.where(lane == 0, 1.0, 0.0).astype(v.dtype)
    pieces = []
    for h in range(n_kv):
        pieces += [v[:, h * dv:(h + 1) * dv], tail]
    return jnp.concatenate(pieces, axis=1)


def _attend(units, dv):
    def scores(unit):
        q, ks, _, masks, _ = unit
        out = []
        for k, msk in zip(ks, masks):
            s = _nt_dot(q, k)
            if msk is not None:
                s = jnp.where(msk, s, NEG_INF)
            out.append(s)
        return out

    results = []
    ss = scores(units[0])
    for idx, unit in enumerate(units):
        nxt = scores(units[idx + 1]) if idx + 1 < len(units) else None
        _, _, vs, _, sink = unit
        m = functools.reduce(jnp.maximum, [jnp.max(s, axis=-1, keepdims=True) for s in ss])
        if sink is not None:
            m = jnp.maximum(m, sink)
        acc = functools.reduce(jnp.add, [jnp.dot(jnp.exp2(s - m).astype(BF16), v, preferred_element_type=F32)
                                         for s, v in zip(ss, vs)])
        l = acc[:, dv:dv + 1]
        if sink is not None:
            l = l + jnp.exp2(sink - m)
        results.append(acc[:, :dv] / l)
        ss = nxt
    return results


def _win_kernel(sink_ref, q_ref, kp_ref, kc_ref, kn_ref, kx_ref, vp_ref, vc_ref, vn_ref, vx_ref, o_ref):
    n = pl.program_id(1)
    k_loc = jnp.concatenate([kp_ref[...], kc_ref[...], kn_ref[...]], axis=0)
    v_loc = jnp.concatenate([vp_ref[...], vc_ref[...], vn_ref[...]], axis=0)
    k_ctx = kx_ref[...]
    v_ctx = vx_ref[...]
    qi = lax.broadcasted_iota(jnp.int32, (WINDOW, 3 * WINDOW), 0)
    kj = lax.broadcasted_iota(jnp.int32, (WINDOW, 3 * WINDOW), 1)
    k_pos = jnp.where(n < SEQ // WINDOW, kj + (n - 1) * WINDOW, SEQ)
    valid = jnp.where(kj >= qi, jnp.where(kj <= qi + 2 * WINDOW, 1, 0), 0)
    valid = jnp.where(k_pos >= 0, jnp.where(k_pos < SEQ, valid, 0), 0) > 0
    group = WIN_HEADS // WIN_KV_HEADS
    units = []
    for h in range(WIN_HEADS):
        kv = h // group
        ksl = slice(kv * HEAD_DIM, (kv + 1) * HEAD_DIM)
        vsl = slice(kv * VAL_PAD, (kv + 1) * VAL_PAD)
        units.append((q_ref[:, h * HEAD_DIM:(h + 1) * HEAD_DIM], [k_ctx[:, ksl], k_loc[:, ksl]],
                      [v_ctx[:, vsl], v_loc[:, vsl]], [None, valid], sink_ref[h] * LOG2E))
    for h, o in enumerate(_attend(units, HEAD_DIM)):
        o_ref[:, h * HEAD_DIM:(h + 1) * HEAD_DIM] = o.astype(BF16)


def _win_attention(sink, q, k, v):
    nblk = SEQ // WINDOW
    cblk = CTX_LEN // WINDOW
    kvw = WIN_KV_HEADS * HEAD_DIM
    ctx0 = R_LAT // CTX_LEN
    lat = lambda n: jnp.minimum(n, nblk - 1)
    prev = lambda b, n: (b * nblk + jnp.maximum(lat(n) - 1, 0), 0)
    cur = lambda b, n: (b * nblk + lat(n), 0)
    nxt = lambda b, n: (b * nblk + jnp.minimum(lat(n) + 1, nblk - 1), 0)
    qrow = lambda b, n: (jnp.where(n < nblk, b * nblk + n, R_LAT // WINDOW + b * cblk + n - nblk), 0)
    ctx = lambda b, n: (ctx0 + b, 0)
    vw = WIN_KV_HEADS * VAL_PAD
    kvb = lambda f, w: pl.BlockSpec((WINDOW, w), f)
    cxb = lambda w: pl.BlockSpec((CTX_LEN, w), ctx)
    return pl.pallas_call(
        _win_kernel,
        grid=(BATCH, nblk + cblk),
        in_specs=[pl.BlockSpec(memory_space=pltpu.SMEM),
                  pl.BlockSpec((WINDOW, WIN_HEADS * HEAD_DIM), qrow),
                  kvb(prev, kvw), kvb(cur, kvw), kvb(nxt, kvw), cxb(kvw),
                  kvb(prev, vw), kvb(cur, vw), kvb(nxt, vw), cxb(vw)],
        out_specs=pl.BlockSpec((WINDOW, WIN_HEADS * HEAD_DIM), qrow),
        out_shape=jax.ShapeDtypeStruct((R_ALL, WIN_HEADS * HEAD_DIM), BF16),
        compiler_params=_params(),
        name="window_attention",
    )(sink, q, k, k, k, k, v, v, v, v)


def _outproj_kernel(a_ref, b_ref, wa_ref, wb_ref, bo_ref, x_ref, g1_ref, sh2_ref, sc2_ref,
                    lng_ref, lnb_ref, wrh_ref, wrl_ref, br_ref, upper_ref, lower_ref, sel_ref,
                    xo_ref, route_ref, cnt_ref, xs_ref):
    y = (jnp.dot(a_ref[...], wa_ref[...], preferred_element_type=F32)
         + jnp.dot(b_ref[...], wb_ref[...], preferred_element_type=F32) + bo_ref[...])
    xn = _layer_norm(DN_ALPHA * x_ref[...] + (1.0 + g1_ref[...]) * y, lng_ref[...], lnb_ref[...])
    xo_ref[...] = xn
    u2 = xn * (1.0 + sc2_ref[...]) + sh2_ref[...]
    u_hi = u2.astype(BF16)
    u_lo = (u2 - u_hi.astype(F32)).astype(BF16)
    logits = (jnp.dot(u_hi, wrh_ref[...], preferred_element_type=F32)
              + jnp.dot(u_lo, wrh_ref[...], preferred_element_type=F32)
              + jnp.dot(u_hi, wrl_ref[...], preferred_element_type=F32) + br_ref[...])
    lane = lax.broadcasted_iota(jnp.int32, logits.shape, 1).astype(F32)
    ninf = -jnp.inf
    big = float(ROUTE_W)
    gl = jnp.where(lane < N_GROUPS, logits, ninf)
    gmax = jnp.max(gl, axis=-1, keepdims=True)
    gidx = jnp.min(jnp.where(gl == gmax, lane, big), axis=-1, keepdims=True)
    g_w = 1.0 / jnp.sum(jnp.exp(gl - gmax), axis=-1, keepdims=True)
    lo = N_GROUPS + EXP_PER_GROUP * gidx
    el = jnp.where(lane >= lo, jnp.where(lane < lo + EXP_PER_GROUP, logits, ninf), ninf)
    v1 = jnp.max(el, axis=-1, keepdims=True)
    i1 = jnp.min(jnp.where(el == v1, lane, big), axis=-1, keepdims=True)
    el2 = jnp.where(lane == i1, ninf, el)
    v2 = jnp.max(el2, axis=-1, keepdims=True)
    i2 = jnp.min(jnp.where(el2 == v2, lane, big), axis=-1, keepdims=True)
    e2 = jnp.exp(v2 - v1)
    w1 = g_w / (1.0 + e2)
    w2 = g_w * e2 / (1.0 + e2)
    onehot = [jnp.where(lane == i1 - N_GROUPS, 1.0, 0.0), jnp.where(lane == i2 - N_GROUPS, 1.0, 0.0)]
    cnt = [jnp.sum(o, axis=0, keepdims=True) for o in onehot]
    run_units = jnp.floor((cnt[0] + cnt[1] + (RUN_ALIGN - 1)) * (1.0 / RUN_ALIGN))
    below = RUN_ALIGN * jnp.dot(jnp.broadcast_to(run_units, (8, ROUTE_W)).astype(BF16), upper_ref[...],
                                preferred_element_type=F32)[0:1]
    lower = lower_ref[...]
    base = [below, below + cnt[0]]
    lp = []
    for s in range(2):
        earlier = jnp.dot(lower, onehot[s].astype(BF16), preferred_element_type=F32)
        lp.append(jnp.sum(onehot[s] * (base[s] + earlier), axis=-1, keepdims=True))
    rec = jnp.where(lane == 0.0, i1 - N_GROUPS,
                    jnp.where(lane == 1.0, i2 - N_GROUPS,
                              jnp.where(lane == 2.0, w1,
                                        jnp.where(lane == 3.0, w2,
                                                  jnp.where(lane == 4.0, lp[0],
                                                            jnp.where(lane == 5.0, lp[1], 0.0))))))
    route_ref[...] = rec
    cnt_ref[...] = jnp.broadcast_to(run_units * RUN_ALIGN, (8, ROUTE_W))
    sel = sel_ref[...]
    pos = lax.broadcasted_iota(jnp.int32, (LOCAL_ROWS, TM), 0).astype(F32)
    lp_lanes = []
    for s in range(2):
        hi = jnp.floor(lp[s] * (1.0 / 256.0))
        parts = jnp.where(lane == 0.0, lp[s] - 256.0 * hi, jnp.where(lane == 1.0, hi, 0.0)).astype(BF16)
        t = _nt_dot(sel, parts)
        lp_lanes.append(t[0:1] + 256.0 * t[1:2])
    perm = jnp.where(pos == lp_lanes[0], 1.0, jnp.where(pos == lp_lanes[1], 1.0, 0.0)).astype(BF16)
    xs_ref[...] = jnp.dot(perm, u_hi, preferred_element_type=F32)


def _outproj(n_tiles, mix_a, mix_b, w_a, w_b, b_out, x_all, mods, ln_g, ln_b, w_r, b_r):
    rows = n_tiles * TM
    half = mix_a.shape[1]
    w_rh = w_r.astype(BF16)
    w_rl = (w_r - w_rh.astype(F32)).astype(BF16)
    upper = (jnp.arange(ROUTE_W)[:, None] < jnp.arange(ROUTE_W)[None, :]).astype(BF16)
    lower = (jnp.arange(TM)[:, None] > jnp.arange(TM)[None, :]).astype(BF16)
    sel = (jnp.arange(8)[:, None] == jnp.arange(ROUTE_W)[None, :]).astype(BF16)
    row = lambda w: pl.BlockSpec((TM, w), lambda i: (i, 0))
    vec = _full((1, D_MODEL))
    return pl.pallas_call(
        _outproj_kernel,
        grid=(n_tiles,),
        in_specs=[row(half), row(half), _full((half, D_MODEL)), _full((half, D_MODEL)), vec,
                  row(D_MODEL), _mod_spec(G1), _mod_spec(SH2), _mod_spec(SC2), vec, vec,
                  _full((D_MODEL, ROUTE_W)), _full((D_MODEL, ROUTE_W)), _full((1, ROUTE_W)),
                  _full((ROUTE_W, ROUTE_W)), _full((TM, TM)), _full((8, ROUTE_W))],
        out_specs=[row(D_MODEL), row(ROUTE_W), pl.BlockSpec((8, ROUTE_W), lambda i: (i, 0)),
                   pl.BlockSpec((LOCAL_ROWS, D_MODEL), lambda i: (i, 0))],
        out_shape=[jax.ShapeDtypeStruct((rows, D_MODEL), F32),
                   jax.ShapeDtypeStruct((rows, ROUTE_W), F32),
                   jax.ShapeDtypeStruct((n_tiles * 8, ROUTE_W), F32),
                   jax.ShapeDtypeStruct((n_tiles * LOCAL_ROWS, D_MODEL), F32)],
        compiler_params=_params(),
        name="outproj_ln_router",
    )(mix_a, mix_b, w_a, w_b, b_out, x_all, mods, mods, mods, ln_g, ln_b, w_rh, w_rl, b_r, upper, lower, sel)


def _aligned(i):
    return pl.multiple_of(i, RUN_ALIGN)


def _moe_kernel(te_ref, tk_ref, rows_ref, lo_ref, hi_ref, cnt_ref, bt_ref, be_ref, xs_hbm, w1_ref, w3_ref,
                w2_ref, ys_ref, xbuf, wb1, wb3, wb2, sem):
    j = pl.program_id(0)
    nt = pl.num_programs(0)
    slot = j % 2

    def issue(tile, slot_):
        e = te_ref[tile]
        first = tk_ref[tile] * TMM

        def body(i, carry):
            idx = i * N_EXPERTS + e
            start = bt_ref[idx]
            lo = jnp.maximum(start, first)
            n = jnp.minimum(start + cnt_ref[idx], first + TMM) - lo

            @pl.when(n > 0)
            def _():
                src = i * LOCAL_ROWS + be_ref[idx] + lo - start
                pltpu.make_async_copy(xs_hbm.at[pl.ds(_aligned(src), _aligned(n))],
                                      xbuf.at[slot_, pl.ds(_aligned(lo - first), _aligned(n))],
                                      sem.at[slot_]).start()
            return carry
        lax.fori_loop(lo_ref[tile], hi_ref[tile], body, 0)

    @pl.when(j == 0)
    def _():
        xbuf[...] = jnp.zeros_like(xbuf)
        issue(0, 0)

    @pl.when(j + 1 < nt)
    def _():
        issue(j + 1, 1 - slot)

    @pl.when(jnp.logical_or(j == 0, te_ref[j] != te_ref[jnp.maximum(j - 1, 0)]))
    def _():
        wb1[...] = w1_ref[...].astype(BF16)
        wb3[...] = w3_ref[...].astype(BF16)
        wb2[...] = w2_ref[...].astype(BF16)

    n_real = rows_ref[j]

    @pl.when(n_real > 0)
    def _():
        pltpu.make_async_copy(xs_hbm.at[pl.ds(0, _aligned(n_real))], xbuf.at[slot, pl.ds(0, _aligned(n_real))],
                              sem.at[slot]).wait()
        x = xbuf[slot].astype(BF16)
        h1 = jnp.dot(x, wb1[...], preferred_element_type=F32)
        h3 = jnp.dot(x, wb3[...], preferred_element_type=F32)
        hid = h1 * _sigmoid(h1) * h3
        ys_ref[...] = jnp.dot(hid.astype(BF16), wb2[...], preferred_element_type=F32)

    @pl.when(n_real == 0)
    def _():
        ys_ref[...] = jnp.zeros_like(ys_ref)


def _moe_experts(layer, plan, xs_local, w1, w3, w2):
    tile_expert, tile_k, tile_rows, src_lo, src_hi, cnt, before_tile, before_expert, _, _ = plan
    nt = tile_expert.shape[0]
    wmap = lambda j, te, *_: (layer, te[j], 0, 0)
    grid_spec = pltpu.PrefetchScalarGridSpec(
        num_scalar_prefetch=8,
        grid=(nt,),
        in_specs=[pl.BlockSpec(memory_space=pl.ANY),
                  pl.BlockSpec((None, None, D_MODEL, EXPERT_FF), wmap),
                  pl.BlockSpec((None, None, D_MODEL, EXPERT_FF), wmap),
                  pl.BlockSpec((None, None, EXPERT_FF, D_MODEL), wmap)],
        out_specs=pl.BlockSpec((TMM, D_MODEL), lambda j, *_: (j, 0)),
        scratch_shapes=[pltpu.VMEM((2, TMM, D_MODEL), F32),
                        pltpu.VMEM((D_MODEL, EXPERT_FF), BF16),
                        pltpu.VMEM((D_MODEL, EXPERT_FF), BF16),
                        pltpu.VMEM((EXPERT_FF, D_MODEL), BF16),
                        pltpu.SemaphoreType.DMA((2,))])
    return pl.pallas_call(
        _moe_kernel,
        grid_spec=grid_spec,
        out_shape=jax.ShapeDtypeStruct((nt * TMM, D_MODEL), F32),
        compiler_params=_params(),
        name="moe_experts",
    )(tile_expert, tile_k, tile_rows, src_lo, src_hi, cnt, before_tile, before_expert, xs_local, w1, w3, w2)


def _combine_kernel(cnt_ref, bt_ref, be_ref, gs_ref, used_ref, ys_hbm, x_ref, route_ref, g2_ref, lng_ref, lnb_ref,
                    o_ref, ybuf, sem):
    i = pl.program_id(0)
    nt = pl.num_programs(0)
    slot = i % 2

    def issue(tile, slot_):
        def body(e, carry):
            idx = tile * N_EXPERTS + e
            n = cnt_ref[idx]

            @pl.when(n > 0)
            def _():
                pltpu.make_async_copy(ys_hbm.at[pl.ds(_aligned(gs_ref[e] + bt_ref[idx]), _aligned(n))],
                                      ybuf.at[slot_, pl.ds(_aligned(be_ref[idx]), _aligned(n))],
                                      sem.at[slot_]).start()
            return carry
        lax.fori_loop(0, N_EXPERTS, body, 0)

    @pl.when(i == 0)
    def _():
        ybuf[...] = jnp.zeros_like(ybuf)
        issue(0, 0)

    @pl.when(i + 1 < nt)
    def _():
        issue(i + 1, 1 - slot)

    used = _aligned(used_ref[i])
    pltpu.make_async_copy(ys_hbm.at[pl.ds(0, used)], ybuf.at[slot, pl.ds(0, used)], sem.at[slot]).wait()
    route = route_ref[...]
    pos = lax.broadcasted_iota(jnp.int32, (TM, LOCAL_ROWS), 1).astype(F32)
    sel = (jnp.where(pos == route[:, 4:5], route[:, 2:3], 0.0)
           + jnp.where(pos == route[:, 5:6], route[:, 3:4], 0.0))
    sel_hi = sel.astype(BF16)
    sel_lo = (sel - sel_hi.astype(F32)).astype(BF16)
    y = ybuf[slot]
    y_hi = y.astype(BF16)
    y_lo = (y - y_hi.astype(F32)).astype(BF16)
    f = (jnp.dot(sel_hi, y_hi, preferred_element_type=F32) + jnp.dot(sel_lo, y_hi, preferred_element_type=F32)
         + jnp.dot(sel_hi, y_lo, preferred_element_type=F32))
    z = DN_ALPHA * x_ref[...] + (1.0 + g2_ref[...]) * f
    o_ref[...] = _layer_norm(z, lng_ref[...], lnb_ref[...])


def _moe_combine(n_tiles, plan, ys, x_all, route, mods, ln_g, ln_b):
    _, _, _, _, _, cnt, before_tile, before_expert, group_start, used = plan
    rows = n_tiles * TM
    row = lambda w: pl.BlockSpec((TM, w), lambda i, *_: (i, 0))
    vec = pl.BlockSpec((1, D_MODEL), lambda i, *_: (0, 0))
    grid_spec = pltpu.PrefetchScalarGridSpec(
        num_scalar_prefetch=5,
        grid=(n_tiles,),
        in_specs=[pl.BlockSpec(memory_space=pl.ANY), row(D_MODEL), row(ROUTE_W),
                  pl.BlockSpec((None, None, 1, D_MODEL), lambda i, *_: (_mod_row(i), G2, 0, 0)), vec, vec],
        out_specs=row(D_MODEL),
        scratch_shapes=[pltpu.VMEM((2, LOCAL_ROWS, D_MODEL), F32), pltpu.SemaphoreType.DMA((2,))])
    return pl.pallas_call(
        _combine_kernel,
        grid_spec=grid_spec,
        out_shape=jax.ShapeDtypeStruct((rows, D_MODEL), F32),
        compiler_params=_params(),
        name="moe_combine_ln",
    )(cnt, before_tile, before_expert, group_start, used, ys, x_all, route, mods, ln_g, ln_b)


def _moe_plan(cnt_rec, n_tiles):
    cnt = cnt_rec.reshape(n_tiles, 8, ROUTE_W)[:, 0, :N_EXPERTS].astype(jnp.int32)
    nt_max = (n_tiles * (2 * TM + N_EXPERTS * (RUN_ALIGN - 1))) // TMM + N_EXPERTS
    total = jnp.sum(cnt, axis=0)
    tiles_e = (total + TMM - 1) // TMM
    tile_end = jnp.cumsum(tiles_e)
    first_tile = tile_end - tiles_e
    before_tile = jnp.cumsum(cnt, axis=0) - cnt
    before_expert = jnp.cumsum(cnt, axis=1) - cnt
    tile_id = jnp.arange(nt_max, dtype=jnp.int32)
    tile_expert = jnp.minimum(jnp.sum((tile_id[:, None] >= tile_end[None, :]).astype(jnp.int32), axis=1),
                              N_EXPERTS - 1)
    tile_k = tile_id - first_tile[tile_expert]
    tile_rows = jnp.clip(total[tile_expert] - tile_k * TMM, 0, TMM)
    first = (tile_k * TMM)[:, None]
    run_start = before_tile.T[tile_expert]
    run_end = run_start + cnt.T[tile_expert]
    src_lo = jnp.sum((run_end <= first).astype(jnp.int32), axis=1)
    src_hi = jnp.sum((run_start < first + TMM).astype(jnp.int32), axis=1)
    return (tile_expert, tile_k, tile_rows, src_lo, src_hi, cnt.reshape(-1), before_tile.reshape(-1),
            before_expert.reshape(-1), first_tile * TMM, jnp.sum(cnt, axis=1))


def _router_weights(w_rg, b_rg, w_re, b_re):
    w = jnp.concatenate([w_rg, jnp.transpose(w_re, (1, 0, 2)).reshape(D_MODEL, N_EXPERTS)], axis=1)
    b = jnp.concatenate([b_rg, b_re.reshape(-1)])
    pad = ROUTE_W - w.shape[1]
    return jnp.pad(w, ((0, 0), (0, pad))), jnp.pad(b, (0, pad)).reshape(1, ROUTE_W)


def _proj1_kernel(x_ref, sh_ref, sc_ref, w_ref, b_ref, cos_ref, sin_ref, cosm_ref, sinm_ref, cosr_ref,
                  sinr_ref, gq_ref, gk_ref, gqc_ref, gkv_ref, avg_ref, wuq_ref, wuk_ref, wuv_ref,
                  q_ref, qm_ref, k_ref, v_ref, km_ref, vm_ref):
    u = x_ref[...] * (1.0 + sc_ref[...]) + sh_ref[...]
    y = jnp.dot(u.astype(BF16), w_ref[...], preferred_element_type=F32) + b_ref[...]
    c_q = GQA_HEADS * HEAD_DIM
    c_qc = c_q + MLA_Q_RANK
    c_k = c_qc + GQA_KV_HEADS * HEAD_DIM
    c_v = c_k + GQA_KV_HEADS * HEAD_DIM
    c_kv = c_v + MLA_KV_RANK
    avg = avg_ref[...]

    def head_rms(t, gain):
        sq = t * t
        hi = sq.astype(BF16)
        lo = (sq - hi.astype(F32)).astype(BF16)
        a = avg[:t.shape[1], :t.shape[1]]
        ms = jnp.dot(hi, a, preferred_element_type=F32) + jnp.dot(lo, a, preferred_element_type=F32)
        return t * lax.rsqrt(ms + RMS_EPS) * gain

    def row_rms(t, gain):
        ms = jnp.mean(t * t, axis=-1, keepdims=True)
        return t * lax.rsqrt(ms + RMS_EPS) * gain

    cos = cos_ref[...]
    sin = sin_ref[...]
    cos4 = jnp.concatenate([cos] * 4, axis=1)
    sin4 = jnp.concatenate([sin] * 4, axis=1)
    q = _rope(head_rms(y[:, :c_q], gq_ref[...]), cos4, sin4, HEAD_DIM // 4) * (HEAD_DIM ** -0.5 * LOG2E)
    q_ref[...] = q.astype(BF16)
    k = _rope(head_rms(y[:, c_qc:c_k], gk_ref[...]), cos, sin, HEAD_DIM // 4)
    k_ref[...] = k.astype(BF16)
    v_ref[...] = _values_with_ones(y[:, c_k:c_v].astype(BF16), GQA_KV_HEADS, HEAD_DIM)

    qc = row_rms(y[:, c_q:c_qc], gqc_ref[...]).astype(BF16)
    qm = jnp.dot(qc, wuq_ref[...], preferred_element_type=F32)
    cosm = jnp.concatenate([cosm_ref[...]] * MLA_HEADS, axis=1)
    sinm = jnp.concatenate([sinm_ref[...]] * MLA_HEADS, axis=1)
    qm = _rope(qm, cosm, sinm, MLA_ROPE // 4) * ((MLA_NOPE + MLA_ROPE) ** -0.5 * LOG2E)
    qm_ref[...] = qm.astype(BF16)

    kvn = row_rms(y[:, c_v:c_kv], gkv_ref[...]).astype(BF16)
    kr = _rope(y[:, c_kv:], cosr_ref[...], sinr_ref[...], MLA_ROPE // 4).astype(BF16)
    km = jnp.dot(jnp.concatenate([kvn, kr], axis=1), wuk_ref[...], preferred_element_type=F32)
    km_ref[...] = km.astype(BF16)
    vm = jnp.dot(kvn, wuv_ref[...], preferred_element_type=F32)
    lane = lax.broadcasted_iota(jnp.int32, vm.shape, 1)
    vm_ref[...] = jnp.where(lane % VAL_PAD == MLA_V, 1.0, vm).astype(BF16)


def _proj1(x_all, mods, w_in, b_in, tabs, gq, gk, gqc, gkv, avg, wuq, wuk, wuv):
    cos_hd, sin_hd, cos_m, sin_m, cos_r, sin_r = tabs
    kvw = GQA_KV_HEADS * HEAD_DIM
    qw = GQA_HEADS * HEAD_DIM
    mw = MLA_HEADS * MLA_PAD
    vw = MLA_HEADS * VAL_PAD
    gvw = GQA_KV_HEADS * VAL_PAD
    row = lambda w: pl.BlockSpec((TM, w), lambda i: (i, 0))
    tab = pl.BlockSpec((TM, 128), lambda i: (_rope_row_block(i), 0))
    return pl.pallas_call(
        _proj1_kernel,
        grid=(NT_ALL,),
        in_specs=[row(D_MODEL), _mod_spec(SH1), _mod_spec(SC1),
                  _full((D_MODEL, ODD_IN_PAD)), _full((1, ODD_IN_PAD)), tab, tab, tab, tab, tab, tab,
                  _full((1, qw)), _full((1, kvw)), _full((1, MLA_Q_RANK)), _full((1, MLA_KV_RANK)),
                  _full((qw, qw)), _full((MLA_Q_RANK, mw)), _full((MLA_KV_RANK + 128, mw)),
                  _full((MLA_KV_RANK, vw))],
        out_specs=[row(qw), row(mw), row(kvw), row(gvw), row(mw), row(vw)],
        out_shape=[jax.ShapeDtypeStruct((R_ALL, qw), BF16),
                   jax.ShapeDtypeStruct((R_ALL, mw), BF16),
                   jax.ShapeDtypeStruct((R_ALL, kvw), BF16),
                   jax.ShapeDtypeStruct((R_ALL, gvw), BF16),
                   jax.ShapeDtypeStruct((R_ALL, mw), BF16),
                   jax.ShapeDtypeStruct((R_ALL, vw), BF16)],
        compiler_params=_params(),
        name="proj1",
    )(x_all, mods, mods, w_in, b_in, cos_hd, sin_hd, cos_m, sin_m, cos_r, sin_r,
      gq, gk, gqc, gkv, avg, wuq, wuk, wuv)


def _dense_kernel(q_ref, kl_ref, kc_ref, vl_ref, vc_ref, o_ref, *, n_heads, group, dk, dv):
    tq = q_ref.shape[0]
    units = []
    for kv in range(n_heads // group):
        qs = [q_ref[:, h * dk:(h + 1) * dk] for h in range(kv * group, (kv + 1) * group)]
        q = qs[0] if group == 1 else jnp.concatenate(qs, axis=0)
        ks = slice(kv * dk, (kv + 1) * dk)
        vs = slice(kv * VAL_PAD, (kv + 1) * VAL_PAD)
        units.append((q, [kl_ref[:, ks], kc_ref[:, ks]], [vl_ref[:, vs], vc_ref[:, vs]], [None, None], None))
    for kv, o in enumerate(_attend(units, dv)):
        for g in range(group):
            h = kv * group + g
            o_ref[:, h * dv:(h + 1) * dv] = o[g * tq:(g + 1) * tq].astype(BF16)


def _dense_attention(q, k, v, *, n_heads, group, dk, dv, tq, name):
    n_kv = n_heads // group
    nq = SEQ // tq
    ctx0 = R_LAT // CTX_LEN
    lat = lambda w: pl.BlockSpec((SEQ, w), lambda b, j: (b, 0), pipeline_mode=pl.Buffered(1))
    ctx = lambda w: pl.BlockSpec((CTX_LEN, w), lambda b, j: (ctx0 + b, 0))
    return pl.pallas_call(
        functools.partial(_dense_kernel, n_heads=n_heads, group=group, dk=dk, dv=dv),
        grid=(BATCH, nq),
        in_specs=[pl.BlockSpec((tq, n_heads * dk), lambda b, j: (b * nq + j, 0)),
                  lat(n_kv * dk), ctx(n_kv * dk), lat(n_kv * VAL_PAD), ctx(n_kv * VAL_PAD)],
        out_specs=pl.BlockSpec((tq, n_heads * dv), lambda b, j: (b * nq + j, 0)),
        out_shape=jax.ShapeDtypeStruct((R_LAT, n_heads * dv), BF16),
        compiler_params=_params(),
        name=name,
    )(q, k, k, v, v)


def _mla_weights(w_uq, w_ukv):
    wq = w_uq.reshape(MLA_Q_RANK, MLA_HEADS, MLA_NOPE + MLA_ROPE)
    wq = jnp.pad(wq, ((0, 0), (0, 0), (0, MLA_PAD - MLA_NOPE - MLA_ROPE))).reshape(MLA_Q_RANK, -1)
    wkv = w_ukv.reshape(MLA_KV_RANK, MLA_HEADS, MLA_NOPE + MLA_V)
    wk = jnp.pad(wkv[:, :, :MLA_NOPE], ((0, 0), (0, 0), (0, MLA_PAD - MLA_NOPE))).reshape(MLA_KV_RANK, -1)
    wv = jnp.pad(wkv[:, :, MLA_NOPE:], ((0, 0), (0, 0), (0, VAL_PAD - MLA_V))).reshape(MLA_KV_RANK, -1)
    r = jnp.arange(128)[:, None]
    c = jnp.arange(MLA_HEADS * MLA_PAD)[None, :]
    place = jnp.logical_and(r < MLA_ROPE, (c % MLA_PAD) == MLA_NOPE + r).astype(F32)
    wk = jnp.concatenate([wk, place], axis=0)
    return wq.astype(BF16), wk.astype(BF16), wv.astype(BF16)


def kernel(x, c, ctx, c_ctx, even_w_in, even_b_in, even_conv_w, even_conv_b, even_conv_ln_g, even_conv_ln_b, even_sink, even_w_out, even_b_out, odd_w_in, odd_b_in, odd_q_norm, odd_k_norm, odd_mla_q_norm, odd_mla_kv_norm, odd_mla_w_uq, odd_mla_w_ukv, odd_w_out, odd_b_out, ada_w, ada_b, ln1_g, ln1_b, ln2_g, ln2_b, moe_w_rg, moe_b_rg, moe_w_re, moe_b_re, moe_w1, moe_w3, moe_w2):
    vec = lambda a: a.reshape(1, -1)
    x_all = jnp.concatenate([x.reshape(R_LAT, D_MODEL), ctx.reshape(R_CTX, D_MODEL)], axis=0)

    cv = jnp.concatenate([c, c_ctx[None, :], jnp.zeros((8 - BATCH - 1, D_MODEL), F32)], axis=0)
    mods = _ada_table(cv, ada_w, ada_b).reshape(DEPTH, 8, 6, 1, D_MODEL)

    cos64, sin64 = _rope_tables(HEAD_DIM)
    cos_hd, sin_hd = _pad_table(cos64, sin64, 0, HEAD_DIM, 128)
    cos32, sin32 = _rope_tables(MLA_ROPE)
    cos_m, sin_m = _pad_table(cos32, sin32, MLA_NOPE, MLA_PAD, 128)
    cos_r, sin_r = _pad_table(cos32, sin32, 0, 128, 128)

    m0 = mods[0]
    h, q0, k0, v0 = _proj0(x_all, m0, even_w_in[0].astype(BF16), vec(even_b_in[0]), cos_hd, sin_hd)
    conv_out = _conv(h, even_conv_w[0].reshape(CONV_WIDTH, CONV_CH), vec(even_conv_b[0]),
                     vec(even_conv_ln_g[0]), vec(even_conv_ln_b[0]))
    attn = _win_attention(even_sink[0], q0, k0, v0)
    w_out = even_w_out[0].astype(BF16)
    w_r, b_r = _router_weights(moe_w_rg[0], moe_b_rg[0], moe_w_re[0], moe_b_re[0])
    x_all, route, cnt_rec, xs_local = _outproj(
        NT_ALL, conv_out, attn, w_out[:CONV_CH], w_out[CONV_CH:], vec(even_b_out[0]),
        x_all, m0, vec(ln1_g[0]), vec(ln1_b[0]), w_r, b_r)
    plan = _moe_plan(cnt_rec, NT_ALL)
    ys = _moe_experts(0, plan, xs_local, moe_w1, moe_w3, moe_w2)
    x_all = _moe_combine(NT_ALL, plan, ys, x_all, route, m0, vec(ln2_g[0]), vec(ln2_b[0]))

    m1 = mods[1]
    w_in1 = jnp.pad(odd_w_in[0], ((0, 0), (0, ODD_IN_PAD - ODD_IN))).astype(BF16)
    b_in1 = jnp.pad(odd_b_in[0], (0, ODD_IN_PAD - ODD_IN)).reshape(1, -1)
    wuq, wuk, wuv = _mla_weights(odd_mla_w_uq[0], odd_mla_w_ukv[0])
    qw = GQA_HEADS * HEAD_DIM
    hid = jnp.arange(qw) // HEAD_DIM
    avg = ((hid[:, None] == hid[None, :]).astype(F32) / HEAD_DIM).astype(BF16)
    q1, qm, k1, v1, km, vm = _proj1(
        x_all, m1, w_in1, b_in1, (cos_hd, sin_hd, cos_m, sin_m, cos_r, sin_r),
        vec(jnp.tile(odd_q_norm[0], GQA_HEADS)), vec(jnp.tile(odd_k_norm[0], GQA_KV_HEADS)),
        vec(odd_mla_q_norm[0]), vec(odd_mla_kv_norm[0]), avg, wuq, wuk, wuv)
    o_g = _dense_attention(q1, k1, v1, n_heads=GQA_HEADS, group=GQA_HEADS // GQA_KV_HEADS,
                           dk=HEAD_DIM, dv=HEAD_DIM, tq=128, name="gqa_attention")
    o_m = _dense_attention(qm, km, vm, n_heads=MLA_HEADS, group=1, dk=MLA_PAD, dv=MLA_V, tq=256,
                           name="mla_attention")
    w_out = odd_w_out[0].astype(BF16)
    w_r, b_r = _router_weights(moe_w_rg[1], moe_b_rg[1], moe_w_re[1], moe_b_re[1])
    x_lat, route, cnt_rec, xs_local = _outproj(
        NT_LAT, o_g, o_m, w_out[:qw], w_out[qw:], vec(odd_b_out[0]),
        x_all, m1, vec(ln1_g[1]), vec(ln1_b[1]), w_r, b_r)
    plan = _moe_plan(cnt_rec, NT_LAT)
    ys = _moe_experts(1, plan, xs_local, moe_w1, moe_w3, moe_w2)
    x_lat = _moe_combine(NT_LAT, plan, ys, x_lat, route, m1, vec(ln2_g[1]), vec(ln2_b[1]))
    return x_lat.reshape(BATCH, SEQ, D_MODEL)
```

```python
import functools

import jax
import jax.numpy as jnp
from jax import lax
from jax.experimental import pallas as pl
from jax.experimental.pallas import tpu as pltpu

F32 = jnp.float32
BF16 = jnp.bfloat16

D_MODEL = 1024
BATCH = 4
SEQ = 4096
DEPTH = 2
GRID_W = 64
CTX_LEN = 256
HEAD_DIM = 64
ROPE_THETA = 10000.0
LN_EPS = 1e-5
RMS_EPS = 1e-6
NEG_INF = -1e30

CONV_CH = 512
CONV_WIDTH = 31
WIN_HEADS = 8
WIN_KV_HEADS = 2
WINDOW = 128
GQA_HEADS = 8
GQA_KV_HEADS = 2
MLA_HEADS = 8
MLA_Q_RANK = 256
MLA_KV_RANK = 128
MLA_NOPE = 64
MLA_ROPE = 32
MLA_V = 64
N_GROUPS = 4
EXP_PER_GROUP = 8
N_EXPERTS = N_GROUPS * EXP_PER_GROUP
EXPERT_FF = 512
DN_ALPHA = float((2 * DEPTH) ** 0.25)

EVEN_IN = 2 * CONV_CH + (WIN_HEADS + 2 * WIN_KV_HEADS) * HEAD_DIM
ODD_IN = 1184
ODD_IN_PAD = 1280
MLA_PAD = 128
VAL_PAD = 128
LOG2E = 1.4426950408889634

R_LAT = BATCH * SEQ
R_CTX = BATCH * CTX_LEN
R_ALL = R_LAT + R_CTX
TM = 256
NT_LAT = R_LAT // TM
NT_ALL = R_ALL // TM
TILES_PER_SEQ = SEQ // TM
HALO = 16
CONV_CHUNK = 32
SHIFTS = 8
TMM = 256
ROUTE_W = 128
RUN_ALIGN = 8
LOCAL_ROWS = 768
VMEM_LIMIT = 56 * 1024 * 1024

SH1, SC1, G1, SH2, SC2, G2 = range(6)


def _sigmoid(x):
    return 1.0 / (1.0 + jnp.exp(-x))


def _layer_norm(z, g, b):
    mu = jnp.mean(z, axis=-1, keepdims=True)
    zc = z - mu
    var = jnp.mean(zc * zc, axis=-1, keepdims=True)
    return zc * lax.rsqrt(var + LN_EPS) * g + b


def _rope(x, cos, sin, half):
    n = x.shape[-1]
    lane = lax.broadcasted_iota(jnp.int32, x.shape, 1)
    first = (lane % (2 * half)) < half
    partner = jnp.where(first, pltpu.roll(x, n - half, 1), pltpu.roll(x, half, 1))
    return x * cos + partner * sin


def _mod_row(i):
    return jnp.where(i < NT_LAT, i // TILES_PER_SEQ, BATCH)


def _mod_spec(chunk):
    return pl.BlockSpec((None, None, 1, D_MODEL), lambda i: (_mod_row(i), chunk, 0, 0))


def _rope_row_block(i):
    return jnp.where(i < NT_LAT, i % TILES_PER_SEQ, TILES_PER_SEQ)


def _full(shape):
    nd = len(shape)
    return pl.BlockSpec(shape, lambda *_: (0,) * nd)


def _params():
    return pltpu.CompilerParams(vmem_limit_bytes=VMEM_LIMIT)


def _ada_kernel(cv_ref, w_ref, b_ref, o_ref):
    cv = cv_ref[...]
    s = cv * _sigmoid(cv)
    o_ref[...] = jnp.dot(s, w_ref[...], precision=lax.Precision.HIGHEST,
                         preferred_element_type=F32) + b_ref[...]


def _ada_table(cv, ada_w, ada_b):
    bn = 1536
    nb = (6 * D_MODEL) // bn
    return pl.pallas_call(
        _ada_kernel,
        grid=(DEPTH, nb),
        in_specs=[pl.BlockSpec((8, D_MODEL), lambda l, j: (0, 0)),
                  pl.BlockSpec((None, D_MODEL, bn), lambda l, j: (l, 0, j)),
                  pl.BlockSpec((None, 1, bn), lambda l, j: (l, 0, j))],
        out_specs=pl.BlockSpec((None, 8, bn), lambda l, j: (l, 0, j)),
        out_shape=jax.ShapeDtypeStruct((DEPTH, 8, 6 * D_MODEL), F32),
        compiler_params=_params(),
        name="ada_table",
    )(cv, ada_w, ada_b.reshape(DEPTH, 1, 6 * D_MODEL))


def _rope_tables(rot_dim):
    axis_dim = rot_dim // 2
    inv_freq = ROPE_THETA ** (-jnp.arange(0, axis_dim, 2, dtype=F32) / axis_dim)
    t = jnp.arange(SEQ)
    ang_r = (t // GRID_W).astype(F32)[:, None] * inv_freq[None, :]
    ang_c = (t % GRID_W).astype(F32)[:, None] * inv_freq[None, :]
    cos = jnp.concatenate([jnp.cos(ang_r), jnp.cos(ang_r), jnp.cos(ang_c), jnp.cos(ang_c)], axis=-1)
    sin = jnp.concatenate([-jnp.sin(ang_r), jnp.sin(ang_r), -jnp.sin(ang_c), jnp.sin(ang_c)], axis=-1)
    return cos, sin


def _pad_table(cos, sin, lead, period, width):
    rot = cos.shape[1]
    one = jnp.ones((SEQ, period), F32).at[:, lead:lead + rot].set(cos)
    zero = jnp.zeros((SEQ, period), F32).at[:, lead:lead + rot].set(sin)
    cos_w = jnp.tile(one, (1, width // period))
    sin_w = jnp.tile(zero, (1, width // period))
    cos_w = jnp.concatenate([cos_w, jnp.ones((TM, width), F32)], axis=0)
    sin_w = jnp.concatenate([sin_w, jnp.zeros((TM, width), F32)], axis=0)
    return cos_w, sin_w


def _proj0_kernel(x_ref, sh_ref, sc_ref, w_ref, b_ref, cos_ref, sin_ref,
                  h_ref, q_ref, k_ref, v_ref):
    u = x_ref[...] * (1.0 + sc_ref[...]) + sh_ref[...]
    y = jnp.dot(u.astype(BF16), w_ref[...], preferred_element_type=F32) + b_ref[...]
    h_ref[...] = y[:, :CONV_CH] * _sigmoid(y[:, CONV_CH:2 * CONV_CH])
    cos = cos_ref[...]
    sin = sin_ref[...]
    q0 = 2 * CONV_CH
    k0 = q0 + WIN_HEADS * HEAD_DIM
    v0 = k0 + WIN_KV_HEADS * HEAD_DIM
    cos4 = jnp.concatenate([cos] * 4, axis=1)
    sin4 = jnp.concatenate([sin] * 4, axis=1)
    q = _rope(y[:, q0:k0], cos4, sin4, HEAD_DIM // 4) * (HEAD_DIM ** -0.5 * LOG2E)
    q_ref[...] = q.astype(BF16)
    k_ref[...] = _rope(y[:, k0:v0], cos, sin, HEAD_DIM // 4).astype(BF16)
    v_ref[...] = _values_with_ones(y[:, v0:].astype(BF16), WIN_KV_HEADS, HEAD_DIM)


def _proj0(x_all, mods, w_in, b_in, cos_hd, sin_hd):
    kvw = WIN_KV_HEADS * HEAD_DIM
    row = lambda w: pl.BlockSpec((TM, w), lambda i: (i, 0))
    tab = pl.BlockSpec((TM, 128), lambda i: (_rope_row_block(i), 0))
    return pl.pallas_call(
        _proj0_kernel,
        grid=(NT_ALL,),
        in_specs=[row(D_MODEL), _mod_spec(SH1), _mod_spec(SC1),
                  _full((D_MODEL, EVEN_IN)), _full((1, EVEN_IN)), tab, tab],
        out_specs=[row(CONV_CH), row(WIN_HEADS * HEAD_DIM), row(kvw), row(WIN_KV_HEADS * VAL_PAD)],
        out_shape=[jax.ShapeDtypeStruct((R_ALL, CONV_CH), F32),
                   jax.ShapeDtypeStruct((R_ALL, WIN_HEADS * HEAD_DIM), BF16),
                   jax.ShapeDtypeStruct((R_ALL, kvw), BF16),
                   jax.ShapeDtypeStruct((R_ALL, WIN_KV_HEADS * VAL_PAD), BF16)],
        compiler_params=_params(),
        name="proj0",
    )(x_all, mods, mods, w_in, b_in, cos_hd, sin_hd)


def _conv_kernel(prev_ref, cur_ref, next_ref, w_ref, cb_ref, g_ref, b_ref, o_ref, buf):
    i = pl.program_id(0)
    is_ctx = i >= NT_LAT
    first = jnp.logical_or(is_ctx, i % TILES_PER_SEQ == 0)
    last = jnp.logical_or(is_ctx, i % TILES_PER_SEQ == TILES_PER_SEQ - 1)
    buf[0, 0:HALO, :] = jnp.where(first, 0.0, prev_ref[...])
    buf[0, HALO:HALO + TM, :] = cur_ref[...]
    buf[0, HALO + TM:, :] = jnp.where(last, 0.0, next_ref[...])
    span = TM + 2 * HALO - SHIFTS
    for r in range(1, SHIFTS):
        buf[r, 0:span, :] = buf[0, r:r + span, :]
    off = HALO - CONV_WIDTH // 2
    for c in range(TM // CONV_CHUNK):
        r0 = c * CONV_CHUNK
        acc = jnp.zeros((CONV_CHUNK, CONV_CH), F32)
        for k in range(CONV_WIDTH):
            r = (off + k) % SHIFTS
            base = r0 + off + k - r
            acc = acc + buf[r, base:base + CONV_CHUNK, :] * w_ref[k:k + 1, :]
        z = _layer_norm(acc + cb_ref[...], g_ref[...], b_ref[...])
        o_ref[r0:r0 + CONV_CHUNK, :] = (z * _sigmoid(z)).astype(BF16)


def _conv(h, conv_w, conv_b, ln_g, ln_b):
    nh = R_ALL // HALO
    per = TM // HALO
    vec = _full((1, CONV_CH))
    return pl.pallas_call(
        _conv_kernel,
        grid=(NT_ALL,),
        in_specs=[pl.BlockSpec((HALO, CONV_CH), lambda i: (jnp.maximum(i * per - 1, 0), 0)),
                  pl.BlockSpec((TM, CONV_CH), lambda i: (i, 0)),
                  pl.BlockSpec((HALO, CONV_CH), lambda i: (jnp.minimum((i + 1) * per, nh - 1), 0)),
                  _full((CONV_WIDTH, CONV_CH)), vec, vec, vec],
        out_specs=pl.BlockSpec((TM, CONV_CH), lambda i: (i, 0)),
        out_shape=jax.ShapeDtypeStruct((R_ALL, CONV_CH), BF16),
        scratch_shapes=[pltpu.VMEM((SHIFTS, TM + 2 * HALO, CONV_CH), F32)],
        compiler_params=_params(),
        name="conv_module",
    )(h, h, h, conv_w, conv_b, ln_g, ln_b)


def _nt_dot(a, b):
    return lax.dot_general(a, b, (((1,), (1,)), ((), ())), preferred_element_type=F32)


def _values_with_ones(v, n_kv, dv):
    lane = lax.broadcasted_iota(jnp.int32, (v.shape[0], VAL_PAD - dv), 1)
    tail = jnp.where(lane == 0, 1.0, 0.0).astype(v.dtype)
    pieces = []
    for h in range(n_kv):
        pieces += [v[:, h * dv:(h + 1) * dv], tail]
    return jnp.concatenate(pieces, axis=1)


def _attend(units, dv):
    def scores(unit):
        q, ks, _, masks, _ = unit
        out = []
        for k, msk in zip(ks, masks):
            s = _nt_dot(q, k)
            if msk is not None:
                s = jnp.where(msk, s, NEG_INF)
            out.append(s)
        return out

    results = []
    ss = scores(units[0])
    for idx, unit in enumerate(units):
        nxt = scores(units[idx + 1]) if idx + 1 < len(units) else None
        _, _, vs, _, sink = unit
        m = functools.reduce(jnp.maximum, [jnp.max(s, axis=-1, keepdims=True) for s in ss])
        if sink is not None:
            m = jnp.maximum(m, sink)
        acc = functools.reduce(jnp.add, [jnp.dot(jnp.exp2(s - m).astype(BF16), v, preferred_element_type=F32)
                                         for s, v in zip(ss, vs)])
        l = acc[:, dv:dv + 1]
        if sink is not None:
            l = l + jnp.exp2(sink - m)
        results.append(acc[:, :dv] / l)
        ss = nxt
    return results


def _attend_keys_major(units, dv):
    def scores(unit):
        q, ks, _ = unit
        return [_nt_dot(k, q) for k in ks]

    results = []
    ss = scores(units[0])
    for idx, unit in enumerate(units):
        nxt = scores(units[idx + 1]) if idx + 1 < len(units) else None
        m = functools.reduce(jnp.maximum, [jnp.max(s, axis=0, keepdims=True) for s in ss])
        acc = functools.reduce(jnp.add, [jnp.dot(vt, jnp.exp2(s - m).astype(BF16), preferred_element_type=F32)
                                         for s, vt in zip(ss, unit[2])])
        results.append(acc[:dv] / acc[dv:dv + 1])
        ss = nxt
    return results


def _win_kernel(sink_ref, q_ref, kp_ref, kc_ref, kn_ref, kx_ref, vp_ref, vc_ref, vn_ref, vx_ref, o_ref):
    n = pl.program_id(1)
    k_loc = jnp.concatenate([kp_ref[...], kc_ref[...], kn_ref[...]], axis=0)
    v_loc = jnp.concatenate([vp_ref[...], vc_ref[...], vn_ref[...]], axis=0)
    k_ctx = kx_ref[...]
    v_ctx = vx_ref[...]
    qi = lax.broadcasted_iota(jnp.int32, (WINDOW, 3 * WINDOW), 0)
    kj = lax.broadcasted_iota(jnp.int32, (WINDOW, 3 * WINDOW), 1)
    k_pos = jnp.where(n < SEQ // WINDOW, kj + (n - 1) * WINDOW, SEQ)
    valid = jnp.where(kj >= qi, jnp.where(kj <= qi + 2 * WINDOW, 1, 0), 0)
    valid = jnp.where(k_pos >= 0, jnp.where(k_pos < SEQ, valid, 0), 0) > 0
    group = WIN_HEADS // WIN_KV_HEADS
    units = []
    for h in range(WIN_HEADS):
        kv = h // group
        ksl = slice(kv * HEAD_DIM, (kv + 1) * HEAD_DIM)
        vsl = slice(kv * VAL_PAD, (kv + 1) * VAL_PAD)
        units.append((q_ref[:, h * HEAD_DIM:(h + 1) * HEAD_DIM], [k_ctx[:, ksl], k_loc[:, ksl]],
                      [v_ctx[:, vsl], v_loc[:, vsl]], [None, valid], sink_ref[h] * LOG2E))
    for h, o in enumerate(_attend(units, HEAD_DIM)):
        o_ref[:, h * HEAD_DIM:(h + 1) * HEAD_DIM] = o.astype(BF16)


def _win_attention(sink, q, k, v):
    nblk = SEQ // WINDOW
    cblk = CTX_LEN // WINDOW
    kvw = WIN_KV_HEADS * HEAD_DIM
    ctx0 = R_LAT // CTX_LEN
    lat = lambda n: jnp.minimum(n, nblk - 1)
    prev = lambda b, n: (b * nblk + jnp.maximum(lat(n) - 1, 0), 0)
    cur = lambda b, n: (b * nblk + lat(n), 0)
    nxt = lambda b, n: (b * nblk + jnp.minimum(lat(n) + 1, nblk - 1), 0)
    qrow = lambda b, n: (jnp.where(n < nblk, b * nblk + n, R_LAT // WINDOW + b * cblk + n - nblk), 0)
    ctx = lambda b, n: (ctx0 + b, 0)
    vw = WIN_KV_HEADS * VAL_PAD
    kvb = lambda f, w: pl.BlockSpec((WINDOW, w), f)
    cxb = lambda w: pl.BlockSpec((CTX_LEN, w), ctx)
    return pl.pallas_call(
        _win_kernel,
        grid=(BATCH, nblk + cblk),
        in_specs=[pl.BlockSpec(memory_space=pltpu.SMEM),
                  pl.BlockSpec((WINDOW, WIN_HEADS * HEAD_DIM), qrow),
                  kvb(prev, kvw), kvb(cur, kvw), kvb(nxt, kvw), cxb(kvw),
                  kvb(prev, vw), kvb(cur, vw), kvb(nxt, vw), cxb(vw)],
        out_specs=pl.BlockSpec((WINDOW, WIN_HEADS * HEAD_DIM), qrow),
        out_shape=jax.ShapeDtypeStruct((R_ALL, WIN_HEADS * HEAD_DIM), BF16),
        compiler_params=_params(),
        name="window_attention",
    )(sink, q, k, k, k, k, v, v, v, v)


def _outproj_kernel(a_ref, b_ref, wa_ref, wb_ref, bo_ref, x_ref, g1_ref, sh2_ref, sc2_ref,
                    lng_ref, lnb_ref, wrh_ref, wrl_ref, br_ref, upper_ref, lower_ref, sel_ref,
                    xo_ref, route_ref, cnt_ref, xs_ref):
    y = (jnp.dot(a_ref[...], wa_ref[...], preferred_element_type=F32)
         + jnp.dot(b_ref[...], wb_ref[...], preferred_element_type=F32) + bo_ref[...])
    xn = _layer_norm(DN_ALPHA * x_ref[...] + (1.0 + g1_ref[...]) * y, lng_ref[...], lnb_ref[...])
    xo_ref[...] = xn
    u2 = xn * (1.0 + sc2_ref[...]) + sh2_ref[...]
    u_hi = u2.astype(BF16)
    u_lo = (u2 - u_hi.astype(F32)).astype(BF16)
    logits = (jnp.dot(u_hi, wrh_ref[...], preferred_element_type=F32)
              + jnp.dot(u_lo, wrh_ref[...], preferred_element_type=F32)
              + jnp.dot(u_hi, wrl_ref[...], preferred_element_type=F32) + br_ref[...])
    lane = lax.broadcasted_iota(jnp.int32, logits.shape, 1).astype(F32)
    ninf = -jnp.inf
    big = float(ROUTE_W)
    gl = jnp.where(lane < N_GROUPS, logits, ninf)
    gmax = jnp.max(gl, axis=-1, keepdims=True)
    gidx = jnp.min(jnp.where(gl == gmax, lane, big), axis=-1, keepdims=True)
    g_w = 1.0 / jnp.sum(jnp.exp(gl - gmax), axis=-1, keepdims=True)
    lo = N_GROUPS + EXP_PER_GROUP * gidx
    el = jnp.where(lane >= lo, jnp.where(lane < lo + EXP_PER_GROUP, logits, ninf), ninf)
    v1 = jnp.max(el, axis=-1, keepdims=True)
    i1 = jnp.min(jnp.where(el == v1, lane, big), axis=-1, keepdims=True)
    el2 = jnp.where(lane == i1, ninf, el)
    v2 = jnp.max(el2, axis=-1, keepdims=True)
    i2 = jnp.min(jnp.where(el2 == v2, lane, big), axis=-1, keepdims=True)
    e2 = jnp.exp(v2 - v1)
    w1 = g_w / (1.0 + e2)
    w2 = g_w * e2 / (1.0 + e2)
    onehot = [jnp.where(lane == i1 - N_GROUPS, 1.0, 0.0), jnp.where(lane == i2 - N_GROUPS, 1.0, 0.0)]
    cnt = [jnp.sum(o, axis=0, keepdims=True) for o in onehot]
    run_units = jnp.floor((cnt[0] + cnt[1] + (RUN_ALIGN - 1)) * (1.0 / RUN_ALIGN))
    below = RUN_ALIGN * jnp.dot(jnp.broadcast_to(run_units, (8, ROUTE_W)).astype(BF16), upper_ref[...],
                                preferred_element_type=F32)[0:1]
    lower = lower_ref[...]
    base = [below, below + cnt[0]]
    lp = []
    for s in range(2):
        earlier = jnp.dot(lower, onehot[s].astype(BF16), preferred_element_type=F32)
        lp.append(jnp.sum(onehot[s] * (base[s] + earlier), axis=-1, keepdims=True))
    rec = jnp.where(lane == 0.0, i1 - N_GROUPS,
                    jnp.where(lane == 1.0, i2 - N_GROUPS,
                              jnp.where(lane == 2.0, w1,
                                        jnp.where(lane == 3.0, w2,
                                                  jnp.where(lane == 4.0, lp[0],
                                                            jnp.where(lane == 5.0, lp[1], 0.0))))))
    route_ref[...] = rec
    cnt_ref[...] = jnp.broadcast_to(run_units * RUN_ALIGN, (8, ROUTE_W))
    sel = sel_ref[...]
    pos = lax.broadcasted_iota(jnp.int32, (LOCAL_ROWS, TM), 0).astype(F32)
    lp_lanes = []
    for s in range(2):
        hi = jnp.floor(lp[s] * (1.0 / 256.0))
        parts = jnp.where(lane == 0.0, lp[s] - 256.0 * hi, jnp.where(lane == 1.0, hi, 0.0)).astype(BF16)
        t = _nt_dot(sel, parts)
        lp_lanes.append(t[0:1] + 256.0 * t[1:2])
    perm = jnp.where(pos == lp_lanes[0], 1.0, jnp.where(pos == lp_lanes[1], 1.0, 0.0)).astype(BF16)
    xs_ref[...] = jnp.dot(perm, u_hi, preferred_element_type=F32)


def _outproj(n_tiles, mix_a, mix_b, w_a, w_b, b_out, x_all, mods, ln_g, ln_b, w_r, b_r):
    rows = n_tiles * TM
    half = mix_a.shape[1]
    w_rh = w_r.astype(BF16)
    w_rl = (w_r - w_rh.astype(F32)).astype(BF16)
    upper = (jnp.arange(ROUTE_W)[:, None] < jnp.arange(ROUTE_W)[None, :]).astype(BF16)
    lower = (jnp.arange(TM)[:, None] > jnp.arange(TM)[None, :]).astype(BF16)
    sel = (jnp.arange(8)[:, None] == jnp.arange(ROUTE_W)[None, :]).astype(BF16)
    row = lambda w: pl.BlockSpec((TM, w), lambda i: (i, 0))
    vec = _full((1, D_MODEL))
    return pl.pallas_call(
        _outproj_kernel,
        grid=(n_tiles,),
        in_specs=[row(half), row(half), _full((half, D_MODEL)), _full((half, D_MODEL)), vec,
                  row(D_MODEL), _mod_spec(G1), _mod_spec(SH2), _mod_spec(SC2), vec, vec,
                  _full((D_MODEL, ROUTE_W)), _full((D_MODEL, ROUTE_W)), _full((1, ROUTE_W)),
                  _full((ROUTE_W, ROUTE_W)), _full((TM, TM)), _full((8, ROUTE_W))],
        out_specs=[row(D_MODEL), row(ROUTE_W), pl.BlockSpec((8, ROUTE_W), lambda i: (i, 0)),
                   pl.BlockSpec((LOCAL_ROWS, D_MODEL), lambda i: (i, 0))],
        out_shape=[jax.ShapeDtypeStruct((rows, D_MODEL), F32),
                   jax.ShapeDtypeStruct((rows, ROUTE_W), F32),
                   jax.ShapeDtypeStruct((n_tiles * 8, ROUTE_W), F32),
                   jax.ShapeDtypeStruct((n_tiles * LOCAL_ROWS, D_MODEL), F32)],
        compiler_params=_params(),
        name="outproj_ln_router",
    )(mix_a, mix_b, w_a, w_b, b_out, x_all, mods, mods, mods, ln_g, ln_b, w_rh, w_rl, b_r, upper, lower, sel)


def _aligned(i):
    return pl.multiple_of(i, RUN_ALIGN)


def _moe_kernel(te_ref, tk_ref, rows_ref, lo_ref, hi_ref, cnt_ref, bt_ref, be_ref, xs_hbm, w1_ref, w3_ref,
                w2_ref, ys_ref, xbuf, wb1, wb3, wb2, sem):
    j = pl.program_id(0)
    nt = pl.num_programs(0)
    slot = j % 2

    def issue(tile, slot_):
        e = te_ref[tile]
        first = tk_ref[tile] * TMM

        def body(i, carry):
            idx = i * N_EXPERTS + e
            start = bt_ref[idx]
            lo = jnp.maximum(start, first)
            n = jnp.minimum(start + cnt_ref[idx], first + TMM) - lo

            @pl.when(n > 0)
            def _():
                src = i * LOCAL_ROWS + be_ref[idx] + lo - start
                pltpu.make_async_copy(xs_hbm.at[pl.ds(_aligned(src), _aligned(n))],
                                      xbuf.at[slot_, pl.ds(_aligned(lo - first), _aligned(n))],
                                      sem.at[slot_]).start()
            return carry
        lax.fori_loop(lo_ref[tile], hi_ref[tile], body, 0)

    @pl.when(j == 0)
    def _():
        xbuf[...] = jnp.zeros_like(xbuf)
        issue(0, 0)

    @pl.when(j + 1 < nt)
    def _():
        issue(j + 1, 1 - slot)

    @pl.when(jnp.logical_or(j == 0, te_ref[j] != te_ref[jnp.maximum(j - 1, 0)]))
    def _():
        wb1[...] = w1_ref[...].astype(BF16)
        wb3[...] = w3_ref[...].astype(BF16)
        wb2[...] = w2_ref[...].astype(BF16)

    n_real = rows_ref[j]

    @pl.when(n_real > 0)
    def _():
        pltpu.make_async_copy(xs_hbm.at[pl.ds(0, _aligned(n_real))], xbuf.at[slot, pl.ds(0, _aligned(n_real))],
                              sem.at[slot]).wait()
        x = xbuf[slot].astype(BF16)
        h1 = jnp.dot(x, wb1[...], preferred_element_type=F32)
        h3 = jnp.dot(x, wb3[...], preferred_element_type=F32)
        hid = h1 * _sigmoid(h1) * h3
        ys_ref[...] = jnp.dot(hid.astype(BF16), wb2[...], preferred_element_type=F32)

    @pl.when(n_real == 0)
    def _():
        ys_ref[...] = jnp.zeros_like(ys_ref)


def _moe_experts(layer, plan, xs_local, w1, w3, w2):
    tile_expert, tile_k, tile_rows, src_lo, src_hi, cnt, before_tile, before_expert, _, _ = plan
    nt = tile_expert.shape[0]
    wmap = lambda j, te, *_: (layer, te[j], 0, 0)
    grid_spec = pltpu.PrefetchScalarGridSpec(
        num_scalar_prefetch=8,
        grid=(nt,),
        in_specs=[pl.BlockSpec(memory_space=pl.ANY),
                  pl.BlockSpec((None, None, D_MODEL, EXPERT_FF), wmap),
                  pl.BlockSpec((None, None, D_MODEL, EXPERT_FF), wmap),
                  pl.BlockSpec((None, None, EXPERT_FF, D_MODEL), wmap)],
        out_specs=pl.BlockSpec((TMM, D_MODEL), lambda j, *_: (j, 0)),
        scratch_shapes=[pltpu.VMEM((2, TMM, D_MODEL), F32),
                        pltpu.VMEM((D_MODEL, EXPERT_FF), BF16),
                        pltpu.VMEM((D_MODEL, EXPERT_FF), BF16),
                        pltpu.VMEM((EXPERT_FF, D_MODEL), BF16),
                        pltpu.SemaphoreType.DMA((2,))])
    return pl.pallas_call(
        _moe_kernel,
        grid_spec=grid_spec,
        out_shape=jax.ShapeDtypeStruct((nt * TMM, D_MODEL), F32),
        compiler_params=_params(),
        name="moe_experts",
    )(tile_expert, tile_k, tile_rows, src_lo, src_hi, cnt, before_tile, before_expert, xs_local, w1, w3, w2)


def _combine_kernel(cnt_ref, bt_ref, be_ref, gs_ref, used_ref, ys_hbm, x_ref, route_ref, g2_ref, lng_ref, lnb_ref,
                    o_ref, ybuf, sem):
    i = pl.program_id(0)
    nt = pl.num_programs(0)
    slot = i % 2

    def issue(tile, slot_):
        def body(e, carry):
            idx = tile * N_EXPERTS + e
            n = cnt_ref[idx]

            @pl.when(n > 0)
            def _():
                pltpu.make_async_copy(ys_hbm.at[pl.ds(_aligned(gs_ref[e] + bt_ref[idx]), _aligned(n))],
                                      ybuf.at[slot_, pl.ds(_aligned(be_ref[idx]), _aligned(n))],
                                      sem.at[slot_]).start()
            return carry
        lax.fori_loop(0, N_EXPERTS, body, 0)

    @pl.when(i == 0)
    def _():
        ybuf[...] = jnp.zeros_like(ybuf)
        issue(0, 0)

    @pl.when(i + 1 < nt)
    def _():
        issue(i + 1, 1 - slot)

    used = _aligned(used_ref[i])
    pltpu.make_async_copy(ys_hbm.at[pl.ds(0, used)], ybuf.at[slot, pl.ds(0, used)], sem.at[slot]).wait()
    route = route_ref[...]
    pos = lax.broadcasted_iota(jnp.int32, (TM, LOCAL_ROWS), 1).astype(F32)
    sel = (jnp.where(pos == route[:, 4:5], route[:, 2:3], 0.0)
           + jnp.where(pos == route[:, 5:6], route[:, 3:4], 0.0))
    sel_hi = sel.astype(BF16)
    sel_lo = (sel - sel_hi.astype(F32)).astype(BF16)
    y = ybuf[slot]
    y_hi = y.astype(BF16)
    y_lo = (y - y_hi.astype(F32)).astype(BF16)
    f = (jnp.dot(sel_hi, y_hi, preferred_element_type=F32) + jnp.dot(sel_lo, y_hi, preferred_element_type=F32)
         + jnp.dot(sel_hi, y_lo, preferred_element_type=F32))
    z = DN_ALPHA * x_ref[...] + (1.0 + g2_ref[...]) * f
    o_ref[...] = _layer_norm(z, lng_ref[...], lnb_ref[...])


def _moe_combine(n_tiles, plan, ys, x_all, route, mods, ln_g, ln_b):
    _, _, _, _, _, cnt, before_tile, before_expert, group_start, used = plan
    rows = n_tiles * TM
    row = lambda w: pl.BlockSpec((TM, w), lambda i, *_: (i, 0))
    vec = pl.BlockSpec((1, D_MODEL), lambda i, *_: (0, 0))
    grid_spec = pltpu.PrefetchScalarGridSpec(
        num_scalar_prefetch=5,
        grid=(n_tiles,),
        in_specs=[pl.BlockSpec(memory_space=pl.ANY), row(D_MODEL), row(ROUTE_W),
                  pl.BlockSpec((None, None, 1, D_MODEL), lambda i, *_: (_mod_row(i), G2, 0, 0)), vec, vec],
        out_specs=row(D_MODEL),
        scratch_shapes=[pltpu.VMEM((2, LOCAL_ROWS, D_MODEL), F32), pltpu.SemaphoreType.DMA((2,))])
    return pl.pallas_call(
        _combine_kernel,
        grid_spec=grid_spec,
        out_shape=jax.ShapeDtypeStruct((rows, D_MODEL), F32),
        compiler_params=_params(),
        name="moe_combine_ln",
    )(cnt, before_tile, before_expert, group_start, used, ys, x_all, route, mods, ln_g, ln_b)


def _moe_plan(cnt_rec, n_tiles):
    cnt = cnt_rec.reshape(n_tiles, 8, ROUTE_W)[:, 0, :N_EXPERTS].astype(jnp.int32)
    nt_max = (n_tiles * (2 * TM + N_EXPERTS * (RUN_ALIGN - 1))) // TMM + N_EXPERTS
    total = jnp.sum(cnt, axis=0)
    tiles_e = (total + TMM - 1) // TMM
    tile_end = jnp.cumsum(tiles_e)
    first_tile = tile_end - tiles_e
    before_tile = jnp.cumsum(cnt, axis=0) - cnt
    before_expert = jnp.cumsum(cnt, axis=1) - cnt
    tile_id = jnp.arange(nt_max, dtype=jnp.int32)
    tile_expert = jnp.minimum(jnp.sum((tile_id[:, None] >= tile_end[None, :]).astype(jnp.int32), axis=1),
                              N_EXPERTS - 1)
    tile_k = tile_id - first_tile[tile_expert]
    tile_rows = jnp.clip(total[tile_expert] - tile_k * TMM, 0, TMM)
    first = (tile_k * TMM)[:, None]
    run_start = before_tile.T[tile_expert]
    run_end = run_start + cnt.T[tile_expert]
    src_lo = jnp.sum((run_end <= first).astype(jnp.int32), axis=1)
    src_hi = jnp.sum((run_start < first + TMM).astype(jnp.int32), axis=1)
    return (tile_expert, tile_k, tile_rows, src_lo, src_hi, cnt.reshape(-1), before_tile.reshape(-1),
            before_expert.reshape(-1), first_tile * TMM, jnp.sum(cnt, axis=1))


def _router_weights(w_rg, b_rg, w_re, b_re):
    w = jnp.concatenate([w_rg, jnp.transpose(w_re, (1, 0, 2)).reshape(D_MODEL, N_EXPERTS)], axis=1)
    b = jnp.concatenate([b_rg, b_re.reshape(-1)])
    pad = ROUTE_W - w.shape[1]
    return jnp.pad(w, ((0, 0), (0, pad))), jnp.pad(b, (0, pad)).reshape(1, ROUTE_W)


def _proj1_kernel(x_ref, sh_ref, sc_ref, w_ref, b_ref, cos_ref, sin_ref, cosm_ref, sinm_ref, cosr_ref,
                  sinr_ref, gq_ref, gk_ref, gqc_ref, gkv_ref, avg_ref, wuq_ref, wuk_ref, wuv_ref, vplace_ref,
                  q_ref, qm_ref, k_ref, v_ref, km_ref, vm_ref):
    u = x_ref[...] * (1.0 + sc_ref[...]) + sh_ref[...]
    y = jnp.dot(u.astype(BF16), w_ref[...], preferred_element_type=F32) + b_ref[...]
    c_q = GQA_HEADS * HEAD_DIM
    c_qc = c_q + MLA_Q_RANK
    c_k = c_qc + GQA_KV_HEADS * HEAD_DIM
    c_v = c_k + GQA_KV_HEADS * HEAD_DIM
    c_kv = c_v + MLA_KV_RANK
    avg = avg_ref[...]

    def head_rms(t, gain):
        sq = t * t
        hi = sq.astype(BF16)
        lo = (sq - hi.astype(F32)).astype(BF16)
        a = avg[:t.shape[1], :t.shape[1]]
        ms = jnp.dot(hi, a, preferred_element_type=F32) + jnp.dot(lo, a, preferred_element_type=F32)
        return t * lax.rsqrt(ms + RMS_EPS) * gain

    def row_rms(t, gain):
        ms = jnp.mean(t * t, axis=-1, keepdims=True)
        return t * lax.rsqrt(ms + RMS_EPS) * gain

    cos = cos_ref[...]
    sin = sin_ref[...]
    cos4 = jnp.concatenate([cos] * 4, axis=1)
    sin4 = jnp.concatenate([sin] * 4, axis=1)
    q = _rope(head_rms(y[:, :c_q], gq_ref[...]), cos4, sin4, HEAD_DIM // 4) * (HEAD_DIM ** -0.5 * LOG2E)
    q_ref[...] = q.astype(BF16)
    k = _rope(head_rms(y[:, c_qc:c_k], gk_ref[...]), cos, sin, HEAD_DIM // 4)
    k_ref[...] = k.astype(BF16)

    def transposed_values(w_t, src, dv):
        vt = _nt_dot(w_t, src)
        r = lax.broadcasted_iota(jnp.int32, vt.shape, 0)
        return jnp.where(r % VAL_PAD == dv, 1.0, vt).astype(BF16)

    v_ref[...] = transposed_values(vplace_ref[...], y[:, c_k:c_v].astype(BF16), HEAD_DIM)

    qc = row_rms(y[:, c_q:c_qc], gqc_ref[...]).astype(BF16)
    qm = jnp.dot(qc, wuq_ref[...], preferred_element_type=F32)
    cosm = jnp.concatenate([cosm_ref[...]] * MLA_HEADS, axis=1)
    sinm = jnp.concatenate([sinm_ref[...]] * MLA_HEADS, axis=1)
    qm = _rope(qm, cosm, sinm, MLA_ROPE // 4) * ((MLA_NOPE + MLA_ROPE) ** -0.5 * LOG2E)
    qm_ref[...] = qm.astype(BF16)

    kvn = row_rms(y[:, c_v:c_kv], gkv_ref[...]).astype(BF16)
    kr = _rope(y[:, c_kv:], cosr_ref[...], sinr_ref[...], MLA_ROPE // 4).astype(BF16)
    km = jnp.dot(jnp.concatenate([kvn, kr], axis=1), wuk_ref[...], preferred_element_type=F32)
    km_ref[...] = km.astype(BF16)
    vm_ref[...] = transposed_values(wuv_ref[...], kvn, MLA_V)


def _proj1(x_all, mods, w_in, b_in, tabs, gq, gk, gqc, gkv, avg, wuq, wuk, wuv):
    cos_hd, sin_hd, cos_m, sin_m, cos_r, sin_r = tabs
    kvw = GQA_KV_HEADS * HEAD_DIM
    qw = GQA_HEADS * HEAD_DIM
    mw = MLA_HEADS * MLA_PAD
    vw = MLA_HEADS * VAL_PAD
    gvw = GQA_KV_HEADS * VAL_PAD
    row = lambda w: pl.BlockSpec((TM, w), lambda i: (i, 0))
    col = lambda h: pl.BlockSpec((h, TM), lambda i: (0, i))
    tab = pl.BlockSpec((TM, 128), lambda i: (_rope_row_block(i), 0))
    r = jnp.arange(gvw)[:, None]
    c = jnp.arange(kvw)[None, :]
    vplace = jnp.logical_and(r // VAL_PAD == c // HEAD_DIM, r % VAL_PAD == c % HEAD_DIM).astype(BF16)
    return pl.pallas_call(
        _proj1_kernel,
        grid=(NT_ALL,),
        in_specs=[row(D_MODEL), _mod_spec(SH1), _mod_spec(SC1),
                  _full((D_MODEL, ODD_IN_PAD)), _full((1, ODD_IN_PAD)), tab, tab, tab, tab, tab, tab,
                  _full((1, qw)), _full((1, kvw)), _full((1, MLA_Q_RANK)), _full((1, MLA_KV_RANK)),
                  _full((qw, qw)), _full((MLA_Q_RANK, mw)), _full((MLA_KV_RANK + 128, mw)),
                  _full((vw, MLA_KV_RANK)), _full((gvw, kvw))],
        out_specs=[row(qw), row(mw), row(kvw), col(gvw), row(mw), col(vw)],
        out_shape=[jax.ShapeDtypeStruct((R_ALL, qw), BF16),
                   jax.ShapeDtypeStruct((R_ALL, mw), BF16),
                   jax.ShapeDtypeStruct((R_ALL, kvw), BF16),
                   jax.ShapeDtypeStruct((gvw, R_ALL), BF16),
                   jax.ShapeDtypeStruct((R_ALL, mw), BF16),
                   jax.ShapeDtypeStruct((vw, R_ALL), BF16)],
        compiler_params=_params(),
        name="proj1",
    )(x_all, mods, mods, w_in, b_in, cos_hd, sin_hd, cos_m, sin_m, cos_r, sin_r,
      gq, gk, gqc, gkv, avg, wuq, wuk, wuv.T, vplace)


def _dense_kernel(q_ref, kl_ref, kc_ref, vl_ref, vc_ref, o_ref, *, n_heads, group, dk, dv):
    tq = q_ref.shape[0]
    units = []
    for kv in range(n_heads // group):
        qs = [q_ref[:, h * dk:(h + 1) * dk] for h in range(kv * group, (kv + 1) * group)]
        q = qs[0] if group == 1 else jnp.concatenate(qs, axis=0)
        ks = slice(kv * dk, (kv + 1) * dk)
        vs = slice(kv * VAL_PAD, (kv + 1) * VAL_PAD)
        units.append((q, [kl_ref[:, ks], kc_ref[:, ks]], [vl_ref[vs, :], vc_ref[vs, :]]))
    for kv, o_t in enumerate(_attend_keys_major(units, dv)):
        o = o_t.T
        for g in range(group):
            h = kv * group + g
            o_ref[:, h * dv:(h + 1) * dv] = o[g * tq:(g + 1) * tq].astype(BF16)


def _dense_attention(q, k, v, *, n_heads, group, dk, dv, tq, name):
    n_kv = n_heads // group
    nq = SEQ // tq
    ctx0 = R_LAT // CTX_LEN
    lat = lambda w: pl.BlockSpec((SEQ, w), lambda b, j: (b, 0), pipeline_mode=pl.Buffered(1))
    ctx = lambda w: pl.BlockSpec((CTX_LEN, w), lambda b, j: (ctx0 + b, 0))
    lat_t = pl.BlockSpec((n_kv * VAL_PAD, SEQ), lambda b, j: (0, b), pipeline_mode=pl.Buffered(1))
    ctx_t = pl.BlockSpec((n_kv * VAL_PAD, CTX_LEN), lambda b, j: (0, ctx0 + b))
    return pl.pallas_call(
        functools.partial(_dense_kernel, n_heads=n_heads, group=group, dk=dk, dv=dv),
        grid=(BATCH, nq),
        in_specs=[pl.BlockSpec((tq, n_heads * dk), lambda b, j: (b * nq + j, 0)),
                  lat(n_kv * dk), ctx(n_kv * dk), lat_t, ctx_t],
        out_specs=pl.BlockSpec((tq, n_heads * dv), lambda b, j: (b * nq + j, 0)),
        out_shape=jax.ShapeDtypeStruct((R_LAT, n_heads * dv), BF16),
        compiler_params=_params(),
        name=name,
    )(q, k, k, v, v)


def _mla_weights(w_uq, w_ukv):
    wq = w_uq.reshape(MLA_Q_RANK, MLA_HEADS, MLA_NOPE + MLA_ROPE)
    wq = jnp.pad(wq, ((0, 0), (0, 0), (0, MLA_PAD - MLA_NOPE - MLA_ROPE))).reshape(MLA_Q_RANK, -1)
    wkv = w_ukv.reshape(MLA_KV_RANK, MLA_HEADS, MLA_NOPE + MLA_V)
    wk = jnp.pad(wkv[:, :, :MLA_NOPE], ((0, 0), (0, 0), (0, MLA_PAD - MLA_NOPE))).reshape(MLA_KV_RANK, -1)
    wv = jnp.pad(wkv[:, :, MLA_NOPE:], ((0, 0), (0, 0), (0, VAL_PAD - MLA_V))).reshape(MLA_KV_RANK, -1)
    r = jnp.arange(128)[:, None]
    c = jnp.arange(MLA_HEADS * MLA_PAD)[None, :]
    place = jnp.logical_and(r < MLA_ROPE, (c % MLA_PAD) == MLA_NOPE + r).astype(F32)
    wk = jnp.concatenate([wk, place], axis=0)
    return wq.astype(BF16), wk.astype(BF16), wv.astype(BF16)


def kernel(x, c, ctx, c_ctx, even_w_in, even_b_in, even_conv_w, even_conv_b, even_conv_ln_g, even_conv_ln_b, even_sink, even_w_out, even_b_out, odd_w_in, odd_b_in, odd_q_norm, odd_k_norm, odd_mla_q_norm, odd_mla_kv_norm, odd_mla_w_uq, odd_mla_w_ukv, odd_w_out, odd_b_out, ada_w, ada_b, ln1_g, ln1_b, ln2_g, ln2_b, moe_w_rg, moe_b_rg, moe_w_re, moe_b_re, moe_w1, moe_w3, moe_w2):
    vec = lambda a: a.reshape(1, -1)
    x_all = jnp.concatenate([x.reshape(R_LAT, D_MODEL), ctx.reshape(R_CTX, D_MODEL)], axis=0)

    cv = jnp.concatenate([c, c_ctx[None, :], jnp.zeros((8 - BATCH - 1, D_MODEL), F32)], axis=0)
    mods = _ada_table(cv, ada_w, ada_b).reshape(DEPTH, 8, 6, 1, D_MODEL)

    cos64, sin64 = _rope_tables(HEAD_DIM)
    cos_hd, sin_hd = _pad_table(cos64, sin64, 0, HEAD_DIM, 128)
    cos32, sin32 = _rope_tables(MLA_ROPE)
    cos_m, sin_m = _pad_table(cos32, sin32, MLA_NOPE, MLA_PAD, 128)
    cos_r, sin_r = _pad_table(cos32, sin32, 0, 128, 128)

    m0 = mods[0]
    h, q0, k0, v0 = _proj0(x_all, m0, even_w_in[0].astype(BF16), vec(even_b_in[0]), cos_hd, sin_hd)
    conv_out = _conv(h, even_conv_w[0].reshape(CONV_WIDTH, CONV_CH), vec(even_conv_b[0]),
                     vec(even_conv_ln_g[0]), vec(even_conv_ln_b[0]))
    attn = _win_attention(even_sink[0], q0, k0, v0)
    w_out = even_w_out[0].astype(BF16)
    w_r, b_r = _router_weights(moe_w_rg[0], moe_b_rg[0], moe_w_re[0], moe_b_re[0])
    x_all, route, cnt_rec, xs_local = _outproj(
        NT_ALL, conv_out, attn, w_out[:CONV_CH], w_out[CONV_CH:], vec(even_b_out[0]),
        x_all, m0, vec(ln1_g[0]), vec(ln1_b[0]), w_r, b_r)
    plan = _moe_plan(cnt_rec, NT_ALL)
    ys = _moe_experts(0, plan, xs_local, moe_w1, moe_w3, moe_w2)
    x_all = _moe_combine(NT_ALL, plan, ys, x_all, route, m0, vec(ln2_g[0]), vec(ln2_b[0]))

    m1 = mods[1]
    w_in1 = jnp.pad(odd_w_in[0], ((0, 0), (0, ODD_IN_PAD - ODD_IN))).astype(BF16)
    b_in1 = jnp.pad(odd_b_in[0], (0, ODD_IN_PAD - ODD_IN)).reshape(1, -1)
    wuq, wuk, wuv = _mla_weights(odd_mla_w_uq[0], odd_mla_w_ukv[0])
    qw = GQA_HEADS * HEAD_DIM
    hid = jnp.arange(qw) // HEAD_DIM
    avg = ((hid[:, None] == hid[None, :]).astype(F32) / HEAD_DIM).astype(BF16)
    q1, qm, k1, v1, km, vm = _proj1(
        x_all, m1, w_in1, b_in1, (cos_hd, sin_hd, cos_m, sin_m, cos_r, sin_r),
        vec(jnp.tile(odd_q_norm[0], GQA_HEADS)), vec(jnp.tile(odd_k_norm[0], GQA_KV_HEADS)),
        vec(odd_mla_q_norm[0]), vec(odd_mla_kv_norm[0]), avg, wuq, wuk, wuv)
    o_g = _dense_attention(q1, k1, v1, n_heads=GQA_HEADS, group=GQA_HEADS // GQA_KV_HEADS,
                           dk=HEAD_DIM, dv=HEAD_DIM, tq=128, name="gqa_attention")
    o_m = _dense_attention(qm, km, vm, n_heads=MLA_HEADS, group=1, dk=MLA_PAD, dv=MLA_V, tq=256,
                           name="mla_attention")
    w_out = odd_w_out[0].astype(BF16)
    w_r, b_r = _router_weights(moe_w_rg[1], moe_b_rg[1], moe_w_re[1], moe_b_re[1])
    x_lat, route, cnt_rec, xs_local = _outproj(
        NT_LAT, o_g, o_m, w_out[:qw], w_out[qw:], vec(odd_b_out[0]),
        x_all, m1, vec(ln1_g[1]), vec(ln1_b[1]), w_r, b_r)
    plan = _moe_plan(cnt_rec, NT_LAT)
    ys = _moe_experts(1, plan, xs_local, moe_w1, moe_w3, moe_w2)
    x_lat = _moe_combine(NT_LAT, plan, ys, x_lat, route, m1, vec(ln2_g[1]), vec(ln2_b[1]))
    return x_lat.reshape(BATCH, SEQ, D_MODEL)
```

```python
import functools

import jax
import jax.numpy as jnp
from jax import lax
from jax.experimental import pallas as pl
from jax.experimental.pallas import tpu as pltpu

F32 = jnp.float32
BF16 = jnp.bfloat16

D_MODEL = 1024
BATCH = 4
SEQ = 4096
DEPTH = 2
GRID_W = 64
CTX_LEN = 256
HEAD_DIM = 64
ROPE_THETA = 10000.0
LN_EPS = 1e-5
RMS_EPS = 1e-6
NEG_INF = -1e30

CONV_CH = 512
CONV_WIDTH = 31
WIN_HEADS = 8
WIN_KV_HEADS = 2
WINDOW = 128
GQA_HEADS = 8
GQA_KV_HEADS = 2
MLA_HEADS = 8
MLA_Q_RANK = 256
MLA_KV_RANK = 128
MLA_NOPE = 64
MLA_ROPE = 32
MLA_V = 64
N_GROUPS = 4
EXP_PER_GROUP = 8
N_EXPERTS = N_GROUPS * EXP_PER_GROUP
EXPERT_FF = 512
DN_ALPHA = float((2 * DEPTH) ** 0.25)

EVEN_IN = 2 * CONV_CH + (WIN_HEADS + 2 * WIN_KV_HEADS) * HEAD_DIM
ODD_IN = 1184
ODD_IN_PAD = 1280
MLA_PAD = 128
VAL_PAD = 128
KEY_PARTS = 4
LOG2E = 1.4426950408889634

R_LAT = BATCH * SEQ
R_CTX = BATCH * CTX_LEN
R_ALL = R_LAT + R_CTX
TM = 256
NT_LAT = R_LAT // TM
NT_ALL = R_ALL // TM
TILES_PER_SEQ = SEQ // TM
HALO = 16
CONV_CHUNK = 32
SHIFTS = 8
TMM = 256
ROUTE_W = 128
RUN_ALIGN = 8
LOCAL_ROWS = 768
VMEM_LIMIT = 56 * 1024 * 1024

SH1, SC1, G1, SH2, SC2, G2 = range(6)


def _sigmoid(x):
    return 1.0 / (1.0 + jnp.exp(-x))


def _layer_norm(z, g, b):
    mu = jnp.mean(z, axis=-1, keepdims=True)
    zc = z - mu
    var = jnp.mean(zc * zc, axis=-1, keepdims=True)
    return zc * lax.rsqrt(var + LN_EPS) * g + b


def _rope(x, cos, sin, half):
    n = x.shape[-1]
    lane = lax.broadcasted_iota(jnp.int32, x.shape, 1)
    first = (lane % (2 * half)) < half
    partner = jnp.where(first, pltpu.roll(x, n - half, 1), pltpu.roll(x, half, 1))
    return x * cos + partner * sin


def _mod_row(i):
    return jnp.where(i < NT_LAT, i // TILES_PER_SEQ, BATCH)


def _mod_spec(chunk):
    return pl.BlockSpec((None, None, 1, D_MODEL), lambda i: (_mod_row(i), chunk, 0, 0))


def _rope_row_block(i):
    return jnp.where(i < NT_LAT, i % TILES_PER_SEQ, TILES_PER_SEQ)


def _full(shape):
    nd = len(shape)
    return pl.BlockSpec(shape, lambda *_: (0,) * nd)


def _params():
    return pltpu.CompilerParams(vmem_limit_bytes=VMEM_LIMIT)


def _ada_kernel(cv_ref, w_ref, b_ref, o_ref):
    cv = cv_ref[...]
    s = cv * _sigmoid(cv)
    o_ref[...] = jnp.dot(s, w_ref[...], precision=lax.Precision.HIGHEST,
                         preferred_element_type=F32) + b_ref[...]


def _ada_table(cv, ada_w, ada_b):
    bn = 1536
    nb = (6 * D_MODEL) // bn
    return pl.pallas_call(
        _ada_kernel,
        grid=(DEPTH, nb),
        in_specs=[pl.BlockSpec((8, D_MODEL), lambda l, j: (0, 0)),
                  pl.BlockSpec((None, D_MODEL, bn), lambda l, j: (l, 0, j)),
                  pl.BlockSpec((None, 1, bn), lambda l, j: (l, 0, j))],
        out_specs=pl.BlockSpec((None, 8, bn), lambda l, j: (l, 0, j)),
        out_shape=jax.ShapeDtypeStruct((DEPTH, 8, 6 * D_MODEL), F32),
        compiler_params=_params(),
        name="ada_table",
    )(cv, ada_w, ada_b.reshape(DEPTH, 1, 6 * D_MODEL))


def _rope_tables(rot_dim):
    axis_dim = rot_dim // 2
    inv_freq = ROPE_THETA ** (-jnp.arange(0, axis_dim, 2, dtype=F32) / axis_dim)
    t = jnp.arange(SEQ)
    ang_r = (t // GRID_W).astype(F32)[:, None] * inv_freq[None, :]
    ang_c = (t % GRID_W).astype(F32)[:, None] * inv_freq[None, :]
    cos = jnp.concatenate([jnp.cos(ang_r), jnp.cos(ang_r), jnp.cos(ang_c), jnp.cos(ang_c)], axis=-1)
    sin = jnp.concatenate([-jnp.sin(ang_r), jnp.sin(ang_r), -jnp.sin(ang_c), jnp.sin(ang_c)], axis=-1)
    return cos, sin


def _pad_table(cos, sin, lead, period, width):
    rot = cos.shape[1]
    one = jnp.ones((SEQ, period), F32).at[:, lead:lead + rot].set(cos)
    zero = jnp.zeros((SEQ, period), F32).at[:, lead:lead + rot].set(sin)
    cos_w = jnp.tile(one, (1, width // period))
    sin_w = jnp.tile(zero, (1, width // period))
    cos_w = jnp.concatenate([cos_w, jnp.ones((TM, width), F32)], axis=0)
    sin_w = jnp.concatenate([sin_w, jnp.zeros((TM, width), F32)], axis=0)
    return cos_w, sin_w


def _proj0_kernel(x_ref, sh_ref, sc_ref, w_ref, b_ref, cos_ref, sin_ref,
                  h_ref, q_ref, k_ref, v_ref):
    u = x_ref[...] * (1.0 + sc_ref[...]) + sh_ref[...]
    y = jnp.dot(u.astype(BF16), w_ref[...], preferred_element_type=F32) + b_ref[...]
    h_ref[...] = y[:, :CONV_CH] * _sigmoid(y[:, CONV_CH:2 * CONV_CH])
    cos = cos_ref[...]
    sin = sin_ref[...]
    q0 = 2 * CONV_CH
    k0 = q0 + WIN_HEADS * HEAD_DIM
    v0 = k0 + WIN_KV_HEADS * HEAD_DIM
    cos4 = jnp.concatenate([cos] * 4, axis=1)
    sin4 = jnp.concatenate([sin] * 4, axis=1)
    q = _rope(y[:, q0:k0], cos4, sin4, HEAD_DIM // 4) * (HEAD_DIM ** -0.5 * LOG2E)
    q_ref[...] = q.astype(BF16)
    k_ref[...] = _rope(y[:, k0:v0], cos, sin, HEAD_DIM // 4).astype(BF16)
    v_ref[...] = _values_with_ones(y[:, v0:].astype(BF16), WIN_KV_HEADS, HEAD_DIM)


def _proj0(x_all, mods, w_in, b_in, cos_hd, sin_hd):
    kvw = WIN_KV_HEADS * HEAD_DIM
    row = lambda w: pl.BlockSpec((TM, w), lambda i: (i, 0))
    tab = pl.BlockSpec((TM, 128), lambda i: (_rope_row_block(i), 0))
    return pl.pallas_call(
        _proj0_kernel,
        grid=(NT_ALL,),
        in_specs=[row(D_MODEL), _mod_spec(SH1), _mod_spec(SC1),
                  _full((D_MODEL, EVEN_IN)), _full((1, EVEN_IN)), tab, tab],
        out_specs=[row(CONV_CH), row(WIN_HEADS * HEAD_DIM), row(kvw), row(WIN_KV_HEADS * VAL_PAD)],
        out_shape=[jax.ShapeDtypeStruct((R_ALL, CONV_CH), F32),
                   jax.ShapeDtypeStruct((R_ALL, WIN_HEADS * HEAD_DIM), BF16),
                   jax.ShapeDtypeStruct((R_ALL, kvw), BF16),
                   jax.ShapeDtypeStruct((R_ALL, WIN_KV_HEADS * VAL_PAD), BF16)],
        compiler_params=_params(),
        name="proj0",
    )(x_all, mods, mods, w_in, b_in, cos_hd, sin_hd)


def _conv_kernel(prev_ref, cur_ref, next_ref, w_ref, cb_ref, g_ref, b_ref, o_ref, buf):
    i = pl.program_id(0)
    is_ctx = i >= NT_LAT
    first = jnp.logical_or(is_ctx, i % TILES_PER_SEQ == 0)
    last = jnp.logical_or(is_ctx, i % TILES_PER_SEQ == TILES_PER_SEQ - 1)
    buf[0, 0:HALO, :] = jnp.where(first, 0.0, prev_ref[...])
    buf[0, HALO:HALO + TM, :] = cur_ref[...]
    buf[0, HALO + TM:, :] = jnp.where(last, 0.0, next_ref[...])
    span = TM + 2 * HALO - SHIFTS
    for r in range(1, SHIFTS):
        buf[r, 0:span, :] = buf[0, r:r + span, :]
    off = HALO - CONV_WIDTH // 2
    for c in range(TM // CONV_CHUNK):
        r0 = c * CONV_CHUNK
        acc = jnp.zeros((CONV_CHUNK, CONV_CH), F32)
        for k in range(CONV_WIDTH):
            r = (off + k) % SHIFTS
            base = r0 + off + k - r
            acc = acc + buf[r, base:base + CONV_CHUNK, :] * w_ref[k:k + 1, :]
        z = _layer_norm(acc + cb_ref[...], g_ref[...], b_ref[...])
        o_ref[r0:r0 + CONV_CHUNK, :] = (z * _sigmoid(z)).astype(BF16)


def _conv(h, conv_w, conv_b, ln_g, ln_b):
    nh = R_ALL // HALO
    per = TM // HALO
    vec = _full((1, CONV_CH))
    return pl.pallas_call(
        _conv_kernel,
        grid=(NT_ALL,),
        in_specs=[pl.BlockSpec((HALO, CONV_CH), lambda i: (jnp.maximum(i * per - 1, 0), 0)),
                  pl.BlockSpec((TM, CONV_CH), lambda i: (i, 0)),
                  pl.BlockSpec((HALO, CONV_CH), lambda i: (jnp.minimum((i + 1) * per, nh - 1), 0)),
                  _full((CONV_WIDTH, CONV_CH)), vec, vec, vec],
        out_specs=pl.BlockSpec((TM, CONV_CH), lambda i: (i, 0)),
        out_shape=jax.ShapeDtypeStruct((R_ALL, CONV_CH), BF16),
        scratch_shapes=[pltpu.VMEM((SHIFTS, TM + 2 * HALO, CONV_CH), F32)],
        compiler_params=_params(),
        name="conv_module",
    )(h, h, h, conv_w, conv_b, ln_g, ln_b)


def _nt_dot(a, b):
    return lax.dot_general(a, b, (((1,), (1,)), ((), ())), preferred_element_type=F32)


def _values_with_ones(v, n_kv, dv):
    lane = lax.broadcasted_iota(jnp.int32, (v.shape[0], VAL_PAD - dv), 1)
    tail = jnp.where(lane == 0, 1.0, 0.0).astype(v.dtype)
    pieces = []
    for h in range(n_kv):
        pieces += [v[:, h * dv:(h + 1) * dv], tail]
    return jnp.concatenate(pieces, axis=1)


def _attend(units, dv):
    def scores(unit):
        q, ks, _, masks, _ = unit
        out = []
        for k, msk in zip(ks, masks):
            s = _nt_dot(q, k)
            if msk is not None:
                s = jnp.where(msk, s, NEG_INF)
            out.append(s)
        return out

    results = []
    ss = scores(units[0])
    for idx, unit in enumerate(units):
        nxt = scores(units[idx + 1]) if idx + 1 < len(units) else None
        _, _, vs, _, sink = unit
        m = functools.reduce(jnp.maximum, [jnp.max(s, axis=-1, keepdims=True) for s in ss])
        if sink is not None:
            m = jnp.maximum(m, sink)
        acc = functools.reduce(jnp.add, [jnp.dot(jnp.exp2(s - m).astype(BF16), v, preferred_element_type=F32)
                                         for s, v in zip(ss, vs)])
        l = acc[:, dv:dv + 1]
        if sink is not None:
            l = l + jnp.exp2(sink - m)
        results.append(acc[:, :dv] / l)
        ss = nxt
    return results


def _attend_keys_major(units, dv, s_buf, p_buf):
    def scores(unit, slot):
        q, ks, _ = unit
        row, ms = 0, []
        for k in ks:
            s = _nt_dot(k, q)
            s_buf[slot, row:row + k.shape[0], :] = s
            ms.append(jnp.max(s, axis=0, keepdims=True))
            row += k.shape[0]
        return functools.reduce(jnp.maximum, ms)

    def run_next_scores_with(slot, pieces):
        row = 0
        for k in pieces:
            tile = (slice(row + k.shape[0] - 8, row + k.shape[0]), slice(0, 128))
            s_buf[slot, tile[0], tile[1]] = s_buf[slot, tile[0], tile[1]] + 0.0 * s_buf[1 - slot, tile[0], tile[1]]
            row += k.shape[0]

    results = []
    m = scores(units[0], 0)
    for idx, unit in enumerate(units):
        slot = idx % 2
        m_next = None
        if idx + 1 < len(units):
            m_next = scores(units[idx + 1], 1 - slot)
            run_next_scores_with(slot, unit[1])
        row, acc = 0, None
        for vt in unit[2]:
            rows = slice(row, row + vt.shape[1])
            p_buf[slot, rows, :] = jnp.exp2(s_buf[slot, rows, :] - m).astype(BF16)
            part = jnp.dot(vt, p_buf[slot, rows, :], preferred_element_type=F32)
            acc = part if acc is None else acc + part
            row += vt.shape[1]
        results.append(acc[:dv] / acc[dv:dv + 1])
        m = m_next
    return results


def _win_kernel(sink_ref, q_ref, kp_ref, kc_ref, kn_ref, kx_ref, vp_ref, vc_ref, vn_ref, vx_ref, o_ref):
    n = pl.program_id(1)
    k_loc = jnp.concatenate([kp_ref[...], kc_ref[...], kn_ref[...]], axis=0)
    v_loc = jnp.concatenate([vp_ref[...], vc_ref[...], vn_ref[...]], axis=0)
    k_ctx = kx_ref[...]
    v_ctx = vx_ref[...]
    qi = lax.broadcasted_iota(jnp.int32, (WINDOW, 3 * WINDOW), 0)
    kj = lax.broadcasted_iota(jnp.int32, (WINDOW, 3 * WINDOW), 1)
    k_pos = jnp.where(n < SEQ // WINDOW, kj + (n - 1) * WINDOW, SEQ)
    valid = jnp.where(kj >= qi, jnp.where(kj <= qi + 2 * WINDOW, 1, 0), 0)
    valid = jnp.where(k_pos >= 0, jnp.where(k_pos < SEQ, valid, 0), 0) > 0
    group = WIN_HEADS // WIN_KV_HEADS
    units = []
    for h in range(WIN_HEADS):
        kv = h // group
        ksl = slice(kv * HEAD_DIM, (kv + 1) * HEAD_DIM)
        vsl = slice(kv * VAL_PAD, (kv + 1) * VAL_PAD)
        units.append((q_ref[:, h * HEAD_DIM:(h + 1) * HEAD_DIM], [k_ctx[:, ksl], k_loc[:, ksl]],
                      [v_ctx[:, vsl], v_loc[:, vsl]], [None, valid], sink_ref[h] * LOG2E))
    for h, o in enumerate(_attend(units, HEAD_DIM)):
        o_ref[:, h * HEAD_DIM:(h + 1) * HEAD_DIM] = o.astype(BF16)


def _win_attention(sink, q, k, v):
    nblk = SEQ // WINDOW
    cblk = CTX_LEN // WINDOW
    kvw = WIN_KV_HEADS * HEAD_DIM
    ctx0 = R_LAT // CTX_LEN
    lat = lambda n: jnp.minimum(n, nblk - 1)
    prev = lambda b, n: (b * nblk + jnp.maximum(lat(n) - 1, 0), 0)
    cur = lambda b, n: (b * nblk + lat(n), 0)
    nxt = lambda b, n: (b * nblk + jnp.minimum(lat(n) + 1, nblk - 1), 0)
    qrow = lambda b, n: (jnp.where(n < nblk, b * nblk + n, R_LAT // WINDOW + b * cblk + n - nblk), 0)
    ctx = lambda b, n: (ctx0 + b, 0)
    vw = WIN_KV_HEADS * VAL_PAD
    kvb = lambda f, w: pl.BlockSpec((WINDOW, w), f)
    cxb = lambda w: pl.BlockSpec((CTX_LEN, w), ctx)
    return pl.pallas_call(
        _win_kernel,
        grid=(BATCH, nblk + cblk),
        in_specs=[pl.BlockSpec(memory_space=pltpu.SMEM),
                  pl.BlockSpec((WINDOW, WIN_HEADS * HEAD_DIM), qrow),
                  kvb(prev, kvw), kvb(cur, kvw), kvb(nxt, kvw), cxb(kvw),
                  kvb(prev, vw), kvb(cur, vw), kvb(nxt, vw), cxb(vw)],
        out_specs=pl.BlockSpec((WINDOW, WIN_HEADS * HEAD_DIM), qrow),
        out_shape=jax.ShapeDtypeStruct((R_ALL, WIN_HEADS * HEAD_DIM), BF16),
        compiler_params=_params(),
        name="window_attention",
    )(sink, q, k, k, k, k, v, v, v, v)


def _outproj_kernel(a_ref, b_ref, wa_ref, wb_ref, bo_ref, x_ref, g1_ref, sh2_ref, sc2_ref,
                    lng_ref, lnb_ref, wrh_ref, wrl_ref, br_ref, upper_ref, lower_ref, sel_ref,
                    xo_ref, route_ref, cnt_ref, xs_ref):
    y = (jnp.dot(a_ref[...], wa_ref[...], preferred_element_type=F32)
         + jnp.dot(b_ref[...], wb_ref[...], preferred_element_type=F32) + bo_ref[...])
    xn = _layer_norm(DN_ALPHA * x_ref[...] + (1.0 + g1_ref[...]) * y, lng_ref[...], lnb_ref[...])
    xo_ref[...] = xn
    u2 = xn * (1.0 + sc2_ref[...]) + sh2_ref[...]
    u_hi = u2.astype(BF16)
    u_lo = (u2 - u_hi.astype(F32)).astype(BF16)
    logits = (jnp.dot(u_hi, wrh_ref[...], preferred_element_type=F32)
              + jnp.dot(u_lo, wrh_ref[...], preferred_element_type=F32)
              + jnp.dot(u_hi, wrl_ref[...], preferred_element_type=F32) + br_ref[...])
    lane = lax.broadcasted_iota(jnp.int32, logits.shape, 1).astype(F32)
    ninf = -jnp.inf
    big = float(ROUTE_W)
    gl = jnp.where(lane < N_GROUPS, logits, ninf)
    gmax = jnp.max(gl, axis=-1, keepdims=True)
    gidx = jnp.min(jnp.where(gl == gmax, lane, big), axis=-1, keepdims=True)
    g_w = 1.0 / jnp.sum(jnp.exp(gl - gmax), axis=-1, keepdims=True)
    lo = N_GROUPS + EXP_PER_GROUP * gidx
    el = jnp.where(lane >= lo, jnp.where(lane < lo + EXP_PER_GROUP, logits, ninf), ninf)
    v1 = jnp.max(el, axis=-1, keepdims=True)
    i1 = jnp.min(jnp.where(el == v1, lane, big), axis=-1, keepdims=True)
    el2 = jnp.where(lane == i1, ninf, el)
    v2 = jnp.max(el2, axis=-1, keepdims=True)
    i2 = jnp.min(jnp.where(el2 == v2, lane, big), axis=-1, keepdims=True)
    e2 = jnp.exp(v2 - v1)
    w1 = g_w / (1.0 + e2)
    w2 = g_w * e2 / (1.0 + e2)
    onehot = [jnp.where(lane == i1 - N_GROUPS, 1.0, 0.0), jnp.where(lane == i2 - N_GROUPS, 1.0, 0.0)]
    cnt = [jnp.sum(o, axis=0, keepdims=True) for o in onehot]
    run_units = jnp.floor((cnt[0] + cnt[1] + (RUN_ALIGN - 1)) * (1.0 / RUN_ALIGN))
    below = RUN_ALIGN * jnp.dot(jnp.broadcast_to(run_units, (8, ROUTE_W)).astype(BF16), upper_ref[...],
                                preferred_element_type=F32)[0:1]
    lower = lower_ref[...]
    base = [below, below + cnt[0]]
    lp = []
    for s in range(2):
        earlier = jnp.dot(lower, onehot[s].astype(BF16), preferred_element_type=F32)
        lp.append(jnp.sum(onehot[s] * (base[s] + earlier), axis=-1, keepdims=True))
    rec = jnp.where(lane == 0.0, i1 - N_GROUPS,
                    jnp.where(lane == 1.0, i2 - N_GROUPS,
                              jnp.where(lane == 2.0, w1,
                                        jnp.where(lane == 3.0, w2,
                                                  jnp.where(lane == 4.0, lp[0],
                                                            jnp.where(lane == 5.0, lp[1], 0.0))))))
    route_ref[...] = rec
    cnt_ref[...] = jnp.broadcast_to(run_units * RUN_ALIGN, (8, ROUTE_W))
    sel = sel_ref[...]
    pos = lax.broadcasted_iota(jnp.int32, (LOCAL_ROWS, TM), 0).astype(F32)
    lp_lanes = []
    for s in range(2):
        hi = jnp.floor(lp[s] * (1.0 / 256.0))
        parts = jnp.where(lane == 0.0, lp[s] - 256.0 * hi, jnp.where(lane == 1.0, hi, 0.0)).astype(BF16)
        t = _nt_dot(sel, parts)
        lp_lanes.append(t[0:1] + 256.0 * t[1:2])
    perm = jnp.where(pos == lp_lanes[0], 1.0, jnp.where(pos == lp_lanes[1], 1.0, 0.0)).astype(BF16)
    xs_ref[...] = jnp.dot(perm, u_hi, preferred_element_type=F32)


def _outproj(n_tiles, mix_a, mix_b, w_a, w_b, b_out, x_all, mods, ln_g, ln_b, w_r, b_r):
    rows = n_tiles * TM
    half = mix_a.shape[1]
    w_rh = w_r.astype(BF16)
    w_rl = (w_r - w_rh.astype(F32)).astype(BF16)
    upper = (jnp.arange(ROUTE_W)[:, None] < jnp.arange(ROUTE_W)[None, :]).astype(BF16)
    lower = (jnp.arange(TM)[:, None] > jnp.arange(TM)[None, :]).astype(BF16)
    sel = (jnp.arange(8)[:, None] == jnp.arange(ROUTE_W)[None, :]).astype(BF16)
    row = lambda w: pl.BlockSpec((TM, w), lambda i: (i, 0))
    vec = _full((1, D_MODEL))
    return pl.pallas_call(
        _outproj_kernel,
        grid=(n_tiles,),
        in_specs=[row(half), row(half), _full((half, D_MODEL)), _full((half, D_MODEL)), vec,
                  row(D_MODEL), _mod_spec(G1), _mod_spec(SH2), _mod_spec(SC2), vec, vec,
                  _full((D_MODEL, ROUTE_W)), _full((D_MODEL, ROUTE_W)), _full((1, ROUTE_W)),
                  _full((ROUTE_W, ROUTE_W)), _full((TM, TM)), _full((8, ROUTE_W))],
        out_specs=[row(D_MODEL), row(ROUTE_W), pl.BlockSpec((8, ROUTE_W), lambda i: (i, 0)),
                   pl.BlockSpec((LOCAL_ROWS, D_MODEL), lambda i: (i, 0))],
        out_shape=[jax.ShapeDtypeStruct((rows, D_MODEL), F32),
                   jax.ShapeDtypeStruct((rows, ROUTE_W), F32),
                   jax.ShapeDtypeStruct((n_tiles * 8, ROUTE_W), F32),
                   jax.ShapeDtypeStruct((n_tiles * LOCAL_ROWS, D_MODEL), F32)],
        compiler_params=_params(),
        name="outproj_ln_router",
    )(mix_a, mix_b, w_a, w_b, b_out, x_all, mods, mods, mods, ln_g, ln_b, w_rh, w_rl, b_r, upper, lower, sel)


def _aligned(i):
    return pl.multiple_of(i, RUN_ALIGN)


def _moe_kernel(te_ref, tk_ref, rows_ref, lo_ref, hi_ref, cnt_ref, bt_ref, be_ref, xs_hbm, w1_ref, w3_ref,
                w2_ref, ys_ref, xbuf, wb1, wb3, wb2, sem):
    j = pl.program_id(0)
    nt = pl.num_programs(0)
    slot = j % 2

    def issue(tile, slot_):
        e = te_ref[tile]
        first = tk_ref[tile] * TMM

        def body(i, carry):
            idx = i * N_EXPERTS + e
            start = bt_ref[idx]
            lo = jnp.maximum(start, first)
            n = jnp.minimum(start + cnt_ref[idx], first + TMM) - lo

            @pl.when(n > 0)
            def _():
                src = i * LOCAL_ROWS + be_ref[idx] + lo - start
                pltpu.make_async_copy(xs_hbm.at[pl.ds(_aligned(src), _aligned(n))],
                                      xbuf.at[slot_, pl.ds(_aligned(lo - first), _aligned(n))],
                                      sem.at[slot_]).start()
            return carry
        lax.fori_loop(lo_ref[tile], hi_ref[tile], body, 0)

    @pl.when(j == 0)
    def _():
        xbuf[...] = jnp.zeros_like(xbuf)
        issue(0, 0)

    @pl.when(j + 1 < nt)
    def _():
        issue(j + 1, 1 - slot)

    @pl.when(jnp.logical_or(j == 0, te_ref[j] != te_ref[jnp.maximum(j - 1, 0)]))
    def _():
        wb1[...] = w1_ref[...].astype(BF16)
        wb3[...] = w3_ref[...].astype(BF16)
        wb2[...] = w2_ref[...].astype(BF16)

    n_real = rows_ref[j]

    @pl.when(n_real > 0)
    def _():
        pltpu.make_async_copy(xs_hbm.at[pl.ds(0, _aligned(n_real))], xbuf.at[slot, pl.ds(0, _aligned(n_real))],
                              sem.at[slot]).wait()
        x = xbuf[slot].astype(BF16)
        h1 = jnp.dot(x, wb1[...], preferred_element_type=F32)
        h3 = jnp.dot(x, wb3[...], preferred_element_type=F32)
        hid = h1 * _sigmoid(h1) * h3
        ys_ref[...] = jnp.dot(hid.astype(BF16), wb2[...], preferred_element_type=F32)

    @pl.when(n_real == 0)
    def _():
        ys_ref[...] = jnp.zeros_like(ys_ref)


def _moe_experts(layer, plan, xs_local, w1, w3, w2):
    tile_expert, tile_k, tile_rows, src_lo, src_hi, cnt, before_tile, before_expert, _, _ = plan
    nt = tile_expert.shape[0]
    wmap = lambda j, te, *_: (layer, te[j], 0, 0)
    grid_spec = pltpu.PrefetchScalarGridSpec(
        num_scalar_prefetch=8,
        grid=(nt,),
        in_specs=[pl.BlockSpec(memory_space=pl.ANY),
                  pl.BlockSpec((None, None, D_MODEL, EXPERT_FF), wmap),
                  pl.BlockSpec((None, None, D_MODEL, EXPERT_FF), wmap),
                  pl.BlockSpec((None, None, EXPERT_FF, D_MODEL), wmap)],
        out_specs=pl.BlockSpec((TMM, D_MODEL), lambda j, *_: (j, 0)),
        scratch_shapes=[pltpu.VMEM((2, TMM, D_MODEL), F32),
                        pltpu.VMEM((D_MODEL, EXPERT_FF), BF16),
                        pltpu.VMEM((D_MODEL, EXPERT_FF), BF16),
                        pltpu.VMEM((EXPERT_FF, D_MODEL), BF16),
                        pltpu.SemaphoreType.DMA((2,))])
    return pl.pallas_call(
        _moe_kernel,
        grid_spec=grid_spec,
        out_shape=jax.ShapeDtypeStruct((nt * TMM, D_MODEL), F32),
        compiler_params=_params(),
        name="moe_experts",
    )(tile_expert, tile_k, tile_rows, src_lo, src_hi, cnt, before_tile, before_expert, xs_local, w1, w3, w2)


def _combine_kernel(cnt_ref, bt_ref, be_ref, gs_ref, used_ref, ys_hbm, x_ref, route_ref, g2_ref, lng_ref, lnb_ref,
                    o_ref, ybuf, sem):
    i = pl.program_id(0)
    nt = pl.num_programs(0)
    slot = i % 2

    def issue(tile, slot_):
        def body(e, carry):
            idx = tile * N_EXPERTS + e
            n = cnt_ref[idx]

            @pl.when(n > 0)
            def _():
                pltpu.make_async_copy(ys_hbm.at[pl.ds(_aligned(gs_ref[e] + bt_ref[idx]), _aligned(n))],
                                      ybuf.at[slot_, pl.ds(_aligned(be_ref[idx]), _aligned(n))],
                                      sem.at[slot_]).start()
            return carry
        lax.fori_loop(0, N_EXPERTS, body, 0)

    @pl.when(i == 0)
    def _():
        ybuf[...] = jnp.zeros_like(ybuf)
        issue(0, 0)

    @pl.when(i + 1 < nt)
    def _():
        issue(i + 1, 1 - slot)

    used = _aligned(used_ref[i])
    pltpu.make_async_copy(ys_hbm.at[pl.ds(0, used)], ybuf.at[slot, pl.ds(0, used)], sem.at[slot]).wait()
    route = route_ref[...]
    pos = lax.broadcasted_iota(jnp.int32, (TM, LOCAL_ROWS), 1).astype(F32)
    sel = (jnp.where(pos == route[:, 4:5], route[:, 2:3], 0.0)
           + jnp.where(pos == route[:, 5:6], route[:, 3:4], 0.0))
    sel_hi = sel.astype(BF16)
    sel_lo = (sel - sel_hi.astype(F32)).astype(BF16)
    y = ybuf[slot]
    y_hi = y.astype(BF16)
    y_lo = (y - y_hi.astype(F32)).astype(BF16)
    f = (jnp.dot(sel_hi, y_hi, preferred_element_type=F32) + jnp.dot(sel_lo, y_hi, preferred_element_type=F32)
         + jnp.dot(sel_hi, y_lo, preferred_element_type=F32))
    z = DN_ALPHA * x_ref[...] + (1.0 + g2_ref[...]) * f
    o_ref[...] = _layer_norm(z, lng_ref[...], lnb_ref[...])


def _moe_combine(n_tiles, plan, ys, x_all, route, mods, ln_g, ln_b):
    _, _, _, _, _, cnt, before_tile, before_expert, group_start, used = plan
    rows = n_tiles * TM
    row = lambda w: pl.BlockSpec((TM, w), lambda i, *_: (i, 0))
    vec = pl.BlockSpec((1, D_MODEL), lambda i, *_: (0, 0))
    grid_spec = pltpu.PrefetchScalarGridSpec(
        num_scalar_prefetch=5,
        grid=(n_tiles,),
        in_specs=[pl.BlockSpec(memory_space=pl.ANY), row(D_MODEL), row(ROUTE_W),
                  pl.BlockSpec((None, None, 1, D_MODEL), lambda i, *_: (_mod_row(i), G2, 0, 0)), vec, vec],
        out_specs=row(D_MODEL),
        scratch_shapes=[pltpu.VMEM((2, LOCAL_ROWS, D_MODEL), F32), pltpu.SemaphoreType.DMA((2,))])
    return pl.pallas_call(
        _combine_kernel,
        grid_spec=grid_spec,
        out_shape=jax.ShapeDtypeStruct((rows, D_MODEL), F32),
        compiler_params=_params(),
        name="moe_combine_ln",
    )(cnt, before_tile, before_expert, group_start, used, ys, x_all, route, mods, ln_g, ln_b)


def _moe_plan(cnt_rec, n_tiles):
    cnt = cnt_rec.reshape(n_tiles, 8, ROUTE_W)[:, 0, :N_EXPERTS].astype(jnp.int32)
    nt_max = (n_tiles * (2 * TM + N_EXPERTS * (RUN_ALIGN - 1))) // TMM + N_EXPERTS
    total = jnp.sum(cnt, axis=0)
    tiles_e = (total + TMM - 1) // TMM
    tile_end = jnp.cumsum(tiles_e)
    first_tile = tile_end - tiles_e
    before_tile = jnp.cumsum(cnt, axis=0) - cnt
    before_expert = jnp.cumsum(cnt, axis=1) - cnt
    tile_id = jnp.arange(nt_max, dtype=jnp.int32)
    tile_expert = jnp.minimum(jnp.sum((tile_id[:, None] >= tile_end[None, :]).astype(jnp.int32), axis=1),
                              N_EXPERTS - 1)
    tile_k = tile_id - first_tile[tile_expert]
    tile_rows = jnp.clip(total[tile_expert] - tile_k * TMM, 0, TMM)
    first = (tile_k * TMM)[:, None]
    run_start = before_tile.T[tile_expert]
    run_end = run_start + cnt.T[tile_expert]
    src_lo = jnp.sum((run_end <= first).astype(jnp.int32), axis=1)
    src_hi = jnp.sum((run_start < first + TMM).astype(jnp.int32), axis=1)
    return (tile_expert, tile_k, tile_rows, src_lo, src_hi, cnt.reshape(-1), before_tile.reshape(-1),
            before_expert.reshape(-1), first_tile * TMM, jnp.sum(cnt, axis=1))


def _router_weights(w_rg, b_rg, w_re, b_re):
    w = jnp.concatenate([w_rg, jnp.transpose(w_re, (1, 0, 2)).reshape(D_MODEL, N_EXPERTS)], axis=1)
    b = jnp.concatenate([b_rg, b_re.reshape(-1)])
    pad = ROUTE_W - w.shape[1]
    return jnp.pad(w, ((0, 0), (0, pad))), jnp.pad(b, (0, pad)).reshape(1, ROUTE_W)


def _proj1_kernel(x_ref, sh_ref, sc_ref, w_ref, b_ref, cos_ref, sin_ref, cosm_ref, sinm_ref, cosr_ref,
                  sinr_ref, gq_ref, gk_ref, gqc_ref, gkv_ref, avg_ref, wuq_ref, wuk_ref, wuv_ref, vplace_ref,
                  q_ref, qm_ref, k_ref, v_ref, km_ref, vm_ref):
    u = x_ref[...] * (1.0 + sc_ref[...]) + sh_ref[...]
    y = jnp.dot(u.astype(BF16), w_ref[...], preferred_element_type=F32) + b_ref[...]
    c_q = GQA_HEADS * HEAD_DIM
    c_qc = c_q + MLA_Q_RANK
    c_k = c_qc + GQA_KV_HEADS * HEAD_DIM
    c_v = c_k + GQA_KV_HEADS * HEAD_DIM
    c_kv = c_v + MLA_KV_RANK
    avg = avg_ref[...]

    def head_rms(t, gain):
        sq = t * t
        hi = sq.astype(BF16)
        lo = (sq - hi.astype(F32)).astype(BF16)
        a = avg[:t.shape[1], :t.shape[1]]
        ms = jnp.dot(hi, a, preferred_element_type=F32) + jnp.dot(lo, a, preferred_element_type=F32)
        return t * lax.rsqrt(ms + RMS_EPS) * gain

    def row_rms(t, gain):
        ms = jnp.mean(t * t, axis=-1, keepdims=True)
        return t * lax.rsqrt(ms + RMS_EPS) * gain

    cos = cos_ref[...]
    sin = sin_ref[...]
    cos4 = jnp.concatenate([cos] * 4, axis=1)
    sin4 = jnp.concatenate([sin] * 4, axis=1)
    q = _rope(head_rms(y[:, :c_q], gq_ref[...]), cos4, sin4, HEAD_DIM // 4) * (HEAD_DIM ** -0.5 * LOG2E)
    q_ref[...] = q.astype(BF16)
    k = _rope(head_rms(y[:, c_qc:c_k], gk_ref[...]), cos, sin, HEAD_DIM // 4)
    k_ref[...] = k.astype(BF16)

    def transposed_values(w_t, src, dv):
        vt = _nt_dot(w_t, src)
        r = lax.broadcasted_iota(jnp.int32, vt.shape, 0)
        return jnp.where(r % VAL_PAD == dv, 1.0, vt).astype(BF16)

    v_ref[...] = transposed_values(vplace_ref[...], y[:, c_k:c_v].astype(BF16), HEAD_DIM)

    qc = row_rms(y[:, c_q:c_qc], gqc_ref[...]).astype(BF16)
    qm = jnp.dot(qc, wuq_ref[...], preferred_element_type=F32)
    cosm = jnp.concatenate([cosm_ref[...]] * MLA_HEADS, axis=1)
    sinm = jnp.concatenate([sinm_ref[...]] * MLA_HEADS, axis=1)
    qm = _rope(qm, cosm, sinm, MLA_ROPE // 4) * ((MLA_NOPE + MLA_ROPE) ** -0.5 * LOG2E)
    qm_ref[...] = qm.astype(BF16)

    kvn = row_rms(y[:, c_v:c_kv], gkv_ref[...]).astype(BF16)
    kr = _rope(y[:, c_kv:], cosr_ref[...], sinr_ref[...], MLA_ROPE // 4).astype(BF16)
    km = jnp.dot(jnp.concatenate([kvn, kr], axis=1), wuk_ref[...], preferred_element_type=F32)
    km_ref[...] = km.astype(BF16)
    vm_ref[...] = transposed_values(wuv_ref[...], kvn, MLA_V)


def _proj1(x_all, mods, w_in, b_in, tabs, gq, gk, gqc, gkv, avg, wuq, wuk, wuv):
    cos_hd, sin_hd, cos_m, sin_m, cos_r, sin_r = tabs
    kvw = GQA_KV_HEADS * HEAD_DIM
    qw = GQA_HEADS * HEAD_DIM
    mw = MLA_HEADS * MLA_PAD
    vw = MLA_HEADS * VAL_PAD
    gvw = GQA_KV_HEADS * VAL_PAD
    row = lambda w: pl.BlockSpec((TM, w), lambda i: (i, 0))
    col = lambda h: pl.BlockSpec((h, TM), lambda i: (0, i))
    tab = pl.BlockSpec((TM, 128), lambda i: (_rope_row_block(i), 0))
    r = jnp.arange(gvw)[:, None]
    c = jnp.arange(kvw)[None, :]
    vplace = jnp.logical_and(r // VAL_PAD == c // HEAD_DIM, r % VAL_PAD == c % HEAD_DIM).astype(BF16)
    return pl.pallas_call(
        _proj1_kernel,
        grid=(NT_ALL,),
        in_specs=[row(D_MODEL), _mod_spec(SH1), _mod_spec(SC1),
                  _full((D_MODEL, ODD_IN_PAD)), _full((1, ODD_IN_PAD)), tab, tab, tab, tab, tab, tab,
                  _full((1, qw)), _full((1, kvw)), _full((1, MLA_Q_RANK)), _full((1, MLA_KV_RANK)),
                  _full((qw, qw)), _full((MLA_Q_RANK, mw)), _full((MLA_KV_RANK + 128, mw)),
                  _full((vw, MLA_KV_RANK)), _full((gvw, kvw))],
        out_specs=[row(qw), row(mw), row(kvw), col(gvw), row(mw), col(vw)],
        out_shape=[jax.ShapeDtypeStruct((R_ALL, qw), BF16),
                   jax.ShapeDtypeStruct((R_ALL, mw), BF16),
                   jax.ShapeDtypeStruct((R_ALL, kvw), BF16),
                   jax.ShapeDtypeStruct((gvw, R_ALL), BF16),
                   jax.ShapeDtypeStruct((R_ALL, mw), BF16),
                   jax.ShapeDtypeStruct((vw, R_ALL), BF16)],
        compiler_params=_params(),
        name="proj1",
    )(x_all, mods, mods, w_in, b_in, cos_hd, sin_hd, cos_m, sin_m, cos_r, sin_r,
      gq, gk, gqc, gkv, avg, wuq, wuk, wuv.T, vplace)


def _dense_kernel(q_ref, kl_ref, kc_ref, vl_ref, vc_ref, o_ref, s_buf, p_buf, *, n_heads, group, stack, dk, dv):
    tq = q_ref.shape[0]
    units = []
    for h0 in range(0, n_heads, stack):
        kv = h0 // group
        qs = [q_ref[:, h * dk:(h + 1) * dk] for h in range(h0, h0 + stack)]
        q = qs[0] if stack == 1 else jnp.concatenate(qs, axis=0)
        ks = slice(kv * dk, (kv + 1) * dk)
        vs = slice(kv * VAL_PAD, (kv + 1) * VAL_PAD)
        parts = [slice(c * (SEQ // KEY_PARTS), (c + 1) * (SEQ // KEY_PARTS)) for c in range(KEY_PARTS)]
        units.append((q, [kl_ref[c, ks] for c in parts] + [kc_ref[:, ks]],
                      [vl_ref[vs, c] for c in parts] + [vc_ref[vs, :]]))
    for u, o_t in enumerate(_attend_keys_major(units, dv, s_buf, p_buf)):
        o = o_t.T
        for g in range(stack):
            h = u * stack + g
            o_ref[:, h * dv:(h + 1) * dv] = o[g * tq:(g + 1) * tq].astype(BF16)


def _dense_attention(q, k, v, *, n_heads, group, stack, dk, dv, tq, name):
    n_kv = n_heads // group
    nq = SEQ // tq
    ctx0 = R_LAT // CTX_LEN
    lat = lambda w: pl.BlockSpec((SEQ, w), lambda b, j: (b, 0), pipeline_mode=pl.Buffered(1))
    ctx = lambda w: pl.BlockSpec((CTX_LEN, w), lambda b, j: (ctx0 + b, 0))
    lat_t = pl.BlockSpec((n_kv * VAL_PAD, SEQ), lambda b, j: (0, b), pipeline_mode=pl.Buffered(1))
    ctx_t = pl.BlockSpec((n_kv * VAL_PAD, CTX_LEN), lambda b, j: (0, ctx0 + b))
    return pl.pallas_call(
        functools.partial(_dense_kernel, n_heads=n_heads, group=group, stack=stack, dk=dk, dv=dv),
        grid=(BATCH, nq),
        in_specs=[pl.BlockSpec((tq, n_heads * dk), lambda b, j: (b * nq + j, 0)),
                  lat(n_kv * dk), ctx(n_kv * dk), lat_t, ctx_t],
        out_specs=pl.BlockSpec((tq, n_heads * dv), lambda b, j: (b * nq + j, 0)),
        out_shape=jax.ShapeDtypeStruct((R_LAT, n_heads * dv), BF16),
        scratch_shapes=[pltpu.VMEM((2, SEQ + CTX_LEN, stack * tq), F32),
                        pltpu.VMEM((2, SEQ + CTX_LEN, stack * tq), BF16)],
        compiler_params=_params(),
        name=name,
    )(q, k, k, v, v)


def _mla_weights(w_uq, w_ukv):
    wq = w_uq.reshape(MLA_Q_RANK, MLA_HEADS, MLA_NOPE + MLA_ROPE)
    wq = jnp.pad(wq, ((0, 0), (0, 0), (0, MLA_PAD - MLA_NOPE - MLA_ROPE))).reshape(MLA_Q_RANK, -1)
    wkv = w_ukv.reshape(MLA_KV_RANK, MLA_HEADS, MLA_NOPE + MLA_V)
    wk = jnp.pad(wkv[:, :, :MLA_NOPE], ((0, 0), (0, 0), (0, MLA_PAD - MLA_NOPE))).reshape(MLA_KV_RANK, -1)
    wv = jnp.pad(wkv[:, :, MLA_NOPE:], ((0, 0), (0, 0), (0, VAL_PAD - MLA_V))).reshape(MLA_KV_RANK, -1)
    r = jnp.arange(128)[:, None]
    c = jnp.arange(MLA_HEADS * MLA_PAD)[None, :]
    place = jnp.logical_and(r < MLA_ROPE, (c % MLA_PAD) == MLA_NOPE + r).astype(F32)
    wk = jnp.concatenate([wk, place], axis=0)
    return wq.astype(BF16), wk.astype(BF16), wv.astype(BF16)


def kernel(x, c, ctx, c_ctx, even_w_in, even_b_in, even_conv_w, even_conv_b, even_conv_ln_g, even_conv_ln_b, even_sink, even_w_out, even_b_out, odd_w_in, odd_b_in, odd_q_norm, odd_k_norm, odd_mla_q_norm, odd_mla_kv_norm, odd_mla_w_uq, odd_mla_w_ukv, odd_w_out, odd_b_out, ada_w, ada_b, ln1_g, ln1_b, ln2_g, ln2_b, moe_w_rg, moe_b_rg, moe_w_re, moe_b_re, moe_w1, moe_w3, moe_w2):
    vec = lambda a: a.reshape(1, -1)
    x_all = jnp.concatenate([x.reshape(R_LAT, D_MODEL), ctx.reshape(R_CTX, D_MODEL)], axis=0)

    cv = jnp.concatenate([c, c_ctx[None, :], jnp.zeros((8 - BATCH - 1, D_MODEL), F32)], axis=0)
    mods = _ada_table(cv, ada_w, ada_b).reshape(DEPTH, 8, 6, 1, D_MODEL)

    cos64, sin64 = _rope_tables(HEAD_DIM)
    cos_hd, sin_hd = _pad_table(cos64, sin64, 0, HEAD_DIM, 128)
    cos32, sin32 = _rope_tables(MLA_ROPE)
    cos_m, sin_m = _pad_table(cos32, sin32, MLA_NOPE, MLA_PAD, 128)
    cos_r, sin_r = _pad_table(cos32, sin32, 0, 128, 128)

    m0 = mods[0]
    h, q0, k0, v0 = _proj0(x_all, m0, even_w_in[0].astype(BF16), vec(even_b_in[0]), cos_hd, sin_hd)
    conv_out = _conv(h, even_conv_w[0].reshape(CONV_WIDTH, CONV_CH), vec(even_conv_b[0]),
                     vec(even_conv_ln_g[0]), vec(even_conv_ln_b[0]))
    attn = _win_attention(even_sink[0], q0, k0, v0)
    w_out = even_w_out[0].astype(BF16)
    w_r, b_r = _router_weights(moe_w_rg[0], moe_b_rg[0], moe_w_re[0], moe_b_re[0])
    x_all, route, cnt_rec, xs_local = _outproj(
        NT_ALL, conv_out, attn, w_out[:CONV_CH], w_out[CONV_CH:], vec(even_b_out[0]),
        x_all, m0, vec(ln1_g[0]), vec(ln1_b[0]), w_r, b_r)
    plan = _moe_plan(cnt_rec, NT_ALL)
    ys = _moe_experts(0, plan, xs_local, moe_w1, moe_w3, moe_w2)
    x_all = _moe_combine(NT_ALL, plan, ys, x_all, route, m0, vec(ln2_g[0]), vec(ln2_b[0]))

    m1 = mods[1]
    w_in1 = jnp.pad(odd_w_in[0], ((0, 0), (0, ODD_IN_PAD - ODD_IN))).astype(BF16)
    b_in1 = jnp.pad(odd_b_in[0], (0, ODD_IN_PAD - ODD_IN)).reshape(1, -1)
    wuq, wuk, wuv = _mla_weights(odd_mla_w_uq[0], odd_mla_w_ukv[0])
    qw = GQA_HEADS * HEAD_DIM
    hid = jnp.arange(qw) // HEAD_DIM
    avg = ((hid[:, None] == hid[None, :]).astype(F32) / HEAD_DIM).astype(BF16)
    q1, qm, k1, v1, km, vm = _proj1(
        x_all, m1, w_in1, b_in1, (cos_hd, sin_hd, cos_m, sin_m, cos_r, sin_r),
        vec(jnp.tile(odd_q_norm[0], GQA_HEADS)), vec(jnp.tile(odd_k_norm[0], GQA_KV_HEADS)),
        vec(odd_mla_q_norm[0]), vec(odd_mla_kv_norm[0]), avg, wuq, wuk, wuv)
    o_g = _dense_attention(q1, k1, v1, n_heads=GQA_HEADS, group=GQA_HEADS // GQA_KV_HEADS, stack=2,
                           dk=HEAD_DIM, dv=HEAD_DIM, tq=256, name="gqa_attention")
    o_m = _dense_attention(qm, km, vm, n_heads=MLA_HEADS, group=1, stack=1, dk=MLA_PAD, dv=MLA_V, tq=256,
                           name="mla_attention")
    w_out = odd_w_out[0].astype(BF16)
    w_r, b_r = _router_weights(moe_w_rg[1], moe_b_rg[1], moe_w_re[1], moe_b_re[1])
    x_lat, route, cnt_rec, xs_local = _outproj(
        NT_LAT, o_g, o_m, w_out[:qw], w_out[qw:], vec(odd_b_out[0]),
        x_all, m1, vec(ln1_g[1]), vec(ln1_b[1]), w_r, b_r)
    plan = _moe_plan(cnt_rec, NT_LAT)
    ys = _moe_experts(1, plan, xs_local, moe_w1, moe_w3, moe_w2)
    x_lat = _moe_combine(NT_LAT, plan, ys, x_lat, route, m1, vec(ln2_g[1]), vec(ln2_b[1]))
    return x_lat.reshape(BATCH, SEQ, D_MODEL)
```

```python
import functools

import jax
import jax.numpy as jnp
from jax import lax
from jax.experimental import pallas as pl
from jax.experimental.pallas import tpu as pltpu

F32 = jnp.float32
BF16 = jnp.bfloat16

D_MODEL = 1024
BATCH = 4
SEQ = 4096
DEPTH = 2
GRID_W = 64
CTX_LEN = 256
HEAD_DIM = 64
ROPE_THETA = 10000.0
LN_EPS = 1e-5
RMS_EPS = 1e-6
NEG_INF = -1e30

CONV_CH = 512
CONV_WIDTH = 31
WIN_HEADS = 8
WIN_KV_HEADS = 2
WINDOW = 128
GQA_HEADS = 8
GQA_KV_HEADS = 2
MLA_HEADS = 8
MLA_Q_RANK = 256
MLA_KV_RANK = 128
MLA_NOPE = 64
MLA_ROPE = 32
MLA_V = 64
N_GROUPS = 4
EXP_PER_GROUP = 8
N_EXPERTS = N_GROUPS * EXP_PER_GROUP
EXPERT_FF = 512
DN_ALPHA = float((2 * DEPTH) ** 0.25)

EVEN_IN = 2 * CONV_CH + (WIN_HEADS + 2 * WIN_KV_HEADS) * HEAD_DIM
ODD_IN = 1184
ODD_IN_PAD = 1280
MLA_PAD = 128
VAL_PAD = 128
KEY_PARTS = 4
LOG2E = 1.4426950408889634

R_LAT = BATCH * SEQ
R_CTX = BATCH * CTX_LEN
R_ALL = R_LAT + R_CTX
TM = 256
NT_LAT = R_LAT // TM
NT_ALL = R_ALL // TM
TILES_PER_SEQ = SEQ // TM
HALO = 16
CONV_CHUNK = 32
SHIFTS = 8
OUT_SUB = 2
TMM = 256
ROUTE_W = 128
RUN_ALIGN = 8
LOCAL_ROWS = 768
VMEM_LIMIT = 56 * 1024 * 1024

SH1, SC1, G1, SH2, SC2, G2 = range(6)


def _sigmoid(x):
    return 1.0 / (1.0 + jnp.exp(-x))


def _layer_norm(z, g, b):
    mu = jnp.mean(z, axis=-1, keepdims=True)
    zc = z - mu
    var = jnp.mean(zc * zc, axis=-1, keepdims=True)
    return zc * lax.rsqrt(var + LN_EPS) * g + b


def _rope(x, cos, sin, half):
    n = x.shape[-1]
    lane = lax.broadcasted_iota(jnp.int32, x.shape, 1)
    first = (lane % (2 * half)) < half
    partner = jnp.where(first, pltpu.roll(x, n - half, 1), pltpu.roll(x, half, 1))
    return x * cos + partner * sin


def _mod_row(i):
    return jnp.where(i < NT_LAT, i // TILES_PER_SEQ, BATCH)


def _mod_spec(chunk):
    return pl.BlockSpec((None, None, 1, D_MODEL), lambda i: (_mod_row(i), chunk, 0, 0))


def _rope_row_block(i):
    return jnp.where(i < NT_LAT, i % TILES_PER_SEQ, TILES_PER_SEQ)


def _full(shape):
    nd = len(shape)
    return pl.BlockSpec(shape, lambda *_: (0,) * nd)


def _params():
    return pltpu.CompilerParams(vmem_limit_bytes=VMEM_LIMIT)


def _ada_kernel(cv_ref, w_ref, b_ref, o_ref):
    cv = cv_ref[...]
    s = cv * _sigmoid(cv)
    o_ref[...] = jnp.dot(s, w_ref[...], precision=lax.Precision.HIGHEST,
                         preferred_element_type=F32) + b_ref[...]


def _ada_table(cv, ada_w, ada_b):
    bn = 1536
    nb = (6 * D_MODEL) // bn
    return pl.pallas_call(
        _ada_kernel,
        grid=(DEPTH, nb),
        in_specs=[pl.BlockSpec((8, D_MODEL), lambda l, j: (0, 0)),
                  pl.BlockSpec((None, D_MODEL, bn), lambda l, j: (l, 0, j)),
                  pl.BlockSpec((None, 1, bn), lambda l, j: (l, 0, j))],
        out_specs=pl.BlockSpec((None, 8, bn), lambda l, j: (l, 0, j)),
        out_shape=jax.ShapeDtypeStruct((DEPTH, 8, 6 * D_MODEL), F32),
        compiler_params=_params(),
        name="ada_table",
    )(cv, ada_w, ada_b.reshape(DEPTH, 1, 6 * D_MODEL))


def _rope_tables(rot_dim):
    axis_dim = rot_dim // 2
    inv_freq = ROPE_THETA ** (-jnp.arange(0, axis_dim, 2, dtype=F32) / axis_dim)
    t = jnp.arange(SEQ)
    ang_r = (t // GRID_W).astype(F32)[:, None] * inv_freq[None, :]
    ang_c = (t % GRID_W).astype(F32)[:, None] * inv_freq[None, :]
    cos = jnp.concatenate([jnp.cos(ang_r), jnp.cos(ang_r), jnp.cos(ang_c), jnp.cos(ang_c)], axis=-1)
    sin = jnp.concatenate([-jnp.sin(ang_r), jnp.sin(ang_r), -jnp.sin(ang_c), jnp.sin(ang_c)], axis=-1)
    return cos, sin


def _pad_table(cos, sin, lead, period, width):
    rot = cos.shape[1]
    one = jnp.ones((SEQ, period), F32).at[:, lead:lead + rot].set(cos)
    zero = jnp.zeros((SEQ, period), F32).at[:, lead:lead + rot].set(sin)
    cos_w = jnp.tile(one, (1, width // period))
    sin_w = jnp.tile(zero, (1, width // period))
    cos_w = jnp.concatenate([cos_w, jnp.ones((TM, width), F32)], axis=0)
    sin_w = jnp.concatenate([sin_w, jnp.zeros((TM, width), F32)], axis=0)
    return cos_w, sin_w


def _proj0_kernel(xl_ref, xc_ref, sh_ref, sc_ref, w_ref, b_ref, cos_ref, sin_ref,
                  h_ref, q_ref, k_ref, v_ref):
    x = jnp.where(pl.program_id(0) < NT_LAT, xl_ref[...], xc_ref[...])
    u = x * (1.0 + sc_ref[...]) + sh_ref[...]
    y = jnp.dot(u.astype(BF16), w_ref[...], preferred_element_type=F32) + b_ref[...]
    h_ref[...] = y[:, :CONV_CH] * _sigmoid(y[:, CONV_CH:2 * CONV_CH])
    cos = cos_ref[...]
    sin = sin_ref[...]
    q0 = 2 * CONV_CH
    k0 = q0 + WIN_HEADS * HEAD_DIM
    v0 = k0 + WIN_KV_HEADS * HEAD_DIM
    cos4 = jnp.concatenate([cos] * 4, axis=1)
    sin4 = jnp.concatenate([sin] * 4, axis=1)
    q = _rope(y[:, q0:k0], cos4, sin4, HEAD_DIM // 4) * (HEAD_DIM ** -0.5 * LOG2E)
    q_ref[...] = q.astype(BF16)
    k_ref[...] = _rope(y[:, k0:v0], cos, sin, HEAD_DIM // 4).astype(BF16)
    v_ref[...] = _values_with_ones(y[:, v0:].astype(BF16), WIN_KV_HEADS, HEAD_DIM)


def _proj0(x_lat, x_ctx, mods, w_in, b_in, cos_hd, sin_hd):
    kvw = WIN_KV_HEADS * HEAD_DIM
    row = lambda w: pl.BlockSpec((TM, w), lambda i: (i, 0))
    tab = pl.BlockSpec((TM, 128), lambda i: (_rope_row_block(i), 0))
    return pl.pallas_call(
        _proj0_kernel,
        grid=(NT_ALL,),
        in_specs=[pl.BlockSpec((TM, D_MODEL), lambda i: (jnp.minimum(i, NT_LAT - 1), 0)),
                  pl.BlockSpec((TM, D_MODEL), lambda i: (jnp.maximum(i - NT_LAT, 0), 0)),
                  _mod_spec(SH1), _mod_spec(SC1),
                  _full((D_MODEL, EVEN_IN)), _full((1, EVEN_IN)), tab, tab],
        out_specs=[row(CONV_CH), row(WIN_HEADS * HEAD_DIM), row(kvw), row(WIN_KV_HEADS * VAL_PAD)],
        out_shape=[jax.ShapeDtypeStruct((R_ALL, CONV_CH), F32),
                   jax.ShapeDtypeStruct((R_ALL, WIN_HEADS * HEAD_DIM), BF16),
                   jax.ShapeDtypeStruct((R_ALL, kvw), BF16),
                   jax.ShapeDtypeStruct((R_ALL, WIN_KV_HEADS * VAL_PAD), BF16)],
        compiler_params=_params(),
        name="proj0",
    )(x_lat, x_ctx, mods, mods, w_in, b_in, cos_hd, sin_hd)


def _conv_kernel(prev_ref, cur_ref, next_ref, w_ref, cb_ref, g_ref, b_ref, o_ref, buf):
    i = pl.program_id(0)
    is_ctx = i >= NT_LAT
    first = jnp.logical_or(is_ctx, i % TILES_PER_SEQ == 0)
    last = jnp.logical_or(is_ctx, i % TILES_PER_SEQ == TILES_PER_SEQ - 1)
    buf[0, 0:HALO, :] = jnp.where(first, 0.0, prev_ref[...])
    buf[0, HALO:HALO + TM, :] = cur_ref[...]
    buf[0, HALO + TM:, :] = jnp.where(last, 0.0, next_ref[...])
    span = TM + 2 * HALO - SHIFTS
    for r in range(1, SHIFTS):
        buf[r, 0:span, :] = buf[0, r:r + span, :]
    off = HALO - CONV_WIDTH // 2
    for c in range(TM // CONV_CHUNK):
        r0 = c * CONV_CHUNK
        acc = jnp.zeros((CONV_CHUNK, CONV_CH), F32)
        for k in range(CONV_WIDTH):
            r = (off + k) % SHIFTS
            base = r0 + off + k - r
            acc = acc + buf[r, base:base + CONV_CHUNK, :] * w_ref[k:k + 1, :]
        z = _layer_norm(acc + cb_ref[...], g_ref[...], b_ref[...])
        o_ref[r0:r0 + CONV_CHUNK, :] = (z * _sigmoid(z)).astype(BF16)


def _conv(h, conv_w, conv_b, ln_g, ln_b):
    nh = R_ALL // HALO
    per = TM // HALO
    vec = _full((1, CONV_CH))
    return pl.pallas_call(
        _conv_kernel,
        grid=(NT_ALL,),
        in_specs=[pl.BlockSpec((HALO, CONV_CH), lambda i: (jnp.maximum(i * per - 1, 0), 0)),
                  pl.BlockSpec((TM, CONV_CH), lambda i: (i, 0)),
                  pl.BlockSpec((HALO, CONV_CH), lambda i: (jnp.minimum((i + 1) * per, nh - 1), 0)),
                  _full((CONV_WIDTH, CONV_CH)), vec, vec, vec],
        out_specs=pl.BlockSpec((TM, CONV_CH), lambda i: (i, 0)),
        out_shape=jax.ShapeDtypeStruct((R_ALL, CONV_CH), BF16),
        scratch_shapes=[pltpu.VMEM((SHIFTS, TM + 2 * HALO, CONV_CH), F32)],
        compiler_params=_params(),
        name="conv_module",
    )(h, h, h, conv_w, conv_b, ln_g, ln_b)


def _nt_dot(a, b):
    return lax.dot_general(a, b, (((1,), (1,)), ((), ())), preferred_element_type=F32)


def _values_with_ones(v, n_kv, dv):
    lane = lax.broadcasted_iota(jnp.int32, (v.shape[0], VAL_PAD - dv), 1)
    tail = jnp.where(lane == 0, 1.0, 0.0).astype(v.dtype)
    pieces = []
    for h in range(n_kv):
        pieces += [v[:, h * dv:(h + 1) * dv], tail]
    return jnp.concatenate(pieces, axis=1)


def _attend(units, dv):
    def scores(unit):
        q, ks, _, masks, _ = unit
        out = []
        for k, msk in zip(ks, masks):
            s = _nt_dot(q, k)
            if msk is not None:
                s = jnp.where(msk, s, NEG_INF)
            out.append(s)
        return out

    results = []
    ss = scores(units[0])
    for idx, unit in enumerate(units):
        nxt = scores(units[idx + 1]) if idx + 1 < len(units) else None
        _, _, vs, _, sink = unit
        m = functools.reduce(jnp.maximum, [jnp.max(s, axis=-1, keepdims=True) for s in ss])
        if sink is not None:
            m = jnp.maximum(m, sink)
        acc = functools.reduce(jnp.add, [jnp.dot(jnp.exp2(s - m).astype(BF16), v, preferred_element_type=F32)
                                         for s, v in zip(ss, vs)])
        l = acc[:, dv:dv + 1]
        if sink is not None:
            l = l + jnp.exp2(sink - m)
        results.append(acc[:, :dv] / l)
        ss = nxt
    return results


def _attend_keys_major(units, dv, s_buf, p_buf):
    def scores(unit, slot):
        q, ks, _ = unit
        row, ms = 0, []
        for k in ks:
            s = _nt_dot(k, q)
            s_buf[slot, row:row + k.shape[0], :] = s
            ms.append(jnp.max(s, axis=0, keepdims=True))
            row += k.shape[0]
        return functools.reduce(jnp.maximum, ms)

    def run_next_scores_with(slot, pieces):
        row = 0
        for k in pieces:
            tile = (slice(row + k.shape[0] - 8, row + k.shape[0]), slice(0, 128))
            s_buf[slot, tile[0], tile[1]] = s_buf[slot, tile[0], tile[1]] + 0.0 * s_buf[1 - slot, tile[0], tile[1]]
            row += k.shape[0]

    results = []
    m = scores(units[0], 0)
    for idx, unit in enumerate(units):
        slot = idx % 2
        m_next = None
        if idx + 1 < len(units):
            m_next = scores(units[idx + 1], 1 - slot)
            run_next_scores_with(slot, unit[1])
        row, acc = 0, None
        for vt in unit[2]:
            rows = slice(row, row + vt.shape[1])
            p_buf[slot, rows, :] = jnp.exp2(s_buf[slot, rows, :] - m).astype(BF16)
            part = jnp.dot(vt, p_buf[slot, rows, :], preferred_element_type=F32)
            acc = part if acc is None else acc + part
            row += vt.shape[1]
        results.append(acc[:dv] / acc[dv:dv + 1])
        m = m_next
    return results


def _win_kernel(sink_ref, q_ref, kp_ref, kc_ref, kn_ref, kx_ref, vp_ref, vc_ref, vn_ref, vx_ref, o_ref):
    n = pl.program_id(1)
    k_loc = jnp.concatenate([kp_ref[...], kc_ref[...], kn_ref[...]], axis=0)
    v_loc = jnp.concatenate([vp_ref[...], vc_ref[...], vn_ref[...]], axis=0)
    k_ctx = kx_ref[...]
    v_ctx = vx_ref[...]
    qi = lax.broadcasted_iota(jnp.int32, (WINDOW, 3 * WINDOW), 0)
    kj = lax.broadcasted_iota(jnp.int32, (WINDOW, 3 * WINDOW), 1)
    k_pos = jnp.where(n < SEQ // WINDOW, kj + (n - 1) * WINDOW, SEQ)
    valid = jnp.where(kj >= qi, jnp.where(kj <= qi + 2 * WINDOW, 1, 0), 0)
    valid = jnp.where(k_pos >= 0, jnp.where(k_pos < SEQ, valid, 0), 0) > 0
    group = WIN_HEADS // WIN_KV_HEADS
    units = []
    for h in range(WIN_HEADS):
        kv = h // group
        ksl = slice(kv * HEAD_DIM, (kv + 1) * HEAD_DIM)
        vsl = slice(kv * VAL_PAD, (kv + 1) * VAL_PAD)
        units.append((q_ref[:, h * HEAD_DIM:(h + 1) * HEAD_DIM], [k_ctx[:, ksl], k_loc[:, ksl]],
                      [v_ctx[:, vsl], v_loc[:, vsl]], [None, valid], sink_ref[h] * LOG2E))
    for h, o in enumerate(_attend(units, HEAD_DIM)):
        o_ref[:, h * HEAD_DIM:(h + 1) * HEAD_DIM] = o.astype(BF16)


def _win_attention(sink, q, k, v):
    nblk = SEQ // WINDOW
    cblk = CTX_LEN // WINDOW
    kvw = WIN_KV_HEADS * HEAD_DIM
    ctx0 = R_LAT // CTX_LEN
    lat = lambda n: jnp.minimum(n, nblk - 1)
    prev = lambda b, n: (b * nblk + jnp.maximum(lat(n) - 1, 0), 0)
    cur = lambda b, n: (b * nblk + lat(n), 0)
    nxt = lambda b, n: (b * nblk + jnp.minimum(lat(n) + 1, nblk - 1), 0)
    qrow = lambda b, n: (jnp.where(n < nblk, b * nblk + n, R_LAT // WINDOW + b * cblk + n - nblk), 0)
    ctx = lambda b, n: (ctx0 + b, 0)
    vw = WIN_KV_HEADS * VAL_PAD
    kvb = lambda f, w: pl.BlockSpec((WINDOW, w), f)
    cxb = lambda w: pl.BlockSpec((CTX_LEN, w), ctx)
    return pl.pallas_call(
        _win_kernel,
        grid=(BATCH, nblk + cblk),
        in_specs=[pl.BlockSpec(memory_space=pltpu.SMEM),
                  pl.BlockSpec((WINDOW, WIN_HEADS * HEAD_DIM), qrow),
                  kvb(prev, kvw), kvb(cur, kvw), kvb(nxt, kvw), cxb(kvw),
                  kvb(prev, vw), kvb(cur, vw), kvb(nxt, vw), cxb(vw)],
        out_specs=pl.BlockSpec((WINDOW, WIN_HEADS * HEAD_DIM), qrow),
        out_shape=jax.ShapeDtypeStruct((R_ALL, WIN_HEADS * HEAD_DIM), BF16),
        compiler_params=_params(),
        name="window_attention",
    )(sink, q, k, k, k, k, v, v, v, v)


def _outproj_kernel(a_ref, b_ref, wa_ref, wb_ref, bo_ref, xl_ref, xc_ref, g1_ref, sh2_ref, sc2_ref,
                    lng_ref, lnb_ref, wrh_ref, wrl_ref, br_ref, upper_ref, lower_ref, sel_ref,
                    xo_ref, route_ref, cnt_ref, xs_ref):
    is_lat = pl.program_id(0) < NT_LAT // OUT_SUB
    for t in range(OUT_SUB):
        rows = slice(t * TM, (t + 1) * TM)
        x = jnp.where(is_lat, xl_ref[rows, :], xc_ref[rows, :])
        _outproj_tile(a_ref[rows, :], b_ref[rows, :], wa_ref, wb_ref, bo_ref, x, g1_ref, sh2_ref, sc2_ref,
                      lng_ref, lnb_ref, wrh_ref, wrl_ref, br_ref, upper_ref, lower_ref, sel_ref,
                      xo_ref.at[rows, :], route_ref.at[rows, :], cnt_ref.at[t * 8:(t + 1) * 8, :],
                      xs_ref.at[t * LOCAL_ROWS:(t + 1) * LOCAL_ROWS, :])


def _outproj_tile(a, b, wa_ref, wb_ref, bo_ref, x, g1_ref, sh2_ref, sc2_ref,
                  lng_ref, lnb_ref, wrh_ref, wrl_ref, br_ref, upper_ref, lower_ref, sel_ref,
                  xo_ref, route_ref, cnt_ref, xs_ref):
    y = (jnp.dot(a, wa_ref[...], preferred_element_type=F32)
         + jnp.dot(b, wb_ref[...], preferred_element_type=F32) + bo_ref[...])
    xn = _layer_norm(DN_ALPHA * x + (1.0 + g1_ref[...]) * y, lng_ref[...], lnb_ref[...])
    xo_ref[...] = xn
    u2 = xn * (1.0 + sc2_ref[...]) + sh2_ref[...]
    u_hi = u2.astype(BF16)
    u_lo = (u2 - u_hi.astype(F32)).astype(BF16)
    logits = (jnp.dot(u_hi, wrh_ref[...], preferred_element_type=F32)
              + jnp.dot(u_lo, wrh_ref[...], preferred_element_type=F32)
              + jnp.dot(u_hi, wrl_ref[...], preferred_element_type=F32) + br_ref[...])
    lane = lax.broadcasted_iota(jnp.int32, logits.shape, 1).astype(F32)
    ninf = -jnp.inf
    big = float(ROUTE_W)
    gl = jnp.where(lane < N_GROUPS, logits, ninf)
    gmax = jnp.max(gl, axis=-1, keepdims=True)
    gidx = jnp.min(jnp.where(gl == gmax, lane, big), axis=-1, keepdims=True)
    g_w = 1.0 / jnp.sum(jnp.exp(gl - gmax), axis=-1, keepdims=True)
    lo = N_GROUPS + EXP_PER_GROUP * gidx
    el = jnp.where(lane >= lo, jnp.where(lane < lo + EXP_PER_GROUP, logits, ninf), ninf)
    v1 = jnp.max(el, axis=-1, keepdims=True)
    i1 = jnp.min(jnp.where(el == v1, lane, big), axis=-1, keepdims=True)
    el2 = jnp.where(lane == i1, ninf, el)
    v2 = jnp.max(el2, axis=-1, keepdims=True)
    i2 = jnp.min(jnp.where(el2 == v2, lane, big), axis=-1, keepdims=True)
    e2 = jnp.exp(v2 - v1)
    w1 = g_w / (1.0 + e2)
    w2 = g_w * e2 / (1.0 + e2)
    onehot = [jnp.where(lane == i1 - N_GROUPS, 1.0, 0.0), jnp.where(lane == i2 - N_GROUPS, 1.0, 0.0)]
    cnt = [jnp.sum(o, axis=0, keepdims=True) for o in onehot]
    run_units = jnp.floor((cnt[0] + cnt[1] + (RUN_ALIGN - 1)) * (1.0 / RUN_ALIGN))
    below = RUN_ALIGN * jnp.dot(jnp.broadcast_to(run_units, (8, ROUTE_W)).astype(BF16), upper_ref[...],
                                preferred_element_type=F32)[0:1]
    lower = lower_ref[...]
    base = [below, below + cnt[0]]
    lp = []
    for s in range(2):
        earlier = jnp.dot(lower, onehot[s].astype(BF16), preferred_element_type=F32)
        lp.append(jnp.sum(onehot[s] * (base[s] + earlier), axis=-1, keepdims=True))
    rec = jnp.where(lane == 0.0, i1 - N_GROUPS,
                    jnp.where(lane == 1.0, i2 - N_GROUPS,
                              jnp.where(lane == 2.0, w1,
                                        jnp.where(lane == 3.0, w2,
                                                  jnp.where(lane == 4.0, lp[0],
                                                            jnp.where(lane == 5.0, lp[1], 0.0))))))
    route_ref[...] = rec
    cnt_ref[...] = jnp.broadcast_to(run_units * RUN_ALIGN, (8, ROUTE_W))
    sel = sel_ref[...]
    pos = lax.broadcasted_iota(jnp.int32, (LOCAL_ROWS, TM), 0).astype(F32)
    lp_lanes = []
    for s in range(2):
        hi = jnp.floor(lp[s] * (1.0 / 256.0))
        parts = jnp.where(lane == 0.0, lp[s] - 256.0 * hi, jnp.where(lane == 1.0, hi, 0.0)).astype(BF16)
        t = _nt_dot(sel, parts)
        lp_lanes.append(t[0:1] + 256.0 * t[1:2])
    perm = jnp.where(pos == lp_lanes[0], 1.0, jnp.where(pos == lp_lanes[1], 1.0, 0.0)).astype(BF16)
    xs_ref[...] = jnp.dot(perm, u_hi, preferred_element_type=F32)


def _outproj(n_tiles, mix_a, mix_b, w_a, w_b, b_out, x_lat, x_ctx, mods, ln_g, ln_b, w_r, b_r):
    rows = n_tiles * TM
    steps = n_tiles // OUT_SUB
    lat_steps = NT_LAT // OUT_SUB
    half = mix_a.shape[1]
    w_rh = w_r.astype(BF16)
    w_rl = (w_r - w_rh.astype(F32)).astype(BF16)
    upper = (jnp.arange(ROUTE_W)[:, None] < jnp.arange(ROUTE_W)[None, :]).astype(BF16)
    lower = (jnp.arange(TM)[:, None] > jnp.arange(TM)[None, :]).astype(BF16)
    sel = (jnp.arange(8)[:, None] == jnp.arange(ROUTE_W)[None, :]).astype(BF16)
    row = lambda w: pl.BlockSpec((OUT_SUB * TM, w), lambda i: (i, 0))
    mod = lambda chunk: pl.BlockSpec((None, None, 1, D_MODEL), lambda i: (_mod_row(i * OUT_SUB), chunk, 0, 0))
    vec = _full((1, D_MODEL))
    return pl.pallas_call(
        _outproj_kernel,
        grid=(steps,),
        in_specs=[row(half), row(half), _full((half, D_MODEL)), _full((half, D_MODEL)), vec,
                  pl.BlockSpec((OUT_SUB * TM, D_MODEL), lambda i: (jnp.minimum(i, lat_steps - 1), 0)),
                  pl.BlockSpec((OUT_SUB * TM, D_MODEL), lambda i: (jnp.maximum(i - lat_steps, 0), 0)),
                  mod(G1), mod(SH2), mod(SC2), vec, vec,
                  _full((D_MODEL, ROUTE_W)), _full((D_MODEL, ROUTE_W)), _full((1, ROUTE_W)),
                  _full((ROUTE_W, ROUTE_W)), _full((TM, TM)), _full((8, ROUTE_W))],
        out_specs=[row(D_MODEL), row(ROUTE_W), pl.BlockSpec((OUT_SUB * 8, ROUTE_W), lambda i: (i, 0)),
                   pl.BlockSpec((OUT_SUB * LOCAL_ROWS, D_MODEL), lambda i: (i, 0))],
        out_shape=[jax.ShapeDtypeStruct((rows, D_MODEL), F32),
                   jax.ShapeDtypeStruct((rows, ROUTE_W), F32),
                   jax.ShapeDtypeStruct((n_tiles * 8, ROUTE_W), F32),
                   jax.ShapeDtypeStruct((n_tiles * LOCAL_ROWS, D_MODEL), F32)],
        compiler_params=_params(),
        name="outproj_ln_router",
    )(mix_a, mix_b, w_a, w_b, b_out, x_lat, x_ctx, mods, mods, mods, ln_g, ln_b, w_rh, w_rl, b_r,
      upper, lower, sel)


def _aligned(i):
    return pl.multiple_of(i, RUN_ALIGN)


def _moe_kernel(te_ref, tk_ref, rows_ref, lo_ref, hi_ref, cnt_ref, bt_ref, be_ref, xs_hbm, w1_ref, w3_ref,
                w2_ref, ys_ref, xbuf, wb1, wb3, wb2, sem):
    j = pl.program_id(0)
    nt = pl.num_programs(0)
    slot = j % 2

    def issue(tile, slot_):
        e = te_ref[tile]
        first = tk_ref[tile] * TMM

        def body(i, carry):
            idx = i * N_EXPERTS + e
            start = bt_ref[idx]
            lo = jnp.maximum(start, first)
            n = jnp.minimum(start + cnt_ref[idx], first + TMM) - lo

            @pl.when(n > 0)
            def _():
                src = i * LOCAL_ROWS + be_ref[idx] + lo - start
                pltpu.make_async_copy(xs_hbm.at[pl.ds(_aligned(src), _aligned(n))],
                                      xbuf.at[slot_, pl.ds(_aligned(lo - first), _aligned(n))],
                                      sem.at[slot_]).start()
            return carry
        lax.fori_loop(lo_ref[tile], hi_ref[tile], body, 0)

    @pl.when(j == 0)
    def _():
        xbuf[...] = jnp.zeros_like(xbuf)
        issue(0, 0)

    @pl.when(j + 1 < nt)
    def _():
        issue(j + 1, 1 - slot)

    @pl.when(jnp.logical_or(j == 0, te_ref[j] != te_ref[jnp.maximum(j - 1, 0)]))
    def _():
        wb1[...] = w1_ref[...].astype(BF16)
        wb3[...] = w3_ref[...].astype(BF16)
        wb2[...] = w2_ref[...].astype(BF16)

    n_real = rows_ref[j]

    @pl.when(n_real > 0)
    def _():
        pltpu.make_async_copy(xs_hbm.at[pl.ds(0, _aligned(n_real))], xbuf.at[slot, pl.ds(0, _aligned(n_real))],
                              sem.at[slot]).wait()
        x = xbuf[slot].astype(BF16)
        h1 = jnp.dot(x, wb1[...], preferred_element_type=F32)
        h3 = jnp.dot(x, wb3[...], preferred_element_type=F32)
        hid = h1 * _sigmoid(h1) * h3
        ys_ref[...] = jnp.dot(hid.astype(BF16), wb2[...], preferred_element_type=F32)

    @pl.when(n_real == 0)
    def _():
        ys_ref[...] = jnp.zeros_like(ys_ref)


def _moe_experts(layer, plan, xs_local, w1, w3, w2):
    tile_expert, tile_k, tile_rows, src_lo, src_hi, cnt, before_tile, before_expert, _, _ = plan
    nt = tile_expert.shape[0]
    wmap = lambda j, te, *_: (layer, te[j], 0, 0)
    grid_spec = pltpu.PrefetchScalarGridSpec(
        num_scalar_prefetch=8,
        grid=(nt,),
        in_specs=[pl.BlockSpec(memory_space=pl.ANY),
                  pl.BlockSpec((None, None, D_MODEL, EXPERT_FF), wmap),
                  pl.BlockSpec((None, None, D_MODEL, EXPERT_FF), wmap),
                  pl.BlockSpec((None, None, EXPERT_FF, D_MODEL), wmap)],
        out_specs=pl.BlockSpec((TMM, D_MODEL), lambda j, *_: (j, 0)),
        scratch_shapes=[pltpu.VMEM((2, TMM, D_MODEL), F32),
                        pltpu.VMEM((D_MODEL, EXPERT_FF), BF16),
                        pltpu.VMEM((D_MODEL, EXPERT_FF), BF16),
                        pltpu.VMEM((EXPERT_FF, D_MODEL), BF16),
                        pltpu.SemaphoreType.DMA((2,))])
    return pl.pallas_call(
        _moe_kernel,
        grid_spec=grid_spec,
        out_shape=jax.ShapeDtypeStruct((nt * TMM, D_MODEL), F32),
        compiler_params=_params(),
        name="moe_experts",
    )(tile_expert, tile_k, tile_rows, src_lo, src_hi, cnt, before_tile, before_expert, xs_local, w1, w3, w2)


def _combine_kernel(cnt_ref, bt_ref, be_ref, gs_ref, used_ref, ys_hbm, x_ref, route_ref, g2_ref, lng_ref, lnb_ref,
                    o_ref, ybuf, sem):
    i = pl.program_id(0)
    nt = pl.num_programs(0)
    slot = i % 2

    def issue(tile, slot_):
        def body(e, carry):
            idx = tile * N_EXPERTS + e
            n = cnt_ref[idx]

            @pl.when(n > 0)
            def _():
                pltpu.make_async_copy(ys_hbm.at[pl.ds(_aligned(gs_ref[e] + bt_ref[idx]), _aligned(n))],
                                      ybuf.at[slot_, pl.ds(_aligned(be_ref[idx]), _aligned(n))],
                                      sem.at[slot_]).start()
            return carry
        lax.fori_loop(0, N_EXPERTS, body, 0)

    @pl.when(i == 0)
    def _():
        ybuf[...] = jnp.zeros_like(ybuf)
        issue(0, 0)

    @pl.when(i + 1 < nt)
    def _():
        issue(i + 1, 1 - slot)

    used = _aligned(used_ref[i])
    pltpu.make_async_copy(ys_hbm.at[pl.ds(0, used)], ybuf.at[slot, pl.ds(0, used)], sem.at[slot]).wait()
    route = route_ref[...]
    pos = lax.broadcasted_iota(jnp.int32, (TM, LOCAL_ROWS), 1).astype(F32)
    y = ybuf[slot].astype(BF16)
    picked = [jnp.dot(jnp.where(pos == route[:, 4 + s:5 + s], 1.0, 0.0).astype(BF16), y,
                      preferred_element_type=F32) for s in range(2)]
    f = route[:, 2:3] * picked[0] + route[:, 3:4] * picked[1]
    z = DN_ALPHA * x_ref[...] + (1.0 + g2_ref[...]) * f
    o_ref[...] = _layer_norm(z, lng_ref[...], lnb_ref[...])


def _moe_combine(n_tiles, plan, ys, x_all, route, mods, ln_g, ln_b):
    _, _, _, _, _, cnt, before_tile, before_expert, group_start, used = plan
    rows = n_tiles * TM
    row = lambda w: pl.BlockSpec((TM, w), lambda i, *_: (i, 0))
    vec = pl.BlockSpec((1, D_MODEL), lambda i, *_: (0, 0))
    grid_spec = pltpu.PrefetchScalarGridSpec(
        num_scalar_prefetch=5,
        grid=(n_tiles,),
        in_specs=[pl.BlockSpec(memory_space=pl.ANY), row(D_MODEL), row(ROUTE_W),
                  pl.BlockSpec((None, None, 1, D_MODEL), lambda i, *_: (_mod_row(i), G2, 0, 0)), vec, vec],
        out_specs=row(D_MODEL),
        scratch_shapes=[pltpu.VMEM((2, LOCAL_ROWS, D_MODEL), F32), pltpu.SemaphoreType.DMA((2,))])
    return pl.pallas_call(
        _combine_kernel,
        grid_spec=grid_spec,
        out_shape=jax.ShapeDtypeStruct((rows, D_MODEL), F32),
        compiler_params=_params(),
        name="moe_combine_ln",
    )(cnt, before_tile, before_expert, group_start, used, ys, x_all, route, mods, ln_g, ln_b)


def _moe_plan(cnt_rec, n_tiles):
    cnt = cnt_rec.reshape(n_tiles, 8, ROUTE_W)[:, 0, :N_EXPERTS].astype(jnp.int32)
    nt_max = (n_tiles * (2 * TM + N_EXPERTS * (RUN_ALIGN - 1))) // TMM + N_EXPERTS
    total = jnp.sum(cnt, axis=0)
    tiles_e = (total + TMM - 1) // TMM
    tile_end = jnp.cumsum(tiles_e)
    first_tile = tile_end - tiles_e
    before_tile = jnp.cumsum(cnt, axis=0) - cnt
    before_expert = jnp.cumsum(cnt, axis=1) - cnt
    tile_id = jnp.arange(nt_max, dtype=jnp.int32)
    tile_expert = jnp.minimum(jnp.sum((tile_id[:, None] >= tile_end[None, :]).astype(jnp.int32), axis=1),
                              N_EXPERTS - 1)
    tile_k = tile_id - first_tile[tile_expert]
    tile_rows = jnp.clip(total[tile_expert] - tile_k * TMM, 0, TMM)
    first = (tile_k * TMM)[:, None]
    run_start = before_tile.T[tile_expert]
    run_end = run_start + cnt.T[tile_expert]
    src_lo = jnp.sum((run_end <= first).astype(jnp.int32), axis=1)
    src_hi = jnp.sum((run_start < first + TMM).astype(jnp.int32), axis=1)
    return (tile_expert, tile_k, tile_rows, src_lo, src_hi, cnt.reshape(-1), before_tile.reshape(-1),
            before_expert.reshape(-1), first_tile * TMM, jnp.sum(cnt, axis=1))


def _router_weights(w_rg, b_rg, w_re, b_re):
    w = jnp.concatenate([w_rg, jnp.transpose(w_re, (1, 0, 2)).reshape(D_MODEL, N_EXPERTS)], axis=1)
    b = jnp.concatenate([b_rg, b_re.reshape(-1)])
    pad = ROUTE_W - w.shape[1]
    return jnp.pad(w, ((0, 0), (0, pad))), jnp.pad(b, (0, pad)).reshape(1, ROUTE_W)


def _proj1_kernel(x_ref, sh_ref, sc_ref, w_ref, b_ref, cos_ref, sin_ref, cosm_ref, sinm_ref, cosr_ref,
                  sinr_ref, gq_ref, gk_ref, gqc_ref, gkv_ref, avg_ref, wuq_ref, wuk_ref, wuv_ref, vplace_ref,
                  q_ref, qm_ref, k_ref, v_ref, km_ref, vm_ref):
    u = x_ref[...] * (1.0 + sc_ref[...]) + sh_ref[...]
    y = jnp.dot(u.astype(BF16), w_ref[...], preferred_element_type=F32) + b_ref[...]
    c_q = GQA_HEADS * HEAD_DIM
    c_qc = c_q + MLA_Q_RANK
    c_k = c_qc + GQA_KV_HEADS * HEAD_DIM
    c_v = c_k + GQA_KV_HEADS * HEAD_DIM
    c_kv = c_v + MLA_KV_RANK
    avg = avg_ref[...]

    def head_rms(t, gain):
        sq = t * t
        hi = sq.astype(BF16)
        lo = (sq - hi.astype(F32)).astype(BF16)
        a = avg[:t.shape[1], :t.shape[1]]
        ms = jnp.dot(hi, a, preferred_element_type=F32) + jnp.dot(lo, a, preferred_element_type=F32)
        return t * lax.rsqrt(ms + RMS_EPS) * gain

    def row_rms(t, gain):
        ms = jnp.mean(t * t, axis=-1, keepdims=True)
        return t * lax.rsqrt(ms + RMS_EPS) * gain

    cos = cos_ref[...]
    sin = sin_ref[...]
    cos4 = jnp.concatenate([cos] * 4, axis=1)
    sin4 = jnp.concatenate([sin] * 4, axis=1)
    q = _rope(head_rms(y[:, :c_q], gq_ref[...]), cos4, sin4, HEAD_DIM // 4) * (HEAD_DIM ** -0.5 * LOG2E)
    q_ref[...] = q.astype(BF16)
    k = _rope(head_rms(y[:, c_qc:c_k], gk_ref[...]), cos, sin, HEAD_DIM // 4)
    k_ref[...] = k.astype(BF16)

    def transposed_values(w_t, src, dv):
        vt = _nt_dot(w_t, src)
        r = lax.broadcasted_iota(jnp.int32, vt.shape, 0)
        return jnp.where(r % VAL_PAD == dv, 1.0, vt).astype(BF16)

    v_ref[...] = transposed_values(vplace_ref[...], y[:, c_k:c_v].astype(BF16), HEAD_DIM)

    qc = row_rms(y[:, c_q:c_qc], gqc_ref[...]).astype(BF16)
    qm = jnp.dot(qc, wuq_ref[...], preferred_element_type=F32)
    cosm = jnp.concatenate([cosm_ref[...]] * MLA_HEADS, axis=1)
    sinm = jnp.concatenate([sinm_ref[...]] * MLA_HEADS, axis=1)
    qm = _rope(qm, cosm, sinm, MLA_ROPE // 4) * ((MLA_NOPE + MLA_ROPE) ** -0.5 * LOG2E)
    qm_ref[...] = qm.astype(BF16)

    kvn = row_rms(y[:, c_v:c_kv], gkv_ref[...]).astype(BF16)
    kr = _rope(y[:, c_kv:], cosr_ref[...], sinr_ref[...], MLA_ROPE // 4).astype(BF16)
    km = jnp.dot(jnp.concatenate([kvn, kr], axis=1), wuk_ref[...], preferred_element_type=F32)
    km_ref[...] = km.astype(BF16)
    vm_ref[...] = transposed_values(wuv_ref[...], kvn, MLA_V)


def _proj1(x_all, mods, w_in, b_in, tabs, gq, gk, gqc, gkv, avg, wuq, wuk, wuv):
    cos_hd, sin_hd, cos_m, sin_m, cos_r, sin_r = tabs
    kvw = GQA_KV_HEADS * HEAD_DIM
    qw = GQA_HEADS * HEAD_DIM
    mw = MLA_HEADS * MLA_PAD
    vw = MLA_HEADS * VAL_PAD
    gvw = GQA_KV_HEADS * VAL_PAD
    row = lambda w: pl.BlockSpec((TM, w), lambda i: (i, 0))
    col = lambda h: pl.BlockSpec((h, TM), lambda i: (0, i))
    tab = pl.BlockSpec((TM, 128), lambda i: (_rope_row_block(i), 0))
    r = jnp.arange(gvw)[:, None]
    c = jnp.arange(kvw)[None, :]
    vplace = jnp.logical_and(r // VAL_PAD == c // HEAD_DIM, r % VAL_PAD == c % HEAD_DIM).astype(BF16)
    return pl.pallas_call(
        _proj1_kernel,
        grid=(NT_ALL,),
        in_specs=[row(D_MODEL), _mod_spec(SH1), _mod_spec(SC1),
                  _full((D_MODEL, ODD_IN_PAD)), _full((1, ODD_IN_PAD)), tab, tab, tab, tab, tab, tab,
                  _full((1, qw)), _full((1, kvw)), _full((1, MLA_Q_RANK)), _full((1, MLA_KV_RANK)),
                  _full((qw, qw)), _full((MLA_Q_RANK, mw)), _full((MLA_KV_RANK + 128, mw)),
                  _full((vw, MLA_KV_RANK)), _full((gvw, kvw))],
        out_specs=[row(qw), row(mw), row(kvw), col(gvw), row(mw), col(vw)],
        out_shape=[jax.ShapeDtypeStruct((R_ALL, qw), BF16),
                   jax.ShapeDtypeStruct((R_ALL, mw), BF16),
                   jax.ShapeDtypeStruct((R_ALL, kvw), BF16),
                   jax.ShapeDtypeStruct((gvw, R_ALL), BF16),
                   jax.ShapeDtypeStruct((R_ALL, mw), BF16),
                   jax.ShapeDtypeStruct((vw, R_ALL), BF16)],
        compiler_params=_params(),
        name="proj1",
    )(x_all, mods, mods, w_in, b_in, cos_hd, sin_hd, cos_m, sin_m, cos_r, sin_r,
      gq, gk, gqc, gkv, avg, wuq, wuk, wuv.T, vplace)


def _dense_kernel(q_ref, kl_ref, kc_ref, vl_ref, vc_ref, o_ref, s_buf, p_buf, *, n_heads, group, stack, dk, dv):
    tq = q_ref.shape[0]
    units = []
    for h0 in range(0, n_heads, stack):
        kv = h0 // group
        qs = [q_ref[:, h * dk:(h + 1) * dk] for h in range(h0, h0 + stack)]
        q = qs[0] if stack == 1 else jnp.concatenate(qs, axis=0)
        ks = slice(kv * dk, (kv + 1) * dk)
        vs = slice(kv * VAL_PAD, (kv + 1) * VAL_PAD)
        parts = [slice(c * (SEQ // KEY_PARTS), (c + 1) * (SEQ // KEY_PARTS)) for c in range(KEY_PARTS)]
        units.append((q, [kl_ref[c, ks] for c in parts] + [kc_ref[:, ks]],
                      [vl_ref[vs, c] for c in parts] + [vc_ref[vs, :]]))
    for u, o_t in enumerate(_attend_keys_major(units, dv, s_buf, p_buf)):
        o = o_t.T
        for g in range(stack):
            h = u * stack + g
            o_ref[:, h * dv:(h + 1) * dv] = o[g * tq:(g + 1) * tq].astype(BF16)


def _dense_attention(q, k, v, *, n_heads, group, stack, dk, dv, tq, name):
    n_kv = n_heads // group
    nq = SEQ // tq
    ctx0 = R_LAT // CTX_LEN
    lat = lambda w: pl.BlockSpec((SEQ, w), lambda b, j: (b, 0), pipeline_mode=pl.Buffered(1))
    ctx = lambda w: pl.BlockSpec((CTX_LEN, w), lambda b, j: (ctx0 + b, 0))
    lat_t = pl.BlockSpec((n_kv * VAL_PAD, SEQ), lambda b, j: (0, b), pipeline_mode=pl.Buffered(1))
    ctx_t = pl.BlockSpec((n_kv * VAL_PAD, CTX_LEN), lambda b, j: (0, ctx0 + b))
    return pl.pallas_call(
        functools.partial(_dense_kernel, n_heads=n_heads, group=group, stack=stack, dk=dk, dv=dv),
        grid=(BATCH, nq),
        in_specs=[pl.BlockSpec((tq, n_heads * dk), lambda b, j: (b * nq + j, 0)),
                  lat(n_kv * dk), ctx(n_kv * dk), lat_t, ctx_t],
        out_specs=pl.BlockSpec((tq, n_heads * dv), lambda b, j: (b * nq + j, 0)),
        out_shape=jax.ShapeDtypeStruct((R_LAT, n_heads * dv), BF16),
        scratch_shapes=[pltpu.VMEM((2, SEQ + CTX_LEN, stack * tq), F32),
                        pltpu.VMEM((2, SEQ + CTX_LEN, stack * tq), BF16)],
        compiler_params=_params(),
        name=name,
    )(q, k, k, v, v)


def _mla_weights(w_uq, w_ukv):
    wq = w_uq.reshape(MLA_Q_RANK, MLA_HEADS, MLA_NOPE + MLA_ROPE)
    wq = jnp.pad(wq, ((0, 0), (0, 0), (0, MLA_PAD - MLA_NOPE - MLA_ROPE))).reshape(MLA_Q_RANK, -1)
    wkv = w_ukv.reshape(MLA_KV_RANK, MLA_HEADS, MLA_NOPE + MLA_V)
    wk = jnp.pad(wkv[:, :, :MLA_NOPE], ((0, 0), (0, 0), (0, MLA_PAD - MLA_NOPE))).reshape(MLA_KV_RANK, -1)
    wv = jnp.pad(wkv[:, :, MLA_NOPE:], ((0, 0), (0, 0), (0, VAL_PAD - MLA_V))).reshape(MLA_KV_RANK, -1)
    r = jnp.arange(128)[:, None]
    c = jnp.arange(MLA_HEADS * MLA_PAD)[None, :]
    place = jnp.logical_and(r < MLA_ROPE, (c % MLA_PAD) == MLA_NOPE + r).astype(F32)
    wk = jnp.concatenate([wk, place], axis=0)
    return wq.astype(BF16), wk.astype(BF16), wv.astype(BF16)


def kernel(x, c, ctx, c_ctx, even_w_in, even_b_in, even_conv_w, even_conv_b, even_conv_ln_g, even_conv_ln_b, even_sink, even_w_out, even_b_out, odd_w_in, odd_b_in, odd_q_norm, odd_k_norm, odd_mla_q_norm, odd_mla_kv_norm, odd_mla_w_uq, odd_mla_w_ukv, odd_w_out, odd_b_out, ada_w, ada_b, ln1_g, ln1_b, ln2_g, ln2_b, moe_w_rg, moe_b_rg, moe_w_re, moe_b_re, moe_w1, moe_w3, moe_w2):
    vec = lambda a: a.reshape(1, -1)
    x_lat0 = x.reshape(R_LAT, D_MODEL)
    x_ctx0 = ctx.reshape(R_CTX, D_MODEL)

    cv =jnp.concatenate([c, c_ctx[None, :], jnp.zeros((8 - BATCH - 1, D_MODEL), F32)], axis=0)
    mods = _ada_table(cv, ada_w, ada_b).reshape(DEPTH, 8, 6, 1, D_MODEL)

    cos64, sin64 = _rope_tables(HEAD_DIM)
    cos_hd, sin_hd = _pad_table(cos64, sin64, 0, HEAD_DIM, 128)
    cos32, sin32 = _rope_tables(MLA_ROPE)
    cos_m, sin_m = _pad_table(cos32, sin32, MLA_NOPE, MLA_PAD, 128)
    cos_r, sin_r = _pad_table(cos32, sin32, 0, 128, 128)

    m0 = mods[0]
    h, q0, k0, v0 = _proj0(x_lat0, x_ctx0, m0, even_w_in[0].astype(BF16), vec(even_b_in[0]), cos_hd, sin_hd)
    conv_out = _conv(h, even_conv_w[0].reshape(CONV_WIDTH, CONV_CH), vec(even_conv_b[0]),
                     vec(even_conv_ln_g[0]), vec(even_conv_ln_b[0]))
    attn = _win_attention(even_sink[0], q0, k0, v0)
    w_out = even_w_out[0].astype(BF16)
    w_r, b_r = _router_weights(moe_w_rg[0], moe_b_rg[0], moe_w_re[0], moe_b_re[0])
    x_all, route, cnt_rec, xs_local = _outproj(
        NT_ALL, conv_out, attn, w_out[:CONV_CH], w_out[CONV_CH:], vec(even_b_out[0]),
        x_lat0, x_ctx0, m0, vec(ln1_g[0]), vec(ln1_b[0]), w_r, b_r)
    plan = _moe_plan(cnt_rec, NT_ALL)
    ys = _moe_experts(0, plan, xs_local, moe_w1, moe_w3, moe_w2)
    x_all = _moe_combine(NT_ALL, plan, ys, x_all, route, m0, vec(ln2_g[0]), vec(ln2_b[0]))

    m1 = mods[1]
    w_in1 = jnp.pad(odd_w_in[0], ((0, 0), (0, ODD_IN_PAD - ODD_IN))).astype(BF16)
    b_in1 = jnp.pad(odd_b_in[0], (0, ODD_IN_PAD - ODD_IN)).reshape(1, -1)
    wuq, wuk, wuv = _mla_weights(odd_mla_w_uq[0], odd_mla_w_ukv[0])
    qw = GQA_HEADS * HEAD_DIM
    hid = jnp.arange(qw) // HEAD_DIM
    avg = ((hid[:, None] == hid[None, :]).astype(F32) / HEAD_DIM).astype(BF16)
    q1, qm, k1, v1, km, vm = _proj1(
        x_all, m1, w_in1, b_in1, (cos_hd, sin_hd, cos_m, sin_m, cos_r, sin_r),
        vec(jnp.tile(odd_q_norm[0], GQA_HEADS)), vec(jnp.tile(odd_k_norm[0], GQA_KV_HEADS)),
        vec(odd_mla_q_norm[0]), vec(odd_mla_kv_norm[0]), avg, wuq, wuk, wuv)
    o_g = _dense_attention(q1, k1, v1, n_heads=GQA_HEADS, group=GQA_HEADS // GQA_KV_HEADS, stack=2,
                           dk=HEAD_DIM, dv=HEAD_DIM, tq=256, name="gqa_attention")
    o_m = _dense_attention(qm, km, vm, n_heads=MLA_HEADS, group=1, stack=1, dk=MLA_PAD, dv=MLA_V, tq=256,
                           name="mla_attention")
    w_out = odd_w_out[0].astype(BF16)
    w_r, b_r = _router_weights(moe_w_rg[1], moe_b_rg[1], moe_w_re[1], moe_b_re[1])
    x_lat, route, cnt_rec, xs_local = _outproj(
        NT_LAT, o_g, o_m, w_out[:qw], w_out[qw:], vec(odd_b_out[0]),
        x_all, x_all, m1, vec(ln1_g[1]), vec(ln1_b[1]), w_r, b_r)
    plan = _moe_plan(cnt_rec, NT_LAT)
    ys = _moe_experts(1, plan, xs_local, moe_w1, moe_w3, moe_w2)
    x_lat = _moe_combine(NT_LAT, plan, ys, x_lat, route, m1, vec(ln2_g[1]), vec(ln2_b[1]))
    return x_lat.reshape(BATCH, SEQ, D_MODEL)
```

```python
import functools

import jax
import jax.numpy as jnp
from jax import lax
from jax.experimental import pallas as pl
from jax.experimental.pallas import tpu as pltpu

F32 = jnp.float32
BF16 = jnp.bfloat16

D_MODEL = 1024
BATCH = 4
SEQ = 4096
DEPTH = 2
GRID_W = 64
CTX_LEN = 256
HEAD_DIM = 64
ROPE_THETA = 10000.0
LN_EPS = 1e-5
RMS_EPS = 1e-6
NEG_INF = -1e30

CONV_CH = 512
CONV_WIDTH = 31
WIN_HEADS = 8
WIN_KV_HEADS = 2
WINDOW = 128
GQA_HEADS = 8
GQA_KV_HEADS = 2
MLA_HEADS = 8
MLA_Q_RANK = 256
MLA_KV_RANK = 128
MLA_NOPE = 64
MLA_ROPE = 32
MLA_V = 64
N_GROUPS = 4
EXP_PER_GROUP = 8
N_EXPERTS = N_GROUPS * EXP_PER_GROUP
EXPERT_FF = 512
DN_ALPHA = float((2 * DEPTH) ** 0.25)

EVEN_IN = 2 * CONV_CH + (WIN_HEADS + 2 * WIN_KV_HEADS) * HEAD_DIM
ODD_IN = 1184
ODD_IN_PAD = 1280
MLA_PAD = 128
VAL_PAD = 128
SCORE_SLOTS = 2
KEY_PARTS = 4
LOG2E = 1.4426950408889634

R_LAT = BATCH * SEQ
R_CTX = BATCH * CTX_LEN
R_ALL = R_LAT + R_CTX
TM = 256
NT_LAT = R_LAT // TM
NT_ALL = R_ALL // TM
TILES_PER_SEQ = SEQ // TM
HALO = 16
CONV_CHUNK = 32
SHIFTS = 8
OUT_SUB = 2
TMM = 256
ROUTE_W = 128
RUN_ALIGN = 8
LOCAL_ROWS = 768
VMEM_LIMIT = 56 * 1024 * 1024

SH1, SC1, G1, SH2, SC2, G2 = range(6)


def _sigmoid(x):
    return 1.0 / (1.0 + jnp.exp(-x))


def _layer_norm(z, g, b):
    mu = jnp.mean(z, axis=-1, keepdims=True)
    zc = z - mu
    var = jnp.mean(zc * zc, axis=-1, keepdims=True)
    return zc * lax.rsqrt(var + LN_EPS) * g + b


def _rope(x, cos, sin, half):
    n = x.shape[-1]
    lane = lax.broadcasted_iota(jnp.int32, x.shape, 1)
    first = (lane % (2 * half)) < half
    partner = jnp.where(first, pltpu.roll(x, n - half, 1), pltpu.roll(x, half, 1))
    return x * cos + partner * sin


def _mod_row(i):
    return jnp.where(i < NT_LAT, i // TILES_PER_SEQ, BATCH)


def _mod_spec(chunk):
    return pl.BlockSpec((None, None, 1, D_MODEL), lambda i: (_mod_row(i), chunk, 0, 0))


def _rope_row_block(i):
    return jnp.where(i < NT_LAT, i % TILES_PER_SEQ, TILES_PER_SEQ)


def _full(shape):
    nd = len(shape)
    return pl.BlockSpec(shape, lambda *_: (0,) * nd)


def _params():
    return pltpu.CompilerParams(vmem_limit_bytes=VMEM_LIMIT)


def _ada_kernel(cv_ref, w_ref, b_ref, o_ref):
    cv = cv_ref[...]
    s = cv * _sigmoid(cv)
    o_ref[...] = jnp.dot(s, w_ref[...], precision=lax.Precision.HIGHEST,
                         preferred_element_type=F32) + b_ref[...]


def _ada_table(cv, ada_w, ada_b):
    bn = 1536
    nb = (6 * D_MODEL) // bn
    return pl.pallas_call(
        _ada_kernel,
        grid=(DEPTH, nb),
        in_specs=[pl.BlockSpec((8, D_MODEL), lambda l, j: (0, 0)),
                  pl.BlockSpec((None, D_MODEL, bn), lambda l, j: (l, 0, j)),
                  pl.BlockSpec((None, 1, bn), lambda l, j: (l, 0, j))],
        out_specs=pl.BlockSpec((None, 8, bn), lambda l, j: (l, 0, j)),
        out_shape=jax.ShapeDtypeStruct((DEPTH, 8, 6 * D_MODEL), F32),
        compiler_params=_params(),
        name="ada_table",
    )(cv, ada_w, ada_b.reshape(DEPTH, 1, 6 * D_MODEL))


def _rope_tables(rot_dim):
    axis_dim = rot_dim // 2
    inv_freq = ROPE_THETA ** (-jnp.arange(0, axis_dim, 2, dtype=F32) / axis_dim)
    t = jnp.arange(SEQ)
    ang_r = (t // GRID_W).astype(F32)[:, None] * inv_freq[None, :]
    ang_c = (t % GRID_W).astype(F32)[:, None] * inv_freq[None, :]
    cos = jnp.concatenate([jnp.cos(ang_r), jnp.cos(ang_r), jnp.cos(ang_c), jnp.cos(ang_c)], axis=-1)
    sin = jnp.concatenate([-jnp.sin(ang_r), jnp.sin(ang_r), -jnp.sin(ang_c), jnp.sin(ang_c)], axis=-1)
    return cos, sin


def _pad_table(cos, sin, lead, period, width):
    rot = cos.shape[1]
    one = jnp.ones((SEQ, period), F32).at[:, lead:lead + rot].set(cos)
    zero = jnp.zeros((SEQ, period), F32).at[:, lead:lead + rot].set(sin)
    cos_w = jnp.tile(one, (1, width // period))
    sin_w = jnp.tile(zero, (1, width // period))
    cos_w = jnp.concatenate([cos_w, jnp.ones((TM, width), F32)], axis=0)
    sin_w = jnp.concatenate([sin_w, jnp.zeros((TM, width), F32)], axis=0)
    return cos_w, sin_w


def _proj0_kernel(xl_ref, xc_ref, sh_ref, sc_ref, w_ref, b_ref, cos_ref, sin_ref, vplace_ref,
                  h_ref, q_ref, k_ref, v_ref):
    x = jnp.where(pl.program_id(0) < NT_LAT, xl_ref[...], xc_ref[...])
    u = x * (1.0 + sc_ref[...]) + sh_ref[...]
    y = jnp.dot(u.astype(BF16), w_ref[...], preferred_element_type=F32) + b_ref[...]
    h_ref[...] = y[:, :CONV_CH] * _sigmoid(y[:, CONV_CH:2 * CONV_CH])
    cos = cos_ref[...]
    sin = sin_ref[...]
    q0 = 2 * CONV_CH
    k0 = q0 + WIN_HEADS * HEAD_DIM
    v0 = k0 + WIN_KV_HEADS * HEAD_DIM
    cos4 = jnp.concatenate([cos] * 4, axis=1)
    sin4 = jnp.concatenate([sin] * 4, axis=1)
    q = _rope(y[:, q0:k0], cos4, sin4, HEAD_DIM // 4) * (HEAD_DIM ** -0.5 * LOG2E)
    q_ref[...] = q.astype(BF16)
    k_ref[...] = _rope(y[:, k0:v0], cos, sin, HEAD_DIM // 4).astype(BF16)
    v_ref[...] = _transposed_values(vplace_ref[...], y[:, v0:].astype(BF16), HEAD_DIM)


def _proj0(x_lat, x_ctx, mods, w_in, b_in, cos_hd, sin_hd):
    kvw = WIN_KV_HEADS * HEAD_DIM
    row = lambda w: pl.BlockSpec((TM, w), lambda i: (i, 0))
    tab = pl.BlockSpec((TM, 128), lambda i: (_rope_row_block(i), 0))
    return pl.pallas_call(
        _proj0_kernel,
        grid=(NT_ALL,),
        in_specs=[pl.BlockSpec((TM, D_MODEL), lambda i: (jnp.minimum(i, NT_LAT - 1), 0)),
                  pl.BlockSpec((TM, D_MODEL), lambda i: (jnp.maximum(i - NT_LAT, 0), 0)),
                  _mod_spec(SH1), _mod_spec(SC1),
                  _full((D_MODEL, EVEN_IN)), _full((1, EVEN_IN)), tab, tab,
                  _full((WIN_KV_HEADS * VAL_PAD, kvw))],
        out_specs=[row(CONV_CH), row(WIN_HEADS * HEAD_DIM), row(kvw),
                   pl.BlockSpec((WIN_KV_HEADS * VAL_PAD, TM), lambda i: (0, i))],
        out_shape=[jax.ShapeDtypeStruct((R_ALL, CONV_CH), F32),
                   jax.ShapeDtypeStruct((R_ALL, WIN_HEADS * HEAD_DIM), BF16),
                   jax.ShapeDtypeStruct((R_ALL, kvw), BF16),
                   jax.ShapeDtypeStruct((WIN_KV_HEADS * VAL_PAD, R_ALL), BF16)],
        compiler_params=_params(),
        name="proj0",
    )(x_lat, x_ctx, mods, mods, w_in, b_in, cos_hd, sin_hd, _value_placement(WIN_KV_HEADS, HEAD_DIM))


def _conv_kernel(prev_ref, cur_ref, next_ref, w_ref, cb_ref, g_ref, b_ref, o_ref, buf):
    i = pl.program_id(0)
    is_ctx = i >= NT_LAT
    first = jnp.logical_or(is_ctx, i % TILES_PER_SEQ == 0)
    last = jnp.logical_or(is_ctx, i % TILES_PER_SEQ == TILES_PER_SEQ - 1)
    buf[0, 0:HALO, :] = jnp.where(first, 0.0, prev_ref[...])
    buf[0, HALO:HALO + TM, :] = cur_ref[...]
    buf[0, HALO + TM:, :] = jnp.where(last, 0.0, next_ref[...])
    span = TM + 2 * HALO - SHIFTS
    for r in range(1, SHIFTS):
        buf[r, 0:span, :] = buf[0, r:r + span, :]
    off = HALO - CONV_WIDTH // 2
    for c in range(TM // CONV_CHUNK):
        r0 = c * CONV_CHUNK
        acc = jnp.zeros((CONV_CHUNK, CONV_CH), F32)
        for k in range(CONV_WIDTH):
            r = (off + k) % SHIFTS
            base = r0 + off + k - r
            acc = acc + buf[r, base:base + CONV_CHUNK, :] * w_ref[k:k + 1, :]
        z = _layer_norm(acc + cb_ref[...], g_ref[...], b_ref[...])
        o_ref[r0:r0 + CONV_CHUNK, :] = (z * _sigmoid(z)).astype(BF16)


def _conv(h, conv_w, conv_b, ln_g, ln_b):
    nh = R_ALL // HALO
    per = TM // HALO
    vec = _full((1, CONV_CH))
    return pl.pallas_call(
        _conv_kernel,
        grid=(NT_ALL,),
        in_specs=[pl.BlockSpec((HALO, CONV_CH), lambda i: (jnp.maximum(i * per - 1, 0), 0)),
                  pl.BlockSpec((TM, CONV_CH), lambda i: (i, 0)),
                  pl.BlockSpec((HALO, CONV_CH), lambda i: (jnp.minimum((i + 1) * per, nh - 1), 0)),
                  _full((CONV_WIDTH, CONV_CH)), vec, vec, vec],
        out_specs=pl.BlockSpec((TM, CONV_CH), lambda i: (i, 0)),
        out_shape=jax.ShapeDtypeStruct((R_ALL, CONV_CH), BF16),
        scratch_shapes=[pltpu.VMEM((SHIFTS, TM + 2 * HALO, CONV_CH), F32)],
        compiler_params=_params(),
        name="conv_module",
    )(h, h, h, conv_w, conv_b, ln_g, ln_b)


def _nt_dot(a, b):
    return lax.dot_general(a, b, (((1,), (1,)), ((), ())), preferred_element_type=F32)


def _transposed_values(w_t, src, dv):
    vt = _nt_dot(w_t, src)
    r = lax.broadcasted_iota(jnp.int32, vt.shape, 0)
    return jnp.where(r % VAL_PAD == dv, 1.0, vt).astype(BF16)


def _value_placement(n_kv, dv):
    r = jnp.arange(n_kv * VAL_PAD)[:, None]
    c = jnp.arange(n_kv * dv)[None, :]
    return jnp.logical_and(r // VAL_PAD == c // dv, r % VAL_PAD == c % dv).astype(BF16)


def _attend_keys_major(units, dv, s_buf, p_buf):
    def scores(unit, slot):
        q, ks, _ = unit
        row, ms = 0, []
        for k in ks:
            s = _nt_dot(k, q)
            s_buf[slot, row:row + k.shape[0], :] = s
            ms.append(jnp.max(s, axis=0, keepdims=True))
            row += k.shape[0]
        return functools.reduce(jnp.maximum, ms)

    def run_next_scores_with(slot, nxt_slot, pieces):
        row = 0
        for k in pieces:
            tile = (slice(row + k.shape[0] - 8, row + k.shape[0]), slice(0, 128))
            s_buf[slot, tile[0], tile[1]] = s_buf[slot, tile[0], tile[1]] + 0.0 * s_buf[nxt_slot, tile[0], tile[1]]
            row += k.shape[0]

    n_s, n_p = s_buf.shape[0], p_buf.shape[0]
    results = []
    m = scores(units[0], 0)
    for idx, unit in enumerate(units):
        slot, nxt_slot, pslot = idx % n_s, (idx + 1) % n_s, idx % n_p
        m_next = None
        if idx + 1 < len(units):
            m_next = scores(units[idx + 1], nxt_slot)
            run_next_scores_with(slot, nxt_slot, unit[1])
        row, acc = 0, None
        for vt in unit[2]:
            rows = slice(row, row + vt.shape[1])
            p_buf[pslot, rows, :] = jnp.exp2(s_buf[slot, rows, :] - m).astype(BF16)
            part = jnp.dot(vt, p_buf[pslot, rows, :], preferred_element_type=F32)
            acc = part if acc is None else acc + part
            row += vt.shape[1]
        results.append(acc[:dv] / acc[dv:dv + 1])
        m = m_next
    return results


def _win_kernel(sink_ref, q_ref, kp_ref, kc_ref, kn_ref, kx_ref, vp_ref, vc_ref, vn_ref, vx_ref, o_ref):
    n = pl.program_id(1)
    group = WIN_HEADS // WIN_KV_HEADS
    k_loc = jnp.concatenate([kp_ref[...], kc_ref[...], kn_ref[...]], axis=0)
    vt_loc = jnp.concatenate([vp_ref[...], vc_ref[...], vn_ref[...]], axis=1)
    k_ctx = kx_ref[...]
    vt_ctx = vx_ref[...]
    kj = lax.broadcasted_iota(jnp.int32, (3 * WINDOW, group * WINDOW), 0)
    qi = lax.broadcasted_iota(jnp.int32, (3 * WINDOW, group * WINDOW), 1) % WINDOW
    k_pos = jnp.where(n < SEQ // WINDOW, kj + (n - 1) * WINDOW, SEQ)
    valid = jnp.where(kj >= qi, jnp.where(kj <= qi + 2 * WINDOW, 1, 0), 0)
    valid = jnp.where(k_pos >= 0, jnp.where(k_pos < SEQ, valid, 0), 0) > 0
    for kv in range(WIN_KV_HEADS):
        heads = range(kv * group, (kv + 1) * group)
        q = jnp.concatenate([q_ref[:, h * HEAD_DIM:(h + 1) * HEAD_DIM] for h in heads], axis=0)
        sink = jnp.concatenate([jnp.full((1, WINDOW), sink_ref[h] * LOG2E, F32) for h in heads], axis=1)
        ksl = slice(kv * HEAD_DIM, (kv + 1) * HEAD_DIM)
        vsl = slice(kv * VAL_PAD, (kv + 1) * VAL_PAD)
        s_ctx = _nt_dot(k_ctx[:, ksl], q)
        s_loc = jnp.where(valid, _nt_dot(k_loc[:, ksl], q), NEG_INF)
        m = jnp.maximum(jnp.maximum(jnp.max(s_ctx, axis=0, keepdims=True),
                                    jnp.max(s_loc, axis=0, keepdims=True)), sink)
        acc = (jnp.dot(vt_ctx[vsl, :], jnp.exp2(s_ctx - m).astype(BF16), preferred_element_type=F32)
               + jnp.dot(vt_loc[vsl, :], jnp.exp2(s_loc - m).astype(BF16), preferred_element_type=F32))
        l = acc[HEAD_DIM:HEAD_DIM + 1] + jnp.exp2(sink - m)
        o = (acc[:HEAD_DIM] / l).T
        for g, h in enumerate(heads):
            o_ref[:, h * HEAD_DIM:(h + 1) * HEAD_DIM] = o[g * WINDOW:(g + 1) * WINDOW].astype(BF16)


def _win_attention(sink, q, k, v):
    nblk = SEQ // WINDOW
    cblk = CTX_LEN // WINDOW
    kvw = WIN_KV_HEADS * HEAD_DIM
    ctx0 = R_LAT // CTX_LEN
    lat = lambda n: jnp.minimum(n, nblk - 1)
    prev = lambda b, n: (b * nblk + jnp.maximum(lat(n) - 1, 0), 0)
    cur = lambda b, n: (b * nblk + lat(n), 0)
    nxt = lambda b, n: (b * nblk + jnp.minimum(lat(n) + 1, nblk - 1), 0)
    qrow = lambda b, n: (jnp.where(n < nblk, b * nblk + n, R_LAT // WINDOW + b * cblk + n - nblk), 0)
    ctx = lambda b, n: (ctx0 + b, 0)
    vw = WIN_KV_HEADS * VAL_PAD
    kvb = lambda f, w: pl.BlockSpec((WINDOW, w), f)
    cxb = lambda w: pl.BlockSpec((CTX_LEN, w), ctx)
    flip = lambda f: (lambda b, n: f(b, n)[::-1])
    vtb = lambda f: pl.BlockSpec((vw, WINDOW), flip(f))
    return pl.pallas_call(
        _win_kernel,
        grid=(BATCH, nblk + cblk),
        in_specs=[pl.BlockSpec(memory_space=pltpu.SMEM),
                  pl.BlockSpec((WINDOW, WIN_HEADS * HEAD_DIM), qrow),
                  kvb(prev, kvw), kvb(cur, kvw), kvb(nxt, kvw), cxb(kvw),
                  vtb(prev), vtb(cur), vtb(nxt), pl.BlockSpec((vw, CTX_LEN), flip(ctx))],
        out_specs=pl.BlockSpec((WINDOW, WIN_HEADS * HEAD_DIM), qrow),
        out_shape=jax.ShapeDtypeStruct((R_ALL, WIN_HEADS * HEAD_DIM), BF16),
        compiler_params=_params(),
        name="window_attention",
    )(sink, q, k, k, k, k, v, v, v, v)


def _outproj_kernel(a_ref, b_ref, wa_ref, wb_ref, bo_ref, xl_ref, xc_ref, g1_ref, sh2_ref, sc2_ref,
                    lng_ref, lnb_ref, wrh_ref, wrl_ref, br_ref, upper_ref, lower_ref, sel_ref,
                    xo_ref, route_ref, cnt_ref, xs_ref):
    is_lat = pl.program_id(0) < NT_LAT // OUT_SUB
    for t in range(OUT_SUB):
        rows = slice(t * TM, (t + 1) * TM)
        x = jnp.where(is_lat, xl_ref[rows, :], xc_ref[rows, :])
        _outproj_tile(a_ref[rows, :], b_ref[rows, :], wa_ref, wb_ref, bo_ref, x, g1_ref, sh2_ref, sc2_ref,
                      lng_ref, lnb_ref, wrh_ref, wrl_ref, br_ref, upper_ref, lower_ref, sel_ref,
                      xo_ref.at[rows, :], route_ref.at[rows, :], cnt_ref.at[t * 8:(t + 1) * 8, :],
                      xs_ref.at[t * LOCAL_ROWS:(t + 1) * LOCAL_ROWS, :])


def _outproj_tile(a, b, wa_ref, wb_ref, bo_ref, x, g1_ref, sh2_ref, sc2_ref,
                  lng_ref, lnb_ref, wrh_ref, wrl_ref, br_ref, upper_ref, lower_ref, sel_ref,
                  xo_ref, route_ref, cnt_ref, xs_ref):
    y = (jnp.dot(a, wa_ref[...], preferred_element_type=F32)
         + jnp.dot(b, wb_ref[...], preferred_element_type=F32) + bo_ref[...])
    xn = _layer_norm(DN_ALPHA * x + (1.0 + g1_ref[...]) * y, lng_ref[...], lnb_ref[...])
    xo_ref[...] = xn
    u2 = xn * (1.0 + sc2_ref[...]) + sh2_ref[...]
    u_hi = u2.astype(BF16)
    u_lo = (u2 - u_hi.astype(F32)).astype(BF16)
    logits = (jnp.dot(u_hi, wrh_ref[...], preferred_element_type=F32)
              + jnp.dot(u_lo, wrh_ref[...], preferred_element_type=F32)
              + jnp.dot(u_hi, wrl_ref[...], preferred_element_type=F32) + br_ref[...])
    lane = lax.broadcasted_iota(jnp.int32, logits.shape, 1).astype(F32)
    ninf = -jnp.inf
    big = float(ROUTE_W)
    gl = jnp.where(lane < N_GROUPS, logits, ninf)
    gmax = jnp.max(gl, axis=-1, keepdims=True)
    gidx = jnp.min(jnp.where(gl == gmax, lane, big), axis=-1, keepdims=True)
    g_w = 1.0 / jnp.sum(jnp.exp(gl - gmax), axis=-1, keepdims=True)
    lo = N_GROUPS + EXP_PER_GROUP * gidx
    el = jnp.where(lane >= lo, jnp.where(lane < lo + EXP_PER_GROUP, logits, ninf), ninf)
    v1 = jnp.max(el, axis=-1, keepdims=True)
    i1 = jnp.min(jnp.where(el == v1, lane, big), axis=-1, keepdims=True)
    el2 = jnp.where(lane == i1, ninf, el)
    v2 = jnp.max(el2, axis=-1, keepdims=True)
    i2 = jnp.min(jnp.where(el2 == v2, lane, big), axis=-1, keepdims=True)
    e2 = jnp.exp(v2 - v1)
    w1 = g_w / (1.0 + e2)
    w2 = g_w * e2 / (1.0 + e2)
    onehot = [jnp.where(lane == i1 - N_GROUPS, 1.0, 0.0), jnp.where(lane == i2 - N_GROUPS, 1.0, 0.0)]
    cnt = [jnp.sum(o, axis=0, keepdims=True) for o in onehot]
    run_units = jnp.floor((cnt[0] + cnt[1] + (RUN_ALIGN - 1)) * (1.0 / RUN_ALIGN))
    below = RUN_ALIGN * jnp.dot(jnp.broadcast_to(run_units, (8, ROUTE_W)).astype(BF16), upper_ref[...],
                                preferred_element_type=F32)[0:1]
    lower = lower_ref[...]
    base = [below, below + cnt[0]]
    lp = []
    for s in range(2):
        earlier = jnp.dot(lower, onehot[s].astype(BF16), preferred_element_type=F32)
        lp.append(jnp.sum(onehot[s] * (base[s] + earlier), axis=-1, keepdims=True))
    rec = jnp.where(lane == 0.0, i1 - N_GROUPS,
                    jnp.where(lane == 1.0, i2 - N_GROUPS,
                              jnp.where(lane == 2.0, w1,
                                        jnp.where(lane == 3.0, w2,
                                                  jnp.where(lane == 4.0, lp[0],
                                                            jnp.where(lane == 5.0, lp[1], 0.0))))))
    route_ref[...] = rec
    cnt_ref[...] = jnp.broadcast_to(run_units * RUN_ALIGN, (8, ROUTE_W))
    sel = sel_ref[...]
    pos = lax.broadcasted_iota(jnp.int32, (LOCAL_ROWS, TM), 0).astype(F32)
    lp_lanes = []
    for s in range(2):
        hi = jnp.floor(lp[s] * (1.0 / 256.0))
        parts = jnp.where(lane == 0.0, lp[s] - 256.0 * hi, jnp.where(lane == 1.0, hi, 0.0)).astype(BF16)
        t = _nt_dot(sel, parts)
        lp_lanes.append(t[0:1] + 256.0 * t[1:2])
    perm = jnp.where(pos == lp_lanes[0], 1.0, jnp.where(pos == lp_lanes[1], 1.0, 0.0)).astype(BF16)
    xs_ref[...] = jnp.dot(perm, u_hi, preferred_element_type=F32)


def _outproj(n_tiles, mix_a, mix_b, w_a, w_b, b_out, x_lat, x_ctx, mods, ln_g, ln_b, w_r, b_r):
    rows = n_tiles * TM
    steps = n_tiles // OUT_SUB
    lat_steps = NT_LAT // OUT_SUB
    half = mix_a.shape[1]
    w_rh = w_r.astype(BF16)
    w_rl = (w_r - w_rh.astype(F32)).astype(BF16)
    upper = (jnp.arange(ROUTE_W)[:, None] < jnp.arange(ROUTE_W)[None, :]).astype(BF16)
    lower = (jnp.arange(TM)[:, None] > jnp.arange(TM)[None, :]).astype(BF16)
    sel = (jnp.arange(8)[:, None] == jnp.arange(ROUTE_W)[None, :]).astype(BF16)
    row = lambda w: pl.BlockSpec((OUT_SUB * TM, w), lambda i: (i, 0))
    mod = lambda chunk: pl.BlockSpec((None, None, 1, D_MODEL), lambda i: (_mod_row(i * OUT_SUB), chunk, 0, 0))
    vec = _full((1, D_MODEL))
    return pl.pallas_call(
        _outproj_kernel,
        grid=(steps,),
        in_specs=[row(half), row(half), _full((half, D_MODEL)), _full((half, D_MODEL)), vec,
                  pl.BlockSpec((OUT_SUB * TM, D_MODEL), lambda i: (jnp.minimum(i, lat_steps - 1), 0)),
                  pl.BlockSpec((OUT_SUB * TM, D_MODEL), lambda i: (jnp.maximum(i - lat_steps, 0), 0)),
                  mod(G1), mod(SH2), mod(SC2), vec, vec,
                  _full((D_MODEL, ROUTE_W)), _full((D_MODEL, ROUTE_W)), _full((1, ROUTE_W)),
                  _full((ROUTE_W, ROUTE_W)), _full((TM, TM)), _full((8, ROUTE_W))],
        out_specs=[row(D_MODEL), row(ROUTE_W), pl.BlockSpec((OUT_SUB * 8, ROUTE_W), lambda i: (i, 0)),
                   pl.BlockSpec((OUT_SUB * LOCAL_ROWS, D_MODEL), lambda i: (i, 0))],
        out_shape=[jax.ShapeDtypeStruct((rows, D_MODEL), F32),
                   jax.ShapeDtypeStruct((rows, ROUTE_W), F32),
                   jax.ShapeDtypeStruct((n_tiles * 8, ROUTE_W), F32),
                   jax.ShapeDtypeStruct((n_tiles * LOCAL_ROWS, D_MODEL), F32)],
        compiler_params=_params(),
        name="outproj_ln_router",
    )(mix_a, mix_b, w_a, w_b, b_out, x_lat, x_ctx, mods, mods, mods, ln_g, ln_b, w_rh, w_rl, b_r,
      upper, lower, sel)


def _aligned(i):
    return pl.multiple_of(i, RUN_ALIGN)


def _moe_kernel(te_ref, tk_ref, rows_ref, lo_ref, hi_ref, cnt_ref, bt_ref, be_ref, xs_hbm, w1_ref, w3_ref,
                w2_ref, ys_ref, xbuf, wb1, wb3, wb2, sem):
    j = pl.program_id(0)
    nt = pl.num_programs(0)
    slot = j % 2

    def issue(tile, slot_):
        e = te_ref[tile]
        first = tk_ref[tile] * TMM

        def body(i, carry):
            idx = i * N_EXPERTS + e
            start = bt_ref[idx]
            lo = jnp.maximum(start, first)
            n = jnp.minimum(start + cnt_ref[idx], first + TMM) - lo

            @pl.when(n > 0)
            def _():
                src = i * LOCAL_ROWS + be_ref[idx] + lo - start
                pltpu.make_async_copy(xs_hbm.at[pl.ds(_aligned(src), _aligned(n))],
                                      xbuf.at[slot_, pl.ds(_aligned(lo - first), _aligned(n))],
                                      sem.at[slot_]).start()
            return carry
        lax.fori_loop(lo_ref[tile], hi_ref[tile], body, 0)

    @pl.when(j == 0)
    def _():
        xbuf[...] = jnp.zeros_like(xbuf)
        issue(0, 0)

    @pl.when(j + 1 < nt)
    def _():
        issue(j + 1, 1 - slot)

    @pl.when(jnp.logical_or(j == 0, te_ref[j] != te_ref[jnp.maximum(j - 1, 0)]))
    def _():
        wb1[...] = w1_ref[...].astype(BF16)
        wb3[...] = w3_ref[...].astype(BF16)
        wb2[...] = w2_ref[...].astype(BF16)

    n_real = rows_ref[j]

    @pl.when(n_real > 0)
    def _():
        pltpu.make_async_copy(xs_hbm.at[pl.ds(0, _aligned(n_real))], xbuf.at[slot, pl.ds(0, _aligned(n_real))],
                              sem.at[slot]).wait()
        x = xbuf[slot].astype(BF16)
        h1 = jnp.dot(x, wb1[...], preferred_element_type=F32)
        h3 = jnp.dot(x, wb3[...], preferred_element_type=F32)
        hid = h1 * _sigmoid(h1) * h3
        ys_ref[...] = jnp.dot(hid.astype(BF16), wb2[...], preferred_element_type=F32)

    @pl.when(n_real == 0)
    def _():
        ys_ref[...] = jnp.zeros_like(ys_ref)


def _moe_experts(layer, plan, xs_local, w1, w3, w2):
    tile_expert, tile_k, tile_rows, src_lo, src_hi, cnt, before_tile, before_expert, _, _ = plan
    nt = tile_expert.shape[0]
    wmap = lambda j, te, *_: (layer, te[j], 0, 0)
    grid_spec = pltpu.PrefetchScalarGridSpec(
        num_scalar_prefetch=8,
        grid=(nt,),
        in_specs=[pl.BlockSpec(memory_space=pl.ANY),
                  pl.BlockSpec((None, None, D_MODEL, EXPERT_FF), wmap),
                  pl.BlockSpec((None, None, D_MODEL, EXPERT_FF), wmap),
                  pl.BlockSpec((None, None, EXPERT_FF, D_MODEL), wmap)],
        out_specs=pl.BlockSpec((TMM, D_MODEL), lambda j, *_: (j, 0)),
        scratch_shapes=[pltpu.VMEM((2, TMM, D_MODEL), F32),
                        pltpu.VMEM((D_MODEL, EXPERT_FF), BF16),
                        pltpu.VMEM((D_MODEL, EXPERT_FF), BF16),
                        pltpu.VMEM((EXPERT_FF, D_MODEL), BF16),
                        pltpu.SemaphoreType.DMA((2,))])
    return pl.pallas_call(
        _moe_kernel,
        grid_spec=grid_spec,
        out_shape=jax.ShapeDtypeStruct((nt * TMM, D_MODEL), F32),
        compiler_params=_params(),
        name="moe_experts",
    )(tile_expert, tile_k, tile_rows, src_lo, src_hi, cnt, before_tile, before_expert, xs_local, w1, w3, w2)


def _combine_kernel(cnt_ref, bt_ref, be_ref, gs_ref, used_ref, ys_hbm, x_ref, route_ref, g2_ref, lng_ref, lnb_ref,
                    o_ref, ybuf, sem):
    i = pl.program_id(0)
    nt = pl.num_programs(0)
    slot = i % 2

    def issue(tile, slot_):
        def body(e, carry):
            idx = tile * N_EXPERTS + e
            n = cnt_ref[idx]

            @pl.when(n > 0)
            def _():
                pltpu.make_async_copy(ys_hbm.at[pl.ds(_aligned(gs_ref[e] + bt_ref[idx]), _aligned(n))],
                                      ybuf.at[slot_, pl.ds(_aligned(be_ref[idx]), _aligned(n))],
                                      sem.at[slot_]).start()
            return carry
        lax.fori_loop(0, N_EXPERTS, body, 0)

    @pl.when(i == 0)
    def _():
        ybuf[...] = jnp.zeros_like(ybuf)
        issue(0, 0)

    @pl.when(i + 1 < nt)
    def _():
        issue(i + 1, 1 - slot)

    used = _aligned(used_ref[i])
    pltpu.make_async_copy(ys_hbm.at[pl.ds(0, used)], ybuf.at[slot, pl.ds(0, used)], sem.at[slot]).wait()
    route = route_ref[...]
    pos = lax.broadcasted_iota(jnp.int32, (TM, LOCAL_ROWS), 1).astype(F32)
    y = ybuf[slot].astype(BF16)
    picked = [jnp.dot(jnp.where(pos == route[:, 4 + s:5 + s], 1.0, 0.0).astype(BF16), y,
                      preferred_element_type=F32) for s in range(2)]
    f = route[:, 2:3] * picked[0] + route[:, 3:4] * picked[1]
    z = DN_ALPHA * x_ref[...] + (1.0 + g2_ref[...]) * f
    o_ref[...] = _layer_norm(z, lng_ref[...], lnb_ref[...])


def _moe_combine(n_tiles, plan, ys, x_all, route, mods, ln_g, ln_b):
    _, _, _, _, _, cnt, before_tile, before_expert, group_start, used = plan
    rows = n_tiles * TM
    row = lambda w: pl.BlockSpec((TM, w), lambda i, *_: (i, 0))
    vec = pl.BlockSpec((1, D_MODEL), lambda i, *_: (0, 0))
    grid_spec = pltpu.PrefetchScalarGridSpec(
        num_scalar_prefetch=5,
        grid=(n_tiles,),
        in_specs=[pl.BlockSpec(memory_space=pl.ANY), row(D_MODEL), row(ROUTE_W),
                  pl.BlockSpec((None, None, 1, D_MODEL), lambda i, *_: (_mod_row(i), G2, 0, 0)), vec, vec],
        out_specs=row(D_MODEL),
        scratch_shapes=[pltpu.VMEM((2, LOCAL_ROWS, D_MODEL), F32), pltpu.SemaphoreType.DMA((2,))])
    return pl.pallas_call(
        _combine_kernel,
        grid_spec=grid_spec,
        out_shape=jax.ShapeDtypeStruct((rows, D_MODEL), F32),
        compiler_params=_params(),
        name="moe_combine_ln",
    )(cnt, before_tile, before_expert, group_start, used, ys, x_all, route, mods, ln_g, ln_b)


def _moe_plan(cnt_rec, n_tiles):
    cnt = cnt_rec.reshape(n_tiles, 8, ROUTE_W)[:, 0, :N_EXPERTS].astype(jnp.int32)
    nt_max = (n_tiles * (2 * TM + N_EXPERTS * (RUN_ALIGN - 1))) // TMM + N_EXPERTS
    total = jnp.sum(cnt, axis=0)
    tiles_e = (total + TMM - 1) // TMM
    tile_end = jnp.cumsum(tiles_e)
    first_tile = tile_end - tiles_e
    before_tile = jnp.cumsum(cnt, axis=0) - cnt
    before_expert = jnp.cumsum(cnt, axis=1) - cnt
    tile_id = jnp.arange(nt_max, dtype=jnp.int32)
    tile_expert = jnp.minimum(jnp.sum((tile_id[:, None] >= tile_end[None, :]).astype(jnp.int32), axis=1),
                              N_EXPERTS - 1)
    onehot = (tile_expert[:, None] == jnp.arange(N_EXPERTS, dtype=jnp.int32)[None, :]).astype(jnp.int32)
    tile_k = tile_id - jnp.sum(onehot * first_tile[None, :], axis=1)
    tile_rows = jnp.clip(jnp.sum(onehot * total[None, :], axis=1) - tile_k * TMM, 0, TMM)
    first = (tile_k * TMM)[:, None]
    run_start = jnp.sum(onehot[:, None, :] * before_tile[None, :, :], axis=2)
    run_end = run_start + jnp.sum(onehot[:, None, :] * cnt[None, :, :], axis=2)
    src_lo = jnp.sum((run_end <= first).astype(jnp.int32), axis=1)
    src_hi = jnp.sum((run_start < first + TMM).astype(jnp.int32), axis=1)
    return (tile_expert, tile_k, tile_rows, src_lo, src_hi, cnt.reshape(-1), before_tile.reshape(-1),
            before_expert.reshape(-1), first_tile * TMM, jnp.sum(cnt, axis=1))


def _router_weights(w_rg, b_rg, w_re, b_re):
    w = jnp.concatenate([w_rg, jnp.transpose(w_re, (1, 0, 2)).reshape(D_MODEL, N_EXPERTS)], axis=1)
    b = jnp.concatenate([b_rg, b_re.reshape(-1)])
    pad = ROUTE_W - w.shape[1]
    return jnp.pad(w, ((0, 0), (0, pad))), jnp.pad(b, (0, pad)).reshape(1, ROUTE_W)


def _proj1_kernel(x_ref, sh_ref, sc_ref, w_ref, b_ref, cos_ref, sin_ref, cosm_ref, sinm_ref, cosr_ref,
                  sinr_ref, gq_ref, gk_ref, gqc_ref, gkv_ref, avg_ref, wuq_ref, wuk_ref, wuv_ref, vplace_ref,
                  q_ref, qm_ref, k_ref, v_ref, km_ref, vm_ref):
    u = x_ref[...] * (1.0 + sc_ref[...]) + sh_ref[...]
    y = jnp.dot(u.astype(BF16), w_ref[...], preferred_element_type=F32) + b_ref[...]
    c_q = GQA_HEADS * HEAD_DIM
    c_qc = c_q + MLA_Q_RANK
    c_k = c_qc + GQA_KV_HEADS * HEAD_DIM
    c_v = c_k + GQA_KV_HEADS * HEAD_DIM
    c_kv = c_v + MLA_KV_RANK
    avg = avg_ref[...]

    def head_rms(t, gain):
        sq = t * t
        hi = sq.astype(BF16)
        lo = (sq - hi.astype(F32)).astype(BF16)
        a = avg[:t.shape[1], :t.shape[1]]
        ms = jnp.dot(hi, a, preferred_element_type=F32) + jnp.dot(lo, a, preferred_element_type=F32)
        return t * lax.rsqrt(ms + RMS_EPS) * gain

    def row_rms(t, gain):
        ms = jnp.mean(t * t, axis=-1, keepdims=True)
        return t * lax.rsqrt(ms + RMS_EPS) * gain

    cos = cos_ref[...]
    sin = sin_ref[...]
    cos4 = jnp.concatenate([cos] * 4, axis=1)
    sin4 = jnp.concatenate([sin] * 4, axis=1)
    q = _rope(head_rms(y[:, :c_q], gq_ref[...]), cos4, sin4, HEAD_DIM // 4) * (HEAD_DIM ** -0.5 * LOG2E)
    q_ref[...] = q.astype(BF16)
    k = _rope(head_rms(y[:, c_qc:c_k], gk_ref[...]), cos, sin, HEAD_DIM // 4)
    k_ref[...] = k.astype(BF16)

    v_ref[...] = _transposed_values(vplace_ref[...], y[:, c_k:c_v].astype(BF16), HEAD_DIM)

    qc = row_rms(y[:, c_q:c_qc], gqc_ref[...]).astype(BF16)
    qm = jnp.dot(qc, wuq_ref[...], preferred_element_type=F32)
    cosm = jnp.concatenate([cosm_ref[...]] * MLA_HEADS, axis=1)
    sinm = jnp.concatenate([sinm_ref[...]] * MLA_HEADS, axis=1)
    qm = _rope(qm, cosm, sinm, MLA_ROPE // 4) * ((MLA_NOPE + MLA_ROPE) ** -0.5 * LOG2E)
    qm_ref[...] = qm.astype(BF16)

    kvn = row_rms(y[:, c_v:c_kv], gkv_ref[...]).astype(BF16)
    kr = _rope(y[:, c_kv:], cosr_ref[...], sinr_ref[...], MLA_ROPE // 4).astype(BF16)
    km = jnp.dot(jnp.concatenate([kvn, kr], axis=1), wuk_ref[...], preferred_element_type=F32)
    km_ref[...] = km.astype(BF16)
    vm_ref[...] = _transposed_values(wuv_ref[...], kvn, MLA_V)


def _proj1(x_all, mods, w_in, b_in, tabs, gq, gk, gqc, gkv, avg, wuq, wuk, wuv):
    cos_hd, sin_hd, cos_m, sin_m, cos_r, sin_r = tabs
    kvw = GQA_KV_HEADS * HEAD_DIM
    qw = GQA_HEADS * HEAD_DIM
    mw = MLA_HEADS * MLA_PAD
    vw = MLA_HEADS * VAL_PAD
    gvw = GQA_KV_HEADS * VAL_PAD
    row = lambda w: pl.BlockSpec((TM, w), lambda i: (i, 0))
    col = lambda h: pl.BlockSpec((h, TM), lambda i: (0, i))
    tab = pl.BlockSpec((TM, 128), lambda i: (_rope_row_block(i), 0))
    vplace = _value_placement(GQA_KV_HEADS, HEAD_DIM)
    return pl.pallas_call(
        _proj1_kernel,
        grid=(NT_ALL,),
        in_specs=[row(D_MODEL), _mod_spec(SH1), _mod_spec(SC1),
                  _full((D_MODEL, ODD_IN_PAD)), _full((1, ODD_IN_PAD)), tab, tab, tab, tab, tab, tab,
                  _full((1, qw)), _full((1, kvw)), _full((1, MLA_Q_RANK)), _full((1, MLA_KV_RANK)),
                  _full((qw, qw)), _full((MLA_Q_RANK, mw)), _full((MLA_KV_RANK + 128, mw)),
                  _full((vw, MLA_KV_RANK)), _full((gvw, kvw))],
        out_specs=[row(qw), row(mw), row(kvw), col(gvw), row(mw), col(vw)],
        out_shape=[jax.ShapeDtypeStruct((R_ALL, qw), BF16),
                   jax.ShapeDtypeStruct((R_ALL, mw), BF16),
                   jax.ShapeDtypeStruct((R_ALL, kvw), BF16),
                   jax.ShapeDtypeStruct((gvw, R_ALL), BF16),
                   jax.ShapeDtypeStruct((R_ALL, mw), BF16),
                   jax.ShapeDtypeStruct((vw, R_ALL), BF16)],
        compiler_params=_params(),
        name="proj1",
    )(x_all, mods, mods, w_in, b_in, cos_hd, sin_hd, cos_m, sin_m, cos_r, sin_r,
      gq, gk, gqc, gkv, avg, wuq, wuk, wuv.T, vplace)


def _dense_kernel(q_ref, kl_ref, kc_ref, vl_ref, vc_ref, o_ref, s_buf, p_buf, *, n_heads, group, stack, dk, dv):
    tq = q_ref.shape[0]
    units = []
    for h0 in range(0, n_heads, stack):
        kv = h0 // group
        qs = [q_ref[:, h * dk:(h + 1) * dk] for h in range(h0, h0 + stack)]
        q = qs[0] if stack == 1 else jnp.concatenate(qs, axis=0)
        ks = slice(kv * dk, (kv + 1) * dk)
        vs = slice(kv * VAL_PAD, (kv + 1) * VAL_PAD)
        parts = [slice(c * (SEQ // KEY_PARTS), (c + 1) * (SEQ // KEY_PARTS)) for c in range(KEY_PARTS)]
        units.append((q, [kl_ref[c, ks] for c in parts] + [kc_ref[:, ks]],
                      [vl_ref[vs, c] for c in parts] + [vc_ref[vs, :]]))
    for u, o_t in enumerate(_attend_keys_major(units, dv, s_buf, p_buf)):
        o = o_t.T
        for g in range(stack):
            h = u * stack + g
            o_ref[:, h * dv:(h + 1) * dv] = o[g * tq:(g + 1) * tq].astype(BF16)


def _dense_attention(q, k, v, *, n_heads, group, stack, dk, dv, tq, name):
    n_kv = n_heads // group
    nq = SEQ // tq
    ctx0 = R_LAT // CTX_LEN
    lat = lambda w: pl.BlockSpec((SEQ, w), lambda b, j: (b, 0), pipeline_mode=pl.Buffered(1))
    ctx = lambda w: pl.BlockSpec((CTX_LEN, w), lambda b, j: (ctx0 + b, 0))
    lat_t = pl.BlockSpec((n_kv * VAL_PAD, SEQ), lambda b, j: (0, b), pipeline_mode=pl.Buffered(1))
    ctx_t = pl.BlockSpec((n_kv * VAL_PAD, CTX_LEN), lambda b, j: (0, ctx0 + b))
    return pl.pallas_call(
        functools.partial(_dense_kernel, n_heads=n_heads, group=group, stack=stack, dk=dk, dv=dv),
        grid=(BATCH, nq),
        in_specs=[pl.BlockSpec((tq, n_heads * dk), lambda b, j: (b * nq + j, 0)),
                  lat(n_kv * dk), ctx(n_kv * dk), lat_t, ctx_t],
        out_specs=pl.BlockSpec((tq, n_heads * dv), lambda b, j: (b * nq + j, 0)),
        out_shape=jax.ShapeDtypeStruct((R_LAT, n_heads * dv), BF16),
        scratch_shapes=[pltpu.VMEM((SCORE_SLOTS, SEQ + CTX_LEN, stack * tq), F32),
                        pltpu.VMEM((2, SEQ + CTX_LEN, stack * tq), BF16)],
        compiler_params=_params(),
        name=name,
    )(q, k, k, v, v)


def _mla_weights(w_uq, w_ukv):
    wq = w_uq.reshape(MLA_Q_RANK, MLA_HEADS, MLA_NOPE + MLA_ROPE)
    wq = jnp.pad(wq, ((0, 0), (0, 0), (0, MLA_PAD - MLA_NOPE - MLA_ROPE))).reshape(MLA_Q_RANK, -1)
    wkv = w_ukv.reshape(MLA_KV_RANK, MLA_HEADS, MLA_NOPE + MLA_V)
    wk = jnp.pad(wkv[:, :, :MLA_NOPE], ((0, 0), (0, 0), (0, MLA_PAD - MLA_NOPE))).reshape(MLA_KV_RANK, -1)
    wv = jnp.pad(wkv[:, :, MLA_NOPE:], ((0, 0), (0, 0), (0, VAL_PAD - MLA_V))).reshape(MLA_KV_RANK, -1)
    r = jnp.arange(128)[:, None]
    c = jnp.arange(MLA_HEADS * MLA_PAD)[None, :]
    place = jnp.logical_and(r < MLA_ROPE, (c % MLA_PAD) == MLA_NOPE + r).astype(F32)
    wk = jnp.concatenate([wk, place], axis=0)
    return wq.astype(BF16), wk.astype(BF16), wv.astype(BF16)


def kernel(x, c, ctx, c_ctx, even_w_in, even_b_in, even_conv_w, even_conv_b, even_conv_ln_g, even_conv_ln_b, even_sink, even_w_out, even_b_out, odd_w_in, odd_b_in, odd_q_norm, odd_k_norm, odd_mla_q_norm, odd_mla_kv_norm, odd_mla_w_uq, odd_mla_w_ukv, odd_w_out, odd_b_out, ada_w, ada_b, ln1_g, ln1_b, ln2_g, ln2_b, moe_w_rg, moe_b_rg, moe_w_re, moe_b_re, moe_w1, moe_w3, moe_w2):
    vec = lambda a: a.reshape(1, -1)
    x_lat0 = x.reshape(R_LAT, D_MODEL)
    x_ctx0 = ctx.reshape(R_CTX, D_MODEL)

    cv =jnp.concatenate([c, c_ctx[None, :], jnp.zeros((8 - BATCH - 1, D_MODEL), F32)], axis=0)
    mods = _ada_table(cv, ada_w, ada_b).reshape(DEPTH, 8, 6, 1, D_MODEL)

    cos64, sin64 = _rope_tables(HEAD_DIM)
    cos_hd, sin_hd = _pad_table(cos64, sin64, 0, HEAD_DIM, 128)
    cos32, sin32 = _rope_tables(MLA_ROPE)
    cos_m, sin_m = _pad_table(cos32, sin32, MLA_NOPE, MLA_PAD, 128)
    cos_r, sin_r = _pad_table(cos32, sin32, 0, 128, 128)

    m0 = mods[0]
    h, q0, k0, v0 = _proj0(x_lat0, x_ctx0, m0, even_w_in[0].astype(BF16), vec(even_b_in[0]), cos_hd, sin_hd)
    conv_out = _conv(h, even_conv_w[0].reshape(CONV_WIDTH, CONV_CH), vec(even_conv_b[0]),
                     vec(even_conv_ln_g[0]), vec(even_conv_ln_b[0]))
    attn = _win_attention(even_sink[0], q0, k0, v0)
    w_out = even_w_out[0].astype(BF16)
    w_r, b_r = _router_weights(moe_w_rg[0], moe_b_rg[0], moe_w_re[0], moe_b_re[0])
    x_all, route, cnt_rec, xs_local = _outproj(
        NT_ALL, conv_out, attn, w_out[:CONV_CH], w_out[CONV_CH:], vec(even_b_out[0]),
        x_lat0, x_ctx0, m0, vec(ln1_g[0]), vec(ln1_b[0]), w_r, b_r)
    plan = _moe_plan(cnt_rec, NT_ALL)
    ys = _moe_experts(0, plan, xs_local, moe_w1, moe_w3, moe_w2)
    x_all = _moe_combine(NT_ALL, plan, ys, x_all, route, m0, vec(ln2_g[0]), vec(ln2_b[0]))

    m1 = mods[1]
    w_in1 = jnp.pad(odd_w_in[0], ((0, 0), (0, ODD_IN_PAD - ODD_IN))).astype(BF16)
    b_in1 = jnp.pad(odd_b_in[0], (0, ODD_IN_PAD - ODD_IN)).reshape(1, -1)
    wuq, wuk, wuv = _mla_weights(odd_mla_w_uq[0], odd_mla_w_ukv[0])
    qw = GQA_HEADS * HEAD_DIM
    hid = jnp.arange(qw) // HEAD_DIM
    avg = ((hid[:, None] == hid[None, :]).astype(F32) / HEAD_DIM).astype(BF16)
    q1, qm, k1, v1, km, vm = _proj1(
        x_all, m1, w_in1, b_in1, (cos_hd, sin_hd, cos_m, sin_m, cos_r, sin_r),
        vec(jnp.tile(odd_q_norm[0], GQA_HEADS)), vec(jnp.tile(odd_k_norm[0], GQA_KV_HEADS)),
        vec(odd_mla_q_norm[0]), vec(odd_mla_kv_norm[0]), avg, wuq, wuk, wuv)
    o_g = _dense_attention(q1, k1, v1, n_heads=GQA_HEADS, group=GQA_HEADS // GQA_KV_HEADS, stack=2,
                           dk=HEAD_DIM, dv=HEAD_DIM, tq=256, name="gqa_attention")
    o_m = _dense_attention(qm, km, vm, n_heads=MLA_HEADS, group=1, stack=1, dk=MLA_PAD, dv=MLA_V, tq=256,
                           name="mla_attention")
    w_out = odd_w_out[0].astype(BF16)
    w_r, b_r = _router_weights(moe_w_rg[1], moe_b_rg[1], moe_w_re[1], moe_b_re[1])
    x_lat, route, cnt_rec, xs_local = _outproj(
        NT_LAT, o_g, o_m, w_out[:qw], w_out[qw:], vec(odd_b_out[0]),
        x_all, x_all, m1, vec(ln1_g[1]), vec(ln1_b[1]), w_r, b_r)
    plan = _moe_plan(cnt_rec, NT_LAT)
    ys = _moe_experts(1, plan, xs_local, moe_w1, moe_w3, moe_w2)
    x_lat = _moe_combine(NT_LAT, plan, ys, x_lat, route, m1, vec(ln2_g[1]), vec(ln2_b[1]))
    return x_lat.reshape(BATCH, SEQ, D_MODEL)
```

```python
import functools

import jax
import jax.numpy as jnp
from jax import lax
from jax.experimental import pallas as pl
from jax.experimental.pallas import tpu as pltpu

F32 = jnp.float32
BF16 = jnp.bfloat16

D_MODEL = 1024
BATCH = 4
SEQ = 4096
DEPTH = 2
GRID_W = 64
CTX_LEN = 256
HEAD_DIM = 64
ROPE_THETA = 10000.0
LN_EPS = 1e-5
RMS_EPS = 1e-6
NEG_INF = -1e30

CONV_CH = 512
CONV_WIDTH = 31
WIN_HEADS = 8
WIN_KV_HEADS = 2
WINDOW = 128
GQA_HEADS = 8
GQA_KV_HEADS = 2
MLA_HEADS = 8
MLA_Q_RANK = 256
MLA_KV_RANK = 128
MLA_NOPE = 64
MLA_ROPE = 32
MLA_V = 64
N_GROUPS = 4
EXP_PER_GROUP = 8
N_EXPERTS = N_GROUPS * EXP_PER_GROUP
EXPERT_FF = 512
DN_ALPHA = float((2 * DEPTH) ** 0.25)

EVEN_IN = 2 * CONV_CH + (WIN_HEADS + 2 * WIN_KV_HEADS) * HEAD_DIM
ODD_IN = 1184
ODD_IN_PAD = 1280
MLA_PAD = 128
VAL_PAD = 128
SCORE_SLOTS = 2
KEY_PARTS = 4
LOG2E = 1.4426950408889634

R_LAT = BATCH * SEQ
R_CTX = BATCH * CTX_LEN
R_ALL = R_LAT + R_CTX
TM = 256
NT_LAT = R_LAT // TM
NT_ALL = R_ALL // TM
TILES_PER_SEQ = SEQ // TM
HALO = 16
CONV_CHUNK = 32
SHIFTS = 8
OUT_SUB = 2
TMM = 256
ROUTE_W = 128
RUN_ALIGN = 8
LOCAL_ROWS = 768
VMEM_LIMIT = 56 * 1024 * 1024

SH1, SC1, G1, SH2, SC2, G2 = range(6)


def _sigmoid(x):
    return 1.0 / (1.0 + jnp.exp(-x))


def _layer_norm(z, g, b):
    mu = jnp.mean(z, axis=-1, keepdims=True)
    zc = z - mu
    var = jnp.mean(zc * zc, axis=-1, keepdims=True)
    return zc * lax.rsqrt(var + LN_EPS) * g + b


def _rope(x, cos, sin, half):
    n = x.shape[-1]
    lane = lax.broadcasted_iota(jnp.int32, x.shape, 1)
    first = (lane % (2 * half)) < half
    partner = jnp.where(first, pltpu.roll(x, n - half, 1), pltpu.roll(x, half, 1))
    return x * cos + partner * sin


def _mod_row(i):
    return jnp.where(i < NT_LAT, i // TILES_PER_SEQ, BATCH)


def _mod_spec(chunk):
    return pl.BlockSpec((None, None, 1, D_MODEL), lambda i: (_mod_row(i), chunk, 0, 0))


def _rope_row_block(i):
    return jnp.where(i < NT_LAT, i % TILES_PER_SEQ, TILES_PER_SEQ)


def _full(shape):
    nd = len(shape)
    return pl.BlockSpec(shape, lambda *_: (0,) * nd)


def _params():
    return pltpu.CompilerParams(vmem_limit_bytes=VMEM_LIMIT)


def _ada_kernel(cv_ref, w_ref, b_ref, o_ref):
    cv = cv_ref[...]
    s = cv * _sigmoid(cv)
    o_ref[...] = jnp.dot(s, w_ref[...], precision=lax.Precision.HIGHEST,
                         preferred_element_type=F32) + b_ref[...]


def _ada_table(cv, ada_w, ada_b):
    bn = 1536
    nb = (6 * D_MODEL) // bn
    return pl.pallas_call(
        _ada_kernel,
        grid=(DEPTH, nb),
        in_specs=[pl.BlockSpec((8, D_MODEL), lambda l, j: (0, 0)),
                  pl.BlockSpec((None, D_MODEL, bn), lambda l, j: (l, 0, j)),
                  pl.BlockSpec((None, 1, bn), lambda l, j: (l, 0, j))],
        out_specs=pl.BlockSpec((None, 8, bn), lambda l, j: (l, 0, j)),
        out_shape=jax.ShapeDtypeStruct((DEPTH, 8, 6 * D_MODEL), F32),
        compiler_params=_params(),
        name="ada_table",
    )(cv, ada_w, ada_b.reshape(DEPTH, 1, 6 * D_MODEL))


def _rope_tables(rot_dim):
    axis_dim = rot_dim // 2
    inv_freq = ROPE_THETA ** (-jnp.arange(0, axis_dim, 2, dtype=F32) / axis_dim)
    t = jnp.arange(SEQ)
    ang_r = (t // GRID_W).astype(F32)[:, None] * inv_freq[None, :]
    ang_c = (t % GRID_W).astype(F32)[:, None] * inv_freq[None, :]
    cos = jnp.concatenate([jnp.cos(ang_r), jnp.cos(ang_r), jnp.cos(ang_c), jnp.cos(ang_c)], axis=-1)
    sin = jnp.concatenate([-jnp.sin(ang_r), jnp.sin(ang_r), -jnp.sin(ang_c), jnp.sin(ang_c)], axis=-1)
    return cos, sin


def _pad_table(cos, sin, lead, period, width):
    rot = cos.shape[1]
    one = jnp.ones((SEQ, period), F32).at[:, lead:lead + rot].set(cos)
    zero = jnp.zeros((SEQ, period), F32).at[:, lead:lead + rot].set(sin)
    cos_w = jnp.tile(one, (1, width // period))
    sin_w = jnp.tile(zero, (1, width // period))
    cos_w = jnp.concatenate([cos_w, jnp.ones((TM, width), F32)], axis=0)
    sin_w = jnp.concatenate([sin_w, jnp.zeros((TM, width), F32)], axis=0)
    return cos_w, sin_w


def _proj0_kernel(xl_ref, xc_ref, sh_ref, sc_ref, w_ref, b_ref, cos_ref, sin_ref, vplace_ref,
                  h_ref, q_ref, k_ref, v_ref):
    x = jnp.where(pl.program_id(0) < NT_LAT, xl_ref[...], xc_ref[...])
    u = x * (1.0 + sc_ref[...]) + sh_ref[...]
    y = jnp.dot(u.astype(BF16), w_ref[...], preferred_element_type=F32) + b_ref[...]
    h_ref[...] = y[:, :CONV_CH] * _sigmoid(y[:, CONV_CH:2 * CONV_CH])
    cos = cos_ref[...]
    sin = sin_ref[...]
    q0 = 2 * CONV_CH
    k0 = q0 + WIN_HEADS * HEAD_DIM
    v0 = k0 + WIN_KV_HEADS * HEAD_DIM
    cos4 = jnp.concatenate([cos] * 4, axis=1)
    sin4 = jnp.concatenate([sin] * 4, axis=1)
    q = _rope(y[:, q0:k0], cos4, sin4, HEAD_DIM // 4) * (HEAD_DIM ** -0.5 * LOG2E)
    q_ref[...] = q.astype(BF16)
    k_ref[...] = _rope(y[:, k0:v0], cos, sin, HEAD_DIM // 4).astype(BF16)
    v_ref[...] = _transposed_values(vplace_ref[...], y[:, v0:].astype(BF16), HEAD_DIM)


def _proj0(x_lat, x_ctx, mods, w_in, b_in, cos_hd, sin_hd):
    kvw = WIN_KV_HEADS * HEAD_DIM
    row = lambda w: pl.BlockSpec((TM, w), lambda i: (i, 0))
    tab = pl.BlockSpec((TM, 128), lambda i: (_rope_row_block(i), 0))
    return pl.pallas_call(
        _proj0_kernel,
        grid=(NT_ALL,),
        in_specs=[pl.BlockSpec((TM, D_MODEL), lambda i: (jnp.minimum(i, NT_LAT - 1), 0)),
                  pl.BlockSpec((TM, D_MODEL), lambda i: (jnp.maximum(i - NT_LAT, 0), 0)),
                  _mod_spec(SH1), _mod_spec(SC1),
                  _full((D_MODEL, EVEN_IN)), _full((1, EVEN_IN)), tab, tab,
                  _full((WIN_KV_HEADS * VAL_PAD, kvw))],
        out_specs=[row(CONV_CH), row(WIN_HEADS * HEAD_DIM), row(kvw),
                   pl.BlockSpec((WIN_KV_HEADS * VAL_PAD, TM), lambda i: (0, i))],
        out_shape=[jax.ShapeDtypeStruct((R_ALL, CONV_CH), F32),
                   jax.ShapeDtypeStruct((R_ALL, WIN_HEADS * HEAD_DIM), BF16),
                   jax.ShapeDtypeStruct((R_ALL, kvw), BF16),
                   jax.ShapeDtypeStruct((WIN_KV_HEADS * VAL_PAD, R_ALL), BF16)],
        compiler_params=_params(),
        name="proj0",
    )(x_lat, x_ctx, mods, mods, w_in, b_in, cos_hd, sin_hd, _value_placement(WIN_KV_HEADS, HEAD_DIM))


def _conv_kernel(prev_ref, cur_ref, next_ref, w_ref, cb_ref, g_ref, b_ref, o_ref, buf):
    i = pl.program_id(0)
    is_ctx = i >= NT_LAT
    first = jnp.logical_or(is_ctx, i % TILES_PER_SEQ == 0)
    last = jnp.logical_or(is_ctx, i % TILES_PER_SEQ == TILES_PER_SEQ - 1)
    buf[0, 0:HALO, :] = jnp.where(first, 0.0, prev_ref[...])
    buf[0, HALO:HALO + TM, :] = cur_ref[...]
    buf[0, HALO + TM:, :] = jnp.where(last, 0.0, next_ref[...])
    span = TM + 2 * HALO - SHIFTS
    for r in range(1, SHIFTS):
        buf[r, 0:span, :] = buf[0, r:r + span, :]
    off = HALO - CONV_WIDTH // 2
    for c in range(TM // CONV_CHUNK):
        r0 = c * CONV_CHUNK
        acc = jnp.zeros((CONV_CHUNK, CONV_CH), F32)
        for k in range(CONV_WIDTH):
            r = (off + k) % SHIFTS
            base = r0 + off + k - r
            w = w_ref[k * SHIFTS:(k + 1) * SHIFTS, :]
            acc = acc + buf[r, base:base + CONV_CHUNK, :] * jnp.concatenate([w] * (CONV_CHUNK // SHIFTS), axis=0)
        z = _layer_norm(acc + cb_ref[...], g_ref[...], b_ref[...])
        o_ref[r0:r0 + CONV_CHUNK, :] = (z * _sigmoid(z)).astype(BF16)


def _conv(h, conv_w, conv_b, ln_g, ln_b):
    nh = R_ALL // HALO
    per = TM // HALO
    vec = _full((1, CONV_CH))
    return pl.pallas_call(
        _conv_kernel,
        grid=(NT_ALL,),
        in_specs=[pl.BlockSpec((HALO, CONV_CH), lambda i: (jnp.maximum(i * per - 1, 0), 0)),
                  pl.BlockSpec((TM, CONV_CH), lambda i: (i, 0)),
                  pl.BlockSpec((HALO, CONV_CH), lambda i: (jnp.minimum((i + 1) * per, nh - 1), 0)),
                  _full((CONV_WIDTH * SHIFTS, CONV_CH)), vec, vec, vec],
        out_specs=pl.BlockSpec((TM, CONV_CH), lambda i: (i, 0)),
        out_shape=jax.ShapeDtypeStruct((R_ALL, CONV_CH), BF16),
        scratch_shapes=[pltpu.VMEM((SHIFTS, TM + 2 * HALO, CONV_CH), F32)],
        compiler_params=_params(),
        name="conv_module",
    )(h, h, h, jnp.repeat(conv_w, SHIFTS, axis=0), conv_b, ln_g, ln_b)


def _nt_dot(a, b):
    return lax.dot_general(a, b, (((1,), (1,)), ((), ())), preferred_element_type=F32)


def _transposed_values(w_t, src, dv):
    vt = _nt_dot(w_t, src)
    r = lax.broadcasted_iota(jnp.int32, vt.shape, 0)
    return jnp.where(r % VAL_PAD == dv, 1.0, vt).astype(BF16)


def _value_placement(n_kv, dv):
    r = jnp.arange(n_kv * VAL_PAD)[:, None]
    c = jnp.arange(n_kv * dv)[None, :]
    return jnp.logical_and(r // VAL_PAD == c // dv, r % VAL_PAD == c % dv).astype(BF16)


def _attend_keys_major(units, dv, s_buf, p_buf):
    def scores(unit, slot):
        q, ks, _ = unit
        row, ms = 0, []
        for k in ks:
            s = _nt_dot(k, q)
            s_buf[slot, row:row + k.shape[0], :] = s
            ms.append(jnp.max(s, axis=0, keepdims=True))
            row += k.shape[0]
        return functools.reduce(jnp.maximum, ms)

    def run_next_scores_with(slot, nxt_slot, pieces):
        row = 0
        for k in pieces:
            tile = (slice(row + k.shape[0] - 8, row + k.shape[0]), slice(0, 128))
            s_buf[slot, tile[0], tile[1]] = s_buf[slot, tile[0], tile[1]] + 0.0 * s_buf[nxt_slot, tile[0], tile[1]]
            row += k.shape[0]

    n_s, n_p = s_buf.shape[0], p_buf.shape[0]
    results = []
    m = scores(units[0], 0)
    for idx, unit in enumerate(units):
        slot, nxt_slot, pslot = idx % n_s, (idx + 1) % n_s, idx % n_p
        m_next = None
        if idx + 1 < len(units):
            m_next = scores(units[idx + 1], nxt_slot)
            run_next_scores_with(slot, nxt_slot, unit[1])
        row, acc = 0, None
        for vt in unit[2]:
            rows = slice(row, row + vt.shape[1])
            p_buf[pslot, rows, :] = jnp.exp2(s_buf[slot, rows, :] - m).astype(BF16)
            part = jnp.dot(vt, p_buf[pslot, rows, :], preferred_element_type=F32)
            acc = part if acc is None else acc + part
            row += vt.shape[1]
        results.append(acc[:dv] / acc[dv:dv + 1])
        m = m_next
    return results


def _win_kernel(sink_ref, q_ref, kp_ref, kc_ref, kn_ref, kx_ref, vp_ref, vc_ref, vn_ref, vx_ref, o_ref):
    n = pl.program_id(1)
    group = WIN_HEADS // WIN_KV_HEADS
    k_loc = jnp.concatenate([kp_ref[...], kc_ref[...], kn_ref[...]], axis=0)
    vt_loc = jnp.concatenate([vp_ref[...], vc_ref[...], vn_ref[...]], axis=1)
    k_ctx = kx_ref[...]
    vt_ctx = vx_ref[...]
    kj = lax.broadcasted_iota(jnp.int32, (3 * WINDOW, group * WINDOW), 0)
    qi = lax.broadcasted_iota(jnp.int32, (3 * WINDOW, group * WINDOW), 1) % WINDOW
    k_pos = jnp.where(n < SEQ // WINDOW, kj + (n - 1) * WINDOW, SEQ)
    valid = jnp.where(kj >= qi, jnp.where(kj <= qi + 2 * WINDOW, 1, 0), 0)
    valid = jnp.where(k_pos >= 0, jnp.where(k_pos < SEQ, valid, 0), 0) > 0
    for kv in range(WIN_KV_HEADS):
        heads = range(kv * group, (kv + 1) * group)
        q = jnp.concatenate([q_ref[:, h * HEAD_DIM:(h + 1) * HEAD_DIM] for h in heads], axis=0)
        sink = jnp.concatenate([jnp.full((1, WINDOW), sink_ref[h] * LOG2E, F32) for h in heads], axis=1)
        ksl = slice(kv * HEAD_DIM, (kv + 1) * HEAD_DIM)
        vsl = slice(kv * VAL_PAD, (kv + 1) * VAL_PAD)
        s_ctx = _nt_dot(k_ctx[:, ksl], q)
        s_loc = jnp.where(valid, _nt_dot(k_loc[:, ksl], q), NEG_INF)
        m = jnp.maximum(jnp.maximum(jnp.max(s_ctx, axis=0, keepdims=True),
                                    jnp.max(s_loc, axis=0, keepdims=True)), sink)
        acc = (jnp.dot(vt_ctx[vsl, :], jnp.exp2(s_ctx - m).astype(BF16), preferred_element_type=F32)
               + jnp.dot(vt_loc[vsl, :], jnp.exp2(s_loc - m).astype(BF16), preferred_element_type=F32))
        l = acc[HEAD_DIM:HEAD_DIM + 1] + jnp.exp2(sink - m)
        o = (acc[:HEAD_DIM] / l).T
        for g, h in enumerate(heads):
            o_ref[:, h * HEAD_DIM:(h + 1) * HEAD_DIM] = o[g * WINDOW:(g + 1) * WINDOW].astype(BF16)


def _win_attention(sink, q, k, v):
    nblk = SEQ // WINDOW
    cblk = CTX_LEN // WINDOW
    kvw = WIN_KV_HEADS * HEAD_DIM
    ctx0 = R_LAT // CTX_LEN
    lat = lambda n: jnp.minimum(n, nblk - 1)
    prev = lambda b, n: (b * nblk + jnp.maximum(lat(n) - 1, 0), 0)
    cur = lambda b, n: (b * nblk + lat(n), 0)
    nxt = lambda b, n: (b * nblk + jnp.minimum(lat(n) + 1, nblk - 1), 0)
    qrow = lambda b, n: (jnp.where(n < nblk, b * nblk + n, R_LAT // WINDOW + b * cblk + n - nblk), 0)
    ctx = lambda b, n: (ctx0 + b, 0)
    vw = WIN_KV_HEADS * VAL_PAD
    kvb = lambda f, w: pl.BlockSpec((WINDOW, w), f)
    cxb = lambda w: pl.BlockSpec((CTX_LEN, w), ctx)
    flip = lambda f: (lambda b, n: f(b, n)[::-1])
    vtb = lambda f: pl.BlockSpec((vw, WINDOW), flip(f))
    return pl.pallas_call(
        _win_kernel,
        grid=(BATCH, nblk + cblk),
        in_specs=[pl.BlockSpec(memory_space=pltpu.SMEM),
                  pl.BlockSpec((WINDOW, WIN_HEADS * HEAD_DIM), qrow),
                  kvb(prev, kvw), kvb(cur, kvw), kvb(nxt, kvw), cxb(kvw),
                  vtb(prev), vtb(cur), vtb(nxt), pl.BlockSpec((vw, CTX_LEN), flip(ctx))],
        out_specs=pl.BlockSpec((WINDOW, WIN_HEADS * HEAD_DIM), qrow),
        out_shape=jax.ShapeDtypeStruct((R_ALL, WIN_HEADS * HEAD_DIM), BF16),
        compiler_params=_params(),
        name="window_attention",
    )(sink, q, k, k, k, k, v, v, v, v)


def _outproj_kernel(a_ref, b_ref, wa_ref, wb_ref, bo_ref, xl_ref, xc_ref, g1_ref, sh2_ref, sc2_ref,
                    lng_ref, lnb_ref, wrh_ref, wrl_ref, br_ref, upper_ref, lower_ref, sel_ref,
                    xo_ref, route_ref, cnt_ref, xs_ref):
    is_lat = pl.program_id(0) < NT_LAT // OUT_SUB
    u2s = []
    for t in range(OUT_SUB):
        rows = slice(t * TM, (t + 1) * TM)
        x = jnp.where(is_lat, xl_ref[rows, :], xc_ref[rows, :])
        y = (jnp.dot(a_ref[rows, :], wa_ref[...], preferred_element_type=F32)
             + jnp.dot(b_ref[rows, :], wb_ref[...], preferred_element_type=F32) + bo_ref[...])
        xn = _layer_norm(DN_ALPHA * x + (1.0 + g1_ref[...]) * y, lng_ref[...], lnb_ref[...])
        xo_ref[rows, :] = xn
        u2s.append(xn * (1.0 + sc2_ref[...]) + sh2_ref[...])
    for t in range(OUT_SUB):
        rows = slice(t * TM, (t + 1) * TM)
        anchor = 0.0 * xo_ref[(t + 2) * TM - 1:(t + 2) * TM, :] if t + 1 < OUT_SUB else None
        _route_and_group(u2s[t], anchor, wrh_ref, wrl_ref, br_ref, upper_ref, lower_ref, sel_ref,
                         route_ref.at[rows, :], cnt_ref.at[t * 8:(t + 1) * 8, :],
                         xs_ref.at[t * LOCAL_ROWS:(t + 1) * LOCAL_ROWS, :])


def _route_and_group(u2, anchor, wrh_ref, wrl_ref, br_ref, upper_ref, lower_ref, sel_ref,
                     route_ref, cnt_ref, xs_ref):
    u_hi = u2.astype(BF16)
    u_lo = (u2 - u_hi.astype(F32)).astype(BF16)
    logits = (jnp.dot(u_hi, wrh_ref[...], preferred_element_type=F32)
              + jnp.dot(u_lo, wrh_ref[...], preferred_element_type=F32)
              + jnp.dot(u_hi, wrl_ref[...], preferred_element_type=F32) + br_ref[...])
    lane = lax.broadcasted_iota(jnp.int32, logits.shape, 1).astype(F32)
    ninf = -jnp.inf
    big = float(ROUTE_W)
    gl = jnp.where(lane < N_GROUPS, logits, ninf)
    gmax = jnp.max(gl, axis=-1, keepdims=True)
    gidx = jnp.min(jnp.where(gl == gmax, lane, big), axis=-1, keepdims=True)
    g_w = 1.0 / jnp.sum(jnp.exp(gl - gmax), axis=-1, keepdims=True)
    lo = N_GROUPS + EXP_PER_GROUP * gidx
    el = jnp.where(lane >= lo, jnp.where(lane < lo + EXP_PER_GROUP, logits, ninf), ninf)
    v1 = jnp.max(el, axis=-1, keepdims=True)
    i1 = jnp.min(jnp.where(el == v1, lane, big), axis=-1, keepdims=True)
    el2 = jnp.where(lane == i1, ninf, el)
    v2 = jnp.max(el2, axis=-1, keepdims=True)
    i2 = jnp.min(jnp.where(el2 == v2, lane, big), axis=-1, keepdims=True)
    e2 = jnp.exp(v2 - v1)
    w1 = g_w / (1.0 + e2)
    w2 = g_w * e2 / (1.0 + e2)
    onehot = [jnp.where(lane == i1 - N_GROUPS, 1.0, 0.0), jnp.where(lane == i2 - N_GROUPS, 1.0, 0.0)]
    cnt = [jnp.sum(o, axis=0, keepdims=True) for o in onehot]
    run_units = jnp.floor((cnt[0] + cnt[1] + (RUN_ALIGN - 1)) * (1.0 / RUN_ALIGN))
    below = RUN_ALIGN * jnp.dot(jnp.broadcast_to(run_units, (8, ROUTE_W)).astype(BF16), upper_ref[...],
                                preferred_element_type=F32)[0:1]
    lower = lower_ref[...]
    base = [below, below + cnt[0]]
    lp = []
    for s in range(2):
        earlier = jnp.dot(lower, onehot[s].astype(BF16), preferred_element_type=F32)
        lp.append(jnp.sum(onehot[s] * (base[s] + earlier), axis=-1, keepdims=True))
    rec = jnp.where(lane == 0.0, i1 - N_GROUPS,
                    jnp.where(lane == 1.0, i2 - N_GROUPS,
                              jnp.where(lane == 2.0, w1,
                                        jnp.where(lane == 3.0, w2,
                                                  jnp.where(lane == 4.0, lp[0],
                                                            jnp.where(lane == 5.0, lp[1], 0.0))))))
    route_ref[...] = rec
    cnt_ref[...] = jnp.broadcast_to(run_units * RUN_ALIGN, (8, ROUTE_W))
    sel = sel_ref[...]
    pos = lax.broadcasted_iota(jnp.int32, (LOCAL_ROWS, TM), 0).astype(F32)
    lp_lanes = []
    for s in range(2):
        hi = jnp.floor(lp[s] * (1.0 / 256.0))
        parts = jnp.where(lane == 0.0, lp[s] - 256.0 * hi, jnp.where(lane == 1.0, hi, 0.0)).astype(BF16)
        t = _nt_dot(sel, parts)
        lp_lanes.append(t[0:1] + 256.0 * t[1:2])
    perm = jnp.where(pos == lp_lanes[0], 1.0, jnp.where(pos == lp_lanes[1], 1.0, 0.0)).astype(BF16)
    xs = jnp.dot(perm, u_hi, preferred_element_type=F32)
    xs_ref[...] = xs if anchor is None else xs + anchor


def _outproj(n_tiles, mix_a, mix_b, w_a, w_b, b_out, x_lat, x_ctx, mods, ln_g, ln_b, w_r, b_r):
    rows = n_tiles * TM
    steps = n_tiles // OUT_SUB
    lat_steps = NT_LAT // OUT_SUB
    half = mix_a.shape[1]
    w_rh = w_r.astype(BF16)
    w_rl = (w_r - w_rh.astype(F32)).astype(BF16)
    upper = (jnp.arange(ROUTE_W)[:, None] < jnp.arange(ROUTE_W)[None, :]).astype(BF16)
    lower = (jnp.arange(TM)[:, None] > jnp.arange(TM)[None, :]).astype(BF16)
    sel = (jnp.arange(8)[:, None] == jnp.arange(ROUTE_W)[None, :]).astype(BF16)
    row = lambda w: pl.BlockSpec((OUT_SUB * TM, w), lambda i: (i, 0))
    mod = lambda chunk: pl.BlockSpec((None, None, 1, D_MODEL), lambda i: (_mod_row(i * OUT_SUB), chunk, 0, 0))
    vec = _full((1, D_MODEL))
    return pl.pallas_call(
        _outproj_kernel,
        grid=(steps,),
        in_specs=[row(half), row(half), _full((half, D_MODEL)), _full((half, D_MODEL)), vec,
                  pl.BlockSpec((OUT_SUB * TM, D_MODEL), lambda i: (jnp.minimum(i, lat_steps - 1), 0)),
                  pl.BlockSpec((OUT_SUB * TM, D_MODEL), lambda i: (jnp.maximum(i - lat_steps, 0), 0)),
                  mod(G1), mod(SH2), mod(SC2), vec, vec,
                  _full((D_MODEL, ROUTE_W)), _full((D_MODEL, ROUTE_W)), _full((1, ROUTE_W)),
                  _full((ROUTE_W, ROUTE_W)), _full((TM, TM)), _full((8, ROUTE_W))],
        out_specs=[row(D_MODEL), row(ROUTE_W), pl.BlockSpec((OUT_SUB * 8, ROUTE_W), lambda i: (i, 0)),
                   pl.BlockSpec((OUT_SUB * LOCAL_ROWS, D_MODEL), lambda i: (i, 0))],
        out_shape=[jax.ShapeDtypeStruct((rows, D_MODEL), F32),
                   jax.ShapeDtypeStruct((rows, ROUTE_W), F32),
                   jax.ShapeDtypeStruct((n_tiles * 8, ROUTE_W), F32),
                   jax.ShapeDtypeStruct((n_tiles * LOCAL_ROWS, D_MODEL), F32)],
        compiler_params=_params(),
        name="outproj_ln_router",
    )(mix_a, mix_b, w_a, w_b, b_out, x_lat, x_ctx, mods, mods, mods, ln_g, ln_b, w_rh, w_rl, b_r,
      upper, lower, sel)


def _aligned(i):
    return pl.multiple_of(i, RUN_ALIGN)


def _moe_kernel(te_ref, tk_ref, rows_ref, lo_ref, hi_ref, cnt_ref, bt_ref, be_ref, xs_hbm, w1_ref, w3_ref,
                w2_ref, ys_ref, xbuf, wb1, wb3, wb2, sem):
    j = pl.program_id(0)
    nt = pl.num_programs(0)
    slot = j % 2

    def issue(tile, slot_):
        e = te_ref[tile]
        first = tk_ref[tile] * TMM

        def body(i, carry):
            idx = i * N_EXPERTS + e
            start = bt_ref[idx]
            lo = jnp.maximum(start, first)
            n = jnp.minimum(start + cnt_ref[idx], first + TMM) - lo

            @pl.when(n > 0)
            def _():
                src = i * LOCAL_ROWS + be_ref[idx] + lo - start
                pltpu.make_async_copy(xs_hbm.at[pl.ds(_aligned(src), _aligned(n))],
                                      xbuf.at[slot_, pl.ds(_aligned(lo - first), _aligned(n))],
                                      sem.at[slot_]).start()
            return carry
        lax.fori_loop(lo_ref[tile], hi_ref[tile], body, 0)

    @pl.when(j == 0)
    def _():
        xbuf[...] = jnp.zeros_like(xbuf)
        issue(0, 0)

    @pl.when(j + 1 < nt)
    def _():
        issue(j + 1, 1 - slot)

    @pl.when(jnp.logical_or(j == 0, te_ref[j] != te_ref[jnp.maximum(j - 1, 0)]))
    def _():
        wb1[...] = w1_ref[...].astype(BF16)
        wb3[...] = w3_ref[...].astype(BF16)
        wb2[...] = w2_ref[...].astype(BF16)

    n_real = rows_ref[j]

    @pl.when(n_real > 0)
    def _():
        pltpu.make_async_copy(xs_hbm.at[pl.ds(0, _aligned(n_real))], xbuf.at[slot, pl.ds(0, _aligned(n_real))],
                              sem.at[slot]).wait()
        x = xbuf[slot].astype(BF16)
        h1 = jnp.dot(x, wb1[...], preferred_element_type=F32)
        h3 = jnp.dot(x, wb3[...], preferred_element_type=F32)
        hid = h1 * _sigmoid(h1) * h3
        ys_ref[...] = jnp.dot(hid.astype(BF16), wb2[...], preferred_element_type=F32)

    @pl.when(n_real == 0)
    def _():
        ys_ref[...] = jnp.zeros_like(ys_ref)


def _moe_experts(layer, plan, xs_local, w1, w3, w2):
    tile_expert, tile_k, tile_rows, src_lo, src_hi, cnt, before_tile, before_expert, _, _ = plan
    nt = tile_expert.shape[0]
    wmap = lambda j, te, *_: (layer, te[j], 0, 0)
    grid_spec = pltpu.PrefetchScalarGridSpec(
        num_scalar_prefetch=8,
        grid=(nt,),
        in_specs=[pl.BlockSpec(memory_space=pl.ANY),
                  pl.BlockSpec((None, None, D_MODEL, EXPERT_FF), wmap),
                  pl.BlockSpec((None, None, D_MODEL, EXPERT_FF), wmap),
                  pl.BlockSpec((None, None, EXPERT_FF, D_MODEL), wmap)],
        out_specs=pl.BlockSpec((TMM, D_MODEL), lambda j, *_: (j, 0)),
        scratch_shapes=[pltpu.VMEM((2, TMM, D_MODEL), F32),
                        pltpu.VMEM((D_MODEL, EXPERT_FF), BF16),
                        pltpu.VMEM((D_MODEL, EXPERT_FF), BF16),
                        pltpu.VMEM((EXPERT_FF, D_MODEL), BF16),
                        pltpu.SemaphoreType.DMA((2,))])
    return pl.pallas_call(
        _moe_kernel,
        grid_spec=grid_spec,
        out_shape=jax.ShapeDtypeStruct((nt * TMM, D_MODEL), F32),
        compiler_params=_params(),
        name="moe_experts",
    )(tile_expert, tile_k, tile_rows, src_lo, src_hi, cnt, before_tile, before_expert, xs_local, w1, w3, w2)


def _combine_kernel(cnt_ref, bt_ref, be_ref, gs_ref, used_ref, ys_hbm, x_ref, route_ref, g2_ref, lng_ref, lnb_ref,
                    o_ref, ybuf, sem):
    i = pl.program_id(0)
    nt = pl.num_programs(0)
    slot = i % 2

    def issue(tile, slot_):
        def body(e, carry):
            idx = tile * N_EXPERTS + e
            n = cnt_ref[idx]

            @pl.when(n > 0)
            def _():
                pltpu.make_async_copy(ys_hbm.at[pl.ds(_aligned(gs_ref[e] + bt_ref[idx]), _aligned(n))],
                                      ybuf.at[slot_, pl.ds(_aligned(be_ref[idx]), _aligned(n))],
                                      sem.at[slot_]).start()
            return carry
        lax.fori_loop(0, N_EXPERTS, body, 0)

    @pl.when(i == 0)
    def _():
        ybuf[...] = jnp.zeros_like(ybuf)
        issue(0, 0)

    @pl.when(i + 1 < nt)
    def _():
        issue(i + 1, 1 - slot)

    used = _aligned(used_ref[i])
    pltpu.make_async_copy(ys_hbm.at[pl.ds(0, used)], ybuf.at[slot, pl.ds(0, used)], sem.at[slot]).wait()
    route = route_ref[...]
    pos = lax.broadcasted_iota(jnp.int32, (TM, LOCAL_ROWS), 1).astype(F32)
    y = ybuf[slot].astype(BF16)
    picked = [jnp.dot(jnp.where(pos == route[:, 4 + s:5 + s], 1.0, 0.0).astype(BF16), y,
                      preferred_element_type=F32) for s in range(2)]
    f = route[:, 2:3] * picked[0] + route[:, 3:4] * picked[1]
    z = DN_ALPHA * x_ref[...] + (1.0 + g2_ref[...]) * f
    o_ref[...] = _layer_norm(z, lng_ref[...], lnb_ref[...])


def _moe_combine(n_tiles, plan, ys, x_all, route, mods, ln_g, ln_b):
    _, _, _, _, _, cnt, before_tile, before_expert, group_start, used = plan
    rows = n_tiles * TM
    row = lambda w: pl.BlockSpec((TM, w), lambda i, *_: (i, 0))
    vec = pl.BlockSpec((1, D_MODEL), lambda i, *_: (0, 0))
    grid_spec = pltpu.PrefetchScalarGridSpec(
        num_scalar_prefetch=5,
        grid=(n_tiles,),
        in_specs=[pl.BlockSpec(memory_space=pl.ANY), row(D_MODEL), row(ROUTE_W),
                  pl.BlockSpec((None, None, 1, D_MODEL), lambda i, *_: (_mod_row(i), G2, 0, 0)), vec, vec],
        out_specs=row(D_MODEL),
        scratch_shapes=[pltpu.VMEM((2, LOCAL_ROWS, D_MODEL), F32), pltpu.SemaphoreType.DMA((2,))])
    return pl.pallas_call(
        _combine_kernel,
        grid_spec=grid_spec,
        out_shape=jax.ShapeDtypeStruct((rows, D_MODEL), F32),
        compiler_params=_params(),
        name="moe_combine_ln",
    )(cnt, before_tile, before_expert, group_start, used, ys, x_all, route, mods, ln_g, ln_b)


def _moe_plan(cnt_rec, n_tiles):
    cnt = cnt_rec.reshape(n_tiles, 8, ROUTE_W)[:, 0, :N_EXPERTS].astype(jnp.int32)
    nt_max = (n_tiles * (2 * TM + N_EXPERTS * (RUN_ALIGN - 1))) // TMM + N_EXPERTS
    total = jnp.sum(cnt, axis=0)
    tiles_e = (total + TMM - 1) // TMM
    tile_end = jnp.cumsum(tiles_e)
    first_tile = tile_end - tiles_e
    before_tile = jnp.cumsum(cnt, axis=0) - cnt
    before_expert = jnp.cumsum(cnt, axis=1) - cnt
    tile_id = jnp.arange(nt_max, dtype=jnp.int32)
    tile_expert = jnp.minimum(jnp.sum((tile_id[:, None] >= tile_end[None, :]).astype(jnp.int32), axis=1),
                              N_EXPERTS - 1)
    onehot = (tile_expert[:, None] == jnp.arange(N_EXPERTS, dtype=jnp.int32)[None, :]).astype(jnp.int32)
    tile_k = tile_id - jnp.sum(onehot * first_tile[None, :], axis=1)
    tile_rows = jnp.clip(jnp.sum(onehot * total[None, :], axis=1) - tile_k * TMM, 0, TMM)
    first = (tile_k * TMM)[:, None]
    run_start = jnp.sum(onehot[:, None, :] * before_tile[None, :, :], axis=2)
    run_end = run_start + jnp.sum(onehot[:, None, :] * cnt[None, :, :], axis=2)
    src_lo = jnp.sum((run_end <= first).astype(jnp.int32), axis=1)
    src_hi = jnp.sum((run_start < first + TMM).astype(jnp.int32), axis=1)
    return (tile_expert, tile_k, tile_rows, src_lo, src_hi, cnt.reshape(-1), before_tile.reshape(-1),
            before_expert.reshape(-1), first_tile * TMM, jnp.sum(cnt, axis=1))


def _router_weights(w_rg, b_rg, w_re, b_re):
    w = jnp.concatenate([w_rg, jnp.transpose(w_re, (1, 0, 2)).reshape(D_MODEL, N_EXPERTS)], axis=1)
    b = jnp.concatenate([b_rg, b_re.reshape(-1)])
    pad = ROUTE_W - w.shape[1]
    return jnp.pad(w, ((0, 0), (0, pad))), jnp.pad(b, (0, pad)).reshape(1, ROUTE_W)


def _proj1_kernel(x_ref, sh_ref, sc_ref, w_ref, b_ref, cos_ref, sin_ref, cosm_ref, sinm_ref, cosr_ref,
                  sinr_ref, gq_ref, gk_ref, gqc_ref, gkv_ref, avg_ref, wuq_ref, wuk_ref, wuv_ref, vplace_ref,
                  q_ref, qm_ref, k_ref, v_ref, km_ref, vm_ref):
    u = x_ref[...] * (1.0 + sc_ref[...]) + sh_ref[...]
    y = jnp.dot(u.astype(BF16), w_ref[...], preferred_element_type=F32) + b_ref[...]
    c_q = GQA_HEADS * HEAD_DIM
    c_qc = c_q + MLA_Q_RANK
    c_k = c_qc + GQA_KV_HEADS * HEAD_DIM
    c_v = c_k + GQA_KV_HEADS * HEAD_DIM
    c_kv = c_v + MLA_KV_RANK
    avg = avg_ref[...]

    def head_rms(t, gain):
        sq = t * t
        hi = sq.astype(BF16)
        lo = (sq - hi.astype(F32)).astype(BF16)
        a = avg[:t.shape[1], :t.shape[1]]
        ms = jnp.dot(hi, a, preferred_element_type=F32) + jnp.dot(lo, a, preferred_element_type=F32)
        return t * lax.rsqrt(ms + RMS_EPS) * gain

    def row_rms(t, gain):
        ms = jnp.mean(t * t, axis=-1, keepdims=True)
        return t * lax.rsqrt(ms + RMS_EPS) * gain

    cos = cos_ref[...]
    sin = sin_ref[...]
    cos4 = jnp.concatenate([cos] * 4, axis=1)
    sin4 = jnp.concatenate([sin] * 4, axis=1)
    q = _rope(head_rms(y[:, :c_q], gq_ref[...]), cos4, sin4, HEAD_DIM // 4) * (HEAD_DIM ** -0.5 * LOG2E)
    q_ref[...] = q.astype(BF16)
    k = _rope(head_rms(y[:, c_qc:c_k], gk_ref[...]), cos, sin, HEAD_DIM // 4)
    k_ref[...] = k.astype(BF16)

    v_ref[...] = _transposed_values(vplace_ref[...], y[:, c_k:c_v].astype(BF16), HEAD_DIM)

    qc = row_rms(y[:, c_q:c_qc], gqc_ref[...]).astype(BF16)
    qm = jnp.dot(qc, wuq_ref[...], preferred_element_type=F32)
    cosm = jnp.concatenate([cosm_ref[...]] * MLA_HEADS, axis=1)
    sinm = jnp.concatenate([sinm_ref[...]] * MLA_HEADS, axis=1)
    qm = _rope(qm, cosm, sinm, MLA_ROPE // 4) * ((MLA_NOPE + MLA_ROPE) ** -0.5 * LOG2E)
    qm_ref[...] = qm.astype(BF16)

    kvn = row_rms(y[:, c_v:c_kv], gkv_ref[...]).astype(BF16)
    kr = _rope(y[:, c_kv:], cosr_ref[...], sinr_ref[...], MLA_ROPE // 4).astype(BF16)
    km = jnp.dot(jnp.concatenate([kvn, kr], axis=1), wuk_ref[...], preferred_element_type=F32)
    km_ref[...] = km.astype(BF16)
    vm_ref[...] = _transposed_values(wuv_ref[...], kvn, MLA_V)


def _proj1(x_all, mods, w_in, b_in, tabs, gq, gk, gqc, gkv, avg, wuq, wuk, wuv):
    cos_hd, sin_hd, cos_m, sin_m, cos_r, sin_r = tabs
    kvw = GQA_KV_HEADS * HEAD_DIM
    qw = GQA_HEADS * HEAD_DIM
    mw = MLA_HEADS * MLA_PAD
    vw = MLA_HEADS * VAL_PAD
    gvw = GQA_KV_HEADS * VAL_PAD
    row = lambda w: pl.BlockSpec((TM, w), lambda i: (i, 0))
    col = lambda h: pl.BlockSpec((h, TM), lambda i: (0, i))
    tab = pl.BlockSpec((TM, 128), lambda i: (_rope_row_block(i), 0))
    vplace = _value_placement(GQA_KV_HEADS, HEAD_DIM)
    return pl.pallas_call(
        _proj1_kernel,
        grid=(NT_ALL,),
        in_specs=[row(D_MODEL), _mod_spec(SH1), _mod_spec(SC1),
                  _full((D_MODEL, ODD_IN_PAD)), _full((1, ODD_IN_PAD)), tab, tab, tab, tab, tab, tab,
                  _full((1, qw)), _full((1, kvw)), _full((1, MLA_Q_RANK)), _full((1, MLA_KV_RANK)),
                  _full((qw, qw)), _full((MLA_Q_RANK, mw)), _full((MLA_KV_RANK + 128, mw)),
                  _full((vw, MLA_KV_RANK)), _full((gvw, kvw))],
        out_specs=[row(qw), row(mw), row(kvw), col(gvw), row(mw), col(vw)],
        out_shape=[jax.ShapeDtypeStruct((R_ALL, qw), BF16),
                   jax.ShapeDtypeStruct((R_ALL, mw), BF16),
                   jax.ShapeDtypeStruct((R_ALL, kvw), BF16),
                   jax.ShapeDtypeStruct((gvw, R_ALL), BF16),
                   jax.ShapeDtypeStruct((R_ALL, mw), BF16),
                   jax.ShapeDtypeStruct((vw, R_ALL), BF16)],
        compiler_params=_params(),
        name="proj1",
    )(x_all, mods, mods, w_in, b_in, cos_hd, sin_hd, cos_m, sin_m, cos_r, sin_r,
      gq, gk, gqc, gkv, avg, wuq, wuk, wuv.T, vplace)


def _dense_kernel(q_ref, kl_ref, kc_ref, vl_ref, vc_ref, o_ref, s_buf, p_buf, *, n_heads, group, stack, dk, dv):
    tq = q_ref.shape[0]
    units = []
    for h0 in range(0, n_heads, stack):
        kv = h0 // group
        qs = [q_ref[:, h * dk:(h + 1) * dk] for h in range(h0, h0 + stack)]
        q = qs[0] if stack == 1 else jnp.concatenate(qs, axis=0)
        ks = slice(kv * dk, (kv + 1) * dk)
        vs = slice(kv * VAL_PAD, (kv + 1) * VAL_PAD)
        parts = [slice(c * (SEQ // KEY_PARTS), (c + 1) * (SEQ // KEY_PARTS)) for c in range(KEY_PARTS)]
        units.append((q, [kl_ref[c, ks] for c in parts] + [kc_ref[:, ks]],
                      [vl_ref[vs, c] for c in parts] + [vc_ref[vs, :]]))
    for u, o_t in enumerate(_attend_keys_major(units, dv, s_buf, p_buf)):
        o = o_t.T
        for g in range(stack):
            h = u * stack + g
            o_ref[:, h * dv:(h + 1) * dv] = o[g * tq:(g + 1) * tq].astype(BF16)


def _dense_attention(q, k, v, *, n_heads, group, stack, dk, dv, tq, name):
    n_kv = n_heads // group
    nq = SEQ // tq
    ctx0 = R_LAT // CTX_LEN
    lat = lambda w: pl.BlockSpec((SEQ, w), lambda b, j: (b, 0), pipeline_mode=pl.Buffered(1))
    ctx = lambda w: pl.BlockSpec((CTX_LEN, w), lambda b, j: (ctx0 + b, 0))
    lat_t = pl.BlockSpec((n_kv * VAL_PAD, SEQ), lambda b, j: (0, b), pipeline_mode=pl.Buffered(1))
    ctx_t = pl.BlockSpec((n_kv * VAL_PAD, CTX_LEN), lambda b, j: (0, ctx0 + b))
    return pl.pallas_call(
        functools.partial(_dense_kernel, n_heads=n_heads, group=group, stack=stack, dk=dk, dv=dv),
        grid=(BATCH, nq),
        in_specs=[pl.BlockSpec((tq, n_heads * dk), lambda b, j: (b * nq + j, 0)),
                  lat(n_kv * dk), ctx(n_kv * dk), lat_t, ctx_t],
        out_specs=pl.BlockSpec((tq, n_heads * dv), lambda b, j: (b * nq + j, 0)),
        out_shape=jax.ShapeDtypeStruct((R_LAT, n_heads * dv), BF16),
        scratch_shapes=[pltpu.VMEM((SCORE_SLOTS, SEQ + CTX_LEN, stack * tq), F32),
                        pltpu.VMEM((2, SEQ + CTX_LEN, stack * tq), BF16)],
        compiler_params=_params(),
        name=name,
    )(q, k, k, v, v)


def _mla_weights(w_uq, w_ukv):
    wq = w_uq.reshape(MLA_Q_RANK, MLA_HEADS, MLA_NOPE + MLA_ROPE)
    wq = jnp.pad(wq, ((0, 0), (0, 0), (0, MLA_PAD - MLA_NOPE - MLA_ROPE))).reshape(MLA_Q_RANK, -1)
    wkv = w_ukv.reshape(MLA_KV_RANK, MLA_HEADS, MLA_NOPE + MLA_V)
    wk = jnp.pad(wkv[:, :, :MLA_NOPE], ((0, 0), (0, 0), (0, MLA_PAD - MLA_NOPE))).reshape(MLA_KV_RANK, -1)
    wv = jnp.pad(wkv[:, :, MLA_NOPE:], ((0, 0), (0, 0), (0, VAL_PAD - MLA_V))).reshape(MLA_KV_RANK, -1)
    r = jnp.arange(128)[:, None]
    c = jnp.arange(MLA_HEADS * MLA_PAD)[None, :]
    place = jnp.logical_and(r < MLA_ROPE, (c % MLA_PAD) == MLA_NOPE + r).astype(F32)
    wk = jnp.concatenate([wk, place], axis=0)
    return wq.astype(BF16), wk.astype(BF16), wv.astype(BF16)


def kernel(x, c, ctx, c_ctx, even_w_in, even_b_in, even_conv_w, even_conv_b, even_conv_ln_g, even_conv_ln_b, even_sink, even_w_out, even_b_out, odd_w_in, odd_b_in, odd_q_norm, odd_k_norm, odd_mla_q_norm, odd_mla_kv_norm, odd_mla_w_uq, odd_mla_w_ukv, odd_w_out, odd_b_out, ada_w, ada_b, ln1_g, ln1_b, ln2_g, ln2_b, moe_w_rg, moe_b_rg, moe_w_re, moe_b_re, moe_w1, moe_w3, moe_w2):
    vec = lambda a: a.reshape(1, -1)
    x_lat0 = x.reshape(R_LAT, D_MODEL)
    x_ctx0 = ctx.reshape(R_CTX, D_MODEL)

    cv =jnp.concatenate([c, c_ctx[None, :], jnp.zeros((8 - BATCH - 1, D_MODEL), F32)], axis=0)
    mods = _ada_table(cv, ada_w, ada_b).reshape(DEPTH, 8, 6, 1, D_MODEL)

    cos64, sin64 = _rope_tables(HEAD_DIM)
    cos_hd, sin_hd = _pad_table(cos64, sin64, 0, HEAD_DIM, 128)
    cos32, sin32 = _rope_tables(MLA_ROPE)
    cos_m, sin_m = _pad_table(cos32, sin32, MLA_NOPE, MLA_PAD, 128)
    cos_r, sin_r = _pad_table(cos32, sin32, 0, 128, 128)

    m0 = mods[0]
    h, q0, k0, v0 = _proj0(x_lat0, x_ctx0, m0, even_w_in[0].astype(BF16), vec(even_b_in[0]), cos_hd, sin_hd)
    conv_out = _conv(h, even_conv_w[0].reshape(CONV_WIDTH, CONV_CH), vec(even_conv_b[0]),
                     vec(even_conv_ln_g[0]), vec(even_conv_ln_b[0]))
    attn = _win_attention(even_sink[0], q0, k0, v0)
    w_out = even_w_out[0].astype(BF16)
    w_r, b_r = _router_weights(moe_w_rg[0], moe_b_rg[0], moe_w_re[0], moe_b_re[0])
    x_all, route, cnt_rec, xs_local = _outproj(
        NT_ALL, conv_out, attn, w_out[:CONV_CH], w_out[CONV_CH:], vec(even_b_out[0]),
        x_lat0, x_ctx0, m0, vec(ln1_g[0]), vec(ln1_b[0]), w_r, b_r)
    plan = _moe_plan(cnt_rec, NT_ALL)
    ys = _moe_experts(0, plan, xs_local, moe_w1, moe_w3, moe_w2)
    x_all = _moe_combine(NT_ALL, plan, ys, x_all, route, m0, vec(ln2_g[0]), vec(ln2_b[0]))

    m1 = mods[1]
    w_in1 = jnp.pad(odd_w_in[0], ((0, 0), (0, ODD_IN_PAD - ODD_IN))).astype(BF16)
    b_in1 = jnp.pad(odd_b_in[0], (0, ODD_IN_PAD - ODD_IN)).reshape(1, -1)
    wuq, wuk, wuv = _mla_weights(odd_mla_w_uq[0], odd_mla_w_ukv[0])
    qw = GQA_HEADS * HEAD_DIM
    hid = jnp.arange(qw) // HEAD_DIM
    avg = ((hid[:, None] == hid[None, :]).astype(F32) / HEAD_DIM).astype(BF16)
    q1, qm, k1, v1, km, vm = _proj1(
        x_all, m1, w_in1, b_in1, (cos_hd, sin_hd, cos_m, sin_m, cos_r, sin_r),
        vec(jnp.tile(odd_q_norm[0], GQA_HEADS)), vec(jnp.tile(odd_k_norm[0], GQA_KV_HEADS)),
        vec(odd_mla_q_norm[0]), vec(odd_mla_kv_norm[0]), avg, wuq, wuk, wuv)
    o_g = _dense_attention(q1, k1, v1, n_heads=GQA_HEADS, group=GQA_HEADS // GQA_KV_HEADS, stack=2,
                           dk=HEAD_DIM, dv=HEAD_DIM, tq=256, name="gqa_attention")
    o_m = _dense_attention(qm, km, vm, n_heads=MLA_HEADS, group=1, stack=1, dk=MLA_PAD, dv=MLA_V, tq=256,
                           name="mla_attention")
    w_out = odd_w_out[0].astype(BF16)
    w_r, b_r = _router_weights(moe_w_rg[1], moe_b_rg[1], moe_w_re[1], moe_b_re[1])
    x_lat, route, cnt_rec, xs_local = _outproj(
        NT_LAT, o_g, o_m, w_out[:qw], w_out[qw:], vec(odd_b_out[0]),
        x_all, x_all, m1, vec(ln1_g[1]), vec(ln1_b[1]), w_r, b_r)
    plan = _moe_plan(cnt_rec, NT_LAT)
    ys = _moe_experts(1, plan, xs_local, moe_w1, moe_w3, moe_w2)
    x_lat = _moe_combine(NT_LAT, plan, ys, x_lat, route, m1, vec(ln2_g[1]), vec(ln2_b[1]))
    return x_lat.reshape(BATCH, SEQ, D_MODEL)
```

```python
import functools

import jax
import jax.numpy as jnp
from jax import lax
from jax.experimental import pallas as pl
from jax.experimental.pallas import tpu as pltpu

F32 = jnp.float32
BF16 = jnp.bfloat16

D_MODEL = 1024
BATCH = 4
SEQ = 4096
DEPTH = 2
GRID_W = 64
CTX_LEN = 256
HEAD_DIM = 64
ROPE_THETA = 10000.0
LN_EPS = 1e-5
RMS_EPS = 1e-6
NEG_INF = -1e30

CONV_CH = 512
CONV_WIDTH = 31
WIN_HEADS = 8
WIN_KV_HEADS = 2
WINDOW = 128
GQA_HEADS = 8
GQA_KV_HEADS = 2
MLA_HEADS = 8
MLA_Q_RANK = 256
MLA_KV_RANK = 128
MLA_NOPE = 64
MLA_ROPE = 32
MLA_V = 64
N_GROUPS = 4
EXP_PER_GROUP = 8
N_EXPERTS = N_GROUPS * EXP_PER_GROUP
EXPERT_FF = 512
DN_ALPHA = float((2 * DEPTH) ** 0.25)

EVEN_IN = 2 * CONV_CH + (WIN_HEADS + 2 * WIN_KV_HEADS) * HEAD_DIM
ODD_IN = 1184
ODD_IN_PAD = 1280
MLA_PAD = 128
VAL_PAD = 128
WIN_STACK = 1
SCORE_SLOTS = 2
KEY_PARTS = 4
LOG2E = 1.4426950408889634

R_LAT = BATCH * SEQ
R_CTX = BATCH * CTX_LEN
R_ALL = R_LAT + R_CTX
TM = 256
NT_LAT = R_LAT // TM
NT_ALL = R_ALL // TM
TILES_PER_SEQ = SEQ // TM
HALO = 16
CONV_CHUNK = 32
SHIFTS = 8
OUT_SUB = 2
TMM = 256
ROUTE_W = 128
RUN_ALIGN = 8
LOCAL_ROWS = 768
VMEM_LIMIT = 56 * 1024 * 1024

SH1, SC1, G1, SH2, SC2, G2 = range(6)


def _sigmoid(x):
    return 1.0 / (1.0 + jnp.exp(-x))


def _layer_norm(z, g, b):
    mu = jnp.mean(z, axis=-1, keepdims=True)
    zc = z - mu
    var = jnp.mean(zc * zc, axis=-1, keepdims=True)
    return zc * lax.rsqrt(var + LN_EPS) * g + b


def _rope(x, cos, sin, half):
    n = x.shape[-1]
    lane = lax.broadcasted_iota(jnp.int32, x.shape, 1)
    first = (lane % (2 * half)) < half
    partner = jnp.where(first, pltpu.roll(x, n - half, 1), pltpu.roll(x, half, 1))
    return x * cos + partner * sin


def _mod_row(i):
    return jnp.where(i < NT_LAT, i // TILES_PER_SEQ, BATCH)


def _mod_spec(chunk):
    return pl.BlockSpec((None, None, 1, D_MODEL), lambda i: (_mod_row(i), chunk, 0, 0))


def _rope_row_block(i):
    return jnp.where(i < NT_LAT, i % TILES_PER_SEQ, TILES_PER_SEQ)


def _full(shape):
    nd = len(shape)
    return pl.BlockSpec(shape, lambda *_: (0,) * nd)


def _params():
    return pltpu.CompilerParams(vmem_limit_bytes=VMEM_LIMIT)


def _ada_kernel(cv_ref, w_ref, b_ref, o_ref):
    cv = cv_ref[...]
    s = cv * _sigmoid(cv)
    o_ref[...] = jnp.dot(s, w_ref[...], precision=lax.Precision.HIGHEST,
                         preferred_element_type=F32) + b_ref[...]


def _ada_table(cv, ada_w, ada_b):
    bn = 1536
    nb = (6 * D_MODEL) // bn
    return pl.pallas_call(
        _ada_kernel,
        grid=(DEPTH, nb),
        in_specs=[pl.BlockSpec((8, D_MODEL), lambda l, j: (0, 0)),
                  pl.BlockSpec((None, D_MODEL, bn), lambda l, j: (l, 0, j)),
                  pl.BlockSpec((None, 1, bn), lambda l, j: (l, 0, j))],
        out_specs=pl.BlockSpec((None, 8, bn), lambda l, j: (l, 0, j)),
        out_shape=jax.ShapeDtypeStruct((DEPTH, 8, 6 * D_MODEL), F32),
        compiler_params=_params(),
        name="ada_table",
    )(cv, ada_w, ada_b.reshape(DEPTH, 1, 6 * D_MODEL))


def _rope_tables(rot_dim):
    axis_dim = rot_dim // 2
    inv_freq = ROPE_THETA ** (-jnp.arange(0, axis_dim, 2, dtype=F32) / axis_dim)
    t = jnp.arange(SEQ)
    ang_r = (t // GRID_W).astype(F32)[:, None] * inv_freq[None, :]
    ang_c = (t % GRID_W).astype(F32)[:, None] * inv_freq[None, :]
    cos = jnp.concatenate([jnp.cos(ang_r), jnp.cos(ang_r), jnp.cos(ang_c), jnp.cos(ang_c)], axis=-1)
    sin = jnp.concatenate([-jnp.sin(ang_r), jnp.sin(ang_r), -jnp.sin(ang_c), jnp.sin(ang_c)], axis=-1)
    return cos, sin


def _pad_table(cos, sin, lead, period, width):
    rot = cos.shape[1]
    one = jnp.ones((SEQ, period), F32).at[:, lead:lead + rot].set(cos)
    zero = jnp.zeros((SEQ, period), F32).at[:, lead:lead + rot].set(sin)
    cos_w = jnp.tile(one, (1, width // period))
    sin_w = jnp.tile(zero, (1, width // period))
    cos_w = jnp.concatenate([cos_w, jnp.ones((TM, width), F32)], axis=0)
    sin_w = jnp.concatenate([sin_w, jnp.zeros((TM, width), F32)], axis=0)
    return cos_w, sin_w


def _proj0_kernel(xl_ref, xc_ref, sh_ref, sc_ref, w_ref, b_ref, cos_ref, sin_ref, vplace_ref,
                  h_ref, q_ref, k_ref, v_ref):
    x = jnp.where(pl.program_id(0) < NT_LAT, xl_ref[...], xc_ref[...])
    u = x * (1.0 + sc_ref[...]) + sh_ref[...]
    y = jnp.dot(u.astype(BF16), w_ref[...], preferred_element_type=F32) + b_ref[...]
    h_ref[...] = y[:, :CONV_CH] * _sigmoid(y[:, CONV_CH:2 * CONV_CH])
    cos = cos_ref[...]
    sin = sin_ref[...]
    q0 = 2 * CONV_CH
    k0 = q0 + WIN_HEADS * HEAD_DIM
    v0 = k0 + WIN_KV_HEADS * HEAD_DIM
    cos4 = jnp.concatenate([cos] * 4, axis=1)
    sin4 = jnp.concatenate([sin] * 4, axis=1)
    q = _rope(y[:, q0:k0], cos4, sin4, HEAD_DIM // 4) * (HEAD_DIM ** -0.5 * LOG2E)
    q_ref[...] = q.astype(BF16)
    k_ref[...] = _rope(y[:, k0:v0], cos, sin, HEAD_DIM // 4).astype(BF16)
    v_ref[...] = _transposed_values(vplace_ref[...], y[:, v0:].astype(BF16), HEAD_DIM)


def _proj0(x_lat, x_ctx, mods, w_in, b_in, cos_hd, sin_hd):
    kvw = WIN_KV_HEADS * HEAD_DIM
    row = lambda w: pl.BlockSpec((TM, w), lambda i: (i, 0))
    tab = pl.BlockSpec((TM, 128), lambda i: (_rope_row_block(i), 0))
    return pl.pallas_call(
        _proj0_kernel,
        grid=(NT_ALL,),
        in_specs=[pl.BlockSpec((TM, D_MODEL), lambda i: (jnp.minimum(i, NT_LAT - 1), 0)),
                  pl.BlockSpec((TM, D_MODEL), lambda i: (jnp.maximum(i - NT_LAT, 0), 0)),
                  _mod_spec(SH1), _mod_spec(SC1),
                  _full((D_MODEL, EVEN_IN)), _full((1, EVEN_IN)), tab, tab,
                  _full((WIN_KV_HEADS * VAL_PAD, kvw))],
        out_specs=[row(CONV_CH), row(WIN_HEADS * HEAD_DIM), row(kvw),
                   pl.BlockSpec((WIN_KV_HEADS * VAL_PAD, TM), lambda i: (0, i))],
        out_shape=[jax.ShapeDtypeStruct((R_ALL, CONV_CH), F32),
                   jax.ShapeDtypeStruct((R_ALL, WIN_HEADS * HEAD_DIM), BF16),
                   jax.ShapeDtypeStruct((R_ALL, kvw), BF16),
                   jax.ShapeDtypeStruct((WIN_KV_HEADS * VAL_PAD, R_ALL), BF16)],
        compiler_params=_params(),
        name="proj0",
    )(x_lat, x_ctx, mods, mods, w_in, b_in, cos_hd, sin_hd, _value_placement(WIN_KV_HEADS, HEAD_DIM))


def _conv_kernel(prev_ref, cur_ref, next_ref, w_ref, cb_ref, g_ref, b_ref, o_ref, buf):
    i = pl.program_id(0)
    is_ctx = i >= NT_LAT
    first = jnp.logical_or(is_ctx, i % TILES_PER_SEQ == 0)
    last = jnp.logical_or(is_ctx, i % TILES_PER_SEQ == TILES_PER_SEQ - 1)
    buf[0, 0:HALO, :] = jnp.where(first, 0.0, prev_ref[...])
    buf[0, HALO:HALO + TM, :] = cur_ref[...]
    buf[0, HALO + TM:, :] = jnp.where(last, 0.0, next_ref[...])
    span = TM + 2 * HALO - SHIFTS
    for r in range(1, SHIFTS):
        buf[r, 0:span, :] = buf[0, r:r + span, :]
    off = HALO - CONV_WIDTH // 2
    for c in range(TM // CONV_CHUNK):
        r0 = c * CONV_CHUNK
        acc = jnp.zeros((CONV_CHUNK, CONV_CH), F32)
        for k in range(CONV_WIDTH):
            r = (off + k) % SHIFTS
            base = r0 + off + k - r
            w = w_ref[k * SHIFTS:(k + 1) * SHIFTS, :]
            acc = acc + buf[r, base:base + CONV_CHUNK, :] * jnp.concatenate([w] * (CONV_CHUNK // SHIFTS), axis=0)
        z = _layer_norm(acc + cb_ref[...], g_ref[...], b_ref[...])
        o_ref[r0:r0 + CONV_CHUNK, :] = (z * _sigmoid(z)).astype(BF16)


def _conv(h, conv_w, conv_b, ln_g, ln_b):
    nh = R_ALL // HALO
    per = TM // HALO
    vec = _full((1, CONV_CH))
    return pl.pallas_call(
        _conv_kernel,
        grid=(NT_ALL,),
        in_specs=[pl.BlockSpec((HALO, CONV_CH), lambda i: (jnp.maximum(i * per - 1, 0), 0)),
                  pl.BlockSpec((TM, CONV_CH), lambda i: (i, 0)),
                  pl.BlockSpec((HALO, CONV_CH), lambda i: (jnp.minimum((i + 1) * per, nh - 1), 0)),
                  _full((CONV_WIDTH * SHIFTS, CONV_CH)), vec, vec, vec],
        out_specs=pl.BlockSpec((TM, CONV_CH), lambda i: (i, 0)),
        out_shape=jax.ShapeDtypeStruct((R_ALL, CONV_CH), BF16),
        scratch_shapes=[pltpu.VMEM((SHIFTS, TM + 2 * HALO, CONV_CH), F32)],
        compiler_params=_params(),
        name="conv_module",
    )(h, h, h, jnp.repeat(conv_w, SHIFTS, axis=0), conv_b, ln_g, ln_b)


def _nt_dot(a, b):
    return lax.dot_general(a, b, (((1,), (1,)), ((), ())), preferred_element_type=F32)


def _transposed_values(w_t, src, dv):
    vt = _nt_dot(w_t, src)
    r = lax.broadcasted_iota(jnp.int32, vt.shape, 0)
    return jnp.where(r % VAL_PAD == dv, 1.0, vt).astype(BF16)


def _value_placement(n_kv, dv):
    r = jnp.arange(n_kv * VAL_PAD)[:, None]
    c = jnp.arange(n_kv * dv)[None, :]
    return jnp.logical_and(r // VAL_PAD == c // dv, r % VAL_PAD == c % dv).astype(BF16)


def _attend_keys_major(units, dv, s_buf, p_buf):
    def scores(unit, slot):
        q, ks, _ = unit
        row, ms = 0, []
        for k in ks:
            s = _nt_dot(k, q)
            s_buf[slot, row:row + k.shape[0], :] = s
            ms.append(jnp.max(s, axis=0, keepdims=True))
            row += k.shape[0]
        return functools.reduce(jnp.maximum, ms)

    def run_next_scores_with(slot, nxt_slot, pieces):
        row = 0
        for k in pieces:
            tile = (slice(row + k.shape[0] - 8, row + k.shape[0]), slice(0, 128))
            s_buf[slot, tile[0], tile[1]] = s_buf[slot, tile[0], tile[1]] + 0.0 * s_buf[nxt_slot, tile[0], tile[1]]
            row += k.shape[0]

    n_s, n_p = s_buf.shape[0], p_buf.shape[0]
    results = []
    m = scores(units[0], 0)
    for idx, unit in enumerate(units):
        slot, nxt_slot, pslot = idx % n_s, (idx + 1) % n_s, idx % n_p
        m_next = None
        if idx + 1 < len(units):
            m_next = scores(units[idx + 1], nxt_slot)
            run_next_scores_with(slot, nxt_slot, unit[1])
        row, acc = 0, None
        for vt in unit[2]:
            rows = slice(row, row + vt.shape[1])
            p_buf[pslot, rows, :] = jnp.exp2(s_buf[slot, rows, :] - m).astype(BF16)
            part = jnp.dot(vt, p_buf[pslot, rows, :], preferred_element_type=F32)
            acc = part if acc is None else acc + part
            row += vt.shape[1]
        results.append(acc[:dv] / acc[dv:dv + 1])
        m = m_next
    return results


def _win_kernel(sink_ref, q_ref, kp_ref, kc_ref, kn_ref, kx_ref, vp_ref, vc_ref, vn_ref, vx_ref, o_ref):
    n = pl.program_id(1)
    group = WIN_HEADS // WIN_KV_HEADS
    k_loc = jnp.concatenate([kp_ref[...], kc_ref[...], kn_ref[...]], axis=0)
    vt_loc = jnp.concatenate([vp_ref[...], vc_ref[...], vn_ref[...]], axis=1)
    k_ctx = kx_ref[...]
    vt_ctx = vx_ref[...]
    kj = lax.broadcasted_iota(jnp.int32, (3 * WINDOW, WIN_STACK * WINDOW), 0)
    qi = lax.broadcasted_iota(jnp.int32, (3 * WINDOW, WIN_STACK * WINDOW), 1) % WINDOW
    k_pos = jnp.where(n < SEQ // WINDOW, kj + (n - 1) * WINDOW, SEQ)
    valid = jnp.where(kj >= qi, jnp.where(kj <= qi + 2 * WINDOW, 1, 0), 0)
    valid = jnp.where(k_pos >= 0, jnp.where(k_pos < SEQ, valid, 0), 0) > 0
    staged = []
    for h0 in range(0, WIN_HEADS, WIN_STACK):
        heads = range(h0, h0 + WIN_STACK)
        kv = h0 // group
        q = jnp.concatenate([q_ref[:, h * HEAD_DIM:(h + 1) * HEAD_DIM] for h in heads], axis=0)
        sink = jnp.concatenate([jnp.full((1, WINDOW), sink_ref[h] * LOG2E, F32) for h in heads], axis=1)
        ksl = slice(kv * HEAD_DIM, (kv + 1) * HEAD_DIM)
        s_ctx = _nt_dot(k_ctx[:, ksl], q)
        s_loc = jnp.where(valid, _nt_dot(k_loc[:, ksl], q), NEG_INF)
        m = jnp.maximum(jnp.maximum(jnp.max(s_ctx, axis=0, keepdims=True),
                                    jnp.max(s_loc, axis=0, keepdims=True)), sink)
        staged.append((heads, kv, sink, s_ctx, s_loc, m))
    for u, (heads, kv, sink, s_ctx, s_loc, m) in enumerate(staged):
        vsl = slice(kv * VAL_PAD, (kv + 1) * VAL_PAD)
        if u + 1 < len(staged):
            m = m + 0.0 * staged[u + 1][5]
        acc = (jnp.dot(vt_ctx[vsl, :], jnp.exp2(s_ctx - m).astype(BF16), preferred_element_type=F32)
               + jnp.dot(vt_loc[vsl, :], jnp.exp2(s_loc - m).astype(BF16), preferred_element_type=F32))
        l = acc[HEAD_DIM:HEAD_DIM + 1] + jnp.exp2(sink - m)
        o = (acc[:HEAD_DIM] / l).T
        for g, h in enumerate(heads):
            o_ref[:, h * HEAD_DIM:(h + 1) * HEAD_DIM] = o[g * WINDOW:(g + 1) * WINDOW].astype(BF16)


def _win_attention(sink, q, k, v):
    nblk = SEQ // WINDOW
    cblk = CTX_LEN // WINDOW
    kvw = WIN_KV_HEADS * HEAD_DIM
    ctx0 = R_LAT // CTX_LEN
    lat = lambda n: jnp.minimum(n, nblk - 1)
    prev = lambda b, n: (b * nblk + jnp.maximum(lat(n) - 1, 0), 0)
    cur = lambda b, n: (b * nblk + lat(n), 0)
    nxt = lambda b, n: (b * nblk + jnp.minimum(lat(n) + 1, nblk - 1), 0)
    qrow = lambda b, n: (jnp.where(n < nblk, b * nblk + n, R_LAT // WINDOW + b * cblk + n - nblk), 0)
    ctx = lambda b, n: (ctx0 + b, 0)
    vw = WIN_KV_HEADS * VAL_PAD
    kvb = lambda f, w: pl.BlockSpec((WINDOW, w), f)
    cxb = lambda w: pl.BlockSpec((CTX_LEN, w), ctx)
    flip = lambda f: (lambda b, n: f(b, n)[::-1])
    vtb = lambda f: pl.BlockSpec((vw, WINDOW), flip(f))
    return pl.pallas_call(
        _win_kernel,
        grid=(BATCH, nblk + cblk),
        in_specs=[pl.BlockSpec(memory_space=pltpu.SMEM),
                  pl.BlockSpec((WINDOW, WIN_HEADS * HEAD_DIM), qrow),
                  kvb(prev, kvw), kvb(cur, kvw), kvb(nxt, kvw), cxb(kvw),
                  vtb(prev), vtb(cur), vtb(nxt), pl.BlockSpec((vw, CTX_LEN), flip(ctx))],
        out_specs=pl.BlockSpec((WINDOW, WIN_HEADS * HEAD_DIM), qrow),
        out_shape=jax.ShapeDtypeStruct((R_ALL, WIN_HEADS * HEAD_DIM), BF16),
        compiler_params=_params(),
        name="window_attention",
    )(sink, q, k, k, k, k, v, v, v, v)


def _outproj_kernel(a_ref, b_ref, wa_ref, wb_ref, bo_ref, xl_ref, xc_ref, g1_ref, sh2_ref, sc2_ref,
                    lng_ref, lnb_ref, wrh_ref, wrl_ref, br_ref, upper_ref, lower_ref, sel_ref,
                    xo_ref, route_ref, cnt_ref, xs_ref):
    is_lat = pl.program_id(0) < NT_LAT // OUT_SUB
    u2s = []
    for t in range(OUT_SUB):
        rows = slice(t * TM, (t + 1) * TM)
        x = jnp.where(is_lat, xl_ref[rows, :], xc_ref[rows, :])
        y = (jnp.dot(a_ref[rows, :], wa_ref[...], preferred_element_type=F32)
             + jnp.dot(b_ref[rows, :], wb_ref[...], preferred_element_type=F32) + bo_ref[...])
        xn = _layer_norm(DN_ALPHA * x + (1.0 + g1_ref[...]) * y, lng_ref[...], lnb_ref[...])
        xo_ref[rows, :] = xn
        u2s.append(xn * (1.0 + sc2_ref[...]) + sh2_ref[...])
    for t in range(OUT_SUB):
        rows = slice(t * TM, (t + 1) * TM)
        anchor = 0.0 * xo_ref[(t + 2) * TM - 1:(t + 2) * TM, :] if t + 1 < OUT_SUB else None
        _route_and_group(u2s[t], anchor, wrh_ref, wrl_ref, br_ref, upper_ref, lower_ref, sel_ref,
                         route_ref.at[rows, :], cnt_ref.at[t * 8:(t + 1) * 8, :],
                         xs_ref.at[t * LOCAL_ROWS:(t + 1) * LOCAL_ROWS, :])


def _route_and_group(u2, anchor, wrh_ref, wrl_ref, br_ref, upper_ref, lower_ref, sel_ref,
                     route_ref, cnt_ref, xs_ref):
    u_hi = u2.astype(BF16)
    u_lo = (u2 - u_hi.astype(F32)).astype(BF16)
    logits = (jnp.dot(u_hi, wrh_ref[...], preferred_element_type=F32)
              + jnp.dot(u_lo, wrh_ref[...], preferred_element_type=F32)
              + jnp.dot(u_hi, wrl_ref[...], preferred_element_type=F32) + br_ref[...])
    lane = lax.broadcasted_iota(jnp.int32, logits.shape, 1).astype(F32)
    ninf = -jnp.inf
    big = float(ROUTE_W)
    gl = jnp.where(lane < N_GROUPS, logits, ninf)
    gmax = jnp.max(gl, axis=-1, keepdims=True)
    gidx = jnp.min(jnp.where(gl == gmax, lane, big), axis=-1, keepdims=True)
    g_w = 1.0 / jnp.sum(jnp.exp(gl - gmax), axis=-1, keepdims=True)
    lo = N_GROUPS + EXP_PER_GROUP * gidx
    el = jnp.where(lane >= lo, jnp.where(lane < lo + EXP_PER_GROUP, logits, ninf), ninf)
    v1 = jnp.max(el, axis=-1, keepdims=True)
    i1 = jnp.min(jnp.where(el == v1, lane, big), axis=-1, keepdims=True)
    el2 = jnp.where(lane == i1, ninf, el)
    v2 = jnp.max(el2, axis=-1, keepdims=True)
    i2 = jnp.min(jnp.where(el2 == v2, lane, big), axis=-1, keepdims=True)
    e2 = jnp.exp(v2 - v1)
    w1 = g_w / (1.0 + e2)
    w2 = g_w * e2 / (1.0 + e2)
    onehot = [jnp.where(lane == i1 - N_GROUPS, 1.0, 0.0), jnp.where(lane == i2 - N_GROUPS, 1.0, 0.0)]
    cnt = [jnp.sum(o, axis=0, keepdims=True) for o in onehot]
    run_units = jnp.floor((cnt[0] + cnt[1] + (RUN_ALIGN - 1)) * (1.0 / RUN_ALIGN))
    below = RUN_ALIGN * jnp.dot(jnp.broadcast_to(run_units, (8, ROUTE_W)).astype(BF16), upper_ref[...],
                                preferred_element_type=F32)[0:1]
    lower = lower_ref[...]
    base = [below, below + cnt[0]]
    lp = []
    for s in range(2):
        earlier = jnp.dot(lower, onehot[s].astype(BF16), preferred_element_type=F32)
        lp.append(jnp.sum(onehot[s] * (base[s] + earlier), axis=-1, keepdims=True))
    rec = jnp.where(lane == 0.0, i1 - N_GROUPS,
                    jnp.where(lane == 1.0, i2 - N_GROUPS,
                              jnp.where(lane == 2.0, w1,
                                        jnp.where(lane == 3.0, w2,
                                                  jnp.where(lane == 4.0, lp[0],
                                                            jnp.where(lane == 5.0, lp[1], 0.0))))))
    route_ref[...] = rec
    cnt_ref[...] = jnp.broadcast_to(run_units * RUN_ALIGN, (8, ROUTE_W))
    sel = sel_ref[...]
    pos = lax.broadcasted_iota(jnp.int32, (LOCAL_ROWS, TM), 0).astype(F32)
    lp_lanes = []
    for s in range(2):
        hi = jnp.floor(lp[s] * (1.0 / 256.0))
        parts = jnp.where(lane == 0.0, lp[s] - 256.0 * hi, jnp.where(lane == 1.0, hi, 0.0)).astype(BF16)
        t = _nt_dot(sel, parts)
        lp_lanes.append(t[0:1] + 256.0 * t[1:2])
    perm = jnp.where(pos == lp_lanes[0], 1.0, jnp.where(pos == lp_lanes[1], 1.0, 0.0)).astype(BF16)
    xs = jnp.dot(perm, u_hi, preferred_element_type=F32)
    xs_ref[...] = xs if anchor is None else xs + anchor


def _outproj(n_tiles, mix_a, mix_b, w_a, w_b, b_out, x_lat, x_ctx, mods, ln_g, ln_b, w_r, b_r):
    rows = n_tiles * TM
    steps = n_tiles // OUT_SUB
    lat_steps = NT_LAT // OUT_SUB
    half = mix_a.shape[1]
    w_rh = w_r.astype(BF16)
    w_rl = (w_r - w_rh.astype(F32)).astype(BF16)
    upper = (jnp.arange(ROUTE_W)[:, None] < jnp.arange(ROUTE_W)[None, :]).astype(BF16)
    lower = (jnp.arange(TM)[:, None] > jnp.arange(TM)[None, :]).astype(BF16)
    sel = (jnp.arange(8)[:, None] == jnp.arange(ROUTE_W)[None, :]).astype(BF16)
    row = lambda w: pl.BlockSpec((OUT_SUB * TM, w), lambda i: (i, 0))
    mod = lambda chunk: pl.BlockSpec((None, None, 1, D_MODEL), lambda i: (_mod_row(i * OUT_SUB), chunk, 0, 0))
    vec = _full((1, D_MODEL))
    return pl.pallas_call(
        _outproj_kernel,
        grid=(steps,),
        in_specs=[row(half), row(half), _full((half, D_MODEL)), _full((half, D_MODEL)), vec,
                  pl.BlockSpec((OUT_SUB * TM, D_MODEL), lambda i: (jnp.minimum(i, lat_steps - 1), 0)),
                  pl.BlockSpec((OUT_SUB * TM, D_MODEL), lambda i: (jnp.maximum(i - lat_steps, 0), 0)),
                  mod(G1), mod(SH2), mod(SC2), vec, vec,
                  _full((D_MODEL, ROUTE_W)), _full((D_MODEL, ROUTE_W)), _full((1, ROUTE_W)),
                  _full((ROUTE_W, ROUTE_W)), _full((TM, TM)), _full((8, ROUTE_W))],
        out_specs=[row(D_MODEL), row(ROUTE_W), pl.BlockSpec((OUT_SUB * 8, ROUTE_W), lambda i: (i, 0)),
                   pl.BlockSpec((OUT_SUB * LOCAL_ROWS, D_MODEL), lambda i: (i, 0))],
        out_shape=[jax.ShapeDtypeStruct((rows, D_MODEL), F32),
                   jax.ShapeDtypeStruct((rows, ROUTE_W), F32),
                   jax.ShapeDtypeStruct((n_tiles * 8, ROUTE_W), F32),
                   jax.ShapeDtypeStruct((n_tiles * LOCAL_ROWS, D_MODEL), F32)],
        compiler_params=_params(),
        name="outproj_ln_router",
    )(mix_a, mix_b, w_a, w_b, b_out, x_lat, x_ctx, mods, mods, mods, ln_g, ln_b, w_rh, w_rl, b_r,
      upper, lower, sel)


def _aligned(i):
    return pl.multiple_of(i, RUN_ALIGN)


def _moe_kernel(te_ref, tk_ref, rows_ref, lo_ref, hi_ref, cnt_ref, bt_ref, be_ref, xs_hbm, w1_ref, w3_ref,
                w2_ref, ys_ref, xbuf, wb1, wb3, wb2, sem):
    j = pl.program_id(0)
    nt = pl.num_programs(0)
    slot = j % 2

    def issue(tile, slot_):
        e = te_ref[tile]
        first = tk_ref[tile] * TMM

        def body(i, carry):
            idx = i * N_EXPERTS + e
            start = bt_ref[idx]
            lo = jnp.maximum(start, first)
            n = jnp.minimum(start + cnt_ref[idx], first + TMM) - lo

            @pl.when(n > 0)
            def _():
                src = i * LOCAL_ROWS + be_ref[idx] + lo - start
                pltpu.make_async_copy(xs_hbm.at[pl.ds(_aligned(src), _aligned(n))],
                                      xbuf.at[slot_, pl.ds(_aligned(lo - first), _aligned(n))],
                                      sem.at[slot_]).start()
            return carry
        lax.fori_loop(lo_ref[tile], hi_ref[tile], body, 0)

    @pl.when(j == 0)
    def _():
        xbuf[...] = jnp.zeros_like(xbuf)
        issue(0, 0)

    @pl.when(j + 1 < nt)
    def _():
        issue(j + 1, 1 - slot)

    @pl.when(jnp.logical_or(j == 0, te_ref[j] != te_ref[jnp.maximum(j - 1, 0)]))
    def _():
        wb1[...] = w1_ref[...].astype(BF16)
        wb3[...] = w3_ref[...].astype(BF16)
        wb2[...] = w2_ref[...].astype(BF16)

    n_real = rows_ref[j]

    @pl.when(n_real > 0)
    def _():
        pltpu.make_async_copy(xs_hbm.at[pl.ds(0, _aligned(n_real))], xbuf.at[slot, pl.ds(0, _aligned(n_real))],
                              sem.at[slot]).wait()
        x = xbuf[slot].astype(BF16)
        h1 = jnp.dot(x, wb1[...], preferred_element_type=F32)
        h3 = jnp.dot(x, wb3[...], preferred_element_type=F32)
        hid = h1 * _sigmoid(h1) * h3
        ys_ref[...] = jnp.dot(hid.astype(BF16), wb2[...], preferred_element_type=F32)

    @pl.when(n_real == 0)
    def _():
        ys_ref[...] = jnp.zeros_like(ys_ref)


def _moe_experts(layer, plan, xs_local, w1, w3, w2):
    tile_expert, tile_k, tile_rows, src_lo, src_hi, cnt, before_tile, before_expert, _, _ = plan
    nt = tile_expert.shape[0]
    wmap = lambda j, te, *_: (layer, te[j], 0, 0)
    grid_spec = pltpu.PrefetchScalarGridSpec(
        num_scalar_prefetch=8,
        grid=(nt,),
        in_specs=[pl.BlockSpec(memory_space=pl.ANY),
                  pl.BlockSpec((None, None, D_MODEL, EXPERT_FF), wmap),
                  pl.BlockSpec((None, None, D_MODEL, EXPERT_FF), wmap),
                  pl.BlockSpec((None, None, EXPERT_FF, D_MODEL), wmap)],
        out_specs=pl.BlockSpec((TMM, D_MODEL), lambda j, *_: (j, 0)),
        scratch_shapes=[pltpu.VMEM((2, TMM, D_MODEL), F32),
                        pltpu.VMEM((D_MODEL, EXPERT_FF), BF16),
                        pltpu.VMEM((D_MODEL, EXPERT_FF), BF16),
                        pltpu.VMEM((EXPERT_FF, D_MODEL), BF16),
                        pltpu.SemaphoreType.DMA((2,))])
    return pl.pallas_call(
        _moe_kernel,
        grid_spec=grid_spec,
        out_shape=jax.ShapeDtypeStruct((nt * TMM, D_MODEL), F32),
        compiler_params=_params(),
        name="moe_experts",
    )(tile_expert, tile_k, tile_rows, src_lo, src_hi, cnt, before_tile, before_expert, xs_local, w1, w3, w2)


def _combine_kernel(cnt_ref, bt_ref, be_ref, gs_ref, used_ref, ys_hbm, x_ref, route_ref, g2_ref, lng_ref, lnb_ref,
                    o_ref, ybuf, sem):
    i = pl.program_id(0)
    nt = pl.num_programs(0)
    slot = i % 2

    def issue(tile, slot_):
        def body(e, carry):
            idx = tile * N_EXPERTS + e
            n = cnt_ref[idx]

            @pl.when(n > 0)
            def _():
                pltpu.make_async_copy(ys_hbm.at[pl.ds(_aligned(gs_ref[e] + bt_ref[idx]), _aligned(n))],
                                      ybuf.at[slot_, pl.ds(_aligned(be_ref[idx]), _aligned(n))],
                                      sem.at[slot_]).start()
            return carry
        lax.fori_loop(0, N_EXPERTS, body, 0)

    @pl.when(i == 0)
    def _():
        ybuf[...] = jnp.zeros_like(ybuf)
        issue(0, 0)

    @pl.when(i + 1 < nt)
    def _():
        issue(i + 1, 1 - slot)

    used = _aligned(used_ref[i])
    pltpu.make_async_copy(ys_hbm.at[pl.ds(0, used)], ybuf.at[slot, pl.ds(0, used)], sem.at[slot]).wait()
    route = route_ref[...]
    pos = lax.broadcasted_iota(jnp.int32, (TM, LOCAL_ROWS), 1).astype(F32)
    y = ybuf[slot].astype(BF16)
    picked = [jnp.dot(jnp.where(pos == route[:, 4 + s:5 + s], 1.0, 0.0).astype(BF16), y,
                      preferred_element_type=F32) for s in range(2)]
    f = route[:, 2:3] * picked[0] + route[:, 3:4] * picked[1]
    z = DN_ALPHA * x_ref[...] + (1.0 + g2_ref[...]) * f
    o_ref[...] = _layer_norm(z, lng_ref[...], lnb_ref[...])


def _moe_combine(n_tiles, plan, ys, x_all, route, mods, ln_g, ln_b):
    _, _, _, _, _, cnt, before_tile, before_expert, group_start, used = plan
    rows = n_tiles * TM
    row = lambda w: pl.BlockSpec((TM, w), lambda i, *_: (i, 0))
    vec = pl.BlockSpec((1, D_MODEL), lambda i, *_: (0, 0))
    grid_spec = pltpu.PrefetchScalarGridSpec(
        num_scalar_prefetch=5,
        grid=(n_tiles,),
        in_specs=[pl.BlockSpec(memory_space=pl.ANY), row(D_MODEL), row(ROUTE_W),
                  pl.BlockSpec((None, None, 1, D_MODEL), lambda i, *_: (_mod_row(i), G2, 0, 0)), vec, vec],
        out_specs=row(D_MODEL),
        scratch_shapes=[pltpu.VMEM((2, LOCAL_ROWS, D_MODEL), F32), pltpu.SemaphoreType.DMA((2,))])
    return pl.pallas_call(
        _combine_kernel,
        grid_spec=grid_spec,
        out_shape=jax.ShapeDtypeStruct((rows, D_MODEL), F32),
        compiler_params=_params(),
        name="moe_combine_ln",
    )(cnt, before_tile, before_expert, group_start, used, ys, x_all, route, mods, ln_g, ln_b)


def _moe_plan(cnt_rec, n_tiles):
    cnt = cnt_rec.reshape(n_tiles, 8, ROUTE_W)[:, 0, :N_EXPERTS].astype(jnp.int32)
    nt_max = (n_tiles * (2 * TM + N_EXPERTS * (RUN_ALIGN - 1))) // TMM + N_EXPERTS
    total = jnp.sum(cnt, axis=0)
    tiles_e = (total + TMM - 1) // TMM
    tile_end = jnp.cumsum(tiles_e)
    first_tile = tile_end - tiles_e
    before_tile = jnp.cumsum(cnt, axis=0) - cnt
    before_expert = jnp.cumsum(cnt, axis=1) - cnt
    tile_id = jnp.arange(nt_max, dtype=jnp.int32)
    tile_expert = jnp.minimum(jnp.sum((tile_id[:, None] >= tile_end[None, :]).astype(jnp.int32), axis=1),
                              N_EXPERTS - 1)
    onehot = (tile_expert[:, None] == jnp.arange(N_EXPERTS, dtype=jnp.int32)[None, :]).astype(jnp.int32)
    tile_k = tile_id - jnp.sum(onehot * first_tile[None, :], axis=1)
    tile_rows = jnp.clip(jnp.sum(onehot * total[None, :], axis=1) - tile_k * TMM, 0, TMM)
    first = (tile_k * TMM)[:, None]
    run_start = jnp.sum(onehot[:, None, :] * before_tile[None, :, :], axis=2)
    run_end = run_start + jnp.sum(onehot[:, None, :] * cnt[None, :, :], axis=2)
    src_lo = jnp.sum((run_end <= first).astype(jnp.int32), axis=1)
    src_hi = jnp.sum((run_start < first + TMM).astype(jnp.int32), axis=1)
    return (tile_expert, tile_k, tile_rows, src_lo, src_hi, cnt.reshape(-1), before_tile.reshape(-1),
            before_expert.reshape(-1), first_tile * TMM, jnp.sum(cnt, axis=1))


def _router_weights(w_rg, b_rg, w_re, b_re):
    w = jnp.concatenate([w_rg, jnp.transpose(w_re, (1, 0, 2)).reshape(D_MODEL, N_EXPERTS)], axis=1)
    b = jnp.concatenate([b_rg, b_re.reshape(-1)])
    pad = ROUTE_W - w.shape[1]
    return jnp.pad(w, ((0, 0), (0, pad))), jnp.pad(b, (0, pad)).reshape(1, ROUTE_W)


def _proj1_kernel(x_ref, sh_ref, sc_ref, w_ref, b_ref, cos_ref, sin_ref, cosm_ref, sinm_ref, cosr_ref,
                  sinr_ref, gq_ref, gk_ref, gqc_ref, gkv_ref, avg_ref, wuq_ref, wuk_ref, wuv_ref, vplace_ref,
                  q_ref, qm_ref, k_ref, v_ref, km_ref, vm_ref):
    u = x_ref[...] * (1.0 + sc_ref[...]) + sh_ref[...]
    y = jnp.dot(u.astype(BF16), w_ref[...], preferred_element_type=F32) + b_ref[...]
    c_q = GQA_HEADS * HEAD_DIM
    c_qc = c_q + MLA_Q_RANK
    c_k = c_qc + GQA_KV_HEADS * HEAD_DIM
    c_v = c_k + GQA_KV_HEADS * HEAD_DIM
    c_kv = c_v + MLA_KV_RANK
    avg = avg_ref[...]

    def head_rms(t, gain):
        sq = t * t
        hi = sq.astype(BF16)
        lo = (sq - hi.astype(F32)).astype(BF16)
        a = avg[:t.shape[1], :t.shape[1]]
        ms = jnp.dot(hi, a, preferred_element_type=F32) + jnp.dot(lo, a, preferred_element_type=F32)
        return t * lax.rsqrt(ms + RMS_EPS) * gain

    def row_rms(t, gain):
        ms = jnp.mean(t * t, axis=-1, keepdims=True)
        return t * lax.rsqrt(ms + RMS_EPS) * gain

    cos = cos_ref[...]
    sin = sin_ref[...]
    cos4 = jnp.concatenate([cos] * 4, axis=1)
    sin4 = jnp.concatenate([sin] * 4, axis=1)
    q = _rope(head_rms(y[:, :c_q], gq_ref[...]), cos4, sin4, HEAD_DIM // 4) * (HEAD_DIM ** -0.5 * LOG2E)
    q_ref[...] = q.astype(BF16)
    k = _rope(head_rms(y[:, c_qc:c_k], gk_ref[...]), cos, sin, HEAD_DIM // 4)
    k_ref[...] = k.astype(BF16)

    v_ref[...] = _transposed_values(vplace_ref[...], y[:, c_k:c_v].astype(BF16), HEAD_DIM)

    qc = row_rms(y[:, c_q:c_qc], gqc_ref[...]).astype(BF16)
    qm = jnp.dot(qc, wuq_ref[...], preferred_element_type=F32)
    cosm = jnp.concatenate([cosm_ref[...]] * MLA_HEADS, axis=1)
    sinm = jnp.concatenate([sinm_ref[...]] * MLA_HEADS, axis=1)
    qm = _rope(qm, cosm, sinm, MLA_ROPE // 4) * ((MLA_NOPE + MLA_ROPE) ** -0.5 * LOG2E)
    qm_ref[...] = qm.astype(BF16)

    kvn = row_rms(y[:, c_v:c_kv], gkv_ref[...]).astype(BF16)
    kr = _rope(y[:, c_kv:], cosr_ref[...], sinr_ref[...], MLA_ROPE // 4).astype(BF16)
    km = jnp.dot(jnp.concatenate([kvn, kr], axis=1), wuk_ref[...], preferred_element_type=F32)
    km_ref[...] = km.astype(BF16)
    vm_ref[...] = _transposed_values(wuv_ref[...], kvn, MLA_V)


def _proj1(x_all, mods, w_in, b_in, tabs, gq, gk, gqc, gkv, avg, wuq, wuk, wuv):
    cos_hd, sin_hd, cos_m, sin_m, cos_r, sin_r = tabs
    kvw = GQA_KV_HEADS * HEAD_DIM
    qw = GQA_HEADS * HEAD_DIM
    mw = MLA_HEADS * MLA_PAD
    vw = MLA_HEADS * VAL_PAD
    gvw = GQA_KV_HEADS * VAL_PAD
    row = lambda w: pl.BlockSpec((TM, w), lambda i: (i, 0))
    col = lambda h: pl.BlockSpec((h, TM), lambda i: (0, i))
    tab = pl.BlockSpec((TM, 128), lambda i: (_rope_row_block(i), 0))
    vplace = _value_placement(GQA_KV_HEADS, HEAD_DIM)
    return pl.pallas_call(
        _proj1_kernel,
        grid=(NT_ALL,),
        in_specs=[row(D_MODEL), _mod_spec(SH1), _mod_spec(SC1),
                  _full((D_MODEL, ODD_IN_PAD)), _full((1, ODD_IN_PAD)), tab, tab, tab, tab, tab, tab,
                  _full((1, qw)), _full((1, kvw)), _full((1, MLA_Q_RANK)), _full((1, MLA_KV_RANK)),
                  _full((qw, qw)), _full((MLA_Q_RANK, mw)), _full((MLA_KV_RANK + 128, mw)),
                  _full((vw, MLA_KV_RANK)), _full((gvw, kvw))],
        out_specs=[row(qw), row(mw), row(kvw), col(gvw), row(mw), col(vw)],
        out_shape=[jax.ShapeDtypeStruct((R_ALL, qw), BF16),
                   jax.ShapeDtypeStruct((R_ALL, mw), BF16),
                   jax.ShapeDtypeStruct((R_ALL, kvw), BF16),
                   jax.ShapeDtypeStruct((gvw, R_ALL), BF16),
                   jax.ShapeDtypeStruct((R_ALL, mw), BF16),
                   jax.ShapeDtypeStruct((vw, R_ALL), BF16)],
        compiler_params=_params(),
        name="proj1",
    )(x_all, mods, mods, w_in, b_in, cos_hd, sin_hd, cos_m, sin_m, cos_r, sin_r,
      gq, gk, gqc, gkv, avg, wuq, wuk, wuv.T, vplace)


def _dense_kernel(q_ref, kl_ref, kc_ref, vl_ref, vc_ref, o_ref, s_buf, p_buf, *, n_heads, group, stack, dk, dv):
    tq = q_ref.shape[0]
    units = []
    for h0 in range(0, n_heads, stack):
        kv = h0 // group
        qs = [q_ref[:, h * dk:(h + 1) * dk] for h in range(h0, h0 + stack)]
        q = qs[0] if stack == 1 else jnp.concatenate(qs, axis=0)
        ks = slice(kv * dk, (kv + 1) * dk)
        vs = slice(kv * VAL_PAD, (kv + 1) * VAL_PAD)
        parts = [slice(c * (SEQ // KEY_PARTS), (c + 1) * (SEQ // KEY_PARTS)) for c in range(KEY_PARTS)]
        units.append((q, [kl_ref[c, ks] for c in parts] + [kc_ref[:, ks]],
                      [vl_ref[vs, c] for c in parts] + [vc_ref[vs, :]]))
    for u, o_t in enumerate(_attend_keys_major(units, dv, s_buf, p_buf)):
        o = o_t.T
        for g in range(stack):
            h = u * stack + g
            o_ref[:, h * dv:(h + 1) * dv] = o[g * tq:(g + 1) * tq].astype(BF16)


def _dense_attention(q, k, v, *, n_heads, group, stack, dk, dv, tq, name):
    n_kv = n_heads // group
    nq = SEQ // tq
    ctx0 = R_LAT // CTX_LEN
    lat = lambda w: pl.BlockSpec((SEQ, w), lambda b, j: (b, 0), pipeline_mode=pl.Buffered(1))
    ctx = lambda w: pl.BlockSpec((CTX_LEN, w), lambda b, j: (ctx0 + b, 0))
    lat_t = pl.BlockSpec((n_kv * VAL_PAD, SEQ), lambda b, j: (0, b), pipeline_mode=pl.Buffered(1))
    ctx_t = pl.BlockSpec((n_kv * VAL_PAD, CTX_LEN), lambda b, j: (0, ctx0 + b))
    return pl.pallas_call(
        functools.partial(_dense_kernel, n_heads=n_heads, group=group, stack=stack, dk=dk, dv=dv),
        grid=(BATCH, nq),
        in_specs=[pl.BlockSpec((tq, n_heads * dk), lambda b, j: (b * nq + j, 0)),
                  lat(n_kv * dk), ctx(n_kv * dk), lat_t, ctx_t],
        out_specs=pl.BlockSpec((tq, n_heads * dv), lambda b, j: (b * nq + j, 0)),
        out_shape=jax.ShapeDtypeStruct((R_LAT, n_heads * dv), BF16),
        scratch_shapes=[pltpu.VMEM((SCORE_SLOTS, SEQ + CTX_LEN, stack * tq), F32),
                        pltpu.VMEM((2, SEQ + CTX_LEN, stack * tq), BF16)],
        compiler_params=_params(),
        name=name,
    )(q, k, k, v, v)


def _mla_weights(w_uq, w_ukv):
    wq = w_uq.reshape(MLA_Q_RANK, MLA_HEADS, MLA_NOPE + MLA_ROPE)
    wq = jnp.pad(wq, ((0, 0), (0, 0), (0, MLA_PAD - MLA_NOPE - MLA_ROPE))).reshape(MLA_Q_RANK, -1)
    wkv = w_ukv.reshape(MLA_KV_RANK, MLA_HEADS, MLA_NOPE + MLA_V)
    wk = jnp.pad(wkv[:, :, :MLA_NOPE], ((0, 0), (0, 0), (0, MLA_PAD - MLA_NOPE))).reshape(MLA_KV_RANK, -1)
    wv = jnp.pad(wkv[:, :, MLA_NOPE:], ((0, 0), (0, 0), (0, VAL_PAD - MLA_V))).reshape(MLA_KV_RANK, -1)
    r = jnp.arange(128)[:, None]
    c = jnp.arange(MLA_HEADS * MLA_PAD)[None, :]
    place = jnp.logical_and(r < MLA_ROPE, (c % MLA_PAD) == MLA_NOPE + r).astype(F32)
    wk = jnp.concatenate([wk, place], axis=0)
    return wq.astype(BF16), wk.astype(BF16), wv.astype(BF16)


def kernel(x, c, ctx, c_ctx, even_w_in, even_b_in, even_conv_w, even_conv_b, even_conv_ln_g, even_conv_ln_b, even_sink, even_w_out, even_b_out, odd_w_in, odd_b_in, odd_q_norm, odd_k_norm, odd_mla_q_norm, odd_mla_kv_norm, odd_mla_w_uq, odd_mla_w_ukv, odd_w_out, odd_b_out, ada_w, ada_b, ln1_g, ln1_b, ln2_g, ln2_b, moe_w_rg, moe_b_rg, moe_w_re, moe_b_re, moe_w1, moe_w3, moe_w2):
    vec = lambda a: a.reshape(1, -1)
    x_lat0 = x.reshape(R_LAT, D_MODEL)
    x_ctx0 = ctx.reshape(R_CTX, D_MODEL)

    cv =jnp.concatenate([c, c_ctx[None, :], jnp.zeros((8 - BATCH - 1, D_MODEL), F32)], axis=0)
    mods = _ada_table(cv, ada_w, ada_b).reshape(DEPTH, 8, 6, 1, D_MODEL)

    cos64, sin64 = _rope_tables(HEAD_DIM)
    cos_hd, sin_hd = _pad_table(cos64, sin64, 0, HEAD_DIM, 128)
    cos32, sin32 = _rope_tables(MLA_ROPE)
    cos_m, sin_m = _pad_table(cos32, sin32, MLA_NOPE, MLA_PAD, 128)
    cos_r, sin_r = _pad_table(cos32, sin32, 0, 128, 128)

    m0 = mods[0]
    h, q0, k0, v0 = _proj0(x_lat0, x_ctx0, m0, even_w_in[0].astype(BF16), vec(even_b_in[0]), cos_hd, sin_hd)
    conv_out = _conv(h, even_conv_w[0].reshape(CONV_WIDTH, CONV_CH), vec(even_conv_b[0]),
                     vec(even_conv_ln_g[0]), vec(even_conv_ln_b[0]))
    attn = _win_attention(even_sink[0], q0, k0, v0)
    w_out = even_w_out[0].astype(BF16)
    w_r, b_r = _router_weights(moe_w_rg[0], moe_b_rg[0], moe_w_re[0], moe_b_re[0])
    x_all, route, cnt_rec, xs_local = _outproj(
        NT_ALL, conv_out, attn, w_out[:CONV_CH], w_out[CONV_CH:], vec(even_b_out[0]),
        x_lat0, x_ctx0, m0, vec(ln1_g[0]), vec(ln1_b[0]), w_r, b_r)
    plan = _moe_plan(cnt_rec, NT_ALL)
    ys = _moe_experts(0, plan, xs_local, moe_w1, moe_w3, moe_w2)
    x_all = _moe_combine(NT_ALL, plan, ys, x_all, route, m0, vec(ln2_g[0]), vec(ln2_b[0]))

    m1 = mods[1]
    w_in1 = jnp.pad(odd_w_in[0], ((0, 0), (0, ODD_IN_PAD - ODD_IN))).astype(BF16)
    b_in1 = jnp.pad(odd_b_in[0], (0, ODD_IN_PAD - ODD_IN)).reshape(1, -1)
    wuq, wuk, wuv = _mla_weights(odd_mla_w_uq[0], odd_mla_w_ukv[0])
    qw = GQA_HEADS * HEAD_DIM
    hid = jnp.arange(qw) // HEAD_DIM
    avg = ((hid[:, None] == hid[None, :]).astype(F32) / HEAD_DIM).astype(BF16)
    q1, qm, k1, v1, km, vm = _proj1(
        x_all, m1, w_in1, b_in1, (cos_hd, sin_hd, cos_m, sin_m, cos_r, sin_r),
        vec(jnp.tile(odd_q_norm[0], GQA_HEADS)), vec(jnp.tile(odd_k_norm[0], GQA_KV_HEADS)),
        vec(odd_mla_q_norm[0]), vec(odd_mla_kv_norm[0]), avg, wuq, wuk, wuv)
    o_g = _dense_attention(q1, k1, v1, n_heads=GQA_HEADS, group=GQA_HEADS // GQA_KV_HEADS, stack=2,
                           dk=HEAD_DIM, dv=HEAD_DIM, tq=256, name="gqa_attention")
    o_m = _dense_attention(qm, km, vm, n_heads=MLA_HEADS, group=1, stack=1, dk=MLA_PAD, dv=MLA_V, tq=256,
                           name="mla_attention")
    w_out = odd_w_out[0].astype(BF16)
    w_r, b_r = _router_weights(moe_w_rg[1], moe_b_rg[1], moe_w_re[1], moe_b_re[1])
    x_lat, route, cnt_rec, xs_local = _outproj(
        NT_LAT, o_g, o_m, w_out[:qw], w_out[qw:], vec(odd_b_out[0]),
        x_all, x_all, m1, vec(ln1_g[1]), vec(ln1_b[1]), w_r, b_r)
    plan = _moe_plan(cnt_rec, NT_LAT)
    ys = _moe_experts(1, plan, xs_local, moe_w1, moe_w3, moe_w2)
    x_lat = _moe_combine(NT_LAT, plan, ys, x_lat, route, m1, vec(ln2_g[1]), vec(ln2_b[1]))
    return x_lat.reshape(BATCH, SEQ, D_MODEL)
```

```python
import functools

import jax
import jax.numpy as jnp
from jax import lax
from jax.experimental import pallas as pl
from jax.experimental.pallas import tpu as pltpu

F32 = jnp.float32
BF16 = jnp.bfloat16

D_MODEL = 1024
BATCH = 4
SEQ = 4096
DEPTH = 2
GRID_W = 64
CTX_LEN = 256
HEAD_DIM = 64
ROPE_THETA = 10000.0
LN_EPS = 1e-5
RMS_EPS = 1e-6
NEG_INF = -1e30

CONV_CH = 512
CONV_WIDTH = 31
WIN_HEADS = 8
WIN_KV_HEADS = 2
WINDOW = 128
GQA_HEADS = 8
GQA_KV_HEADS = 2
MLA_HEADS = 8
MLA_Q_RANK = 256
MLA_KV_RANK = 128
MLA_NOPE = 64
MLA_ROPE = 32
MLA_V = 64
N_GROUPS = 4
EXP_PER_GROUP = 8
N_EXPERTS = N_GROUPS * EXP_PER_GROUP
EXPERT_FF = 512
DN_ALPHA = float((2 * DEPTH) ** 0.25)

EVEN_IN = 2 * CONV_CH + (WIN_HEADS + 2 * WIN_KV_HEADS) * HEAD_DIM
ODD_IN = 1184
ODD_IN_PAD = 1280
MLA_PAD = 128
VAL_PAD = 128
WIN_STACK = 1
SCORE_SLOTS = 2
KEY_PARTS = 4
LOG2E = 1.4426950408889634

R_LAT = BATCH * SEQ
R_CTX = BATCH * CTX_LEN
R_ALL = R_LAT + R_CTX
TM = 256
NT_LAT = R_LAT // TM
NT_ALL = R_ALL // TM
TILES_PER_SEQ = SEQ // TM
HALO = 16
CONV_CHUNK = 32
SHIFTS = 8
OUT_SUB = 2
TMM = 512
ROUTE_W = 128
RUN_ALIGN = 8
LOCAL_ROWS = 768
VMEM_LIMIT = 56 * 1024 * 1024

SH1, SC1, G1, SH2, SC2, G2 = range(6)


def _sigmoid(x):
    return 1.0 / (1.0 + jnp.exp(-x))


def _layer_norm(z, g, b):
    mu = jnp.mean(z, axis=-1, keepdims=True)
    zc = z - mu
    var = jnp.mean(zc * zc, axis=-1, keepdims=True)
    return zc * lax.rsqrt(var + LN_EPS) * g + b


def _rope(x, cos, sin, half):
    n = x.shape[-1]
    lane = lax.broadcasted_iota(jnp.int32, x.shape, 1)
    first = (lane % (2 * half)) < half
    partner = jnp.where(first, pltpu.roll(x, n - half, 1), pltpu.roll(x, half, 1))
    return x * cos + partner * sin


def _mod_row(i):
    return jnp.where(i < NT_LAT, i // TILES_PER_SEQ, BATCH)


def _mod_spec(chunk):
    return pl.BlockSpec((None, None, 1, D_MODEL), lambda i: (_mod_row(i), chunk, 0, 0))


def _rope_row_block(i):
    return jnp.where(i < NT_LAT, i % TILES_PER_SEQ, TILES_PER_SEQ)


def _full(shape):
    nd = len(shape)
    return pl.BlockSpec(shape, lambda *_: (0,) * nd)


def _params():
    return pltpu.CompilerParams(vmem_limit_bytes=VMEM_LIMIT)


def _ada_kernel(cv_ref, w_ref, b_ref, o_ref):
    cv = cv_ref[...]
    s = cv * _sigmoid(cv)
    o_ref[...] = jnp.dot(s, w_ref[...], precision=lax.Precision.HIGHEST,
                         preferred_element_type=F32) + b_ref[...]


def _ada_table(cv, ada_w, ada_b):
    bn = 1536
    nb = (6 * D_MODEL) // bn
    return pl.pallas_call(
        _ada_kernel,
        grid=(DEPTH, nb),
        in_specs=[pl.BlockSpec((8, D_MODEL), lambda l, j: (0, 0)),
                  pl.BlockSpec((None, D_MODEL, bn), lambda l, j: (l, 0, j)),
                  pl.BlockSpec((None, 1, bn), lambda l, j: (l, 0, j))],
        out_specs=pl.BlockSpec((None, 8, bn), lambda l, j: (l, 0, j)),
        out_shape=jax.ShapeDtypeStruct((DEPTH, 8, 6 * D_MODEL), F32),
        compiler_params=_params(),
        name="ada_table",
    )(cv, ada_w, ada_b.reshape(DEPTH, 1, 6 * D_MODEL))


def _rope_tables(rot_dim):
    axis_dim = rot_dim // 2
    inv_freq = ROPE_THETA ** (-jnp.arange(0, axis_dim, 2, dtype=F32) / axis_dim)
    t = jnp.arange(SEQ)
    ang_r = (t // GRID_W).astype(F32)[:, None] * inv_freq[None, :]
    ang_c = (t % GRID_W).astype(F32)[:, None] * inv_freq[None, :]
    cos = jnp.concatenate([jnp.cos(ang_r), jnp.cos(ang_r), jnp.cos(ang_c), jnp.cos(ang_c)], axis=-1)
    sin = jnp.concatenate([-jnp.sin(ang_r), jnp.sin(ang_r), -jnp.sin(ang_c), jnp.sin(ang_c)], axis=-1)
    return cos, sin


def _pad_table(cos, sin, lead, period, width):
    rot = cos.shape[1]
    one = jnp.ones((SEQ, period), F32).at[:, lead:lead + rot].set(cos)
    zero = jnp.zeros((SEQ, period), F32).at[:, lead:lead + rot].set(sin)
    cos_w = jnp.tile(one, (1, width // period))
    sin_w = jnp.tile(zero, (1, width // period))
    cos_w = jnp.concatenate([cos_w, jnp.ones((TM, width), F32)], axis=0)
    sin_w = jnp.concatenate([sin_w, jnp.zeros((TM, width), F32)], axis=0)
    return cos_w, sin_w


def _proj0_kernel(xl_ref, xc_ref, sh_ref, sc_ref, w_ref, b_ref, cos_ref, sin_ref, vplace_ref,
                  h_ref, q_ref, k_ref, v_ref):
    x = jnp.where(pl.program_id(0) < NT_LAT, xl_ref[...], xc_ref[...])
    u = x * (1.0 + sc_ref[...]) + sh_ref[...]
    y = jnp.dot(u.astype(BF16), w_ref[...], preferred_element_type=F32) + b_ref[...]
    h_ref[...] = y[:, :CONV_CH] * _sigmoid(y[:, CONV_CH:2 * CONV_CH])
    cos = cos_ref[...]
    sin = sin_ref[...]
    q0 = 2 * CONV_CH
    k0 = q0 + WIN_HEADS * HEAD_DIM
    v0 = k0 + WIN_KV_HEADS * HEAD_DIM
    cos4 = jnp.concatenate([cos] * 4, axis=1)
    sin4 = jnp.concatenate([sin] * 4, axis=1)
    q = _rope(y[:, q0:k0], cos4, sin4, HEAD_DIM // 4) * (HEAD_DIM ** -0.5 * LOG2E)
    q_ref[...] = q.astype(BF16)
    k_ref[...] = _rope(y[:, k0:v0], cos, sin, HEAD_DIM // 4).astype(BF16)
    v_ref[...] = _transposed_values(vplace_ref[...], y[:, v0:].astype(BF16), HEAD_DIM)


def _proj0(x_lat, x_ctx, mods, w_in, b_in, cos_hd, sin_hd):
    kvw = WIN_KV_HEADS * HEAD_DIM
    row = lambda w: pl.BlockSpec((TM, w), lambda i: (i, 0))
    tab = pl.BlockSpec((TM, 128), lambda i: (_rope_row_block(i), 0))
    return pl.pallas_call(
        _proj0_kernel,
        grid=(NT_ALL,),
        in_specs=[pl.BlockSpec((TM, D_MODEL), lambda i: (jnp.minimum(i, NT_LAT - 1), 0)),
                  pl.BlockSpec((TM, D_MODEL), lambda i: (jnp.maximum(i - NT_LAT, 0), 0)),
                  _mod_spec(SH1), _mod_spec(SC1),
                  _full((D_MODEL, EVEN_IN)), _full((1, EVEN_IN)), tab, tab,
                  _full((WIN_KV_HEADS * VAL_PAD, kvw))],
        out_specs=[row(CONV_CH), row(WIN_HEADS * HEAD_DIM), row(kvw),
                   pl.BlockSpec((WIN_KV_HEADS * VAL_PAD, TM), lambda i: (0, i))],
        out_shape=[jax.ShapeDtypeStruct((R_ALL, CONV_CH), F32),
                   jax.ShapeDtypeStruct((R_ALL, WIN_HEADS * HEAD_DIM), BF16),
                   jax.ShapeDtypeStruct((R_ALL, kvw), BF16),
                   jax.ShapeDtypeStruct((WIN_KV_HEADS * VAL_PAD, R_ALL), BF16)],
        compiler_params=_params(),
        name="proj0",
    )(x_lat, x_ctx, mods, mods, w_in, b_in, cos_hd, sin_hd, _value_placement(WIN_KV_HEADS, HEAD_DIM))


def _conv_kernel(prev_ref, cur_ref, next_ref, w_ref, cb_ref, g_ref, b_ref, o_ref, buf):
    i = pl.program_id(0)
    is_ctx = i >= NT_LAT
    first = jnp.logical_or(is_ctx, i % TILES_PER_SEQ == 0)
    last = jnp.logical_or(is_ctx, i % TILES_PER_SEQ == TILES_PER_SEQ - 1)
    buf[0, 0:HALO, :] = jnp.where(first, 0.0, prev_ref[...])
    buf[0, HALO:HALO + TM, :] = cur_ref[...]
    buf[0, HALO + TM:, :] = jnp.where(last, 0.0, next_ref[...])
    span = TM + 2 * HALO - SHIFTS
    for r in range(1, SHIFTS):
        buf[r, 0:span, :] = buf[0, r:r + span, :]
    off = HALO - CONV_WIDTH // 2
    for c in range(TM // CONV_CHUNK):
        r0 = c * CONV_CHUNK
        acc = jnp.zeros((CONV_CHUNK, CONV_CH), F32)
        for k in range(CONV_WIDTH):
            r = (off + k) % SHIFTS
            base = r0 + off + k - r
            w = w_ref[k * SHIFTS:(k + 1) * SHIFTS, :]
            acc = acc + buf[r, base:base + CONV_CHUNK, :] * jnp.concatenate([w] * (CONV_CHUNK // SHIFTS), axis=0)
        z = _layer_norm(acc + cb_ref[...], g_ref[...], b_ref[...])
        o_ref[r0:r0 + CONV_CHUNK, :] = (z * _sigmoid(z)).astype(BF16)


def _conv(h, conv_w, conv_b, ln_g, ln_b):
    nh = R_ALL // HALO
    per = TM // HALO
    vec = _full((1, CONV_CH))
    return pl.pallas_call(
        _conv_kernel,
        grid=(NT_ALL,),
        in_specs=[pl.BlockSpec((HALO, CONV_CH), lambda i: (jnp.maximum(i * per - 1, 0), 0)),
                  pl.BlockSpec((TM, CONV_CH), lambda i: (i, 0)),
                  pl.BlockSpec((HALO, CONV_CH), lambda i: (jnp.minimum((i + 1) * per, nh - 1), 0)),
                  _full((CONV_WIDTH * SHIFTS, CONV_CH)), vec, vec, vec],
        out_specs=pl.BlockSpec((TM, CONV_CH), lambda i: (i, 0)),
        out_shape=jax.ShapeDtypeStruct((R_ALL, CONV_CH), BF16),
        scratch_shapes=[pltpu.VMEM((SHIFTS, TM + 2 * HALO, CONV_CH), F32)],
        compiler_params=_params(),
        name="conv_module",
    )(h, h, h, jnp.repeat(conv_w, SHIFTS, axis=0), conv_b, ln_g, ln_b)


def _nt_dot(a, b):
    return lax.dot_general(a, b, (((1,), (1,)), ((), ())), preferred_element_type=F32)


def _transposed_values(w_t, src, dv):
    vt = _nt_dot(w_t, src)
    r = lax.broadcasted_iota(jnp.int32, vt.shape, 0)
    return jnp.where(r % VAL_PAD == dv, 1.0, vt).astype(BF16)


def _value_placement(n_kv, dv):
    r = jnp.arange(n_kv * VAL_PAD)[:, None]
    c = jnp.arange(n_kv * dv)[None, :]
    return jnp.logical_and(r // VAL_PAD == c // dv, r % VAL_PAD == c % dv).astype(BF16)


def _attend_keys_major(units, dv, s_buf, p_buf):
    def scores(unit, slot):
        q, ks, _ = unit
        row, ms = 0, []
        for k in ks:
            s = _nt_dot(k, q)
            s_buf[slot, row:row + k.shape[0], :] = s
            ms.append(jnp.max(s, axis=0, keepdims=True))
            row += k.shape[0]
        return functools.reduce(jnp.maximum, ms)

    def run_next_scores_with(slot, nxt_slot, pieces):
        row = 0
        for k in pieces:
            tile = (slice(row + k.shape[0] - 8, row + k.shape[0]), slice(0, 128))
            s_buf[slot, tile[0], tile[1]] = s_buf[slot, tile[0], tile[1]] + 0.0 * s_buf[nxt_slot, tile[0], tile[1]]
            row += k.shape[0]

    n_s, n_p = s_buf.shape[0], p_buf.shape[0]
    results = []
    m = scores(units[0], 0)
    for idx, unit in enumerate(units):
        slot, nxt_slot, pslot = idx % n_s, (idx + 1) % n_s, idx % n_p
        m_next = None
        if idx + 1 < len(units):
            m_next = scores(units[idx + 1], nxt_slot)
            run_next_scores_with(slot, nxt_slot, unit[1])
        row, acc = 0, None
        for vt in unit[2]:
            rows = slice(row, row + vt.shape[1])
            p_buf[pslot, rows, :] = jnp.exp2(s_buf[slot, rows, :] - m).astype(BF16)
            part = jnp.dot(vt, p_buf[pslot, rows, :], preferred_element_type=F32)
            acc = part if acc is None else acc + part
            row += vt.shape[1]
        results.append(acc[:dv] / acc[dv:dv + 1])
        m = m_next
    return results


def _win_kernel(sink_ref, q_ref, kp_ref, kc_ref, kn_ref, kx_ref, vp_ref, vc_ref, vn_ref, vx_ref, o_ref):
    n = pl.program_id(1)
    group = WIN_HEADS // WIN_KV_HEADS
    k_loc = jnp.concatenate([kp_ref[...], kc_ref[...], kn_ref[...]], axis=0)
    vt_loc = jnp.concatenate([vp_ref[...], vc_ref[...], vn_ref[...]], axis=1)
    k_ctx = kx_ref[...]
    vt_ctx = vx_ref[...]
    kj = lax.broadcasted_iota(jnp.int32, (3 * WINDOW, WIN_STACK * WINDOW), 0)
    qi = lax.broadcasted_iota(jnp.int32, (3 * WINDOW, WIN_STACK * WINDOW), 1) % WINDOW
    k_pos = jnp.where(n < SEQ // WINDOW, kj + (n - 1) * WINDOW, SEQ)
    valid = jnp.where(kj >= qi, jnp.where(kj <= qi + 2 * WINDOW, 1, 0), 0)
    valid = jnp.where(k_pos >= 0, jnp.where(k_pos < SEQ, valid, 0), 0) > 0
    staged = []
    for h0 in range(0, WIN_HEADS, WIN_STACK):
        heads = range(h0, h0 + WIN_STACK)
        kv = h0 // group
        q = jnp.concatenate([q_ref[:, h * HEAD_DIM:(h + 1) * HEAD_DIM] for h in heads], axis=0)
        sink = jnp.concatenate([jnp.full((1, WINDOW), sink_ref[h] * LOG2E, F32) for h in heads], axis=1)
        ksl = slice(kv * HEAD_DIM, (kv + 1) * HEAD_DIM)
        s_ctx = _nt_dot(k_ctx[:, ksl], q)
        s_loc = jnp.where(valid, _nt_dot(k_loc[:, ksl], q), NEG_INF)
        m = jnp.maximum(jnp.maximum(jnp.max(s_ctx, axis=0, keepdims=True),
                                    jnp.max(s_loc, axis=0, keepdims=True)), sink)
        staged.append((heads, kv, sink, s_ctx, s_loc, m))
    for u, (heads, kv, sink, s_ctx, s_loc, m) in enumerate(staged):
        vsl = slice(kv * VAL_PAD, (kv + 1) * VAL_PAD)
        if u + 1 < len(staged):
            m = m + 0.0 * staged[u + 1][5]
        acc = (jnp.dot(vt_ctx[vsl, :], jnp.exp2(s_ctx - m).astype(BF16), preferred_element_type=F32)
               + jnp.dot(vt_loc[vsl, :], jnp.exp2(s_loc - m).astype(BF16), preferred_element_type=F32))
        l = acc[HEAD_DIM:HEAD_DIM + 1] + jnp.exp2(sink - m)
        o = (acc[:HEAD_DIM] / l).T
        for g, h in enumerate(heads):
            o_ref[:, h * HEAD_DIM:(h + 1) * HEAD_DIM] = o[g * WINDOW:(g + 1) * WINDOW].astype(BF16)


def _win_attention(sink, q, k, v):
    nblk = SEQ // WINDOW
    cblk = CTX_LEN // WINDOW
    kvw = WIN_KV_HEADS * HEAD_DIM
    ctx0 = R_LAT // CTX_LEN
    lat = lambda n: jnp.minimum(n, nblk - 1)
    prev = lambda b, n: (b * nblk + jnp.maximum(lat(n) - 1, 0), 0)
    cur = lambda b, n: (b * nblk + lat(n), 0)
    nxt = lambda b, n: (b * nblk + jnp.minimum(lat(n) + 1, nblk - 1), 0)
    qrow = lambda b, n: (jnp.where(n < nblk, b * nblk + n, R_LAT // WINDOW + b * cblk + n - nblk), 0)
    ctx = lambda b, n: (ctx0 + b, 0)
    vw = WIN_KV_HEADS * VAL_PAD
    kvb = lambda f, w: pl.BlockSpec((WINDOW, w), f)
    cxb = lambda w: pl.BlockSpec((CTX_LEN, w), ctx)
    flip = lambda f: (lambda b, n: f(b, n)[::-1])
    vtb = lambda f: pl.BlockSpec((vw, WINDOW), flip(f))
    return pl.pallas_call(
        _win_kernel,
        grid=(BATCH, nblk + cblk),
        in_specs=[pl.BlockSpec(memory_space=pltpu.SMEM),
                  pl.BlockSpec((WINDOW, WIN_HEADS * HEAD_DIM), qrow),
                  kvb(prev, kvw), kvb(cur, kvw), kvb(nxt, kvw), cxb(kvw),
                  vtb(prev), vtb(cur), vtb(nxt), pl.BlockSpec((vw, CTX_LEN), flip(ctx))],
        out_specs=pl.BlockSpec((WINDOW, WIN_HEADS * HEAD_DIM), qrow),
        out_shape=jax.ShapeDtypeStruct((R_ALL, WIN_HEADS * HEAD_DIM), BF16),
        compiler_params=_params(),
        name="window_attention",
    )(sink, q, k, k, k, k, v, v, v, v)


def _outproj_kernel(a_ref, b_ref, wa_ref, wb_ref, bo_ref, xl_ref, xc_ref, g1_ref, sh2_ref, sc2_ref,
                    lng_ref, lnb_ref, wrh_ref, wrl_ref, br_ref, upper_ref, lower_ref, sel_ref,
                    xo_ref, route_ref, cnt_ref, xs_ref):
    is_lat = pl.program_id(0) < NT_LAT // OUT_SUB
    u2s = []
    for t in range(OUT_SUB):
        rows = slice(t * TM, (t + 1) * TM)
        x = jnp.where(is_lat, xl_ref[rows, :], xc_ref[rows, :])
        y = (jnp.dot(a_ref[rows, :], wa_ref[...], preferred_element_type=F32)
             + jnp.dot(b_ref[rows, :], wb_ref[...], preferred_element_type=F32) + bo_ref[...])
        xn = _layer_norm(DN_ALPHA * x + (1.0 + g1_ref[...]) * y, lng_ref[...], lnb_ref[...])
        xo_ref[rows, :] = xn
        u2s.append(xn * (1.0 + sc2_ref[...]) + sh2_ref[...])
    for t in range(OUT_SUB):
        rows = slice(t * TM, (t + 1) * TM)
        anchor = 0.0 * xo_ref[(t + 2) * TM - 1:(t + 2) * TM, :] if t + 1 < OUT_SUB else None
        _route_and_group(u2s[t], anchor, wrh_ref, wrl_ref, br_ref, upper_ref, lower_ref, sel_ref,
                         route_ref.at[rows, :], cnt_ref.at[t * 8:(t + 1) * 8, :],
                         xs_ref.at[t * LOCAL_ROWS:(t + 1) * LOCAL_ROWS, :])


def _route_and_group(u2, anchor, wrh_ref, wrl_ref, br_ref, upper_ref, lower_ref, sel_ref,
                     route_ref, cnt_ref, xs_ref):
    u_hi = u2.astype(BF16)
    u_lo = (u2 - u_hi.astype(F32)).astype(BF16)
    logits = (jnp.dot(u_hi, wrh_ref[...], preferred_element_type=F32)
              + jnp.dot(u_lo, wrh_ref[...], preferred_element_type=F32)
              + jnp.dot(u_hi, wrl_ref[...], preferred_element_type=F32) + br_ref[...])
    lane = lax.broadcasted_iota(jnp.int32, logits.shape, 1).astype(F32)
    ninf = -jnp.inf
    big = float(ROUTE_W)
    gl = jnp.where(lane < N_GROUPS, logits, ninf)
    gmax = jnp.max(gl, axis=-1, keepdims=True)
    gidx = jnp.min(jnp.where(gl == gmax, lane, big), axis=-1, keepdims=True)
    g_w = 1.0 / jnp.sum(jnp.exp(gl - gmax), axis=-1, keepdims=True)
    lo = N_GROUPS + EXP_PER_GROUP * gidx
    el = jnp.where(lane >= lo, jnp.where(lane < lo + EXP_PER_GROUP, logits, ninf), ninf)
    v1 = jnp.max(el, axis=-1, keepdims=True)
    i1 = jnp.min(jnp.where(el == v1, lane, big), axis=-1, keepdims=True)
    el2 = jnp.where(lane == i1, ninf, el)
    v2 = jnp.max(el2, axis=-1, keepdims=True)
    i2 = jnp.min(jnp.where(el2 == v2, lane, big), axis=-1, keepdims=True)
    e2 = jnp.exp(v2 - v1)
    w1 = g_w / (1.0 + e2)
    w2 = g_w * e2 / (1.0 + e2)
    onehot = [jnp.where(lane == i1 - N_GROUPS, 1.0, 0.0), jnp.where(lane == i2 - N_GROUPS, 1.0, 0.0)]
    cnt = [jnp.sum(o, axis=0, keepdims=True) for o in onehot]
    run_units = jnp.floor((cnt[0] + cnt[1] + (RUN_ALIGN - 1)) * (1.0 / RUN_ALIGN))
    below = RUN_ALIGN * jnp.dot(jnp.broadcast_to(run_units, (8, ROUTE_W)).astype(BF16), upper_ref[...],
                                preferred_element_type=F32)[0:1]
    lower = lower_ref[...]
    base = [below, below + cnt[0]]
    lp = []
    for s in range(2):
        earlier = jnp.dot(lower, onehot[s].astype(BF16), preferred_element_type=F32)
        lp.append(jnp.sum(onehot[s] * (base[s] + earlier), axis=-1, keepdims=True))
    rec = jnp.where(lane == 0.0, i1 - N_GROUPS,
                    jnp.where(lane == 1.0, i2 - N_GROUPS,
                              jnp.where(lane == 2.0, w1,
                                        jnp.where(lane == 3.0, w2,
                                                  jnp.where(lane == 4.0, lp[0],
                                                            jnp.where(lane == 5.0, lp[1], 0.0))))))
    route_ref[...] = rec
    cnt_ref[...] = jnp.broadcast_to(run_units * RUN_ALIGN, (8, ROUTE_W))
    sel = sel_ref[...]
    pos = lax.broadcasted_iota(jnp.int32, (LOCAL_ROWS, TM), 0).astype(F32)
    lp_lanes = []
    for s in range(2):
        hi = jnp.floor(lp[s] * (1.0 / 256.0))
        parts = jnp.where(lane == 0.0, lp[s] - 256.0 * hi, jnp.where(lane == 1.0, hi, 0.0)).astype(BF16)
        t = _nt_dot(sel, parts)
        lp_lanes.append(t[0:1] + 256.0 * t[1:2])
    perm = jnp.where(pos == lp_lanes[0], 1.0, jnp.where(pos == lp_lanes[1], 1.0, 0.0)).astype(BF16)
    xs = jnp.dot(perm, u_hi, preferred_element_type=F32)
    xs_ref[...] = xs if anchor is None else xs + anchor


def _outproj(n_tiles, mix_a, mix_b, w_a, w_b, b_out, x_lat, x_ctx, mods, ln_g, ln_b, w_r, b_r):
    rows = n_tiles * TM
    steps = n_tiles // OUT_SUB
    lat_steps = NT_LAT // OUT_SUB
    half = mix_a.shape[1]
    w_rh = w_r.astype(BF16)
    w_rl = (w_r - w_rh.astype(F32)).astype(BF16)
    upper = (jnp.arange(ROUTE_W)[:, None] < jnp.arange(ROUTE_W)[None, :]).astype(BF16)
    lower = (jnp.arange(TM)[:, None] > jnp.arange(TM)[None, :]).astype(BF16)
    sel = (jnp.arange(8)[:, None] == jnp.arange(ROUTE_W)[None, :]).astype(BF16)
    row = lambda w: pl.BlockSpec((OUT_SUB * TM, w), lambda i: (i, 0))
    mod = lambda chunk: pl.BlockSpec((None, None, 1, D_MODEL), lambda i: (_mod_row(i * OUT_SUB), chunk, 0, 0))
    vec = _full((1, D_MODEL))
    return pl.pallas_call(
        _outproj_kernel,
        grid=(steps,),
        in_specs=[row(half), row(half), _full((half, D_MODEL)), _full((half, D_MODEL)), vec,
                  pl.BlockSpec((OUT_SUB * TM, D_MODEL), lambda i: (jnp.minimum(i, lat_steps - 1), 0)),
                  pl.BlockSpec((OUT_SUB * TM, D_MODEL), lambda i: (jnp.maximum(i - lat_steps, 0), 0)),
                  mod(G1), mod(SH2), mod(SC2), vec, vec,
                  _full((D_MODEL, ROUTE_W)), _full((D_MODEL, ROUTE_W)), _full((1, ROUTE_W)),
                  _full((ROUTE_W, ROUTE_W)), _full((TM, TM)), _full((8, ROUTE_W))],
        out_specs=[row(D_MODEL), row(ROUTE_W), pl.BlockSpec((OUT_SUB * 8, ROUTE_W), lambda i: (i, 0)),
                   pl.BlockSpec((OUT_SUB * LOCAL_ROWS, D_MODEL), lambda i: (i, 0))],
        out_shape=[jax.ShapeDtypeStruct((rows, D_MODEL), F32),
                   jax.ShapeDtypeStruct((rows, ROUTE_W), F32),
                   jax.ShapeDtypeStruct((n_tiles * 8, ROUTE_W), F32),
                   jax.ShapeDtypeStruct((n_tiles * LOCAL_ROWS, D_MODEL), F32)],
        compiler_params=_params(),
        name="outproj_ln_router",
    )(mix_a, mix_b, w_a, w_b, b_out, x_lat, x_ctx, mods, mods, mods, ln_g, ln_b, w_rh, w_rl, b_r,
      upper, lower, sel)


def _aligned(i):
    return pl.multiple_of(i, RUN_ALIGN)


def _moe_kernel(te_ref, tk_ref, rows_ref, lo_ref, hi_ref, cnt_ref, bt_ref, be_ref, xs_hbm, w1_ref, w3_ref,
                w2_ref, ys_ref, xbuf, wb1, wb3, wb2, sem):
    j = pl.program_id(0)
    nt = pl.num_programs(0)
    slot = j % 2

    def issue(tile, slot_):
        e = te_ref[tile]
        first = tk_ref[tile] * TMM

        def body(i, carry):
            idx = i * N_EXPERTS + e
            start = bt_ref[idx]
            lo = jnp.maximum(start, first)
            n = jnp.minimum(start + cnt_ref[idx], first + TMM) - lo

            @pl.when(n > 0)
            def _():
                src = i * LOCAL_ROWS + be_ref[idx] + lo - start
                pltpu.make_async_copy(xs_hbm.at[pl.ds(_aligned(src), _aligned(n))],
                                      xbuf.at[slot_, pl.ds(_aligned(lo - first), _aligned(n))],
                                      sem.at[slot_]).start()
            return carry
        lax.fori_loop(lo_ref[tile], hi_ref[tile], body, 0)

    @pl.when(j == 0)
    def _():
        xbuf[...] = jnp.zeros_like(xbuf)
        issue(0, 0)

    @pl.when(j + 1 < nt)
    def _():
        issue(j + 1, 1 - slot)

    @pl.when(jnp.logical_or(j == 0, te_ref[j] != te_ref[jnp.maximum(j - 1, 0)]))
    def _():
        wb1[...] = w1_ref[...].astype(BF16)
        wb3[...] = w3_ref[...].astype(BF16)
        wb2[...] = w2_ref[...].astype(BF16)

    n_real = rows_ref[j]

    @pl.when(n_real > 0)
    def _():
        pltpu.make_async_copy(xs_hbm.at[pl.ds(0, _aligned(n_real))], xbuf.at[slot, pl.ds(0, _aligned(n_real))],
                              sem.at[slot]).wait()
        x = xbuf[slot].astype(BF16)
        h1 = jnp.dot(x, wb1[...], preferred_element_type=F32)
        h3 = jnp.dot(x, wb3[...], preferred_element_type=F32)
        hid = h1 * _sigmoid(h1) * h3
        ys_ref[...] = jnp.dot(hid.astype(BF16), wb2[...], preferred_element_type=F32)

    @pl.when(n_real == 0)
    def _():
        ys_ref[...] = jnp.zeros_like(ys_ref)


def _moe_experts(layer, plan, xs_local, w1, w3, w2):
    tile_expert, tile_k, tile_rows, src_lo, src_hi, cnt, before_tile, before_expert, _, _ = plan
    nt = tile_expert.shape[0]
    wmap = lambda j, te, *_: (layer, te[j], 0, 0)
    grid_spec = pltpu.PrefetchScalarGridSpec(
        num_scalar_prefetch=8,
        grid=(nt,),
        in_specs=[pl.BlockSpec(memory_space=pl.ANY),
                  pl.BlockSpec((None, None, D_MODEL, EXPERT_FF), wmap),
                  pl.BlockSpec((None, None, D_MODEL, EXPERT_FF), wmap),
                  pl.BlockSpec((None, None, EXPERT_FF, D_MODEL), wmap)],
        out_specs=pl.BlockSpec((TMM, D_MODEL), lambda j, *_: (j, 0)),
        scratch_shapes=[pltpu.VMEM((2, TMM, D_MODEL), F32),
                        pltpu.VMEM((D_MODEL, EXPERT_FF), BF16),
                        pltpu.VMEM((D_MODEL, EXPERT_FF), BF16),
                        pltpu.VMEM((EXPERT_FF, D_MODEL), BF16),
                        pltpu.SemaphoreType.DMA((2,))])
    return pl.pallas_call(
        _moe_kernel,
        grid_spec=grid_spec,
        out_shape=jax.ShapeDtypeStruct((nt * TMM, D_MODEL), F32),
        compiler_params=_params(),
        name="moe_experts",
    )(tile_expert, tile_k, tile_rows, src_lo, src_hi, cnt, before_tile, before_expert, xs_local, w1, w3, w2)


def _combine_kernel(cnt_ref, bt_ref, be_ref, gs_ref, used_ref, ys_hbm, x_ref, route_ref, g2_ref, lng_ref, lnb_ref,
                    o_ref, ybuf, sem):
    i = pl.program_id(0)
    nt = pl.num_programs(0)
    slot = i % 2

    def issue(tile, slot_):
        def body(e, carry):
            idx = tile * N_EXPERTS + e
            n = cnt_ref[idx]

            @pl.when(n > 0)
            def _():
                pltpu.make_async_copy(ys_hbm.at[pl.ds(_aligned(gs_ref[e] + bt_ref[idx]), _aligned(n))],
                                      ybuf.at[slot_, pl.ds(_aligned(be_ref[idx]), _aligned(n))],
                                      sem.at[slot_]).start()
            return carry
        lax.fori_loop(0, N_EXPERTS, body, 0)

    @pl.when(i == 0)
    def _():
        ybuf[...] = jnp.zeros_like(ybuf)
        issue(0, 0)

    @pl.when(i + 1 < nt)
    def _():
        issue(i + 1, 1 - slot)

    used = _aligned(used_ref[i])
    pltpu.make_async_copy(ys_hbm.at[pl.ds(0, used)], ybuf.at[slot, pl.ds(0, used)], sem.at[slot]).wait()
    route = route_ref[...]
    pos = lax.broadcasted_iota(jnp.int32, (TM, LOCAL_ROWS), 1).astype(F32)
    y = ybuf[slot].astype(BF16)
    picked = [jnp.dot(jnp.where(pos == route[:, 4 + s:5 + s], 1.0, 0.0).astype(BF16), y,
                      preferred_element_type=F32) for s in range(2)]
    f = route[:, 2:3] * picked[0] + route[:, 3:4] * picked[1]
    z = DN_ALPHA * x_ref[...] + (1.0 + g2_ref[...]) * f
    o_ref[...] = _layer_norm(z, lng_ref[...], lnb_ref[...])


def _moe_combine(n_tiles, plan, ys, x_all, route, mods, ln_g, ln_b):
    _, _, _, _, _, cnt, before_tile, before_expert, group_start, used = plan
    rows = n_tiles * TM
    row = lambda w: pl.BlockSpec((TM, w), lambda i, *_: (i, 0))
    vec = pl.BlockSpec((1, D_MODEL), lambda i, *_: (0, 0))
    grid_spec = pltpu.PrefetchScalarGridSpec(
        num_scalar_prefetch=5,
        grid=(n_tiles,),
        in_specs=[pl.BlockSpec(memory_space=pl.ANY), row(D_MODEL), row(ROUTE_W),
                  pl.BlockSpec((None, None, 1, D_MODEL), lambda i, *_: (_mod_row(i), G2, 0, 0)), vec, vec],
        out_specs=row(D_MODEL),
        scratch_shapes=[pltpu.VMEM((2, LOCAL_ROWS, D_MODEL), F32), pltpu.SemaphoreType.DMA((2,))])
    return pl.pallas_call(
        _combine_kernel,
        grid_spec=grid_spec,
        out_shape=jax.ShapeDtypeStruct((rows, D_MODEL), F32),
        compiler_params=_params(),
        name="moe_combine_ln",
    )(cnt, before_tile, before_expert, group_start, used, ys, x_all, route, mods, ln_g, ln_b)


def _moe_plan(cnt_rec, n_tiles):
    cnt = cnt_rec.reshape(n_tiles, 8, ROUTE_W)[:, 0, :N_EXPERTS].astype(jnp.int32)
    nt_max = (n_tiles * (2 * TM + N_EXPERTS * (RUN_ALIGN - 1))) // TMM + N_EXPERTS
    total = jnp.sum(cnt, axis=0)
    tiles_e = (total + TMM - 1) // TMM
    tile_end = jnp.cumsum(tiles_e)
    first_tile = tile_end - tiles_e
    before_tile = jnp.cumsum(cnt, axis=0) - cnt
    before_expert = jnp.cumsum(cnt, axis=1) - cnt
    tile_id = jnp.arange(nt_max, dtype=jnp.int32)
    tile_expert = jnp.minimum(jnp.sum((tile_id[:, None] >= tile_end[None, :]).astype(jnp.int32), axis=1),
                              N_EXPERTS - 1)
    onehot = (tile_expert[:, None] == jnp.arange(N_EXPERTS, dtype=jnp.int32)[None, :]).astype(jnp.int32)
    tile_k = tile_id - jnp.sum(onehot * first_tile[None, :], axis=1)
    tile_rows = jnp.clip(jnp.sum(onehot * total[None, :], axis=1) - tile_k * TMM, 0, TMM)
    first = (tile_k * TMM)[:, None]
    run_start = jnp.sum(onehot[:, None, :] * before_tile[None, :, :], axis=2)
    run_end = run_start + jnp.sum(onehot[:, None, :] * cnt[None, :, :], axis=2)
    src_lo = jnp.sum((run_end <= first).astype(jnp.int32), axis=1)
    src_hi = jnp.sum((run_start < first + TMM).astype(jnp.int32), axis=1)
    return (tile_expert, tile_k, tile_rows, src_lo, src_hi, cnt.reshape(-1), before_tile.reshape(-1),
            before_expert.reshape(-1), first_tile * TMM, jnp.sum(cnt, axis=1))


def _router_weights(w_rg, b_rg, w_re, b_re):
    w = jnp.concatenate([w_rg, jnp.transpose(w_re, (1, 0, 2)).reshape(D_MODEL, N_EXPERTS)], axis=1)
    b = jnp.concatenate([b_rg, b_re.reshape(-1)])
    pad = ROUTE_W - w.shape[1]
    return jnp.pad(w, ((0, 0), (0, pad))), jnp.pad(b, (0, pad)).reshape(1, ROUTE_W)


def _proj1_kernel(x_ref, sh_ref, sc_ref, w_ref, b_ref, cos_ref, sin_ref, cosm_ref, sinm_ref, cosr_ref,
                  sinr_ref, gq_ref, gk_ref, gqc_ref, gkv_ref, avg_ref, wuq_ref, wuk_ref, wuv_ref, vplace_ref,
                  q_ref, qm_ref, k_ref, v_ref, km_ref, vm_ref):
    u = x_ref[...] * (1.0 + sc_ref[...]) + sh_ref[...]
    y = jnp.dot(u.astype(BF16), w_ref[...], preferred_element_type=F32) + b_ref[...]
    c_q = GQA_HEADS * HEAD_DIM
    c_qc = c_q + MLA_Q_RANK
    c_k = c_qc + GQA_KV_HEADS * HEAD_DIM
    c_v = c_k + GQA_KV_HEADS * HEAD_DIM
    c_kv = c_v + MLA_KV_RANK
    avg = avg_ref[...]

    def head_rms(t, gain):
        sq = t * t
        hi = sq.astype(BF16)
        lo = (sq - hi.astype(F32)).astype(BF16)
        a = avg[:t.shape[1], :t.shape[1]]
        ms = jnp.dot(hi, a, preferred_element_type=F32) + jnp.dot(lo, a, preferred_element_type=F32)
        return t * lax.rsqrt(ms + RMS_EPS) * gain

    def row_rms(t, gain):
        ms = jnp.mean(t * t, axis=-1, keepdims=True)
        return t * lax.rsqrt(ms + RMS_EPS) * gain

    cos = cos_ref[...]
    sin = sin_ref[...]
    cos4 = jnp.concatenate([cos] * 4, axis=1)
    sin4 = jnp.concatenate([sin] * 4, axis=1)
    q = _rope(head_rms(y[:, :c_q], gq_ref[...]), cos4, sin4, HEAD_DIM // 4) * (HEAD_DIM ** -0.5 * LOG2E)
    q_ref[...] = q.astype(BF16)
    k = _rope(head_rms(y[:, c_qc:c_k], gk_ref[...]), cos, sin, HEAD_DIM // 4)
    k_ref[...] = k.astype(BF16)

    v_ref[...] = _transposed_values(vplace_ref[...], y[:, c_k:c_v].astype(BF16), HEAD_DIM)

    qc = row_rms(y[:, c_q:c_qc], gqc_ref[...]).astype(BF16)
    qm = jnp.dot(qc, wuq_ref[...], preferred_element_type=F32)
    cosm = jnp.concatenate([cosm_ref[...]] * MLA_HEADS, axis=1)
    sinm = jnp.concatenate([sinm_ref[...]] * MLA_HEADS, axis=1)
    qm = _rope(qm, cosm, sinm, MLA_ROPE // 4) * ((MLA_NOPE + MLA_ROPE) ** -0.5 * LOG2E)
    qm_ref[...] = qm.astype(BF16)

    kvn = row_rms(y[:, c_v:c_kv], gkv_ref[...]).astype(BF16)
    kr = _rope(y[:, c_kv:], cosr_ref[...], sinr_ref[...], MLA_ROPE // 4).astype(BF16)
    km = jnp.dot(jnp.concatenate([kvn, kr], axis=1), wuk_ref[...], preferred_element_type=F32)
    km_ref[...] = km.astype(BF16)
    vm_ref[...] = _transposed_values(wuv_ref[...], kvn, MLA_V)


def _proj1(x_all, mods, w_in, b_in, tabs, gq, gk, gqc, gkv, avg, wuq, wuk, wuv):
    cos_hd, sin_hd, cos_m, sin_m, cos_r, sin_r = tabs
    kvw = GQA_KV_HEADS * HEAD_DIM
    qw = GQA_HEADS * HEAD_DIM
    mw = MLA_HEADS * MLA_PAD
    vw = MLA_HEADS * VAL_PAD
    gvw = GQA_KV_HEADS * VAL_PAD
    row = lambda w: pl.BlockSpec((TM, w), lambda i: (i, 0))
    col = lambda h: pl.BlockSpec((h, TM), lambda i: (0, i))
    tab = pl.BlockSpec((TM, 128), lambda i: (_rope_row_block(i), 0))
    vplace = _value_placement(GQA_KV_HEADS, HEAD_DIM)
    return pl.pallas_call(
        _proj1_kernel,
        grid=(NT_ALL,),
        in_specs=[row(D_MODEL), _mod_spec(SH1), _mod_spec(SC1),
                  _full((D_MODEL, ODD_IN_PAD)), _full((1, ODD_IN_PAD)), tab, tab, tab, tab, tab, tab,
                  _full((1, qw)), _full((1, kvw)), _full((1, MLA_Q_RANK)), _full((1, MLA_KV_RANK)),
                  _full((qw, qw)), _full((MLA_Q_RANK, mw)), _full((MLA_KV_RANK + 128, mw)),
                  _full((vw, MLA_KV_RANK)), _full((gvw, kvw))],
        out_specs=[row(qw), row(mw), row(kvw), col(gvw), row(mw), col(vw)],
        out_shape=[jax.ShapeDtypeStruct((R_ALL, qw), BF16),
                   jax.ShapeDtypeStruct((R_ALL, mw), BF16),
                   jax.ShapeDtypeStruct((R_ALL, kvw), BF16),
                   jax.ShapeDtypeStruct((gvw, R_ALL), BF16),
                   jax.ShapeDtypeStruct((R_ALL, mw), BF16),
                   jax.ShapeDtypeStruct((vw, R_ALL), BF16)],
        compiler_params=_params(),
        name="proj1",
    )(x_all, mods, mods, w_in, b_in, cos_hd, sin_hd, cos_m, sin_m, cos_r, sin_r,
      gq, gk, gqc, gkv, avg, wuq, wuk, wuv.T, vplace)


def _dense_kernel(q_ref, kl_ref, kc_ref, vl_ref, vc_ref, o_ref, s_buf, p_buf, *, n_heads, group, stack, dk, dv):
    tq = q_ref.shape[0]
    units = []
    for h0 in range(0, n_heads, stack):
        kv = h0 // group
        qs = [q_ref[:, h * dk:(h + 1) * dk] for h in range(h0, h0 + stack)]
        q = qs[0] if stack == 1 else jnp.concatenate(qs, axis=0)
        ks = slice(kv * dk, (kv + 1) * dk)
        vs = slice(kv * VAL_PAD, (kv + 1) * VAL_PAD)
        parts = [slice(c * (SEQ // KEY_PARTS), (c + 1) * (SEQ // KEY_PARTS)) for c in range(KEY_PARTS)]
        units.append((q, [kl_ref[c, ks] for c in parts] + [kc_ref[:, ks]],
                      [vl_ref[vs, c] for c in parts] + [vc_ref[vs, :]]))
    for u, o_t in enumerate(_attend_keys_major(units, dv, s_buf, p_buf)):
        o = o_t.T
        for g in range(stack):
            h = u * stack + g
            o_ref[:, h * dv:(h + 1) * dv] = o[g * tq:(g + 1) * tq].astype(BF16)


def _dense_attention(q, k, v, *, n_heads, group, stack, dk, dv, tq, name):
    n_kv = n_heads // group
    nq = SEQ // tq
    ctx0 = R_LAT // CTX_LEN
    lat = lambda w: pl.BlockSpec((SEQ, w), lambda b, j: (b, 0), pipeline_mode=pl.Buffered(1))
    ctx = lambda w: pl.BlockSpec((CTX_LEN, w), lambda b, j: (ctx0 + b, 0))
    lat_t = pl.BlockSpec((n_kv * VAL_PAD, SEQ), lambda b, j: (0, b), pipeline_mode=pl.Buffered(1))
    ctx_t = pl.BlockSpec((n_kv * VAL_PAD, CTX_LEN), lambda b, j: (0, ctx0 + b))
    return pl.pallas_call(
        functools.partial(_dense_kernel, n_heads=n_heads, group=group, stack=stack, dk=dk, dv=dv),
        grid=(BATCH, nq),
        in_specs=[pl.BlockSpec((tq, n_heads * dk), lambda b, j: (b * nq + j, 0)),
                  lat(n_kv * dk), ctx(n_kv * dk), lat_t, ctx_t],
        out_specs=pl.BlockSpec((tq, n_heads * dv), lambda b, j: (b * nq + j, 0)),
        out_shape=jax.ShapeDtypeStruct((R_LAT, n_heads * dv), BF16),
        scratch_shapes=[pltpu.VMEM((SCORE_SLOTS, SEQ + CTX_LEN, stack * tq), F32),
                        pltpu.VMEM((2, SEQ + CTX_LEN, stack * tq), BF16)],
        compiler_params=_params(),
        name=name,
    )(q, k, k, v, v)


def _mla_weights(w_uq, w_ukv):
    wq = w_uq.reshape(MLA_Q_RANK, MLA_HEADS, MLA_NOPE + MLA_ROPE)
    wq = jnp.pad(wq, ((0, 0), (0, 0), (0, MLA_PAD - MLA_NOPE - MLA_ROPE))).reshape(MLA_Q_RANK, -1)
    wkv = w_ukv.reshape(MLA_KV_RANK, MLA_HEADS, MLA_NOPE + MLA_V)
    wk = jnp.pad(wkv[:, :, :MLA_NOPE], ((0, 0), (0, 0), (0, MLA_PAD - MLA_NOPE))).reshape(MLA_KV_RANK, -1)
    wv = jnp.pad(wkv[:, :, MLA_NOPE:], ((0, 0), (0, 0), (0, VAL_PAD - MLA_V))).reshape(MLA_KV_RANK, -1)
    r = jnp.arange(128)[:, None]
    c = jnp.arange(MLA_HEADS * MLA_PAD)[None, :]
    place = jnp.logical_and(r < MLA_ROPE, (c % MLA_PAD) == MLA_NOPE + r).astype(F32)
    wk = jnp.concatenate([wk, place], axis=0)
    return wq.astype(BF16), wk.astype(BF16), wv.astype(BF16)


def kernel(x, c, ctx, c_ctx, even_w_in, even_b_in, even_conv_w, even_conv_b, even_conv_ln_g, even_conv_ln_b, even_sink, even_w_out, even_b_out, odd_w_in, odd_b_in, odd_q_norm, odd_k_norm, odd_mla_q_norm, odd_mla_kv_norm, odd_mla_w_uq, odd_mla_w_ukv, odd_w_out, odd_b_out, ada_w, ada_b, ln1_g, ln1_b, ln2_g, ln2_b, moe_w_rg, moe_b_rg, moe_w_re, moe_b_re, moe_w1, moe_w3, moe_w2):
    vec = lambda a: a.reshape(1, -1)
    x_lat0 = x.reshape(R_LAT, D_MODEL)
    x_ctx0 = ctx.reshape(R_CTX, D_MODEL)

    cv =jnp.concatenate([c, c_ctx[None, :], jnp.zeros((8 - BATCH - 1, D_MODEL), F32)], axis=0)
    mods = _ada_table(cv, ada_w, ada_b).reshape(DEPTH, 8, 6, 1, D_MODEL)

    cos64, sin64 = _rope_tables(HEAD_DIM)
    cos_hd, sin_hd = _pad_table(cos64, sin64, 0, HEAD_DIM, 128)
    cos32, sin32 = _rope_tables(MLA_ROPE)
    cos_m, sin_m = _pad_table(cos32, sin32, MLA_NOPE, MLA_PAD, 128)
    cos_r, sin_r = _pad_table(cos32, sin32, 0, 128, 128)

    m0 = mods[0]
    h, q0, k0, v0 = _proj0(x_lat0, x_ctx0, m0, even_w_in[0].astype(BF16), vec(even_b_in[0]), cos_hd, sin_hd)
    conv_out = _conv(h, even_conv_w[0].reshape(CONV_WIDTH, CONV_CH), vec(even_conv_b[0]),
                     vec(even_conv_ln_g[0]), vec(even_conv_ln_b[0]))
    attn = _win_attention(even_sink[0], q0, k0, v0)
    w_out = even_w_out[0].astype(BF16)
    w_r, b_r = _router_weights(moe_w_rg[0], moe_b_rg[0], moe_w_re[0], moe_b_re[0])
    x_all, route, cnt_rec, xs_local = _outproj(
        NT_ALL, conv_out, attn, w_out[:CONV_CH], w_out[CONV_CH:], vec(even_b_out[0]),
        x_lat0, x_ctx0, m0, vec(ln1_g[0]), vec(ln1_b[0]), w_r, b_r)
    plan = _moe_plan(cnt_rec, NT_ALL)
    ys = _moe_experts(0, plan, xs_local, moe_w1, moe_w3, moe_w2)
    x_all = _moe_combine(NT_ALL, plan, ys, x_all, route, m0, vec(ln2_g[0]), vec(ln2_b[0]))

    m1 = mods[1]
    w_in1 = jnp.pad(odd_w_in[0], ((0, 0), (0, ODD_IN_PAD - ODD_IN))).astype(BF16)
    b_in1 = jnp.pad(odd_b_in[0], (0, ODD_IN_PAD - ODD_IN)).reshape(1, -1)
    wuq, wuk, wuv = _mla_weights(odd_mla_w_uq[0], odd_mla_w_ukv[0])
    qw = GQA_HEADS * HEAD_DIM
    hid = jnp.arange(qw) // HEAD_DIM
    avg = ((hid[:, None] == hid[None, :]).astype(F32) / HEAD_DIM).astype(BF16)
    q1, qm, k1, v1, km, vm = _proj1(
        x_all, m1, w_in1, b_in1, (cos_hd, sin_hd, cos_m, sin_m, cos_r, sin_r),
        vec(jnp.tile(odd_q_norm[0], GQA_HEADS)), vec(jnp.tile(odd_k_norm[0], GQA_KV_HEADS)),
        vec(odd_mla_q_norm[0]), vec(odd_mla_kv_norm[0]), avg, wuq, wuk, wuv)
    o_g = _dense_attention(q1, k1, v1, n_heads=GQA_HEADS, group=GQA_HEADS // GQA_KV_HEADS, stack=2,
                           dk=HEAD_DIM, dv=HEAD_DIM, tq=256, name="gqa_attention")
    o_m = _dense_attention(qm, km, vm, n_heads=MLA_HEADS, group=1, stack=1, dk=MLA_PAD, dv=MLA_V, tq=256,
                           name="mla_attention")
    w_out = odd_w_out[0].astype(BF16)
    w_r, b_r = _router_weights(moe_w_rg[1], moe_b_rg[1], moe_w_re[1], moe_b_re[1])
    x_lat, route, cnt_rec, xs_local = _outproj(
        NT_LAT, o_g, o_m, w_out[:qw], w_out[qw:], vec(odd_b_out[0]),
        x_all, x_all, m1, vec(ln1_g[1]), vec(ln1_b[1]), w_r, b_r)
    plan = _moe_plan(cnt_rec, NT_LAT)
    ys = _moe_experts(1, plan, xs_local, moe_w1, moe_w3, moe_w2)
    x_lat = _moe_combine(NT_LAT, plan, ys, x_lat, route, m1, vec(ln2_g[1]), vec(ln2_b[1]))
    return x_lat.reshape(BATCH, SEQ, D_MODEL)
```

```python
import functools

import jax
import jax.numpy as jnp
from jax import lax
from jax.experimental import pallas as pl
from jax.experimental.pallas import tpu as pltpu

F32 = jnp.float32
BF16 = jnp.bfloat16

D_MODEL = 1024
BATCH = 4
SEQ = 4096
DEPTH = 2
GRID_W = 64
CTX_LEN = 256
HEAD_DIM = 64
ROPE_THETA = 10000.0
LN_EPS = 1e-5
RMS_EPS = 1e-6
NEG_INF = -1e30

CONV_CH = 512
CONV_WIDTH = 31
WIN_HEADS = 8
WIN_KV_HEADS = 2
WINDOW = 128
GQA_HEADS = 8
GQA_KV_HEADS = 2
MLA_HEADS = 8
MLA_Q_RANK = 256
MLA_KV_RANK = 128
MLA_NOPE = 64
MLA_ROPE = 32
MLA_V = 64
N_GROUPS = 4
EXP_PER_GROUP = 8
N_EXPERTS = N_GROUPS * EXP_PER_GROUP
EXPERT_FF = 512
DN_ALPHA = float((2 * DEPTH) ** 0.25)

EVEN_IN = 2 * CONV_CH + (WIN_HEADS + 2 * WIN_KV_HEADS) * HEAD_DIM
ODD_IN = 1184
ODD_IN_PAD = 1280
MLA_PAD = 128
VAL_PAD = 128
WIN_STACK = 1
SCORE_SLOTS = 2
KEY_PARTS = 4
LOG2E = 1.4426950408889634

R_LAT = BATCH * SEQ
R_CTX = BATCH * CTX_LEN
R_ALL = R_LAT + R_CTX
TM = 256
NT_LAT = R_LAT // TM
NT_ALL = R_ALL // TM
TILES_PER_SEQ = SEQ // TM
HALO = 16
CONV_CHUNK = 32
SHIFTS = 8
OUT_SUB = 2
TMM = 512
ROUTE_W = 128
RUN_ALIGN = 8
LOCAL_ROWS = 768
VMEM_LIMIT = 56 * 1024 * 1024

SH1, SC1, G1, SH2, SC2, G2 = range(6)


def _sigmoid(x):
    return 1.0 / (1.0 + jnp.exp(-x))


def _layer_norm(z, g, b):
    mu = jnp.mean(z, axis=-1, keepdims=True)
    zc = z - mu
    var = jnp.mean(zc * zc, axis=-1, keepdims=True)
    return zc * lax.rsqrt(var + LN_EPS) * g + b


def _rope(x, cos, sin, half):
    n = x.shape[-1]
    lane = lax.broadcasted_iota(jnp.int32, x.shape, 1)
    first = (lane % (2 * half)) < half
    partner = jnp.where(first, pltpu.roll(x, n - half, 1), pltpu.roll(x, half, 1))
    return x * cos + partner * sin


def _mod_row(i):
    return jnp.where(i < NT_LAT, i // TILES_PER_SEQ, BATCH)


def _mod_spec(chunk):
    return pl.BlockSpec((None, None, 1, D_MODEL), lambda i: (_mod_row(i), chunk, 0, 0))


def _rope_row_block(i):
    return jnp.where(i < NT_LAT, i % TILES_PER_SEQ, TILES_PER_SEQ)


def _full(shape):
    nd = len(shape)
    return pl.BlockSpec(shape, lambda *_: (0,) * nd)


def _params():
    return pltpu.CompilerParams(vmem_limit_bytes=VMEM_LIMIT)


def _ada_kernel(cv_ref, w_ref, b_ref, o_ref):
    cv = cv_ref[...]
    s = cv * _sigmoid(cv)
    o_ref[...] = jnp.dot(s, w_ref[...], precision=lax.Precision.HIGHEST,
                         preferred_element_type=F32) + b_ref[...]


def _ada_table(cv, ada_w, ada_b):
    bn = 1536
    nb = (6 * D_MODEL) // bn
    return pl.pallas_call(
        _ada_kernel,
        grid=(DEPTH, nb),
        in_specs=[pl.BlockSpec((8, D_MODEL), lambda l, j: (0, 0)),
                  pl.BlockSpec((None, D_MODEL, bn), lambda l, j: (l, 0, j)),
                  pl.BlockSpec((None, 1, bn), lambda l, j: (l, 0, j))],
        out_specs=pl.BlockSpec((None, 8, bn), lambda l, j: (l, 0, j)),
        out_shape=jax.ShapeDtypeStruct((DEPTH, 8, 6 * D_MODEL), F32),
        compiler_params=_params(),
        name="ada_table",
    )(cv, ada_w, ada_b.reshape(DEPTH, 1, 6 * D_MODEL))


def _rope_tables(rot_dim):
    axis_dim = rot_dim // 2
    inv_freq = ROPE_THETA ** (-jnp.arange(0, axis_dim, 2, dtype=F32) / axis_dim)
    t = jnp.arange(SEQ)
    ang_r = (t // GRID_W).astype(F32)[:, None] * inv_freq[None, :]
    ang_c = (t % GRID_W).astype(F32)[:, None] * inv_freq[None, :]
    cos = jnp.concatenate([jnp.cos(ang_r), jnp.cos(ang_r), jnp.cos(ang_c), jnp.cos(ang_c)], axis=-1)
    sin = jnp.concatenate([-jnp.sin(ang_r), jnp.sin(ang_r), -jnp.sin(ang_c), jnp.sin(ang_c)], axis=-1)
    return cos, sin


def _pad_table(cos, sin, lead, period, width):
    rot = cos.shape[1]
    one = jnp.ones((SEQ, period), F32).at[:, lead:lead + rot].set(cos)
    zero = jnp.zeros((SEQ, period), F32).at[:, lead:lead + rot].set(sin)
    cos_w = jnp.tile(one, (1, width // period))
    sin_w = jnp.tile(zero, (1, width // period))
    cos_w = jnp.concatenate([cos_w, jnp.ones((TM, width), F32)], axis=0)
    sin_w = jnp.concatenate([sin_w, jnp.zeros((TM, width), F32)], axis=0)
    return cos_w, sin_w


def _proj0_kernel(xl_ref, xc_ref, sh_ref, sc_ref, w_ref, b_ref, cos_ref, sin_ref, vplace_ref,
                  h_ref, q_ref, k_ref, v_ref):
    x = jnp.where(pl.program_id(0) < NT_LAT, xl_ref[...], xc_ref[...])
    u = x * (1.0 + sc_ref[...]) + sh_ref[...]
    y = jnp.dot(u.astype(BF16), w_ref[...], preferred_element_type=F32) + b_ref[...]
    h_ref[...] = y[:, :CONV_CH] * _sigmoid(y[:, CONV_CH:2 * CONV_CH])
    cos = cos_ref[...]
    sin = sin_ref[...]
    q0 = 2 * CONV_CH
    k0 = q0 + WIN_HEADS * HEAD_DIM
    v0 = k0 + WIN_KV_HEADS * HEAD_DIM
    cos4 = jnp.concatenate([cos] * 4, axis=1)
    sin4 = jnp.concatenate([sin] * 4, axis=1)
    q = _rope(y[:, q0:k0], cos4, sin4, HEAD_DIM // 4) * (HEAD_DIM ** -0.5 * LOG2E)
    q_ref[...] = q.astype(BF16)
    k_ref[...] = _rope(y[:, k0:v0], cos, sin, HEAD_DIM // 4).astype(BF16)
    v_ref[...] = _transposed_values(vplace_ref[...], y[:, v0:].astype(BF16), HEAD_DIM)


def _proj0(x_lat, x_ctx, mods, w_in, b_in, cos_hd, sin_hd):
    kvw = WIN_KV_HEADS * HEAD_DIM
    row = lambda w: pl.BlockSpec((TM, w), lambda i: (i, 0))
    tab = pl.BlockSpec((TM, 128), lambda i: (_rope_row_block(i), 0))
    return pl.pallas_call(
        _proj0_kernel,
        grid=(NT_ALL,),
        in_specs=[pl.BlockSpec((TM, D_MODEL), lambda i: (jnp.minimum(i, NT_LAT - 1), 0)),
                  pl.BlockSpec((TM, D_MODEL), lambda i: (jnp.maximum(i - NT_LAT, 0), 0)),
                  _mod_spec(SH1), _mod_spec(SC1),
                  _full((D_MODEL, EVEN_IN)), _full((1, EVEN_IN)), tab, tab,
                  _full((WIN_KV_HEADS * VAL_PAD, kvw))],
        out_specs=[row(CONV_CH), row(WIN_HEADS * HEAD_DIM), row(kvw),
                   pl.BlockSpec((WIN_KV_HEADS * VAL_PAD, TM), lambda i: (0, i))],
        out_shape=[jax.ShapeDtypeStruct((R_ALL, CONV_CH), F32),
                   jax.ShapeDtypeStruct((R_ALL, WIN_HEADS * HEAD_DIM), BF16),
                   jax.ShapeDtypeStruct((R_ALL, kvw), BF16),
                   jax.ShapeDtypeStruct((WIN_KV_HEADS * VAL_PAD, R_ALL), BF16)],
        compiler_params=_params(),
        name="proj0",
    )(x_lat, x_ctx, mods, mods, w_in, b_in, cos_hd, sin_hd, _value_placement(WIN_KV_HEADS, HEAD_DIM))


def _conv_kernel(prev_ref, cur_ref, next_ref, w_ref, cb_ref, g_ref, b_ref, o_ref, buf):
    i = pl.program_id(0)
    is_ctx = i >= NT_LAT
    first = jnp.logical_or(is_ctx, i % TILES_PER_SEQ == 0)
    last = jnp.logical_or(is_ctx, i % TILES_PER_SEQ == TILES_PER_SEQ - 1)
    buf[0, 0:HALO, :] = jnp.where(first, 0.0, prev_ref[...])
    buf[0, HALO:HALO + TM, :] = cur_ref[...]
    buf[0, HALO + TM:, :] = jnp.where(last, 0.0, next_ref[...])
    span = TM + 2 * HALO - SHIFTS
    for r in range(1, SHIFTS):
        buf[r, 0:span, :] = buf[0, r:r + span, :]
    off = HALO - CONV_WIDTH // 2
    for c in range(TM // CONV_CHUNK):
        r0 = c * CONV_CHUNK
        acc = jnp.zeros((CONV_CHUNK, CONV_CH), F32)
        for k in range(CONV_WIDTH):
            r = (off + k) % SHIFTS
            base = r0 + off + k - r
            w = w_ref[k * SHIFTS:(k + 1) * SHIFTS, :]
            acc = acc + buf[r, base:base + CONV_CHUNK, :] * jnp.concatenate([w] * (CONV_CHUNK // SHIFTS), axis=0)
        z = _layer_norm(acc + cb_ref[...], g_ref[...], b_ref[...])
        o_ref[r0:r0 + CONV_CHUNK, :] = (z * _sigmoid(z)).astype(BF16)


def _conv(h, conv_w, conv_b, ln_g, ln_b):
    nh = R_ALL // HALO
    per = TM // HALO
    vec = _full((1, CONV_CH))
    return pl.pallas_call(
        _conv_kernel,
        grid=(NT_ALL,),
        in_specs=[pl.BlockSpec((HALO, CONV_CH), lambda i: (jnp.maximum(i * per - 1, 0), 0)),
                  pl.BlockSpec((TM, CONV_CH), lambda i: (i, 0)),
                  pl.BlockSpec((HALO, CONV_CH), lambda i: (jnp.minimum((i + 1) * per, nh - 1), 0)),
                  _full((CONV_WIDTH * SHIFTS, CONV_CH)), vec, vec, vec],
        out_specs=pl.BlockSpec((TM, CONV_CH), lambda i: (i, 0)),
        out_shape=jax.ShapeDtypeStruct((R_ALL, CONV_CH), BF16),
        scratch_shapes=[pltpu.VMEM((SHIFTS, TM + 2 * HALO, CONV_CH), F32)],
        compiler_params=_params(),
        name="conv_module",
    )(h, h, h, jnp.repeat(conv_w, SHIFTS, axis=0), conv_b, ln_g, ln_b)


def _nt_dot(a, b):
    return lax.dot_general(a, b, (((1,), (1,)), ((), ())), preferred_element_type=F32)


def _transposed_values(w_t, src, dv):
    vt = _nt_dot(w_t, src)
    r = lax.broadcasted_iota(jnp.int32, vt.shape, 0)
    return jnp.where(r % VAL_PAD == dv, 1.0, vt).astype(BF16)


def _value_placement(n_kv, dv):
    r = jnp.arange(n_kv * VAL_PAD)[:, None]
    c = jnp.arange(n_kv * dv)[None, :]
    return jnp.logical_and(r // VAL_PAD == c // dv, r % VAL_PAD == c % dv).astype(BF16)


def _attend_keys_major(units, dv, s_buf, p_buf):
    def scores(unit, slot):
        q, ks, _ = unit
        row, ms = 0, []
        for k in ks:
            s = _nt_dot(k, q)
            s_buf[slot, row:row + k.shape[0], :] = s
            ms.append(jnp.max(s, axis=0, keepdims=True))
            row += k.shape[0]
        return functools.reduce(jnp.maximum, ms)

    def run_next_scores_with(slot, nxt_slot, pieces):
        row = 0
        for k in pieces:
            tile = (slice(row + k.shape[0] - 8, row + k.shape[0]), slice(0, 128))
            s_buf[slot, tile[0], tile[1]] = s_buf[slot, tile[0], tile[1]] + 0.0 * s_buf[nxt_slot, tile[0], tile[1]]
            row += k.shape[0]

    n_s, n_p = s_buf.shape[0], p_buf.shape[0]
    results = []
    m = scores(units[0], 0)
    for idx, unit in enumerate(units):
        slot, nxt_slot, pslot = idx % n_s, (idx + 1) % n_s, idx % n_p
        m_next = None
        if idx + 1 < len(units):
            m_next = scores(units[idx + 1], nxt_slot)
            run_next_scores_with(slot, nxt_slot, unit[1])
        row, acc = 0, None
        for vt in unit[2]:
            rows = slice(row, row + vt.shape[1])
            p_buf[pslot, rows, :] = jnp.exp2(s_buf[slot, rows, :] - m).astype(BF16)
            part = jnp.dot(vt, p_buf[pslot, rows, :], preferred_element_type=F32)
            acc = part if acc is None else acc + part
            row += vt.shape[1]
        results.append(acc[:dv] / acc[dv:dv + 1])
        m = m_next
    return results


def _win_kernel(sink_ref, q_ref, kp_ref, kc_ref, kn_ref, kx_ref, vp_ref, vc_ref, vn_ref, vx_ref, o_ref):
    n = pl.program_id(1)
    group = WIN_HEADS // WIN_KV_HEADS
    k_loc = jnp.concatenate([kp_ref[...], kc_ref[...], kn_ref[...]], axis=0)
    vt_loc = jnp.concatenate([vp_ref[...], vc_ref[...], vn_ref[...]], axis=1)
    k_ctx = kx_ref[...]
    vt_ctx = vx_ref[...]
    kj = lax.broadcasted_iota(jnp.int32, (3 * WINDOW, WIN_STACK * WINDOW), 0)
    qi = lax.broadcasted_iota(jnp.int32, (3 * WINDOW, WIN_STACK * WINDOW), 1) % WINDOW
    k_pos = jnp.where(n < SEQ // WINDOW, kj + (n - 1) * WINDOW, SEQ)
    valid = jnp.where(kj >= qi, jnp.where(kj <= qi + 2 * WINDOW, 1, 0), 0)
    valid = jnp.where(k_pos >= 0, jnp.where(k_pos < SEQ, valid, 0), 0) > 0
    staged = []
    for h0 in range(0, WIN_HEADS, WIN_STACK):
        heads = range(h0, h0 + WIN_STACK)
        kv = h0 // group
        q = jnp.concatenate([q_ref[:, h * HEAD_DIM:(h + 1) * HEAD_DIM] for h in heads], axis=0)
        sink = jnp.concatenate([jnp.full((1, WINDOW), sink_ref[h] * LOG2E, F32) for h in heads], axis=1)
        ksl = slice(kv * HEAD_DIM, (kv + 1) * HEAD_DIM)
        s_ctx = _nt_dot(k_ctx[:, ksl], q)
        s_loc = jnp.where(valid, _nt_dot(k_loc[:, ksl], q), NEG_INF)
        m = jnp.maximum(jnp.maximum(jnp.max(s_ctx, axis=0, keepdims=True),
                                    jnp.max(s_loc, axis=0, keepdims=True)), sink)
        staged.append((heads, kv, sink, s_ctx, s_loc, m))
    for u, (heads, kv, sink, s_ctx, s_loc, m) in enumerate(staged):
        vsl = slice(kv * VAL_PAD, (kv + 1) * VAL_PAD)
        if u + 1 < len(staged):
            m = m + 0.0 * staged[u + 1][5]
        acc = (jnp.dot(vt_ctx[vsl, :], jnp.exp2(s_ctx - m).astype(BF16), preferred_element_type=F32)
               + jnp.dot(vt_loc[vsl, :], jnp.exp2(s_loc - m).astype(BF16), preferred_element_type=F32))
        l = acc[HEAD_DIM:HEAD_DIM + 1] + jnp.exp2(sink - m)
        o = (acc[:HEAD_DIM] / l).T
        for g, h in enumerate(heads):
            o_ref[:, h * HEAD_DIM:(h + 1) * HEAD_DIM] = o[g * WINDOW:(g + 1) * WINDOW].astype(BF16)


def _win_attention(sink, q, k, v):
    nblk = SEQ // WINDOW
    cblk = CTX_LEN // WINDOW
    kvw = WIN_KV_HEADS * HEAD_DIM
    ctx0 = R_LAT // CTX_LEN
    lat = lambda n: jnp.minimum(n, nblk - 1)
    prev = lambda b, n: (b * nblk + jnp.maximum(lat(n) - 1, 0), 0)
    cur = lambda b, n: (b * nblk + lat(n), 0)
    nxt = lambda b, n: (b * nblk + jnp.minimum(lat(n) + 1, nblk - 1), 0)
    qrow = lambda b, n: (jnp.where(n < nblk, b * nblk + n, R_LAT // WINDOW + b * cblk + n - nblk), 0)
    ctx = lambda b, n: (ctx0 + b, 0)
    vw = WIN_KV_HEADS * VAL_PAD
    kvb = lambda f, w: pl.BlockSpec((WINDOW, w), f)
    cxb = lambda w: pl.BlockSpec((CTX_LEN, w), ctx)
    flip = lambda f: (lambda b, n: f(b, n)[::-1])
    vtb = lambda f: pl.BlockSpec((vw, WINDOW), flip(f))
    return pl.pallas_call(
        _win_kernel,
        grid=(BATCH, nblk + cblk),
        in_specs=[pl.BlockSpec(memory_space=pltpu.SMEM),
                  pl.BlockSpec((WINDOW, WIN_HEADS * HEAD_DIM), qrow),
                  kvb(prev, kvw), kvb(cur, kvw), kvb(nxt, kvw), cxb(kvw),
                  vtb(prev), vtb(cur), vtb(nxt), pl.BlockSpec((vw, CTX_LEN), flip(ctx))],
        out_specs=pl.BlockSpec((WINDOW, WIN_HEADS * HEAD_DIM), qrow),
        out_shape=jax.ShapeDtypeStruct((R_ALL, WIN_HEADS * HEAD_DIM), BF16),
        compiler_params=_params(),
        name="window_attention",
    )(sink, q, k, k, k, k, v, v, v, v)


def _outproj_kernel(a_ref, b_ref, wa_ref, wb_ref, bo_ref, xl_ref, xc_ref, g1_ref, sh2_ref, sc2_ref,
                    lng_ref, lnb_ref, wrh_ref, wrl_ref, br_ref, upper_ref, lower_ref, sel_ref,
                    xo_ref, route_ref, cnt_ref, xs_ref):
    is_lat = pl.program_id(0) < NT_LAT // OUT_SUB
    u2s = []
    for t in range(OUT_SUB):
        rows = slice(t * TM, (t + 1) * TM)
        x = jnp.where(is_lat, xl_ref[rows, :], xc_ref[rows, :])
        y = (jnp.dot(a_ref[rows, :], wa_ref[...], preferred_element_type=F32)
             + jnp.dot(b_ref[rows, :], wb_ref[...], preferred_element_type=F32) + bo_ref[...])
        xn = _layer_norm(DN_ALPHA * x + (1.0 + g1_ref[...]) * y, lng_ref[...], lnb_ref[...])
        xo_ref[rows, :] = xn
        u2s.append(xn * (1.0 + sc2_ref[...]) + sh2_ref[...])
    for t in range(OUT_SUB):
        rows = slice(t * TM, (t + 1) * TM)
        anchor = 0.0 * xo_ref[(t + 2) * TM - 1:(t + 2) * TM, :] if t + 1 < OUT_SUB else None
        _route_and_group(u2s[t], anchor, wrh_ref, wrl_ref, br_ref, upper_ref, lower_ref, sel_ref,
                         route_ref.at[rows, :], cnt_ref.at[t * 8:(t + 1) * 8, :],
                         xs_ref.at[t * LOCAL_ROWS:(t + 1) * LOCAL_ROWS, :])


def _route_and_group(u2, anchor, wrh_ref, wrl_ref, br_ref, upper_ref, lower_ref, sel_ref,
                     route_ref, cnt_ref, xs_ref):
    u_hi = u2.astype(BF16)
    u_lo = (u2 - u_hi.astype(F32)).astype(BF16)
    logits = (jnp.dot(u_hi, wrh_ref[...], preferred_element_type=F32)
              + jnp.dot(u_lo, wrh_ref[...], preferred_element_type=F32)
              + jnp.dot(u_hi, wrl_ref[...], preferred_element_type=F32) + br_ref[...])
    lane = lax.broadcasted_iota(jnp.int32, logits.shape, 1).astype(F32)
    ninf = -jnp.inf
    big = float(ROUTE_W)
    gl = jnp.where(lane < N_GROUPS, logits, ninf)
    gmax = jnp.max(gl, axis=-1, keepdims=True)
    gidx = jnp.min(jnp.where(gl == gmax, lane, big), axis=-1, keepdims=True)
    g_w = 1.0 / jnp.sum(jnp.exp(gl - gmax), axis=-1, keepdims=True)
    lo = N_GROUPS + EXP_PER_GROUP * gidx
    el = jnp.where(lane >= lo, jnp.where(lane < lo + EXP_PER_GROUP, logits, ninf), ninf)
    v1 = jnp.max(el, axis=-1, keepdims=True)
    i1 = jnp.min(jnp.where(el == v1, lane, big), axis=-1, keepdims=True)
    el2 = jnp.where(lane == i1, ninf, el)
    v2 = jnp.max(el2, axis=-1, keepdims=True)
    i2 = jnp.min(jnp.where(el2 == v2, lane, big), axis=-1, keepdims=True)
    e2 = jnp.exp(v2 - v1)
    w1 = g_w / (1.0 + e2)
    w2 = g_w * e2 / (1.0 + e2)
    onehot = [jnp.where(lane == i1 - N_GROUPS, 1.0, 0.0), jnp.where(lane == i2 - N_GROUPS, 1.0, 0.0)]
    cnt = [jnp.sum(o, axis=0, keepdims=True) for o in onehot]
    run_units = jnp.floor((cnt[0] + cnt[1] + (RUN_ALIGN - 1)) * (1.0 / RUN_ALIGN))
    below = RUN_ALIGN * jnp.dot(jnp.broadcast_to(run_units, (8, ROUTE_W)).astype(BF16), upper_ref[...],
                                preferred_element_type=F32)[0:1]
    lower = lower_ref[...]
    base = [below, below + cnt[0]]
    lp = []
    for s in range(2):
        earlier = jnp.dot(lower, onehot[s].astype(BF16), preferred_element_type=F32)
        lp.append(jnp.sum(onehot[s] * (base[s] + earlier), axis=-1, keepdims=True))
    rec = jnp.where(lane == 0.0, i1 - N_GROUPS,
                    jnp.where(lane == 1.0, i2 - N_GROUPS,
                              jnp.where(lane == 2.0, w1,
                                        jnp.where(lane == 3.0, w2,
                                                  jnp.where(lane == 4.0, lp[0],
                                                            jnp.where(lane == 5.0, lp[1], 0.0))))))
    route_ref[...] = rec
    cnt_ref[...] = jnp.broadcast_to(run_units * RUN_ALIGN, (8, ROUTE_W))
    sel = sel_ref[...]
    pos = lax.broadcasted_iota(jnp.int32, (LOCAL_ROWS, TM), 0).astype(F32)
    lp_lanes = []
    for s in range(2):
        hi = jnp.floor(lp[s] * (1.0 / 256.0))
        parts = jnp.where(lane == 0.0, lp[s] - 256.0 * hi, jnp.where(lane == 1.0, hi, 0.0)).astype(BF16)
        t = _nt_dot(sel, parts)
        lp_lanes.append(t[0:1] + 256.0 * t[1:2])
    perm = jnp.where(pos == lp_lanes[0], 1.0, jnp.where(pos == lp_lanes[1], 1.0, 0.0)).astype(BF16)
    xs = jnp.dot(perm, u_hi, preferred_element_type=F32)
    xs_ref[...] = xs if anchor is None else xs + anchor


def _outproj(n_tiles, mix_a, mix_b, w_a, w_b, b_out, x_lat, x_ctx, mods, ln_g, ln_b, w_r, b_r):
    rows = n_tiles * TM
    steps = n_tiles // OUT_SUB
    lat_steps = NT_LAT // OUT_SUB
    half = mix_a.shape[1]
    w_rh = w_r.astype(BF16)
    w_rl = (w_r - w_rh.astype(F32)).astype(BF16)
    upper = (jnp.arange(ROUTE_W)[:, None] < jnp.arange(ROUTE_W)[None, :]).astype(BF16)
    lower = (jnp.arange(TM)[:, None] > jnp.arange(TM)[None, :]).astype(BF16)
    sel = (jnp.arange(8)[:, None] == jnp.arange(ROUTE_W)[None, :]).astype(BF16)
    row = lambda w: pl.BlockSpec((OUT_SUB * TM, w), lambda i: (i, 0))
    mod = lambda chunk: pl.BlockSpec((None, None, 1, D_MODEL), lambda i: (_mod_row(i * OUT_SUB), chunk, 0, 0))
    vec = _full((1, D_MODEL))
    return pl.pallas_call(
        _outproj_kernel,
        grid=(steps,),
        in_specs=[row(half), row(half), _full((half, D_MODEL)), _full((half, D_MODEL)), vec,
                  pl.BlockSpec((OUT_SUB * TM, D_MODEL), lambda i: (jnp.minimum(i, lat_steps - 1), 0)),
                  pl.BlockSpec((OUT_SUB * TM, D_MODEL), lambda i: (jnp.maximum(i - lat_steps, 0), 0)),
                  mod(G1), mod(SH2), mod(SC2), vec, vec,
                  _full((D_MODEL, ROUTE_W)), _full((D_MODEL, ROUTE_W)), _full((1, ROUTE_W)),
                  _full((ROUTE_W, ROUTE_W)), _full((TM, TM)), _full((8, ROUTE_W))],
        out_specs=[row(D_MODEL), row(ROUTE_W), pl.BlockSpec((OUT_SUB * 8, ROUTE_W), lambda i: (i, 0)),
                   pl.BlockSpec((OUT_SUB * LOCAL_ROWS, D_MODEL), lambda i: (i, 0))],
        out_shape=[jax.ShapeDtypeStruct((rows, D_MODEL), F32),
                   jax.ShapeDtypeStruct((rows, ROUTE_W), F32),
                   jax.ShapeDtypeStruct((n_tiles * 8, ROUTE_W), F32),
                   jax.ShapeDtypeStruct((n_tiles * LOCAL_ROWS, D_MODEL), F32)],
        compiler_params=_params(),
        name="outproj_ln_router",
    )(mix_a, mix_b, w_a, w_b, b_out, x_lat, x_ctx, mods, mods, mods, ln_g, ln_b, w_rh, w_rl, b_r,
      upper, lower, sel)


def _aligned(i):
    return pl.multiple_of(i, RUN_ALIGN)


def _moe_kernel(te_ref, tk_ref, rows_ref, lo_ref, hi_ref, cnt_ref, bt_ref, be_ref, xs_hbm, w1_ref, w3_ref,
                w2_ref, ys_ref, xbuf, wb1, wb3, wb2, sem):
    j = pl.program_id(0)
    nt = pl.num_programs(0)
    slot = j % 2

    def issue(tile, slot_):
        e = te_ref[tile]
        first = tk_ref[tile] * TMM

        def body(i, carry):
            idx = i * N_EXPERTS + e
            start = bt_ref[idx]
            lo = jnp.maximum(start, first)
            n = jnp.minimum(start + cnt_ref[idx], first + TMM) - lo

            @pl.when(n > 0)
            def _():
                src = i * LOCAL_ROWS + be_ref[idx] + lo - start
                pltpu.make_async_copy(xs_hbm.at[pl.ds(_aligned(src), _aligned(n))],
                                      xbuf.at[slot_, pl.ds(_aligned(lo - first), _aligned(n))],
                                      sem.at[slot_]).start()
            return carry
        lax.fori_loop(lo_ref[tile], hi_ref[tile], body, 0)

    @pl.when(j == 0)
    def _():
        xbuf[...] = jnp.zeros_like(xbuf)
        issue(0, 0)

    @pl.when(j + 1 < nt)
    def _():
        issue(j + 1, 1 - slot)

    @pl.when(jnp.logical_or(j == 0, te_ref[j] != te_ref[jnp.maximum(j - 1, 0)]))
    def _():
        wb1[...] = w1_ref[...].astype(BF16)
        wb3[...] = w3_ref[...].astype(BF16)
        wb2[...] = w2_ref[...].astype(BF16)

    n_real = rows_ref[j]

    @pl.when(n_real > 0)
    def _():
        pltpu.make_async_copy(xs_hbm.at[pl.ds(0, _aligned(n_real))], xbuf.at[slot, pl.ds(0, _aligned(n_real))],
                              sem.at[slot]).wait()
        x = xbuf[slot].astype(BF16)
        h1 = jnp.dot(x, wb1[...], preferred_element_type=F32)
        h3 = jnp.dot(x, wb3[...], preferred_element_type=F32)
        hid = h1 * _sigmoid(h1) * h3
        ys_ref[...] = jnp.dot(hid.astype(BF16), wb2[...], preferred_element_type=F32)

    @pl.when(n_real == 0)
    def _():
        ys_ref[...] = jnp.zeros_like(ys_ref)


def _moe_experts(layer, plan, xs_local, w1, w3, w2):
    tile_expert, tile_k, tile_rows, src_lo, src_hi, cnt, before_tile, before_expert, _, _ = plan
    nt = tile_expert.shape[0]
    wmap = lambda j, te, *_: (layer, te[j], 0, 0)
    grid_spec = pltpu.PrefetchScalarGridSpec(
        num_scalar_prefetch=8,
        grid=(nt,),
        in_specs=[pl.BlockSpec(memory_space=pl.ANY),
                  pl.BlockSpec((None, None, D_MODEL, EXPERT_FF), wmap),
                  pl.BlockSpec((None, None, D_MODEL, EXPERT_FF), wmap),
                  pl.BlockSpec((None, None, EXPERT_FF, D_MODEL), wmap)],
        out_specs=pl.BlockSpec((TMM, D_MODEL), lambda j, *_: (j, 0)),
        scratch_shapes=[pltpu.VMEM((2, TMM, D_MODEL), F32),
                        pltpu.VMEM((D_MODEL, EXPERT_FF), BF16),
                        pltpu.VMEM((D_MODEL, EXPERT_FF), BF16),
                        pltpu.VMEM((EXPERT_FF, D_MODEL), BF16),
                        pltpu.SemaphoreType.DMA((2,))])
    return pl.pallas_call(
        _moe_kernel,
        grid_spec=grid_spec,
        out_shape=jax.ShapeDtypeStruct((nt * TMM, D_MODEL), F32),
        compiler_params=_params(),
        name="moe_experts",
    )(tile_expert, tile_k, tile_rows, src_lo, src_hi, cnt, before_tile, before_expert, xs_local, w1, w3, w2)


def _combine_kernel(cnt_ref, bt_ref, be_ref, gs_ref, used_ref, ys_hbm, x_ref, route_ref, g2_ref, lng_ref, lnb_ref,
                    o_ref, ybuf, sem):
    i = pl.program_id(0)
    nt = pl.num_programs(0)
    slot = i % 2

    def issue(tile, slot_):
        def body(e, carry):
            idx = tile * N_EXPERTS + e
            n = cnt_ref[idx]

            @pl.when(n > 0)
            def _():
                pltpu.make_async_copy(ys_hbm.at[pl.ds(_aligned(gs_ref[e] + bt_ref[idx]), _aligned(n))],
                                      ybuf.at[slot_, pl.ds(_aligned(be_ref[idx]), _aligned(n))],
                                      sem.at[slot_]).start()
            return carry
        lax.fori_loop(0, N_EXPERTS, body, 0)

    @pl.when(i == 0)
    def _():
        ybuf[...] = jnp.zeros_like(ybuf)
        issue(0, 0)

    @pl.when(i + 1 < nt)
    def _():
        issue(i + 1, 1 - slot)

    used = _aligned(used_ref[i])
    pltpu.make_async_copy(ys_hbm.at[pl.ds(0, used)], ybuf.at[slot, pl.ds(0, used)], sem.at[slot]).wait()
    route = route_ref[...]
    pos = lax.broadcasted_iota(jnp.int32, (TM, LOCAL_ROWS), 1).astype(F32)
    y = ybuf[slot].astype(BF16)
    picked = [jnp.dot(jnp.where(pos == route[:, 4 + s:5 + s], 1.0, 0.0).astype(BF16), y,
                      preferred_element_type=F32) for s in range(2)]
    f = route[:, 2:3] * picked[0] + route[:, 3:4] * picked[1]
    z = DN_ALPHA * x_ref[...] + (1.0 + g2_ref[...]) * f
    o_ref[...] = _layer_norm(z, lng_ref[...], lnb_ref[...])


def _moe_combine(n_tiles, plan, ys, x_all, route, mods, ln_g, ln_b):
    _, _, _, _, _, cnt, before_tile, before_expert, group_start, used = plan
    rows = n_tiles * TM
    row = lambda w: pl.BlockSpec((TM, w), lambda i, *_: (i, 0))
    vec = pl.BlockSpec((1, D_MODEL), lambda i, *_: (0, 0))
    grid_spec = pltpu.PrefetchScalarGridSpec(
        num_scalar_prefetch=5,
        grid=(n_tiles,),
        in_specs=[pl.BlockSpec(memory_space=pl.ANY), row(D_MODEL), row(ROUTE_W),
                  pl.BlockSpec((None, None, 1, D_MODEL), lambda i, *_: (_mod_row(i), G2, 0, 0)), vec, vec],
        out_specs=row(D_MODEL),
        scratch_shapes=[pltpu.VMEM((2, LOCAL_ROWS, D_MODEL), F32), pltpu.SemaphoreType.DMA((2,))])
    return pl.pallas_call(
        _combine_kernel,
        grid_spec=grid_spec,
        out_shape=jax.ShapeDtypeStruct((rows, D_MODEL), F32),
        compiler_params=_params(),
        name="moe_combine_ln",
    )(cnt, before_tile, before_expert, group_start, used, ys, x_all, route, mods, ln_g, ln_b)


def _moe_plan(cnt_rec, n_tiles):
    cnt = cnt_rec.reshape(n_tiles, 8, ROUTE_W)[:, 0, :N_EXPERTS].astype(jnp.int32)
    nt_max = (n_tiles * (2 * TM + N_EXPERTS * (RUN_ALIGN - 1))) // TMM + N_EXPERTS
    total = jnp.sum(cnt, axis=0)
    tiles_e = (total + TMM - 1) // TMM
    tile_end = jnp.cumsum(tiles_e)
    first_tile = tile_end - tiles_e
    before_tile = jnp.cumsum(cnt, axis=0) - cnt
    before_expert = jnp.cumsum(cnt, axis=1) - cnt
    tile_id = jnp.arange(nt_max, dtype=jnp.int32)
    tile_expert = jnp.minimum(jnp.sum((tile_id[:, None] >= tile_end[None, :]).astype(jnp.int32), axis=1),
                              N_EXPERTS - 1)
    onehot = (tile_expert[:, None] == jnp.arange(N_EXPERTS, dtype=jnp.int32)[None, :]).astype(jnp.int32)
    tile_k = tile_id - jnp.sum(onehot * first_tile[None, :], axis=1)
    tile_rows = jnp.clip(jnp.sum(onehot * total[None, :], axis=1) - tile_k * TMM, 0, TMM)
    first = (tile_k * TMM)[:, None]
    run_start = jnp.sum(onehot[:, None, :] * before_tile[None, :, :], axis=2)
    run_end = run_start + jnp.sum(onehot[:, None, :] * cnt[None, :, :], axis=2)
    src_lo = jnp.sum((run_end <= first).astype(jnp.int32), axis=1)
    src_hi = jnp.sum((run_start < first + TMM).astype(jnp.int32), axis=1)
    return (tile_expert, tile_k, tile_rows, src_lo, src_hi, cnt.reshape(-1), before_tile.reshape(-1),
            before_expert.reshape(-1), first_tile * TMM, jnp.sum(cnt, axis=1))


def _router_weights(w_rg, b_rg, w_re, b_re):
    w = jnp.concatenate([w_rg, jnp.transpose(w_re, (1, 0, 2)).reshape(D_MODEL, N_EXPERTS)], axis=1)
    b = jnp.concatenate([b_rg, b_re.reshape(-1)])
    pad = ROUTE_W - w.shape[1]
    return jnp.pad(w, ((0, 0), (0, pad))), jnp.pad(b, (0, pad)).reshape(1, ROUTE_W)


def _proj1_kernel(x_ref, sh_ref, sc_ref, w_ref, b_ref, cos_ref, sin_ref, cosm_ref, sinm_ref, cosr_ref,
                  sinr_ref, gq_ref, gk_ref, gqc_ref, gkv_ref, avg_ref, wuq_ref, wuk_ref, wuv_ref, vplace_ref,
                  q_ref, qm_ref, k_ref, v_ref, km_ref, vm_ref):
    u = x_ref[...] * (1.0 + sc_ref[...]) + sh_ref[...]
    y = jnp.dot(u.astype(BF16), w_ref[...], preferred_element_type=F32) + b_ref[...]
    c_q = GQA_HEADS * HEAD_DIM
    c_qc = c_q + MLA_Q_RANK
    c_k = c_qc + GQA_KV_HEADS * HEAD_DIM
    c_v = c_k + GQA_KV_HEADS * HEAD_DIM
    c_kv = c_v + MLA_KV_RANK
    avg = avg_ref[...]

    def head_rms(t, gain):
        sq = t * t
        hi = sq.astype(BF16)
        lo = (sq - hi.astype(F32)).astype(BF16)
        a = avg[:t.shape[1], :t.shape[1]]
        ms = jnp.dot(hi, a, preferred_element_type=F32) + jnp.dot(lo, a, preferred_element_type=F32)
        return t * lax.rsqrt(ms + RMS_EPS) * gain

    def row_rms(t, gain):
        ms = jnp.mean(t * t, axis=-1, keepdims=True)
        return t * lax.rsqrt(ms + RMS_EPS) * gain

    cos = cos_ref[...]
    sin = sin_ref[...]
    cos4 = jnp.concatenate([cos] * 4, axis=1)
    sin4 = jnp.concatenate([sin] * 4, axis=1)
    q = _rope(head_rms(y[:, :c_q], gq_ref[...]), cos4, sin4, HEAD_DIM // 4) * (HEAD_DIM ** -0.5 * LOG2E)
    q_ref[...] = q.astype(BF16)
    k = _rope(head_rms(y[:, c_qc:c_k], gk_ref[...]), cos, sin, HEAD_DIM // 4)
    k_ref[...] = k.astype(BF16)

    v_ref[...] = _transposed_values(vplace_ref[...], y[:, c_k:c_v].astype(BF16), HEAD_DIM)

    qc = row_rms(y[:, c_q:c_qc], gqc_ref[...]).astype(BF16)
    qm = jnp.dot(qc, wuq_ref[...], preferred_element_type=F32)
    cosm = jnp.concatenate([cosm_ref[...]] * MLA_HEADS, axis=1)
    sinm = jnp.concatenate([sinm_ref[...]] * MLA_HEADS, axis=1)
    qm = _rope(qm, cosm, sinm, MLA_ROPE // 4) * ((MLA_NOPE + MLA_ROPE) ** -0.5 * LOG2E)
    qm_ref[...] = qm.astype(BF16)

    kvn = row_rms(y[:, c_v:c_kv], gkv_ref[...]).astype(BF16)
    kr = _rope(y[:, c_kv:], cosr_ref[...], sinr_ref[...], MLA_ROPE // 4).astype(BF16)
    km = jnp.dot(jnp.concatenate([kvn, kr], axis=1), wuk_ref[...], preferred_element_type=F32)
    km_ref[...] = km.astype(BF16)
    vm_ref[...] = _transposed_values(wuv_ref[...], kvn, MLA_V)


def _proj1(x_all, mods, w_in, b_in, tabs, gq, gk, gqc, gkv, avg, wuq, wuk, wuv):
    cos_hd, sin_hd, cos_m, sin_m, cos_r, sin_r = tabs
    kvw = GQA_KV_HEADS * HEAD_DIM
    qw = GQA_HEADS * HEAD_DIM
    mw = MLA_HEADS * MLA_PAD
    vw = MLA_HEADS * VAL_PAD
    gvw = GQA_KV_HEADS * VAL_PAD
    row = lambda w: pl.BlockSpec((TM, w), lambda i: (i, 0))
    col = lambda h: pl.BlockSpec((h, TM), lambda i: (0, i))
    tab = pl.BlockSpec((TM, 128), lambda i: (_rope_row_block(i), 0))
    vplace = _value_placement(GQA_KV_HEADS, HEAD_DIM)
    return pl.pallas_call(
        _proj1_kernel,
        grid=(NT_ALL,),
        in_specs=[row(D_MODEL), _mod_spec(SH1), _mod_spec(SC1),
                  _full((D_MODEL, ODD_IN_PAD)), _full((1, ODD_IN_PAD)), tab, tab, tab, tab, tab, tab,
                  _full((1, qw)), _full((1, kvw)), _full((1, MLA_Q_RANK)), _full((1, MLA_KV_RANK)),
                  _full((qw, qw)), _full((MLA_Q_RANK, mw)), _full((MLA_KV_RANK + 128, mw)),
                  _full((vw, MLA_KV_RANK)), _full((gvw, kvw))],
        out_specs=[row(qw), row(mw), row(kvw), col(gvw), row(mw), col(vw)],
        out_shape=[jax.ShapeDtypeStruct((R_ALL, qw), BF16),
                   jax.ShapeDtypeStruct((R_ALL, mw), BF16),
                   jax.ShapeDtypeStruct((R_ALL, kvw), BF16),
                   jax.ShapeDtypeStruct((gvw, R_ALL), BF16),
                   jax.ShapeDtypeStruct((R_ALL, mw), BF16),
                   jax.ShapeDtypeStruct((vw, R_ALL), BF16)],
        compiler_params=_params(),
        name="proj1",
    )(x_all, mods, mods, w_in, b_in, cos_hd, sin_hd, cos_m, sin_m, cos_r, sin_r,
      gq, gk, gqc, gkv, avg, wuq, wuk, wuv.T, vplace)


def _dense_kernel(q_ref, kl_ref, kc_ref, vl_ref, vc_ref, o_ref, s_buf, p_buf, *, n_heads, group, stack, dk, dv):
    tq = q_ref.shape[0]
    units = []
    for h0 in range(0, n_heads, stack):
        kv = h0 // group
        qs = [q_ref[:, h * dk:(h + 1) * dk] for h in range(h0, h0 + stack)]
        q = qs[0] if stack == 1 else jnp.concatenate(qs, axis=0)
        ks = slice(kv * dk, (kv + 1) * dk)
        vs = slice(kv * VAL_PAD, (kv + 1) * VAL_PAD)
        parts = [slice(c * (SEQ // KEY_PARTS), (c + 1) * (SEQ // KEY_PARTS)) for c in range(KEY_PARTS)]
        units.append((q, [kl_ref[c, ks] for c in parts] + [kc_ref[:, ks]],
                      [vl_ref[vs, c] for c in parts] + [vc_ref[vs, :]]))
    for u, o_t in enumerate(_attend_keys_major(units, dv, s_buf, p_buf)):
        o = o_t.T
        for g in range(stack):
            h = u * stack + g
            o_ref[:, h * dv:(h + 1) * dv] = o[g * tq:(g + 1) * tq].astype(BF16)


def _dense_attention(q, k, v, *, n_heads, group, stack, dk, dv, tq, name):
    n_kv = n_heads // group
    nq = SEQ // tq
    ctx0 = R_LAT // CTX_LEN
    lat = lambda w: pl.BlockSpec((SEQ, w), lambda b, j: (b, 0), pipeline_mode=pl.Buffered(1))
    ctx = lambda w: pl.BlockSpec((CTX_LEN, w), lambda b, j: (ctx0 + b, 0))
    lat_t = pl.BlockSpec((n_kv * VAL_PAD, SEQ), lambda b, j: (0, b), pipeline_mode=pl.Buffered(1))
    ctx_t = pl.BlockSpec((n_kv * VAL_PAD, CTX_LEN), lambda b, j: (0, ctx0 + b))
    return pl.pallas_call(
        functools.partial(_dense_kernel, n_heads=n_heads, group=group, stack=stack, dk=dk, dv=dv),
        grid=(BATCH, nq),
        in_specs=[pl.BlockSpec((tq, n_heads * dk), lambda b, j: (b * nq + j, 0)),
                  lat(n_kv * dk), ctx(n_kv * dk), lat_t, ctx_t],
        out_specs=pl.BlockSpec((tq, n_heads * dv), lambda b, j: (b * nq + j, 0)),
        out_shape=jax.ShapeDtypeStruct((R_LAT, n_heads * dv), BF16),
        scratch_shapes=[pltpu.VMEM((SCORE_SLOTS, SEQ + CTX_LEN, stack * tq), F32),
                        pltpu.VMEM((2, SEQ + CTX_LEN, stack * tq), BF16)],
        compiler_params=_params(),
        name=name,
    )(q, k, k, v, v)


def _mla_weights(w_uq, w_ukv):
    wq = w_uq.reshape(MLA_Q_RANK, MLA_HEADS, MLA_NOPE + MLA_ROPE)
    wq = jnp.pad(wq, ((0, 0), (0, 0), (0, MLA_PAD - MLA_NOPE - MLA_ROPE))).reshape(MLA_Q_RANK, -1)
    wkv = w_ukv.reshape(MLA_KV_RANK, MLA_HEADS, MLA_NOPE + MLA_V)
    wk = jnp.pad(wkv[:, :, :MLA_NOPE], ((0, 0), (0, 0), (0, MLA_PAD - MLA_NOPE))).reshape(MLA_KV_RANK, -1)
    wv = jnp.pad(wkv[:, :, MLA_NOPE:], ((0, 0), (0, 0), (0, VAL_PAD - MLA_V))).reshape(MLA_KV_RANK, -1)
    r = jnp.arange(128)[:, None]
    c = jnp.arange(MLA_HEADS * MLA_PAD)[None, :]
    place = jnp.logical_and(r < MLA_ROPE, (c % MLA_PAD) == MLA_NOPE + r).astype(F32)
    wk = jnp.concatenate([wk, place], axis=0)
    return wq.astype(BF16), wk.astype(BF16), wv.astype(BF16)


def kernel(x, c, ctx, c_ctx, even_w_in, even_b_in, even_conv_w, even_conv_b, even_conv_ln_g, even_conv_ln_b, even_sink, even_w_out, even_b_out, odd_w_in, odd_b_in, odd_q_norm, odd_k_norm, odd_mla_q_norm, odd_mla_kv_norm, odd_mla_w_uq, odd_mla_w_ukv, odd_w_out, odd_b_out, ada_w, ada_b, ln1_g, ln1_b, ln2_g, ln2_b, moe_w_rg, moe_b_rg, moe_w_re, moe_b_re, moe_w1, moe_w3, moe_w2):
    vec = lambda a: a.reshape(1, -1)
    x_lat0 = x.reshape(R_LAT, D_MODEL)
    x_ctx0 = ctx.reshape(R_CTX, D_MODEL)

    cv =jnp.concatenate([c, c_ctx[None, :], jnp.zeros((8 - BATCH - 1, D_MODEL), F32)], axis=0)
    mods = _ada_table(cv, ada_w, ada_b).reshape(DEPTH, 8, 6, 1, D_MODEL)

    cos64, sin64 = _rope_tables(HEAD_DIM)
    cos_hd, sin_hd = _pad_table(cos64, sin64, 0, HEAD_DIM, 128)
    cos32, sin32 = _rope_tables(MLA_ROPE)
    cos_m, sin_m = _pad_table(cos32, sin32, MLA_NOPE, MLA_PAD, 128)
    cos_r, sin_r = _pad_table(cos32, sin32, 0, 128, 128)

    m0 = mods[0]
    h, q0, k0, v0 = _proj0(x_lat0, x_ctx0, m0, even_w_in[0].astype(BF16), vec(even_b_in[0]), cos_hd, sin_hd)
    conv_out = _conv(h, even_conv_w[0].reshape(CONV_WIDTH, CONV_CH), vec(even_conv_b[0]),
                     vec(even_conv_ln_g[0]), vec(even_conv_ln_b[0]))
    attn = _win_attention(even_sink[0], q0, k0, v0)
    w_out = even_w_out[0].astype(BF16)
    w_r, b_r = _router_weights(moe_w_rg[0], moe_b_rg[0], moe_w_re[0], moe_b_re[0])
    x_all, route, cnt_rec, xs_local = _outproj(
        NT_ALL, conv_out, attn, w_out[:CONV_CH], w_out[CONV_CH:], vec(even_b_out[0]),
        x_lat0, x_ctx0, m0, vec(ln1_g[0]), vec(ln1_b[0]), w_r, b_r)
    plan = _moe_plan(cnt_rec, NT_ALL)
    ys = _moe_experts(0, plan, xs_local, moe_w1, moe_w3, moe_w2)
    x_all = _moe_combine(NT_ALL, plan, ys, x_all, route, m0, vec(ln2_g[0]), vec(ln2_b[0]))

    m1 = mods[1]
    w_in1 = jnp.pad(odd_w_in[0], ((0, 0), (0, ODD_IN_PAD - ODD_IN))).astype(BF16)
    b_in1 = jnp.pad(odd_b_in[0], (0, ODD_IN_PAD - ODD_IN)).reshape(1, -1)
    wuq, wuk, wuv = _mla_weights(odd_mla_w_uq[0], odd_mla_w_ukv[0])
    qw = GQA_HEADS * HEAD_DIM
    hid = jnp.arange(qw) // HEAD_DIM
    avg = ((hid[:, None] == hid[None, :]).astype(F32) / HEAD_DIM).astype(BF16)
    q1, qm, k1, v1, km, vm = _proj1(
        x_all, m1, w_in1, b_in1, (cos_hd, sin_hd, cos_m, sin_m, cos_r, sin_r),
        vec(jnp.tile(odd_q_norm[0], GQA_HEADS)), vec(jnp.tile(odd_k_norm[0], GQA_KV_HEADS)),
        vec(odd_mla_q_norm[0]), vec(odd_mla_kv_norm[0]), avg, wuq, wuk, wuv)
    o_g = _dense_attention(q1, k1, v1, n_heads=GQA_HEADS, group=GQA_HEADS // GQA_KV_HEADS, stack=1,
                           dk=HEAD_DIM, dv=HEAD_DIM, tq=512, name="gqa_attention")
    o_m = _dense_attention(qm, km, vm, n_heads=MLA_HEADS, group=1, stack=1, dk=MLA_PAD, dv=MLA_V, tq=512,
                           name="mla_attention")
    w_out = odd_w_out[0].astype(BF16)
    w_r, b_r = _router_weights(moe_w_rg[1], moe_b_rg[1], moe_w_re[1], moe_b_re[1])
    x_lat, route, cnt_rec, xs_local = _outproj(
        NT_LAT, o_g, o_m, w_out[:qw], w_out[qw:], vec(odd_b_out[0]),
        x_all, x_all, m1, vec(ln1_g[1]), vec(ln1_b[1]), w_r, b_r)
    plan = _moe_plan(cnt_rec, NT_LAT)
    ys = _moe_experts(1, plan, xs_local, moe_w1, moe_w3, moe_w2)
    x_lat = _moe_combine(NT_LAT, plan, ys, x_lat, route, m1, vec(ln2_g[1]), vec(ln2_b[1]))
    return x_lat.reshape(BATCH, SEQ, D_MODEL)
```

```python
import functools

import jax
import jax.numpy as jnp
from jax import lax
from jax.experimental import pallas as pl
from jax.experimental.pallas import tpu as pltpu

F32 = jnp.float32
BF16 = jnp.bfloat16

D_MODEL = 1024
BATCH = 4
SEQ = 4096
DEPTH = 2
GRID_W = 64
CTX_LEN = 256
HEAD_DIM = 64
ROPE_THETA = 10000.0
LN_EPS = 1e-5
RMS_EPS = 1e-6
NEG_INF = -1e30

CONV_CH = 512
CONV_WIDTH = 31
WIN_HEADS = 8
WIN_KV_HEADS = 2
WINDOW = 128
GQA_HEADS = 8
GQA_KV_HEADS = 2
MLA_HEADS = 8
MLA_Q_RANK = 256
MLA_KV_RANK = 128
MLA_NOPE = 64
MLA_ROPE = 32
MLA_V = 64
N_GROUPS = 4
EXP_PER_GROUP = 8
N_EXPERTS = N_GROUPS * EXP_PER_GROUP
EXPERT_FF = 512
DN_ALPHA = float((2 * DEPTH) ** 0.25)

EVEN_IN = 2 * CONV_CH + (WIN_HEADS + 2 * WIN_KV_HEADS) * HEAD_DIM
ODD_IN = 1184
ODD_IN_PAD = 1280
MLA_PAD = 128
VAL_PAD = 128
WIN_STACK = 1
SCORE_SLOTS = 2
KEY_PARTS = 4
LOG2E = 1.4426950408889634

R_LAT = BATCH * SEQ
R_CTX = BATCH * CTX_LEN
R_ALL = R_LAT + R_CTX
TM = 256
NT_LAT = R_LAT // TM
NT_ALL = R_ALL // TM
TILES_PER_SEQ = SEQ // TM
HALO = 16
CONV_CHUNK = 32
SHIFTS = 8
OUT_SUB = 2
TMM = 512
ROUTE_W = 128
RUN_ALIGN = 8
LOCAL_ROWS = 768
VMEM_LIMIT = 56 * 1024 * 1024

SH1, SC1, G1, SH2, SC2, G2 = range(6)


def _sigmoid(x):
    return 1.0 / (1.0 + jnp.exp(-x))


def _layer_norm(z, g, b):
    mu = jnp.mean(z, axis=-1, keepdims=True)
    zc = z - mu
    var = jnp.mean(zc * zc, axis=-1, keepdims=True)
    return zc * lax.rsqrt(var + LN_EPS) * g + b


def _rope(x, cos, sin, half):
    n = x.shape[-1]
    lane = lax.broadcasted_iota(jnp.int32, x.shape, 1)
    first = (lane % (2 * half)) < half
    partner = jnp.where(first, pltpu.roll(x, n - half, 1), pltpu.roll(x, half, 1))
    return x * cos + partner * sin


def _mod_row(i):
    return jnp.where(i < NT_LAT, i // TILES_PER_SEQ, BATCH)


def _mod_spec(chunk):
    return pl.BlockSpec((None, None, 1, D_MODEL), lambda i: (_mod_row(i), chunk, 0, 0))


def _rope_row_block(i):
    return jnp.where(i < NT_LAT, i % TILES_PER_SEQ, TILES_PER_SEQ)


def _full(shape):
    nd = len(shape)
    return pl.BlockSpec(shape, lambda *_: (0,) * nd)


def _params():
    return pltpu.CompilerParams(vmem_limit_bytes=VMEM_LIMIT)


def _ada_kernel(cv_ref, w_ref, b_ref, o_ref):
    cv = cv_ref[...]
    s = cv * _sigmoid(cv)
    o_ref[...] = jnp.dot(s, w_ref[...], precision=lax.Precision.HIGHEST,
                         preferred_element_type=F32) + b_ref[...]


def _ada_table(cv, ada_w, ada_b):
    bn = 1536
    nb = (6 * D_MODEL) // bn
    return pl.pallas_call(
        _ada_kernel,
        grid=(DEPTH, nb),
        in_specs=[pl.BlockSpec((8, D_MODEL), lambda l, j: (0, 0)),
                  pl.BlockSpec((None, D_MODEL, bn), lambda l, j: (l, 0, j)),
                  pl.BlockSpec((None, 1, bn), lambda l, j: (l, 0, j))],
        out_specs=pl.BlockSpec((None, 8, bn), lambda l, j: (l, 0, j)),
        out_shape=jax.ShapeDtypeStruct((DEPTH, 8, 6 * D_MODEL), F32),
        compiler_params=_params(),
        name="ada_table",
    )(cv, ada_w, ada_b.reshape(DEPTH, 1, 6 * D_MODEL))


def _rope_tables(rot_dim):
    axis_dim = rot_dim // 2
    inv_freq = ROPE_THETA ** (-jnp.arange(0, axis_dim, 2, dtype=F32) / axis_dim)
    t = jnp.arange(SEQ)
    ang_r = (t // GRID_W).astype(F32)[:, None] * inv_freq[None, :]
    ang_c = (t % GRID_W).astype(F32)[:, None] * inv_freq[None, :]
    cos = jnp.concatenate([jnp.cos(ang_r), jnp.cos(ang_r), jnp.cos(ang_c), jnp.cos(ang_c)], axis=-1)
    sin = jnp.concatenate([-jnp.sin(ang_r), jnp.sin(ang_r), -jnp.sin(ang_c), jnp.sin(ang_c)], axis=-1)
    return cos, sin


def _pad_table(cos, sin, lead, period, width):
    rot = cos.shape[1]
    one = jnp.ones((SEQ, period), F32).at[:, lead:lead + rot].set(cos)
    zero = jnp.zeros((SEQ, period), F32).at[:, lead:lead + rot].set(sin)
    cos_w = jnp.tile(one, (1, width // period))
    sin_w = jnp.tile(zero, (1, width // period))
    cos_w = jnp.concatenate([cos_w, jnp.ones((TM, width), F32)], axis=0)
    sin_w = jnp.concatenate([sin_w, jnp.zeros((TM, width), F32)], axis=0)
    return cos_w, sin_w


def _proj0_kernel(xl_ref, xc_ref, sh_ref, sc_ref, w_ref, b_ref, cos_ref, sin_ref, vplace_ref,
                  h_ref, q_ref, k_ref, v_ref):
    x = jnp.where(pl.program_id(0) < NT_LAT, xl_ref[...], xc_ref[...])
    u = x * (1.0 + sc_ref[...]) + sh_ref[...]
    y = jnp.dot(u.astype(BF16), w_ref[...], preferred_element_type=F32) + b_ref[...]
    h_ref[...] = y[:, :CONV_CH] * _sigmoid(y[:, CONV_CH:2 * CONV_CH])
    cos = cos_ref[...]
    sin = sin_ref[...]
    q0 = 2 * CONV_CH
    k0 = q0 + WIN_HEADS * HEAD_DIM
    v0 = k0 + WIN_KV_HEADS * HEAD_DIM
    cos4 = jnp.concatenate([cos] * 4, axis=1)
    sin4 = jnp.concatenate([sin] * 4, axis=1)
    q = _rope(y[:, q0:k0], cos4, sin4, HEAD_DIM // 4) * (HEAD_DIM ** -0.5 * LOG2E)
    q_ref[...] = q.astype(BF16)
    k_ref[...] = _rope(y[:, k0:v0], cos, sin, HEAD_DIM // 4).astype(BF16)
    v_ref[...] = _transposed_values(vplace_ref[...], y[:, v0:].astype(BF16), HEAD_DIM)


def _proj0(x_lat, x_ctx, mods, w_in, b_in, cos_hd, sin_hd):
    kvw = WIN_KV_HEADS * HEAD_DIM
    row = lambda w: pl.BlockSpec((TM, w), lambda i: (i, 0))
    tab = pl.BlockSpec((TM, 128), lambda i: (_rope_row_block(i), 0))
    return pl.pallas_call(
        _proj0_kernel,
        grid=(NT_ALL,),
        in_specs=[pl.BlockSpec((TM, D_MODEL), lambda i: (jnp.minimum(i, NT_LAT - 1), 0)),
                  pl.BlockSpec((TM, D_MODEL), lambda i: (jnp.maximum(i - NT_LAT, 0), 0)),
                  _mod_spec(SH1), _mod_spec(SC1),
                  _full((D_MODEL, EVEN_IN)), _full((1, EVEN_IN)), tab, tab,
                  _full((WIN_KV_HEADS * VAL_PAD, kvw))],
        out_specs=[row(CONV_CH), row(WIN_HEADS * HEAD_DIM), row(kvw),
                   pl.BlockSpec((WIN_KV_HEADS * VAL_PAD, TM), lambda i: (0, i))],
        out_shape=[jax.ShapeDtypeStruct((R_ALL, CONV_CH), F32),
                   jax.ShapeDtypeStruct((R_ALL, WIN_HEADS * HEAD_DIM), BF16),
                   jax.ShapeDtypeStruct((R_ALL, kvw), BF16),
                   jax.ShapeDtypeStruct((WIN_KV_HEADS * VAL_PAD, R_ALL), BF16)],
        compiler_params=_params(),
        name="proj0",
    )(x_lat, x_ctx, mods, mods, w_in, b_in, cos_hd, sin_hd, _value_placement(WIN_KV_HEADS, HEAD_DIM))


def _conv_kernel(prev_ref, cur_ref, next_ref, w_ref, cb_ref, g_ref, b_ref, o_ref, buf):
    i = pl.program_id(0)
    is_ctx = i >= NT_LAT
    first = jnp.logical_or(is_ctx, i % TILES_PER_SEQ == 0)
    last = jnp.logical_or(is_ctx, i % TILES_PER_SEQ == TILES_PER_SEQ - 1)
    buf[0, 0:HALO, :] = jnp.where(first, 0.0, prev_ref[...])
    buf[0, HALO:HALO + TM, :] = cur_ref[...]
    buf[0, HALO + TM:, :] = jnp.where(last, 0.0, next_ref[...])
    span = TM + 2 * HALO - SHIFTS
    for r in range(1, SHIFTS):
        buf[r, 0:span, :] = buf[0, r:r + span, :]
    off = HALO - CONV_WIDTH // 2
    for c in range(TM // CONV_CHUNK):
        r0 = c * CONV_CHUNK
        acc = jnp.zeros((CONV_CHUNK, CONV_CH), F32)
        for k in range(CONV_WIDTH):
            r = (off + k) % SHIFTS
            base = r0 + off + k - r
            w = w_ref[k * SHIFTS:(k + 1) * SHIFTS, :]
            acc = acc + buf[r, base:base + CONV_CHUNK, :] * jnp.concatenate([w] * (CONV_CHUNK // SHIFTS), axis=0)
        z = _layer_norm(acc + cb_ref[...], g_ref[...], b_ref[...])
        o_ref[r0:r0 + CONV_CHUNK, :] = (z * _sigmoid(z)).astype(BF16)


def _conv(h, conv_w, conv_b, ln_g, ln_b):
    nh = R_ALL // HALO
    per = TM // HALO
    vec = _full((1, CONV_CH))
    return pl.pallas_call(
        _conv_kernel,
        grid=(NT_ALL,),
        in_specs=[pl.BlockSpec((HALO, CONV_CH), lambda i: (jnp.maximum(i * per - 1, 0), 0)),
                  pl.BlockSpec((TM, CONV_CH), lambda i: (i, 0)),
                  pl.BlockSpec((HALO, CONV_CH), lambda i: (jnp.minimum((i + 1) * per, nh - 1), 0)),
                  _full((CONV_WIDTH * SHIFTS, CONV_CH)), vec, vec, vec],
        out_specs=pl.BlockSpec((TM, CONV_CH), lambda i: (i, 0)),
        out_shape=jax.ShapeDtypeStruct((R_ALL, CONV_CH), BF16),
        scratch_shapes=[pltpu.VMEM((SHIFTS, TM + 2 * HALO, CONV_CH), F32)],
        compiler_params=_params(),
        name="conv_module",
    )(h, h, h, jnp.repeat(conv_w, SHIFTS, axis=0), conv_b, ln_g, ln_b)


def _nt_dot(a, b):
    return lax.dot_general(a, b, (((1,), (1,)), ((), ())), preferred_element_type=F32)


def _transposed_values(w_t, src, dv):
    vt = _nt_dot(w_t, src)
    r = lax.broadcasted_iota(jnp.int32, vt.shape, 0)
    return jnp.where(r % VAL_PAD == dv, 1.0, vt).astype(BF16)


def _value_placement(n_kv, dv):
    r = jnp.arange(n_kv * VAL_PAD)[:, None]
    c = jnp.arange(n_kv * dv)[None, :]
    return jnp.logical_and(r // VAL_PAD == c // dv, r % VAL_PAD == c % dv).astype(BF16)


def _attend_keys_major(units, dv, s_buf, p_buf):
    def scores(unit, slot):
        q, ks, _ = unit
        row, ms = 0, []
        for k in ks:
            s = _nt_dot(k, q)
            s_buf[slot, row:row + k.shape[0], :] = s
            ms.append(jnp.max(s, axis=0, keepdims=True))
            row += k.shape[0]
        return functools.reduce(jnp.maximum, ms)

    def run_next_scores_with(slot, nxt_slot, pieces):
        row = 0
        for k in pieces:
            tile = (slice(row + k.shape[0] - 8, row + k.shape[0]), slice(0, 128))
            s_buf[slot, tile[0], tile[1]] = s_buf[slot, tile[0], tile[1]] + 0.0 * s_buf[nxt_slot, tile[0], tile[1]]
            row += k.shape[0]

    n_s, n_p = s_buf.shape[0], p_buf.shape[0]
    results = []
    m = scores(units[0], 0)
    for idx, unit in enumerate(units):
        slot, nxt_slot, pslot = idx % n_s, (idx + 1) % n_s, idx % n_p
        m_next = None
        if idx + 1 < len(units):
            m_next = scores(units[idx + 1], nxt_slot)
            run_next_scores_with(slot, nxt_slot, unit[1])
        row, acc = 0, None
        for vt in unit[2]:
            rows = slice(row, row + vt.shape[1])
            p_buf[pslot, rows, :] = jnp.exp2(s_buf[slot, rows, :] - m).astype(BF16)
            part = jnp.dot(vt, p_buf[pslot, rows, :], preferred_element_type=F32)
            acc = part if acc is None else acc + part
            row += vt.shape[1]
        results.append(acc[:dv] / acc[dv:dv + 1])
        m = m_next
    return results


def _win_kernel(sink_ref, q_ref, kp_ref, kc_ref, kn_ref, kx_ref, vp_ref, vc_ref, vn_ref, vx_ref, o_ref):
    n = pl.program_id(1)
    group = WIN_HEADS // WIN_KV_HEADS
    k_loc = jnp.concatenate([kp_ref[...], kc_ref[...], kn_ref[...]], axis=0)
    vt_loc = jnp.concatenate([vp_ref[...], vc_ref[...], vn_ref[...]], axis=1)
    k_ctx = kx_ref[...]
    vt_ctx = vx_ref[...]
    kj = lax.broadcasted_iota(jnp.int32, (3 * WINDOW, WIN_STACK * WINDOW), 0)
    qi = lax.broadcasted_iota(jnp.int32, (3 * WINDOW, WIN_STACK * WINDOW), 1) % WINDOW
    k_pos = jnp.where(n < SEQ // WINDOW, kj + (n - 1) * WINDOW, SEQ)
    valid = jnp.where(kj >= qi, jnp.where(kj <= qi + 2 * WINDOW, 1, 0), 0)
    valid = jnp.where(k_pos >= 0, jnp.where(k_pos < SEQ, valid, 0), 0) > 0
    staged = []
    for h0 in range(0, WIN_HEADS, WIN_STACK):
        heads = range(h0, h0 + WIN_STACK)
        kv = h0 // group
        q = jnp.concatenate([q_ref[:, h * HEAD_DIM:(h + 1) * HEAD_DIM] for h in heads], axis=0)
        sink = jnp.concatenate([jnp.full((1, WINDOW), sink_ref[h] * LOG2E, F32) for h in heads], axis=1)
        ksl = slice(kv * HEAD_DIM, (kv + 1) * HEAD_DIM)
        s_ctx = _nt_dot(k_ctx[:, ksl], q)
        s_loc = jnp.where(valid, _nt_dot(k_loc[:, ksl], q), NEG_INF)
        m = jnp.maximum(jnp.maximum(jnp.max(s_ctx, axis=0, keepdims=True),
                                    jnp.max(s_loc, axis=0, keepdims=True)), sink)
        staged.append((heads, kv, sink, s_ctx, s_loc, m))
    for u, (heads, kv, sink, s_ctx, s_loc, m) in enumerate(staged):
        vsl = slice(kv * VAL_PAD, (kv + 1) * VAL_PAD)
        if u + 1 < len(staged):
            m = m + 0.0 * staged[u + 1][5]
        acc = (jnp.dot(vt_ctx[vsl, :], jnp.exp2(s_ctx - m).astype(BF16), preferred_element_type=F32)
               + jnp.dot(vt_loc[vsl, :], jnp.exp2(s_loc - m).astype(BF16), preferred_element_type=F32))
        l = acc[HEAD_DIM:HEAD_DIM + 1] + jnp.exp2(sink - m)
        o = (acc[:HEAD_DIM] / l).T
        for g, h in enumerate(heads):
            o_ref[:, h * HEAD_DIM:(h + 1) * HEAD_DIM] = o[g * WINDOW:(g + 1) * WINDOW].astype(BF16)


def _win_attention(sink, q, k, v):
    nblk = SEQ // WINDOW
    cblk = CTX_LEN // WINDOW
    kvw = WIN_KV_HEADS * HEAD_DIM
    ctx0 = R_LAT // CTX_LEN
    lat = lambda n: jnp.minimum(n, nblk - 1)
    prev = lambda b, n: (b * nblk + jnp.maximum(lat(n) - 1, 0), 0)
    cur = lambda b, n: (b * nblk + lat(n), 0)
    nxt = lambda b, n: (b * nblk + jnp.minimum(lat(n) + 1, nblk - 1), 0)
    qrow = lambda b, n: (jnp.where(n < nblk, b * nblk + n, R_LAT // WINDOW + b * cblk + n - nblk), 0)
    ctx = lambda b, n: (ctx0 + b, 0)
    vw = WIN_KV_HEADS * VAL_PAD
    kvb = lambda f, w: pl.BlockSpec((WINDOW, w), f)
    cxb = lambda w: pl.BlockSpec((CTX_LEN, w), ctx)
    flip = lambda f: (lambda b, n: f(b, n)[::-1])
    vtb = lambda f: pl.BlockSpec((vw, WINDOW), flip(f))
    return pl.pallas_call(
        _win_kernel,
        grid=(BATCH, nblk + cblk),
        in_specs=[pl.BlockSpec(memory_space=pltpu.SMEM),
                  pl.BlockSpec((WINDOW, WIN_HEADS * HEAD_DIM), qrow),
                  kvb(prev, kvw), kvb(cur, kvw), kvb(nxt, kvw), cxb(kvw),
                  vtb(prev), vtb(cur), vtb(nxt), pl.BlockSpec((vw, CTX_LEN), flip(ctx))],
        out_specs=pl.BlockSpec((WINDOW, WIN_HEADS * HEAD_DIM), qrow),
        out_shape=jax.ShapeDtypeStruct((R_ALL, WIN_HEADS * HEAD_DIM), BF16),
        compiler_params=_params(),
        name="window_attention",
    )(sink, q, k, k, k, k, v, v, v, v)


def _outproj_kernel(a_ref, b_ref, wa_ref, wb_ref, bo_ref, xl_ref, xc_ref, g1_ref, sh2_ref, sc2_ref,
                    lng_ref, lnb_ref, wrh_ref, wrl_ref, br_ref, upper_ref, lower_ref, sel_ref,
                    xo_ref, route_ref, cnt_ref, xs_ref):
    is_lat = pl.program_id(0) < NT_LAT // OUT_SUB
    u2s = []
    for t in range(OUT_SUB):
        rows = slice(t * TM, (t + 1) * TM)
        x = jnp.where(is_lat, xl_ref[rows, :], xc_ref[rows, :])
        y = (jnp.dot(a_ref[rows, :], wa_ref[...], preferred_element_type=F32)
             + jnp.dot(b_ref[rows, :], wb_ref[...], preferred_element_type=F32) + bo_ref[...])
        xn = _layer_norm(DN_ALPHA * x + (1.0 + g1_ref[...]) * y, lng_ref[...], lnb_ref[...])
        xo_ref[rows, :] = xn
        u2s.append(xn * (1.0 + sc2_ref[...]) + sh2_ref[...])
    for t in range(OUT_SUB):
        rows = slice(t * TM, (t + 1) * TM)
        anchor = 0.0 * xo_ref[(t + 2) * TM - 1:(t + 2) * TM, :] if t + 1 < OUT_SUB else None
        _route_and_group(u2s[t], anchor, wrh_ref, wrl_ref, br_ref, upper_ref, lower_ref, sel_ref,
                         route_ref.at[rows, :], cnt_ref.at[t * 8:(t + 1) * 8, :],
                         xs_ref.at[t * LOCAL_ROWS:(t + 1) * LOCAL_ROWS, :])


def _route_and_group(u2, anchor, wrh_ref, wrl_ref, br_ref, upper_ref, lower_ref, sel_ref,
                     route_ref, cnt_ref, xs_ref):
    u_hi = u2.astype(BF16)
    u_lo = (u2 - u_hi.astype(F32)).astype(BF16)
    logits = (jnp.dot(u_hi, wrh_ref[...], preferred_element_type=F32)
              + jnp.dot(u_lo, wrh_ref[...], preferred_element_type=F32)
              + jnp.dot(u_hi, wrl_ref[...], preferred_element_type=F32) + br_ref[...])
    lane = lax.broadcasted_iota(jnp.int32, logits.shape, 1).astype(F32)
    ninf = -jnp.inf
    big = float(ROUTE_W)
    gl = jnp.where(lane < N_GROUPS, logits, ninf)
    gmax = jnp.max(gl, axis=-1, keepdims=True)
    gidx = jnp.min(jnp.where(gl == gmax, lane, big), axis=-1, keepdims=True)
    g_w = 1.0 / jnp.sum(jnp.exp(gl - gmax), axis=-1, keepdims=True)
    lo = N_GROUPS + EXP_PER_GROUP * gidx
    el = jnp.where(lane >= lo, jnp.where(lane < lo + EXP_PER_GROUP, logits, ninf), ninf)
    v1 = jnp.max(el, axis=-1, keepdims=True)
    i1 = jnp.min(jnp.where(el == v1, lane, big), axis=-1, keepdims=True)
    el2 = jnp.where(lane == i1, ninf, el)
    v2 = jnp.max(el2, axis=-1, keepdims=True)
    i2 = jnp.min(jnp.where(el2 == v2, lane, big), axis=-1, keepdims=True)
    e2 = jnp.exp(v2 - v1)
    w1 = g_w / (1.0 + e2)
    w2 = g_w * e2 / (1.0 + e2)
    onehot = [jnp.where(lane == i1 - N_GROUPS, 1.0, 0.0), jnp.where(lane == i2 - N_GROUPS, 1.0, 0.0)]
    cnt = [jnp.sum(o, axis=0, keepdims=True) for o in onehot]
    run_units = jnp.floor((cnt[0] + cnt[1] + (RUN_ALIGN - 1)) * (1.0 / RUN_ALIGN))
    below = RUN_ALIGN * jnp.dot(jnp.broadcast_to(run_units, (8, ROUTE_W)).astype(BF16), upper_ref[...],
                                preferred_element_type=F32)[0:1]
    lower = lower_ref[...]
    base = [below, below + cnt[0]]
    lp = []
    for s in range(2):
        earlier = jnp.dot(lower, onehot[s].astype(BF16), preferred_element_type=F32)
        lp.append(jnp.sum(onehot[s] * (base[s] + earlier), axis=-1, keepdims=True))
    rec = jnp.where(lane == 0.0, i1 - N_GROUPS,
                    jnp.where(lane == 1.0, i2 - N_GROUPS,
                              jnp.where(lane == 2.0, w1,
                                        jnp.where(lane == 3.0, w2,
                                                  jnp.where(lane == 4.0, lp[0],
                                                            jnp.where(lane == 5.0, lp[1], 0.0))))))
    route_ref[...] = rec
    cnt_ref[...] = jnp.broadcast_to(run_units * RUN_ALIGN, (8, ROUTE_W))
    sel = sel_ref[...]
    pos = lax.broadcasted_iota(jnp.int32, (LOCAL_ROWS, TM), 0).astype(F32)
    lp_lanes = []
    for s in range(2):
        hi = jnp.floor(lp[s] * (1.0 / 256.0))
        parts = jnp.where(lane == 0.0, lp[s] - 256.0 * hi, jnp.where(lane == 1.0, hi, 0.0)).astype(BF16)
        t = _nt_dot(sel, parts)
        lp_lanes.append(t[0:1] + 256.0 * t[1:2])
    perm = jnp.where(pos == lp_lanes[0], 1.0, jnp.where(pos == lp_lanes[1], 1.0, 0.0)).astype(BF16)
    xs = jnp.dot(perm, u_hi, preferred_element_type=F32)
    xs_ref[...] = xs if anchor is None else xs + anchor


def _outproj(n_tiles, mix_a, mix_b, w_a, w_b, b_out, x_lat, x_ctx, mods, ln_g, ln_b, w_r, b_r):
    rows = n_tiles * TM
    steps = n_tiles // OUT_SUB
    lat_steps = NT_LAT // OUT_SUB
    half = mix_a.shape[1]
    w_rh = w_r.astype(BF16)
    w_rl = (w_r - w_rh.astype(F32)).astype(BF16)
    upper = (jnp.arange(ROUTE_W)[:, None] < jnp.arange(ROUTE_W)[None, :]).astype(BF16)
    lower = (jnp.arange(TM)[:, None] > jnp.arange(TM)[None, :]).astype(BF16)
    sel = (jnp.arange(8)[:, None] == jnp.arange(ROUTE_W)[None, :]).astype(BF16)
    row = lambda w: pl.BlockSpec((OUT_SUB * TM, w), lambda i: (i, 0))
    mod = lambda chunk: pl.BlockSpec((None, None, 1, D_MODEL), lambda i: (_mod_row(i * OUT_SUB), chunk, 0, 0))
    vec = _full((1, D_MODEL))
    return pl.pallas_call(
        _outproj_kernel,
        grid=(steps,),
        in_specs=[row(half), row(half), _full((half, D_MODEL)), _full((half, D_MODEL)), vec,
                  pl.BlockSpec((OUT_SUB * TM, D_MODEL), lambda i: (jnp.minimum(i, lat_steps - 1), 0)),
                  pl.BlockSpec((OUT_SUB * TM, D_MODEL), lambda i: (jnp.maximum(i - lat_steps, 0), 0)),
                  mod(G1), mod(SH2), mod(SC2), vec, vec,
                  _full((D_MODEL, ROUTE_W)), _full((D_MODEL, ROUTE_W)), _full((1, ROUTE_W)),
                  _full((ROUTE_W, ROUTE_W)), _full((TM, TM)), _full((8, ROUTE_W))],
        out_specs=[row(D_MODEL), row(ROUTE_W), pl.BlockSpec((OUT_SUB * 8, ROUTE_W), lambda i: (i, 0)),
                   pl.BlockSpec((OUT_SUB * LOCAL_ROWS, D_MODEL), lambda i: (i, 0))],
        out_shape=[jax.ShapeDtypeStruct((rows, D_MODEL), F32),
                   jax.ShapeDtypeStruct((rows, ROUTE_W), F32),
                   jax.ShapeDtypeStruct((n_tiles * 8, ROUTE_W), F32),
                   jax.ShapeDtypeStruct((n_tiles * LOCAL_ROWS, D_MODEL), F32)],
        compiler_params=_params(),
        name="outproj_ln_router",
    )(mix_a, mix_b, w_a, w_b, b_out, x_lat, x_ctx, mods, mods, mods, ln_g, ln_b, w_rh, w_rl, b_r,
      upper, lower, sel)


def _aligned(i):
    return pl.multiple_of(i, RUN_ALIGN)


def _moe_kernel(te_ref, tk_ref, rows_ref, lo_ref, hi_ref, cnt_ref, bt_ref, be_ref, xs_hbm, w1_ref, w3_ref,
                w2_ref, ys_ref, xbuf, wb1, wb3, wb2, sem):
    j = pl.program_id(0)
    nt = pl.num_programs(0)
    slot = j % 2

    def issue(tile, slot_):
        e = te_ref[tile]
        first = tk_ref[tile] * TMM

        def body(i, carry):
            idx = i * N_EXPERTS + e
            start = bt_ref[idx]
            lo = jnp.maximum(start, first)
            n = jnp.minimum(start + cnt_ref[idx], first + TMM) - lo

            @pl.when(n > 0)
            def _():
                src = i * LOCAL_ROWS + be_ref[idx] + lo - start
                pltpu.make_async_copy(xs_hbm.at[pl.ds(_aligned(src), _aligned(n))],
                                      xbuf.at[slot_, pl.ds(_aligned(lo - first), _aligned(n))],
                                      sem.at[slot_]).start(priority=1)
            return carry
        lax.fori_loop(lo_ref[tile], hi_ref[tile], body, 0)

    @pl.when(j == 0)
    def _():
        xbuf[...] = jnp.zeros_like(xbuf)
        issue(0, 0)

    @pl.when(j + 1 < nt)
    def _():
        issue(j + 1, 1 - slot)

    @pl.when(jnp.logical_or(j == 0, te_ref[j] != te_ref[jnp.maximum(j - 1, 0)]))
    def _():
        wb1[...] = w1_ref[...].astype(BF16)
        wb3[...] = w3_ref[...].astype(BF16)
        wb2[...] = w2_ref[...].astype(BF16)

    n_real = rows_ref[j]

    @pl.when(n_real > 0)
    def _():
        pltpu.make_async_copy(xs_hbm.at[pl.ds(0, _aligned(n_real))], xbuf.at[slot, pl.ds(0, _aligned(n_real))],
                              sem.at[slot]).wait()
        x = xbuf[slot].astype(BF16)
        h1 = jnp.dot(x, wb1[...], preferred_element_type=F32)
        h3 = jnp.dot(x, wb3[...], preferred_element_type=F32)
        hid = h1 * _sigmoid(h1) * h3
        ys_ref[...] = jnp.dot(hid.astype(BF16), wb2[...], preferred_element_type=F32)

    @pl.when(n_real == 0)
    def _():
        ys_ref[...] = jnp.zeros_like(ys_ref)


def _moe_experts(layer, plan, xs_local, w1, w3, w2):
    tile_expert, tile_k, tile_rows, src_lo, src_hi, cnt, before_tile, before_expert, _, _ = plan
    nt = tile_expert.shape[0]
    wmap = lambda j, te, *_: (layer, te[j], 0, 0)
    grid_spec = pltpu.PrefetchScalarGridSpec(
        num_scalar_prefetch=8,
        grid=(nt,),
        in_specs=[pl.BlockSpec(memory_space=pl.ANY),
                  pl.BlockSpec((None, None, D_MODEL, EXPERT_FF), wmap),
                  pl.BlockSpec((None, None, D_MODEL, EXPERT_FF), wmap),
                  pl.BlockSpec((None, None, EXPERT_FF, D_MODEL), wmap)],
        out_specs=pl.BlockSpec((TMM, D_MODEL), lambda j, *_: (j, 0)),
        scratch_shapes=[pltpu.VMEM((2, TMM, D_MODEL), F32),
                        pltpu.VMEM((D_MODEL, EXPERT_FF), BF16),
                        pltpu.VMEM((D_MODEL, EXPERT_FF), BF16),
                        pltpu.VMEM((EXPERT_FF, D_MODEL), BF16),
                        pltpu.SemaphoreType.DMA((2,))])
    return pl.pallas_call(
        _moe_kernel,
        grid_spec=grid_spec,
        out_shape=jax.ShapeDtypeStruct((nt * TMM, D_MODEL), F32),
        compiler_params=_params(),
        name="moe_experts",
    )(tile_expert, tile_k, tile_rows, src_lo, src_hi, cnt, before_tile, before_expert, xs_local, w1, w3, w2)


def _combine_kernel(cnt_ref, bt_ref, be_ref, gs_ref, used_ref, ys_hbm, x_ref, route_ref, g2_ref, lng_ref, lnb_ref,
                    o_ref, ybuf, sem):
    i = pl.program_id(0)
    nt = pl.num_programs(0)
    slot = i % 2

    def issue(tile, slot_):
        def body(pair, carry):
            for priority in range(2):
                e = 2 * pair + priority
                idx = tile * N_EXPERTS + e
                n = cnt_ref[idx]

                @pl.when(n > 0)
                def _():
                    pltpu.make_async_copy(ys_hbm.at[pl.ds(_aligned(gs_ref[e] + bt_ref[idx]), _aligned(n))],
                                          ybuf.at[slot_, pl.ds(_aligned(be_ref[idx]), _aligned(n))],
                                          sem.at[slot_]).start(priority=priority)
            return carry
        lax.fori_loop(0, N_EXPERTS // 2, body, 0)

    @pl.when(i == 0)
    def _():
        ybuf[...] = jnp.zeros_like(ybuf)
        issue(0, 0)

    @pl.when(i + 1 < nt)
    def _():
        issue(i + 1, 1 - slot)

    used = _aligned(used_ref[i])
    pltpu.make_async_copy(ys_hbm.at[pl.ds(0, used)], ybuf.at[slot, pl.ds(0, used)], sem.at[slot]).wait()
    route = route_ref[...]
    pos = lax.broadcasted_iota(jnp.int32, (TM, LOCAL_ROWS), 1).astype(F32)
    y = ybuf[slot].astype(BF16)
    picked = [jnp.dot(jnp.where(pos == route[:, 4 + s:5 + s], 1.0, 0.0).astype(BF16), y,
                      preferred_element_type=F32) for s in range(2)]
    f = route[:, 2:3] * picked[0] + route[:, 3:4] * picked[1]
    z = DN_ALPHA * x_ref[...] + (1.0 + g2_ref[...]) * f
    o_ref[...] = _layer_norm(z, lng_ref[...], lnb_ref[...])


def _moe_combine(n_tiles, plan, ys, x_all, route, mods, ln_g, ln_b):
    _, _, _, _, _, cnt, before_tile, before_expert, group_start, used = plan
    rows = n_tiles * TM
    row = lambda w: pl.BlockSpec((TM, w), lambda i, *_: (i, 0))
    vec = pl.BlockSpec((1, D_MODEL), lambda i, *_: (0, 0))
    grid_spec = pltpu.PrefetchScalarGridSpec(
        num_scalar_prefetch=5,
        grid=(n_tiles,),
        in_specs=[pl.BlockSpec(memory_space=pl.ANY), row(D_MODEL), row(ROUTE_W),
                  pl.BlockSpec((None, None, 1, D_MODEL), lambda i, *_: (_mod_row(i), G2, 0, 0)), vec, vec],
        out_specs=row(D_MODEL),
        scratch_shapes=[pltpu.VMEM((2, LOCAL_ROWS, D_MODEL), F32), pltpu.SemaphoreType.DMA((2,))])
    return pl.pallas_call(
        _combine_kernel,
        grid_spec=grid_spec,
        out_shape=jax.ShapeDtypeStruct((rows, D_MODEL), F32),
        compiler_params=_params(),
        name="moe_combine_ln",
    )(cnt, before_tile, before_expert, group_start, used, ys, x_all, route, mods, ln_g, ln_b)


def _moe_plan(cnt_rec, n_tiles):
    cnt = cnt_rec.reshape(n_tiles, 8, ROUTE_W)[:, 0, :N_EXPERTS].astype(jnp.int32)
    nt_max = (n_tiles * (2 * TM + N_EXPERTS * (RUN_ALIGN - 1))) // TMM + N_EXPERTS
    total = jnp.sum(cnt, axis=0)
    tiles_e = (total + TMM - 1) // TMM
    tile_end = jnp.cumsum(tiles_e)
    first_tile = tile_end - tiles_e
    before_tile = jnp.cumsum(cnt, axis=0) - cnt
    before_expert = jnp.cumsum(cnt, axis=1) - cnt
    tile_id = jnp.arange(nt_max, dtype=jnp.int32)
    tile_expert = jnp.minimum(jnp.sum((tile_id[:, None] >= tile_end[None, :]).astype(jnp.int32), axis=1),
                              N_EXPERTS - 1)
    onehot = (tile_expert[:, None] == jnp.arange(N_EXPERTS, dtype=jnp.int32)[None, :]).astype(jnp.int32)
    tile_k = tile_id - jnp.sum(onehot * first_tile[None, :], axis=1)
    tile_rows = jnp.clip(jnp.sum(onehot * total[None, :], axis=1) - tile_k * TMM, 0, TMM)
    first = (tile_k * TMM)[:, None]
    run_start = jnp.sum(onehot[:, None, :] * before_tile[None, :, :], axis=2)
    run_end = run_start + jnp.sum(onehot[:, None, :] * cnt[None, :, :], axis=2)
    src_lo = jnp.sum((run_end <= first).astype(jnp.int32), axis=1)
    src_hi = jnp.sum((run_start < first + TMM).astype(jnp.int32), axis=1)
    return (tile_expert, tile_k, tile_rows, src_lo, src_hi, cnt.reshape(-1), before_tile.reshape(-1),
            before_expert.reshape(-1), first_tile * TMM, jnp.sum(cnt, axis=1))


def _router_weights(w_rg, b_rg, w_re, b_re):
    w = jnp.concatenate([w_rg, jnp.transpose(w_re, (1, 0, 2)).reshape(D_MODEL, N_EXPERTS)], axis=1)
    b = jnp.concatenate([b_rg, b_re.reshape(-1)])
    pad = ROUTE_W - w.shape[1]
    return jnp.pad(w, ((0, 0), (0, pad))), jnp.pad(b, (0, pad)).reshape(1, ROUTE_W)


def _proj1_kernel(x_ref, sh_ref, sc_ref, w_ref, b_ref, cos_ref, sin_ref, cosm_ref, sinm_ref, cosr_ref,
                  sinr_ref, gq_ref, gk_ref, gqc_ref, gkv_ref, avg_ref, wuq_ref, wuk_ref, wuv_ref, vplace_ref,
                  q_ref, qm_ref, k_ref, v_ref, km_ref, vm_ref):
    u = x_ref[...] * (1.0 + sc_ref[...]) + sh_ref[...]
    y = jnp.dot(u.astype(BF16), w_ref[...], preferred_element_type=F32) + b_ref[...]
    c_q = GQA_HEADS * HEAD_DIM
    c_qc = c_q + MLA_Q_RANK
    c_k = c_qc + GQA_KV_HEADS * HEAD_DIM
    c_v = c_k + GQA_KV_HEADS * HEAD_DIM
    c_kv = c_v + MLA_KV_RANK
    avg = avg_ref[...]

    def head_rms(t, gain):
        sq = t * t
        hi = sq.astype(BF16)
        lo = (sq - hi.astype(F32)).astype(BF16)
        a = avg[:t.shape[1], :t.shape[1]]
        ms = jnp.dot(hi, a, preferred_element_type=F32) + jnp.dot(lo, a, preferred_element_type=F32)
        return t * lax.rsqrt(ms + RMS_EPS) * gain

    def row_rms(t, gain):
        ms = jnp.mean(t * t, axis=-1, keepdims=True)
        return t * lax.rsqrt(ms + RMS_EPS) * gain

    cos = cos_ref[...]
    sin = sin_ref[...]
    cos4 = jnp.concatenate([cos] * 4, axis=1)
    sin4 = jnp.concatenate([sin] * 4, axis=1)
    q = _rope(head_rms(y[:, :c_q], gq_ref[...]), cos4, sin4, HEAD_DIM // 4) * (HEAD_DIM ** -0.5 * LOG2E)
    q_ref[...] = q.astype(BF16)
    k = _rope(head_rms(y[:, c_qc:c_k], gk_ref[...]), cos, sin, HEAD_DIM // 4)
    k_ref[...] = k.astype(BF16)

    v_ref[...] = _transposed_values(vplace_ref[...], y[:, c_k:c_v].astype(BF16), HEAD_DIM)

    qc = row_rms(y[:, c_q:c_qc], gqc_ref[...]).astype(BF16)
    qm = jnp.dot(qc, wuq_ref[...], preferred_element_type=F32)
    cosm = jnp.concatenate([cosm_ref[...]] * MLA_HEADS, axis=1)
    sinm = jnp.concatenate([sinm_ref[...]] * MLA_HEADS, axis=1)
    qm = _rope(qm, cosm, sinm, MLA_ROPE // 4) * ((MLA_NOPE + MLA_ROPE) ** -0.5 * LOG2E)
    qm_ref[...] = qm.astype(BF16)

    kvn = row_rms(y[:, c_v:c_kv], gkv_ref[...]).astype(BF16)
    kr = _rope(y[:, c_kv:], cosr_ref[...], sinr_ref[...], MLA_ROPE // 4).astype(BF16)
    km = jnp.dot(jnp.concatenate([kvn, kr], axis=1), wuk_ref[...], preferred_element_type=F32)
    km_ref[...] = km.astype(BF16)
    vm_ref[...] = _transposed_values(wuv_ref[...], kvn, MLA_V)


def _proj1(x_all, mods, w_in, b_in, tabs, gq, gk, gqc, gkv, avg, wuq, wuk, wuv):
    cos_hd, sin_hd, cos_m, sin_m, cos_r, sin_r = tabs
    kvw = GQA_KV_HEADS * HEAD_DIM
    qw = GQA_HEADS * HEAD_DIM
    mw = MLA_HEADS * MLA_PAD
    vw = MLA_HEADS * VAL_PAD
    gvw = GQA_KV_HEADS * VAL_PAD
    row = lambda w: pl.BlockSpec((TM, w), lambda i: (i, 0))
    col = lambda h: pl.BlockSpec((h, TM), lambda i: (0, i))
    tab = pl.BlockSpec((TM, 128), lambda i: (_rope_row_block(i), 0))
    vplace = _value_placement(GQA_KV_HEADS, HEAD_DIM)
    return pl.pallas_call(
        _proj1_kernel,
        grid=(NT_ALL,),
        in_specs=[row(D_MODEL), _mod_spec(SH1), _mod_spec(SC1),
                  _full((D_MODEL, ODD_IN_PAD)), _full((1, ODD_IN_PAD)), tab, tab, tab, tab, tab, tab,
                  _full((1, qw)), _full((1, kvw)), _full((1, MLA_Q_RANK)), _full((1, MLA_KV_RANK)),
                  _full((qw, qw)), _full((MLA_Q_RANK, mw)), _full((MLA_KV_RANK + 128, mw)),
                  _full((vw, MLA_KV_RANK)), _full((gvw, kvw))],
        out_specs=[row(qw), row(mw), row(kvw), col(gvw), row(mw), col(vw)],
        out_shape=[jax.ShapeDtypeStruct((R_ALL, qw), BF16),
                   jax.ShapeDtypeStruct((R_ALL, mw), BF16),
                   jax.ShapeDtypeStruct((R_ALL, kvw), BF16),
                   jax.ShapeDtypeStruct((gvw, R_ALL), BF16),
                   jax.ShapeDtypeStruct((R_ALL, mw), BF16),
                   jax.ShapeDtypeStruct((vw, R_ALL), BF16)],
        compiler_params=_params(),
        name="proj1",
    )(x_all, mods, mods, w_in, b_in, cos_hd, sin_hd, cos_m, sin_m, cos_r, sin_r,
      gq, gk, gqc, gkv, avg, wuq, wuk, wuv.T, vplace)


def _dense_kernel(q_ref, kl_ref, kc_ref, vl_ref, vc_ref, o_ref, s_buf, p_buf, *, n_heads, group, stack, dk, dv):
    tq = q_ref.shape[0]
    units = []
    for h0 in range(0, n_heads, stack):
        kv = h0 // group
        qs = [q_ref[:, h * dk:(h + 1) * dk] for h in range(h0, h0 + stack)]
        q = qs[0] if stack == 1 else jnp.concatenate(qs, axis=0)
        ks = slice(kv * dk, (kv + 1) * dk)
        vs = slice(kv * VAL_PAD, (kv + 1) * VAL_PAD)
        parts = [slice(c * (SEQ // KEY_PARTS), (c + 1) * (SEQ // KEY_PARTS)) for c in range(KEY_PARTS)]
        units.append((q, [kl_ref[c, ks] for c in parts] + [kc_ref[:, ks]],
                      [vl_ref[vs, c] for c in parts] + [vc_ref[vs, :]]))
    for u, o_t in enumerate(_attend_keys_major(units, dv, s_buf, p_buf)):
        o = o_t.T
        for g in range(stack):
            h = u * stack + g
            o_ref[:, h * dv:(h + 1) * dv] = o[g * tq:(g + 1) * tq].astype(BF16)


def _dense_attention(q, k, v, *, n_heads, group, stack, dk, dv, tq, name):
    n_kv = n_heads // group
    nq = SEQ // tq
    ctx0 = R_LAT // CTX_LEN
    lat = lambda w: pl.BlockSpec((SEQ, w), lambda b, j: (b, 0), pipeline_mode=pl.Buffered(1))
    ctx = lambda w: pl.BlockSpec((CTX_LEN, w), lambda b, j: (ctx0 + b, 0))
    lat_t = pl.BlockSpec((n_kv * VAL_PAD, SEQ), lambda b, j: (0, b), pipeline_mode=pl.Buffered(1))
    ctx_t = pl.BlockSpec((n_kv * VAL_PAD, CTX_LEN), lambda b, j: (0, ctx0 + b))
    return pl.pallas_call(
        functools.partial(_dense_kernel, n_heads=n_heads, group=group, stack=stack, dk=dk, dv=dv),
        grid=(BATCH, nq),
        in_specs=[pl.BlockSpec((tq, n_heads * dk), lambda b, j: (b * nq + j, 0)),
                  lat(n_kv * dk), ctx(n_kv * dk), lat_t, ctx_t],
        out_specs=pl.BlockSpec((tq, n_heads * dv), lambda b, j: (b * nq + j, 0)),
        out_shape=jax.ShapeDtypeStruct((R_LAT, n_heads * dv), BF16),
        scratch_shapes=[pltpu.VMEM((SCORE_SLOTS, SEQ + CTX_LEN, stack * tq), F32),
                        pltpu.VMEM((2, SEQ + CTX_LEN, stack * tq), BF16)],
        compiler_params=_params(),
        name=name,
    )(q, k, k, v, v)


def _mla_weights(w_uq, w_ukv):
    wq = w_uq.reshape(MLA_Q_RANK, MLA_HEADS, MLA_NOPE + MLA_ROPE)
    wq = jnp.pad(wq, ((0, 0), (0, 0), (0, MLA_PAD - MLA_NOPE - MLA_ROPE))).reshape(MLA_Q_RANK, -1)
    wkv = w_ukv.reshape(MLA_KV_RANK, MLA_HEADS, MLA_NOPE + MLA_V)
    wk = jnp.pad(wkv[:, :, :MLA_NOPE], ((0, 0), (0, 0), (0, MLA_PAD - MLA_NOPE))).reshape(MLA_KV_RANK, -1)
    wv = jnp.pad(wkv[:, :, MLA_NOPE:], ((0, 0), (0, 0), (0, VAL_PAD - MLA_V))).reshape(MLA_KV_RANK, -1)
    r = jnp.arange(128)[:, None]
    c = jnp.arange(MLA_HEADS * MLA_PAD)[None, :]
    place = jnp.logical_and(r < MLA_ROPE, (c % MLA_PAD) == MLA_NOPE + r).astype(F32)
    wk = jnp.concatenate([wk, place], axis=0)
    return wq.astype(BF16), wk.astype(BF16), wv.astype(BF16)


def kernel(x, c, ctx, c_ctx, even_w_in, even_b_in, even_conv_w, even_conv_b, even_conv_ln_g, even_conv_ln_b, even_sink, even_w_out, even_b_out, odd_w_in, odd_b_in, odd_q_norm, odd_k_norm, odd_mla_q_norm, odd_mla_kv_norm, odd_mla_w_uq, odd_mla_w_ukv, odd_w_out, odd_b_out, ada_w, ada_b, ln1_g, ln1_b, ln2_g, ln2_b, moe_w_rg, moe_b_rg, moe_w_re, moe_b_re, moe_w1, moe_w3, moe_w2):
    vec = lambda a: a.reshape(1, -1)
    x_lat0 = x.reshape(R_LAT, D_MODEL)
    x_ctx0 = ctx.reshape(R_CTX, D_MODEL)

    cv =jnp.concatenate([c, c_ctx[None, :], jnp.zeros((8 - BATCH - 1, D_MODEL), F32)], axis=0)
    mods = _ada_table(cv, ada_w, ada_b).reshape(DEPTH, 8, 6, 1, D_MODEL)

    cos64, sin64 = _rope_tables(HEAD_DIM)
    cos_hd, sin_hd = _pad_table(cos64, sin64, 0, HEAD_DIM, 128)
    cos32, sin32 = _rope_tables(MLA_ROPE)
    cos_m, sin_m = _pad_table(cos32, sin32, MLA_NOPE, MLA_PAD, 128)
    cos_r, sin_r = _pad_table(cos32, sin32, 0, 128, 128)

    m0 = mods[0]
    h, q0, k0, v0 = _proj0(x_lat0, x_ctx0, m0, even_w_in[0].astype(BF16), vec(even_b_in[0]), cos_hd, sin_hd)
    conv_out = _conv(h, even_conv_w[0].reshape(CONV_WIDTH, CONV_CH), vec(even_conv_b[0]),
                     vec(even_conv_ln_g[0]), vec(even_conv_ln_b[0]))
    attn = _win_attention(even_sink[0], q0, k0, v0)
    w_out = even_w_out[0].astype(BF16)
    w_r, b_r = _router_weights(moe_w_rg[0], moe_b_rg[0], moe_w_re[0], moe_b_re[0])
    x_all, route, cnt_rec, xs_local = _outproj(
        NT_ALL, conv_out, attn, w_out[:CONV_CH], w_out[CONV_CH:], vec(even_b_out[0]),
        x_lat0, x_ctx0, m0, vec(ln1_g[0]), vec(ln1_b[0]), w_r, b_r)
    plan = _moe_plan(cnt_rec, NT_ALL)
    ys = _moe_experts(0, plan, xs_local, moe_w1, moe_w3, moe_w2)
    x_all = _moe_combine(NT_ALL, plan, ys, x_all, route, m0, vec(ln2_g[0]), vec(ln2_b[0]))

    m1 = mods[1]
    w_in1 = jnp.pad(odd_w_in[0], ((0, 0), (0, ODD_IN_PAD - ODD_IN))).astype(BF16)
    b_in1 = jnp.pad(odd_b_in[0], (0, ODD_IN_PAD - ODD_IN)).reshape(1, -1)
    wuq, wuk, wuv = _mla_weights(odd_mla_w_uq[0], odd_mla_w_ukv[0])
    qw = GQA_HEADS * HEAD_DIM
    hid = jnp.arange(qw) // HEAD_DIM
    avg = ((hid[:, None] == hid[None, :]).astype(F32) / HEAD_DIM).astype(BF16)
    q1, qm, k1, v1, km, vm = _proj1(
        x_all, m1, w_in1, b_in1, (cos_hd, sin_hd, cos_m, sin_m, cos_r, sin_r),
        vec(jnp.tile(odd_q_norm[0], GQA_HEADS)), vec(jnp.tile(odd_k_norm[0], GQA_KV_HEADS)),
        vec(odd_mla_q_norm[0]), vec(odd_mla_kv_norm[0]), avg, wuq, wuk, wuv)
    o_g = _dense_attention(q1, k1, v1, n_heads=GQA_HEADS, group=GQA_HEADS // GQA_KV_HEADS, stack=1,
                           dk=HEAD_DIM, dv=HEAD_DIM, tq=512, name="gqa_attention")
    o_m = _dense_attention(qm, km, vm, n_heads=MLA_HEADS, group=1, stack=1, dk=MLA_PAD, dv=MLA_V, tq=512,
                           name="mla_attention")
    w_out = odd_w_out[0].astype(BF16)
    w_r, b_r = _router_weights(moe_w_rg[1], moe_b_rg[1], moe_w_re[1], moe_b_re[1])
    x_lat, route, cnt_rec, xs_local = _outproj(
        NT_LAT, o_g, o_m, w_out[:qw], w_out[qw:], vec(odd_b_out[0]),
        x_all, x_all, m1, vec(ln1_g[1]), vec(ln1_b[1]), w_r, b_r)
    plan = _moe_plan(cnt_rec, NT_LAT)
    ys = _moe_experts(1, plan, xs_local, moe_w1, moe_w3, moe_w2)
    x_lat = _moe_combine(NT_LAT, plan, ys, x_lat, route, m1, vec(ln2_g[1]), vec(ln2_b[1]))
    return x_lat.reshape(BATCH, SEQ, D_MODEL)
```
